```python
import math
import jax, jax.numpy as jnp
from jax import lax
import numpy as np

D_MODEL = 1024
BATCH = 8
SEQ = 2048
DEPTH = 2

GRID_W = 64
CTX_LEN = 256
HEAD_DIM = 64
A_HEADS = 8
A_KV_HEADS = 2
ROPE_THETA = 10000.0
Q_BLOCK = 128
B_HEADS = 4
WIN_ROWS_MAX = 8
WIN_COLS = 16
C_CH = 256
C_SHORT = 3
C_POS_BANDS = 16
C_POS_DIM = 2 * C_POS_BANDS + 1
C_FILTER_HIDDEN = 64
C_DECAY_TARGET = 1e-2
C_FAST_DECAY = 0.3
C_SLOW_DECAY = 1.5
A_W = A_HEADS * HEAD_DIM
A_KV_W = A_KV_HEADS * HEAD_DIM
B_W = B_HEADS * HEAD_DIM
MIX_W = A_W + B_W + C_CH
A_Q0 = 0
A_K0 = A_Q0 + A_W
A_V0 = A_K0 + A_KV_W
B_Q0 = A_V0 + A_KV_W
B_K0 = B_Q0 + B_W
B_V0 = B_K0 + B_W
C0 = B_V0 + B_W
PROJ_W = C0 + 3 * C_CH
N_EXPERTS = 64
TOP_K = 8
N_GROUPS = 8
TOPK_GROUPS = 4
EXPERT_HIDDEN = 128
SHARED_HIDDEN = 256
ROUTED_SCALE = 2.5
DN_ALPHA = (2 * DEPTH) ** 0.25
DN_BETA = (8 * DEPTH) ** -0.25
LN_EPS = 1e-6
RMS_EPS = 1e-6

kernel_name = "hybrid_dit_gqa_natten_hyena_moe"

F32 = jnp.float32


def layer_norm(x, g=None, b=None):
    xf = x.astype(F32)
    mu = jnp.mean(xf, -1, keepdims=True)
    var = jnp.mean(jnp.square(xf - mu), -1, keepdims=True)
    y = (xf - mu) * lax.rsqrt(var + LN_EPS)
    if g is not None:
        y = y * g.astype(F32) + b.astype(F32)
    return y.astype(x.dtype)


def rms_norm(x, g):
    xf = x.astype(F32)
    y = xf * lax.rsqrt(jnp.mean(jnp.square(xf), -1, keepdims=True) + RMS_EPS) * g.astype(F32)
    return y.astype(x.dtype)


def ada_modulation(cvec, w, b):
    m = jax.nn.silu(cvec) @ w + b
    return jnp.split(m, 6, axis=-1)


def modulate(x, shift, scale):
    return layer_norm(x) * (1 + scale) + shift


def to_heads(t, n):
    return t.reshape(t.shape[:-1] + (n, HEAD_DIM))


def axial_rope(n_tok):
    t = jnp.arange(n_tok)
    row = (t // GRID_W).astype(F32)
    col = (t % GRID_W).astype(F32)
    half = HEAD_DIM // 2
    inv = ROPE_THETA ** (-jnp.arange(0, half, 2, dtype=F32) / half)
    ang = jnp.concatenate([row[:, None] * inv, col[:, None] * inv], -1)
    return jnp.cos(ang), jnp.sin(ang)


def apply_rope(x, cos, sin):
    xf = x.astype(F32)
    x1, x2 = xf[..., 0::2], xf[..., 1::2]
    out = jnp.stack([x1 * cos - x2 * sin, x1 * sin + x2 * cos], -1).reshape(x.shape)
    return out.astype(x.dtype)


def dense_attention(q, k, v):
    bsz, lq, h, dh = q.shape
    g = k.shape[2]
    qg = q.reshape(bsz, lq, g, h // g, dh)
    s = jnp.einsum('bqgrd,bkgd->bgrqk', qg, k).astype(F32) * (dh ** -0.5)
    p = jax.nn.softmax(s, -1).astype(v.dtype)
    o = jnp.einsum('bgrqk,bkgd->bqgrd', p, v)
    return o.reshape(bsz, lq, h * dh)


def gqa_latent(q, k, v, k_ctx, v_ctx):
    bsz, n = q.shape[:2]
    nb = n // Q_BLOCK
    rep = A_HEADS // A_KV_HEADS
    kk = jnp.concatenate([k_ctx, k], 1)
    vv = jnp.concatenate([v_ctx, v], 1)
    qb = q.reshape(bsz, nb, Q_BLOCK, A_KV_HEADS, rep, HEAD_DIM).transpose(1, 0, 3, 4, 2, 5)
    scale = HEAD_DIM ** -0.5

    def block(qi):
        s = jnp.einsum('bgrqd,bkgd->bgrqk', qi, kk).astype(F32) * scale
        p = jax.nn.softmax(s, -1).astype(vv.dtype)
        return jnp.einsum('bgrqk,bkgd->bgrqd', p, vv)

    o = lax.map(block, qb)
    return o.transpose(1, 0, 4, 2, 3, 5).reshape(bsz, n, A_W)


def neighbourhood_attention(q, k, v, k_ctx, v_ctx, rpb):
    bsz, n = q.shape[:2]
    rows = n // GRID_W
    wr = min(WIN_ROWS_MAX, rows)
    grid = lambda t: t.reshape(bsz, rows, GRID_W, B_HEADS, HEAD_DIM)
    qg, kg, vg = grid(q), grid(k), grid(v)
    col = np.arange(GRID_W)
    col_start = np.clip(col - WIN_COLS // 2, 0, GRID_W - WIN_COLS)
    col_idx = col_start[:, None] + np.arange(WIN_COLS)[None, :]
    dc = col_idx - col[:, None] + WIN_COLS - 1
    scale = HEAD_DIM ** -0.5
    n_win = wr * WIN_COLS

    def row_block(r):
        rs = jnp.clip(r - wr // 2, 0, rows - wr)
        qr = lax.dynamic_index_in_dim(qg, r, axis=1, keepdims=False)
        kr = lax.dynamic_slice_in_dim(kg, rs, wr, axis=1)
        vr = lax.dynamic_slice_in_dim(vg, rs, wr, axis=1)
        kw = kr[:, :, col_idx]
        vw = vr[:, :, col_idx]
        dr = rs + jnp.arange(wr) - r + WIN_ROWS_MAX - 1
        bias = rpb[:, dr[None, :, None], dc[:, None, :]].astype(F32)
        s_win = jnp.einsum('bqhd,biqjhd->bhqij', qr, kw).astype(F32) * scale + bias[None]
        s_ctx = jnp.einsum('bqhd,bkhd->bhqk', qr, k_ctx).astype(F32) * scale
        s = jnp.concatenate([s_win.reshape(bsz, B_HEADS, GRID_W, n_win), s_ctx], -1)
        p = jax.nn.softmax(s, -1).astype(v.dtype)
        p_win = p[..., :n_win].reshape(bsz, B_HEADS, GRID_W, wr, WIN_COLS)
        p_ctx = p[..., n_win:]
        return (jnp.einsum('bhqij,biqjhd->bqhd', p_win, vw)
                + jnp.einsum('bhqk,bkhd->bqhd', p_ctx, v_ctx))

    o = lax.map(row_block, jnp.arange(rows))
    return o.transpose(1, 0, 2, 3, 4).reshape(bsz, n, B_W)


def short_conv(z, w, b):
    n = z.shape[1]
    p = C_SHORT // 2
    zp = jnp.pad(z, ((0, 0), (p, C_SHORT - 1 - p), (0, 0)))
    return sum(zp[:, i:i + n] * w[i] for i in range(C_SHORT)) + b


def implicit_filters(n, w1, b1, w2, b2, w3, freq):
    t = jnp.linspace(0.0, 1.0, n, dtype=F32)[:, None]
    w = 2 * math.pi * jnp.arange(n, dtype=F32)[:, None] / n
    f = jnp.linspace(1e-4, C_POS_BANDS - 1, C_POS_BANDS, dtype=F32)[None]
    z = jnp.concatenate([t, jnp.cos(f * w), -jnp.sin(f * w)], -1)
    h = jnp.sin(freq[0].astype(F32) * (z @ w1.astype(F32) + b1.astype(F32)))
    h = jnp.sin(freq[1].astype(F32) * (h @ w2.astype(F32) + b2.astype(F32)))
    h = h @ w3.astype(F32)
    lo = math.log(C_DECAY_TARGET) / C_SLOW_DECAY
    hi = math.log(C_DECAY_TARGET) / C_FAST_DECAY
    deltas = jnp.abs(jnp.linspace(lo, hi, C_CH, dtype=F32))
    decay = jnp.exp(-t * deltas[None])
    return h[:, :C_CH] * decay, h[:, C_CH:] * decay


def bidir_long_conv(u, hf, hb):
    n = u.shape[1]
    k = jnp.concatenate([hf, jnp.zeros((1, C_CH), F32), hb[1:][::-1]], 0)
    kf = jnp.fft.rfft(k, n=2 * n, axis=0)
    uf = jnp.fft.rfft(u.astype(F32), n=2 * n, axis=1)
    y = jnp.fft.irfft(uf * kf[None], n=2 * n, axis=1)[:, :n]
    return y.astype(u.dtype)


def hyena_mixer(z, conv_w, conv_b, f_w1, f_b1, f_w2, f_b2, f_w3, freq, skip):
    n = z.shape[1]
    z = short_conv(z, conv_w, conv_b)
    x0, x1, v = jnp.split(z, 3, axis=-1)
    hf, hb = implicit_filters(n, f_w1, f_b1, f_w2, f_b2, f_w3, freq)
    u = v * x1
    y = bidir_long_conv(u, hf, hb) + u * skip
    return y * x0


def moe_ffn(h, rw, rb, w1, w3, w2, sw1, sw3, sw2):
    t = h.shape[0]
    scores = jax.nn.sigmoid((h @ rw).astype(F32))
    sel = scores + rb.astype(F32)
    grp = sel.reshape(t, N_GROUPS, N_EXPERTS // N_GROUPS)
    gscore = lax.top_k(grp, 2)[0].sum(-1)
    _, gidx = lax.top_k(gscore, TOPK_GROUPS)
    gmask = jax.nn.one_hot(gidx, N_GROUPS, dtype=F32).sum(1) > 0
    emask = jnp.repeat(gmask, N_EXPERTS // N_GROUPS, axis=1)
    _, eidx = lax.top_k(jnp.where(emask, sel, -jnp.inf), TOP_K)
    wsel = jnp.take_along_axis(scores, eidx, -1)
    wsel = wsel / jnp.sum(wsel, -1, keepdims=True) * ROUTED_SCALE
    gate = jnp.sum(jax.nn.one_hot(eidx, N_EXPERTS, dtype=F32) * wsel[..., None], 1)
    a = jnp.einsum('td,edh->teh', h, w1)
    b = jnp.einsum('td,edh->teh', h, w3)
    act = (jax.nn.silu(a) * b * gate[..., None].astype(h.dtype))
    routed = jnp.einsum('teh,ehd->td', act, w2)
    shared = (jax.nn.silu(h @ sw1) * (h @ sw3)) @ sw2
    return (routed + shared).astype(h.dtype)


def moe_per_sample(h, rw, rb, w1, w3, w2, sw1, sw3, sw2):
    return lax.map(lambda hs: moe_ffn(hs, rw, rb, w1, w3, w2, sw1, sw3, sw2), h)


def setup_inputs(seed: int = 0) -> dict:
    key = jax.random.key(seed)
    ks = iter(jax.random.split(key, 40))
    nrm = lambda shape, s: jax.random.normal(next(ks), shape, F32) * s
    D = D_MODEL
    L = DEPTH
    return {
        "x": nrm((BATCH, SEQ, D), 1.0),
        "c": nrm((BATCH, D), 1.0),
        "ctx": nrm((BATCH, CTX_LEN, D), 1.0),
        "c_ctx": nrm((D,), 1.0),
        "ada_w": nrm((L, D, 6 * D), 0.5 * D ** -0.5),
        "ada_b": nrm((L, 6 * D), 0.02),
        "w_in": nrm((L, D, PROJ_W), D ** -0.5),
        "q_gain": 1.0 + nrm((L, HEAD_DIM), 0.05),
        "k_gain": 1.0 + nrm((L, HEAD_DIM), 0.05),
        "nat_rpb": nrm((L, B_HEADS, 2 * WIN_ROWS_MAX - 1, 2 * WIN_COLS - 1), 0.1),
        "hy_conv_w": nrm((L, C_SHORT, 3 * C_CH), 0.5),
        "hy_conv_b": nrm((L, 3 * C_CH), 0.02),
        "hy_f_w1": nrm((L, C_POS_DIM, C_FILTER_HIDDEN), C_POS_DIM ** -0.5),
        "hy_f_b1": nrm((L, C_FILTER_HIDDEN), 0.1),
        "hy_f_w2": nrm((L, C_FILTER_HIDDEN, C_FILTER_HIDDEN), C_FILTER_HIDDEN ** -0.5),
        "hy_f_b2": nrm((L, C_FILTER_HIDDEN), 0.1),
        "hy_f_w3": nrm((L, C_FILTER_HIDDEN, 2 * C_CH), 0.01),
        "hy_freq": 1.0 + nrm((L, 2, C_FILTER_HIDDEN), 0.1),
        "hy_skip": nrm((L, C_CH), 0.5),
        "w_out": nrm((L, MIX_W, D), DN_BETA * MIX_W ** -0.5),
        "ln1_g": 1.0 + nrm((L, D), 0.05),
        "ln1_b": nrm((L, D), 0.02),
        "router_w": nrm((L, D, N_EXPERTS), D ** -0.5),
        "router_b": nrm((L, N_EXPERTS), 0.01),
        "exp_w1": nrm((L, N_EXPERTS, D, EXPERT_HIDDEN), D ** -0.5),
        "exp_w3": nrm((L, N_EXPERTS, D, EXPERT_HIDDEN), D ** -0.5),
        "exp_w2": nrm((L, N_EXPERTS, EXPERT_HIDDEN, D), DN_BETA * EXPERT_HIDDEN ** -0.5),
        "sh_w1": nrm((L, D, SHARED_HIDDEN), D ** -0.5),
        "sh_w3": nrm((L, D, SHARED_HIDDEN), D ** -0.5),
        "sh_w2": nrm((L, SHARED_HIDDEN, D), DN_BETA * SHARED_HIDDEN ** -0.5),
        "ln2_g": 1.0 + nrm((L, D), 0.05),
        "ln2_b": nrm((L, D), 0.02),
    }


def reference(x, c, ctx, c_ctx, ada_w, ada_b, w_in, q_gain, k_gain, nat_rpb, hy_conv_w, hy_conv_b,
              hy_f_w1, hy_f_b1, hy_f_w2, hy_f_b2, hy_f_w3, hy_freq, hy_skip, w_out, ln1_g, ln1_b,
              router_w, router_b, exp_w1, exp_w3, exp_w2, sh_w1, sh_w3, sh_w2, ln2_g, ln2_b):
    n_lat = x.shape[1]
    cos, sin = axial_rope(n_lat)
    cos, sin = cos[:, None, :], sin[:, None, :]
    xc = ctx
    for l in range(DEPTH):
        last = l == DEPTH - 1
        sh1, sc1, g1, sh2, sc2, g2 = [m[:, None, :] for m in ada_modulation(c, ada_w[l], ada_b[l])]
        csh1, csc1, cg1, csh2, csc2, cg2 = ada_modulation(c_ctx, ada_w[l], ada_b[l])
        w = w_in[l]
        hy_args = (hy_conv_w[l], hy_conv_b[l], hy_f_w1[l], hy_f_b1[l], hy_f_w2[l], hy_f_b2[l],
                   hy_f_w3[l], hy_freq[l], hy_skip[l])
        moe_args = (router_w[l], router_b[l], exp_w1[l], exp_w3[l], exp_w2[l],
                    sh_w1[l], sh_w3[l], sh_w2[l])

        hc = modulate(xc, csh1, csc1)
        if last:
            pkv_a = hc @ w[:, A_K0:B_Q0]
            pkv_b = hc @ w[:, B_K0:C0]
        else:
            pc = hc @ w
            pkv_a = pc[..., A_K0:B_Q0]
            pkv_b = pc[..., B_K0:C0]
        kca = rms_norm(to_heads(pkv_a[..., :A_KV_W], A_KV_HEADS), k_gain[l])
        vca = to_heads(pkv_a[..., A_KV_W:], A_KV_HEADS)
        kcb = to_heads(pkv_b[..., :B_W], B_HEADS)
        vcb = to_heads(pkv_b[..., B_W:], B_HEADS)
        if not last:
            qca = rms_norm(to_heads(pc[..., A_Q0:A_K0], A_HEADS), q_gain[l])
            qcb = to_heads(pc[..., B_Q0:B_K0], B_HEADS)
            oc = jnp.concatenate([dense_attention(qca, kca, vca),
                                  dense_attention(qcb, kcb, vcb),
                                  hyena_mixer(pc[..., C0:], *hy_args)], -1)
            xc_mid = layer_norm(DN_ALPHA * xc + cg1 * (oc @ w_out[l]), ln1_g[l], ln1_b[l])
            hc2 = modulate(xc_mid, csh2, csc2)
            xc = layer_norm(DN_ALPHA * xc_mid + cg2 * moe_per_sample(hc2, *moe_args), ln2_g[l], ln2_b[l])

        hx = modulate(x, sh1, sc1)
        px = hx @ w
        qa = apply_rope(rms_norm(to_heads(px[..., A_Q0:A_K0], A_HEADS), q_gain[l]), cos, sin)
        ka = apply_rope(rms_norm(to_heads(px[..., A_K0:A_V0], A_KV_HEADS), k_gain[l]), cos, sin)
        va = to_heads(px[..., A_V0:B_Q0], A_KV_HEADS)
        qb = to_heads(px[..., B_Q0:B_K0], B_HEADS)
        kb = to_heads(px[..., B_K0:B_V0], B_HEADS)
        vb = to_heads(px[..., B_V0:C0], B_HEADS)
        ox = jnp.concatenate([gqa_latent(qa, ka, va, kca, vca),
                              neighbourhood_attention(qb, kb, vb, kcb, vcb, nat_rpb[l]),
                              hyena_mixer(px[..., C0:], *hy_args)], -1)
        x = layer_norm(DN_ALPHA * x + g1 * (ox @ w_out[l]), ln1_g[l], ln1_b[l])
        hx2 = modulate(x, sh2, sc2)
        x = layer_norm(DN_ALPHA * x + g2 * moe_per_sample(hx2, *moe_args), ln2_g[l], ln2_b[l])
    return x
```

```python
import functools
import math

import numpy as np
import jax
import jax.numpy as jnp
from jax import lax
from jax.experimental import pallas as pl
from jax.experimental.pallas import tpu as pltpu

F32 = jnp.float32
BF16 = jnp.bfloat16
HIGHEST = lax.Precision.HIGHEST

D_MODEL = 1024
SEQ = 2048
DEPTH = 2
GRID_W = 64
GRID_ROWS = SEQ // GRID_W
CTX_LEN = 256
S_ALL = SEQ + CTX_LEN
HEAD_DIM = 64
HALF = HEAD_DIM // 2
A_HEADS = 8
A_KV_HEADS = 2
A_REP = A_HEADS // A_KV_HEADS
ROPE_THETA = 10000.0
B_HEADS = 4
WIN_ROWS = 8
WIN_COLS = 16
C_CH = 256
C_POS_BANDS = 16
C_DECAY_TARGET = 1e-2
C_FAST_DECAY = 0.3
C_SLOW_DECAY = 1.5
A_W = A_HEADS * HEAD_DIM
A_KV_W = A_KV_HEADS * HEAD_DIM
B_W = B_HEADS * HEAD_DIM
A_K0 = A_W
A_V0 = A_K0 + A_KV_W
B_Q0 = A_V0 + A_KV_W
B_K0 = B_Q0 + B_W
B_V0 = B_K0 + B_W
C0 = B_V0 + B_W
PROJ_W = C0 + 3 * C_CH
QK_W = A_W + A_KV_W
N_EXPERTS = 64
TOP_K = 8
N_GROUPS = 8
GROUP_SIZE = N_EXPERTS // N_GROUPS
TOPK_GROUPS = 4
EXPERT_HIDDEN = 128
SHARED_HIDDEN = 256
ROUTED_SCALE = 2.5
DN_ALPHA = (2 * DEPTH) ** 0.25
LN_EPS = 1e-6
RMS_EPS = 1e-6
ATT_SCALE = HEAD_DIM ** -0.5
NEG_BIG = -1e30

VMEM_LIMIT_BYTES = 56 * 1024 * 1024
TOKEN_TILE = 256
GQA_Q_TILE = 128
FREQ_TILE = 256


def _params(*sem):
    return pltpu.CompilerParams(dimension_semantics=sem, vmem_limit_bytes=VMEM_LIMIT_BYTES)


def _layer_norm(v):
    mu = jnp.mean(v, -1, keepdims=True)
    vc = v - mu
    var = jnp.mean(vc * vc, -1, keepdims=True)
    return vc * lax.rsqrt(var + LN_EPS)


def _sigmoid(v):
    return 1.0 / (1.0 + jnp.exp(-v))


def _dot(a, b):
    return jnp.dot(a, b, preferred_element_type=F32)


def _dot_nt(a, b):
    return lax.dot_general(a, b, (((1,), (1,)), ((), ())), preferred_element_type=F32)


def _proj_body(x_ref, mod_ref, w_ref, bd_ref, gain_ref, cos_ref, sin_ref,
               qa_ref, ka_ref, va_ref, qb_ref, kb_ref, vb_ref, z_ref):
    xn = _layer_norm(x_ref[0])
    shift = mod_ref[0, 0, 0:1, :]
    scale = mod_ref[0, 0, 1:2, :]
    hx = (xn * (1.0 + scale) + shift).astype(BF16)
    px = _dot(hx, w_ref[...])
    qk = px[:, :QK_W]
    ss = _dot((qk * qk).astype(BF16), bd_ref[...])
    qkn = qk * lax.rsqrt(ss * (1.0 / HEAD_DIM) + RMS_EPS) * gain_ref[...]
    lane = lax.broadcasted_iota(jnp.int32, qkn.shape, 1)
    partner = jnp.where((lane % HEAD_DIM) < HALF,
                        pltpu.roll(qkn, QK_W - HALF, 1), pltpu.roll(qkn, HALF, 1))
    qkr = (qkn * cos_ref[...] + partner * sin_ref[...]).astype(BF16)
    for h in range(A_HEADS):
        qa_ref[0, h] = qkr[:, h * HEAD_DIM:(h + 1) * HEAD_DIM]
    for g in range(A_KV_HEADS):
        ka_ref[0, g] = qkr[:, A_K0 + g * HEAD_DIM:A_K0 + (g + 1) * HEAD_DIM]
        va_ref[0, g] = px[:, A_V0 + g * HEAD_DIM:A_V0 + (g + 1) * HEAD_DIM].astype(BF16)
    for h in range(B_HEADS):
        qb_ref[0, h] = (px[:, B_Q0 + h * HEAD_DIM:B_Q0 + (h + 1) * HEAD_DIM] * ATT_SCALE).astype(BF16)
        kb_ref[0, h] = px[:, B_K0 + h * HEAD_DIM:B_K0 + (h + 1) * HEAD_DIM].astype(BF16)
        vb_ref[0, h] = px[:, B_V0 + h * HEAD_DIM:B_V0 + (h + 1) * HEAD_DIM].astype(BF16)
    z_ref[0] = px[:, C0:]


def _project(x, mod, w_bf, bd, gain, cos_t, sin_t):
    bsz, s, d = x.shape
    tm = TOKEN_TILE
    nt = s // tm
    lat_tiles = SEQ // tm
    heads = lambda n: pl.BlockSpec((1, n, tm, HEAD_DIM), lambda b, i: (b, 0, i, 0))
    hshape = lambda n: jax.ShapeDtypeStruct((bsz, n, s, HEAD_DIM), BF16)
    return pl.pallas_call(
        _proj_body,
        grid=(bsz, nt),
        in_specs=[
            pl.BlockSpec((1, tm, d), lambda b, i: (b, i, 0)),
            pl.BlockSpec((1, 1, 6, d), lambda b, i: (b, i // lat_tiles, 0, 0)),
            pl.BlockSpec((d, PROJ_W), lambda b, i: (0, 0)),
            pl.BlockSpec((QK_W, QK_W), lambda b, i: (0, 0)),
            pl.BlockSpec((1, QK_W), lambda b, i: (0, 0)),
            pl.BlockSpec((tm, QK_W), lambda b, i: (i, 0)),
            pl.BlockSpec((tm, QK_W), lambda b, i: (i, 0)),
        ],
        out_specs=[heads(A_HEADS), heads(A_KV_HEADS), heads(A_KV_HEADS),
                   heads(B_HEADS), heads(B_HEADS), heads(B_HEADS),
                   pl.BlockSpec((1, tm, 3 * C_CH), lambda b, i: (b, i, 0))],
        out_shape=[hshape(A_HEADS), hshape(A_KV_HEADS), hshape(A_KV_HEADS),
                   hshape(B_HEADS), hshape(B_HEADS), hshape(B_HEADS),
                   jax.ShapeDtypeStruct((bsz, s, 3 * C_CH), F32)],
        compiler_params=_params("parallel", "parallel"),
        name="proj_in",
    )(x, mod, w_bf, bd, gain, cos_t, sin_t)


def _gqa_body(q_ref, k_ref, v_ref, o_ref, *, tq, lat_tiles, with_ctx_queries):
    i = pl.program_id(2)

    def attend(k, v):
        q = q_ref[0].reshape(A_REP * tq, HEAD_DIM)
        s = _dot_nt(q, k)
        m = jnp.max(s, -1, keepdims=True)
        p = jnp.exp(s - m)
        l = jnp.sum(p, -1, keepdims=True)
        o = _dot(p.astype(BF16), v) / l
        for h in range(A_REP):
            o_ref[0, :, h * HEAD_DIM:(h + 1) * HEAD_DIM] = o[h * tq:(h + 1) * tq].astype(BF16)

    @pl.when(i < lat_tiles)
    def _():
        attend(k_ref[0, 0], v_ref[0, 0])

    if with_ctx_queries:
        @pl.when(i >= lat_tiles)
        def _():
            attend(k_ref[0, 0, SEQ:, :], v_ref[0, 0, SEQ:, :])


def _gqa(qa, ka, va, s_out):
    bsz, _, s, _ = qa.shape
    tq = GQA_Q_TILE
    body = functools.partial(_gqa_body, tq=tq, lat_tiles=SEQ // tq, with_ctx_queries=s_out > SEQ)
    return pl.pallas_call(
        body,
        grid=(bsz, A_KV_HEADS, s_out // tq),
        in_specs=[
            pl.BlockSpec((1, A_REP, tq, HEAD_DIM), lambda b, g, i: (b, g, i, 0)),
            pl.BlockSpec((1, 1, s, HEAD_DIM), lambda b, g, i: (b, g, 0, 0)),
            pl.BlockSpec((1, 1, s, HEAD_DIM), lambda b, g, i: (b, g, 0, 0)),
        ],
        out_specs=pl.BlockSpec((1, tq, A_REP * HEAD_DIM), lambda b, g, i: (b, i, g)),
        out_shape=jax.ShapeDtypeStruct((bsz, s_out, A_W), BF16),
        compiler_params=_params("parallel", "parallel", "parallel"),
        name="gqa_attn",
    )(qa, ka, va)


NA_HEADS_PER_STEP = 2
NA_BIAS_PATTERNS = 8
NA_WIN_TOKENS = WIN_ROWS * GRID_W


def _na_body(q_ref, k_ref, v_ref, bias_ref, o_ref, *, with_ctx_queries):
    half_rows = WIN_ROWS // 2
    last_start = GRID_ROWS - WIN_ROWS
    for hh in range(NA_HEADS_PER_STEP):
        lanes = slice(hh * HEAD_DIM, (hh + 1) * HEAD_DIM)
        kc = k_ref[0, hh, SEQ:, :]
        vc = v_ref[0, hh, SEQ:, :]

        def row(r, carry):
            rs = jnp.clip(r - half_rows, 0, last_start)
            pattern = jnp.where(r < half_rows, r,
                                jnp.where(r > last_start + half_rows, r - last_start, half_rows))
            q0 = pl.multiple_of(r * GRID_W, GRID_W)
            k0 = pl.multiple_of(rs * GRID_W, GRID_W)
            q = q_ref[0, hh, pl.ds(q0, GRID_W), :]
            kw = k_ref[0, hh, pl.ds(k0, NA_WIN_TOKENS), :]
            vw = v_ref[0, hh, pl.ds(k0, NA_WIN_TOKENS), :]
            sw = _dot_nt(q, kw) + bias_ref[hh, pattern]
            sc = _dot_nt(q, kc)
            m = jnp.maximum(jnp.max(sw, -1, keepdims=True), jnp.max(sc, -1, keepdims=True))
            pw = jnp.exp(sw - m)
            pc = jnp.exp(sc - m)
            l = jnp.sum(pw, -1, keepdims=True) + jnp.sum(pc, -1, keepdims=True)
            o = (_dot(pw.astype(BF16), vw) + _dot(pc.astype(BF16), vc)) / l
            o_ref[0, pl.ds(q0, GRID_W), lanes] = o.astype(BF16)
            return carry

        lax.fori_loop(0, GRID_ROWS, row, 0)

        if with_ctx_queries:
            s = _dot_nt(q_ref[0, hh, SEQ:, :], kc)
            m = jnp.max(s, -1, keepdims=True)
            p = jnp.exp(s - m)
            l = jnp.sum(p, -1, keepdims=True)
            o_ref[0, SEQ:, lanes] = (_dot(p.astype(BF16), vc) / l).astype(BF16)


def _na(qb, kb, vb, bias, s_out):
    bsz, _, s, _ = qb.shape
    hp = NA_HEADS_PER_STEP
    body = functools.partial(_na_body, with_ctx_queries=s_out > SEQ)
    qkv = pl.BlockSpec((1, hp, s, HEAD_DIM), lambda b, j: (b, j, 0, 0))
    return pl.pallas_call(
        body,
        grid=(bsz, B_HEADS // hp),
        in_specs=[qkv, qkv, qkv,
                  pl.BlockSpec((hp, NA_BIAS_PATTERNS, GRID_W, NA_WIN_TOKENS), lambda b, j: (j, 0, 0, 0))],
        out_specs=pl.BlockSpec((1, s_out, hp * HEAD_DIM), lambda b, j: (b, 0, j)),
        out_shape=jax.ShapeDtypeStruct((bsz, s_out, B_W), BF16),
        compiler_params=_params("parallel", "parallel"),
        name="nbr_attn",
    )(qb, kb, vb, bias)


def _na_bias_table(rpb):
    col = np.arange(GRID_W)
    col_start = np.clip(col - WIN_COLS // 2, 0, GRID_W - WIN_COLS)
    in_win = (col[None, :] >= col_start[:, None]) & (col[None, :] < col_start[:, None] + WIN_COLS)
    dc = np.clip(col[None, :] - col[:, None] + WIN_COLS - 1, 0, 2 * WIN_COLS - 2)
    rows = np.array([0, 1, 2, 3, 4, GRID_ROWS - 3, GRID_ROWS - 2, GRID_ROWS - 1])
    rs = np.clip(rows - WIN_ROWS // 2, 0, GRID_ROWS - WIN_ROWS)
    dr = rs[:, None] + np.arange(WIN_ROWS)[None, :] - rows[:, None] + WIN_ROWS - 1
    t = rpb.astype(F32)[:, dr]
    t = t[:, :, :, dc]
    t = jnp.where(in_win[None, None, None], t, NEG_BIG)
    t = jnp.transpose(t, (0, 1, 3, 2, 4))
    return t.reshape(B_HEADS, NA_BIAS_PATTERNS, GRID_W, NA_WIN_TOKENS)


def _dft_tables(n):
    f = lax.broadcasted_iota(jnp.int32, (n, n), 0)
    s = lax.broadcasted_iota(jnp.int32, (n, n), 1)
    ang = ((f * s) % (2 * n)).astype(F32) * (math.pi / n)
    fc = jnp.cos(ang)
    alt = jnp.where(s % 2 == 0, 1.0, -1.0).astype(F32)
    fs = jnp.where(f == 0, alt, jnp.sin(ang))
    return fc.astype(BF16), fs.astype(BF16), fs.T.astype(BF16)


def _split_bf16(v):
    hi = v.astype(BF16)
    lo = (v - hi.astype(F32)).astype(BF16)
    return hi, lo


def _spectrum_body(fc_ref, fs_ref, kp_ref, km_ref, kc_ref, ks_ref, *, tf):
    kp_hi, kp_lo = _split_bf16(kp_ref[...])
    km_hi, km_lo = _split_bf16(km_ref[...])
    fc = fc_ref[...]
    fs = fs_ref[...]
    kc_ref[...] = _dot(fc, kp_hi) + _dot(fc, kp_lo)
    ks_sin = _dot(fs, km_hi) + _dot(fs, km_lo)
    ks_nyq = _dot(fs, kp_hi) + _dot(fs, kp_lo)
    row = lax.broadcasted_iota(jnp.int32, ks_sin.shape, 0) + pl.program_id(0) * tf
    ks_ref[...] = jnp.where(row == 0, ks_nyq, ks_sin)


def _filter_spectrum(fc, fs, kplus, kminus):
    n = fc.shape[0]
    tf = min(FREQ_TILE, n)
    tab = pl.BlockSpec((tf, n), lambda j: (j, 0))
    filt = pl.BlockSpec((n, C_CH), lambda j: (0, 0))
    out = pl.BlockSpec((tf, C_CH), lambda j: (j, 0))
    return pl.pallas_call(
        functools.partial(_spectrum_body, tf=tf),
        grid=(n // tf,),
        in_specs=[tab, tab, filt, filt],
        out_specs=[out, out],
        out_shape=[jax.ShapeDtypeStruct((n, C_CH), F32)] * 2,
        compiler_params=_params("parallel"),
        name="hyena_spectrum",
    )(fc, fs, kplus, kminus)


def _hyena_body(z_ref, cw_ref, cb_ref, skip_ref, fc_ref, fs_ref, fct_ref, fst_ref, kc_ref, ks_ref,
                o_ref, ub_ref, uf_ref, x0_ref, y_ref, *, n, tf):
    j = pl.program_id(1)

    @pl.when(j == 0)
    def _():
        z = z_ref[0]
        t = lax.broadcasted_iota(jnp.int32, z.shape, 0)
        prev = jnp.where(t == 0, 0.0, pltpu.roll(z, 1, 0))
        nxt = jnp.where(t == n - 1, 0.0, pltpu.roll(z, n - 1, 0))
        zc = prev * cw_ref[0:1, :] + z * cw_ref[1:2, :] + nxt * cw_ref[2:3, :] + cb_ref[...]
        u = zc[:, 2 * C_CH:] * zc[:, C_CH:2 * C_CH]
        x0_ref[...] = zc[:, :C_CH]
        uf_ref[...] = u
        ub_ref[...] = u.astype(BF16)
        y_ref[...] = jnp.zeros_like(y_ref)

    ub = ub_ref[...]
    uc = _dot(fc_ref[...], ub)
    us = _dot(fs_ref[...], ub)
    kc = kc_ref[...]
    ks = ks_ref[...]
    row = lax.broadcasted_iota(jnp.int32, uc.shape, 0) + j * tf
    first = row == 0
    wgt = jnp.where(first, 1.0, 2.0) * (1.0 / (2 * n))
    re = jnp.where(first, uc * kc, uc * kc - us * ks) * wgt
    im = jnp.where(first, us * ks, uc * ks + us * kc) * wgt
    y_ref[...] += _dot(fct_ref[...], re.astype(BF16)) + _dot(fst_ref[...], im.astype(BF16))

    @pl.when(j == pl.num_programs(1) - 1)
    def _():
        o_ref[0] = ((y_ref[...] + uf_ref[...] * skip_ref[...]) * x0_ref[...]).astype(BF16)


def _hyena(z, seg_start, n, cw, cb, skip, tables, kc, ks):
    bsz = z.shape[0]
    fc, fs, fst = tables
    tf = min(FREQ_TILE, n)
    seg = seg_start // n
    row_blk = pl.BlockSpec((tf, n), lambda b, j: (j, 0))
    col_blk = pl.BlockSpec((n, tf), lambda b, j: (0, j))
    spec_blk = pl.BlockSpec((tf, C_CH), lambda b, j: (j, 0))
    small = lambda r, c: pl.BlockSpec((r, c), lambda b, j: (0, 0))
    return pl.pallas_call(
        functools.partial(_hyena_body, n=n, tf=tf),
        grid=(bsz, n // tf),
        in_specs=[pl.BlockSpec((1, n, 3 * C_CH), lambda b, j: (b, seg, 0)),
                  small(3, 3 * C_CH), small(1, 3 * C_CH), small(1, C_CH),
                  row_blk, row_blk, col_blk, col_blk, spec_blk, spec_blk],
        out_specs=pl.BlockSpec((1, n, C_CH), lambda b, j: (b, 0, 0)),
        out_shape=jax.ShapeDtypeStruct((bsz, n, C_CH), BF16),
        scratch_shapes=[pltpu.VMEM((n, C_CH), BF16), pltpu.VMEM((n, C_CH), F32),
                        pltpu.VMEM((n, C_CH), F32), pltpu.VMEM((n, C_CH), F32)],
        compiler_params=_params("parallel", "arbitrary"),
        name="hyena_conv",
    )(z, cw, cb, skip, fc, fs, fc, fst, kc, ks)


def _implicit_filters(n, w1, b1, w2, b2, w3, freq):
    t = jnp.linspace(0.0, 1.0, n, dtype=F32)[:, None]
    w = 2 * math.pi * jnp.arange(n, dtype=F32)[:, None] / n
    f = jnp.linspace(1e-4, C_POS_BANDS - 1, C_POS_BANDS, dtype=F32)[None]
    z = jnp.concatenate([t, jnp.cos(f * w), -jnp.sin(f * w)], -1)
    h = jnp.sin(freq[0] * (jnp.dot(z, w1, precision=HIGHEST) + b1))
    h = jnp.sin(freq[1] * (jnp.dot(h, w2, precision=HIGHEST) + b2))
    h = jnp.dot(h, w3, precision=HIGHEST)
    lo = math.log(C_DECAY_TARGET) / C_SLOW_DECAY
    hi = math.log(C_DECAY_TARGET) / C_FAST_DECAY
    deltas = jnp.abs(jnp.linspace(lo, hi, C_CH, dtype=F32))
    decay = jnp.exp(-t * deltas[None])
    hf, hb = h[:, :C_CH] * decay, h[:, C_CH:] * decay
    hb = hb.at[0].set(0.0)
    return hf + hb, hf - hb


def _router_gates(logits_t, rb):
    tm = logits_t.shape[1]
    scores = _sigmoid(logits_t)
    sel = scores + rb
    sel3 = sel.reshape(N_GROUPS, GROUP_SIZE, tm)
    kk = lax.broadcasted_iota(jnp.int32, sel3.shape, 1)
    m1 = jnp.max(sel3, 1, keepdims=True)
    i1 = jnp.min(jnp.where(sel3 == m1, kk, GROUP_SIZE), 1, keepdims=True)
    m2 = jnp.max(jnp.where(kk == i1, -jnp.inf, sel3), 1, keepdims=True)
    gscore = (m1 + m2).reshape(N_GROUPS, tm)
    gid = lax.broadcasted_iota(jnp.int32, gscore.shape, 0)
    rank = jnp.zeros(gscore.shape, jnp.int32)
    for g in range(N_GROUPS):
        other = gscore[g:g + 1, :]
        ahead = (other > gscore) | ((other == gscore) & (g < gid))
        rank = rank + ahead.astype(jnp.int32)
    gsel = (rank < TOPK_GROUPS).reshape(N_GROUPS, 1, tm)
    cand = jnp.where(gsel, sel3, -jnp.inf).reshape(N_EXPERTS, tm)
    eid = lax.broadcasted_iota(jnp.int32, cand.shape, 0)
    chosen = jnp.zeros(cand.shape, jnp.bool_)
    for _ in range(TOP_K):
        m = jnp.max(cand, 0, keepdims=True)
        pick = eid == jnp.min(jnp.where(cand == m, eid, N_EXPERTS), 0, keepdims=True)
        chosen = chosen | pick
        cand = jnp.where(pick, -jnp.inf, cand)
    wsel = jnp.where(chosen, scores, 0.0)
    return wsel / jnp.sum(wsel, 0, keepdims=True) * ROUTED_SCALE


def _out_body(oa_ref, ob_ref, oc_ref, x_ref, mod_ref, wa_ref, wb_ref, wc_ref, lng_ref, lnb_ref,
              rwt_ref, rb_ref, xmid_ref, h_ref, gate_ref):
    attn = _dot(oa_ref[0], wa_ref[...]) + _dot(ob_ref[0], wb_ref[...]) + _dot(oc_ref[0], wc_ref[...])
    gate1 = mod_ref[0, 0, 2:3, :]
    shift2 = mod_ref[0, 0, 3:4, :]
    scale2 = mod_ref[0, 0, 4:5, :]
    xm = _layer_norm(DN_ALPHA * x_ref[0] + gate1 * attn) * lng_ref[...] + lnb_ref[...]
    xmid_ref[0] = xm
    h = _layer_norm(xm) * (1.0 + scale2) + shift2
    hi, lo = _split_bf16(h)
    h_ref[0] = hi
    logits_t = _dot_nt(rwt_ref[...], jnp.concatenate([hi, lo, hi], axis=1))
    gates_t = _router_gates(logits_t, rb_ref[...])
    pad = jnp.zeros((128 - N_EXPERTS, gates_t.shape[1]), F32)
    gates = jnp.concatenate([gates_t, pad], axis=0).T
    for g in range(N_GROUPS):
        gate_ref[0, g] = gates[:, g * GROUP_SIZE:(g + 1) * GROUP_SIZE]


def _out_proj(oa, ob, oc, x, mod, wa, wb, wc, lng, lnb, rwt3, rb):
    bsz, s, _ = oa.shape
    d = D_MODEL
    tm = TOKEN_TILE
    lat_tiles = SEQ // tm
    tok = lambda w: pl.BlockSpec((1, tm, w), lambda b, i: (b, i, 0))
    full = lambda r, c: pl.BlockSpec((r, c), lambda b, i: (0, 0))
    return pl.pallas_call(
        _out_body,
        grid=(bsz, s // tm),
        in_specs=[tok(A_W), tok(B_W), tok(C_CH), tok(d),
                  pl.BlockSpec((1, 1, 6, d), lambda b, i: (b, i // lat_tiles, 0, 0)),
                  full(A_W, d), full(B_W, d), full(C_CH, d), full(1, d), full(1, d),
                  full(N_EXPERTS, 3 * d), full(N_EXPERTS, 1)],
        out_specs=[tok(d), tok(d),
                   pl.BlockSpec((1, N_GROUPS, tm, GROUP_SIZE), lambda b, i: (b, 0, i, 0))],
        out_shape=[jax.ShapeDtypeStruct((bsz, s, d), F32),
                   jax.ShapeDtypeStruct((bsz, s, d), BF16),
                   jax.ShapeDtypeStruct((bsz, N_GROUPS, s, GROUP_SIZE), F32)],
        compiler_params=_params("parallel", "parallel"),
        name="proj_out_router",
    )(oa, ob, oc, x, mod, wa, wb, wc, lng, lnb, rwt3, rb)


def _moe_body(h_ref, gate_ref, xmid_ref, modl_ref, modc_ref, w1_ref, w3_ref, w2_ref,
              sw1_ref, sw3_ref, sw2_ref, lng_ref, lnb_ref, o_ref, acc_ref, *, tm):
    g = pl.program_id(2)
    h = h_ref[0]

    @pl.when(g == 0)
    def _():
        a = _dot(h, sw1_ref[...])
        act = (a * _sigmoid(a) * _dot(h, sw3_ref[...])).astype(BF16)
        acc_ref[...] = _dot(act, sw2_ref[...])

    a = _dot(h, w1_ref[...])
    act = a * _sigmoid(a) * _dot(h, w3_ref[...])
    gt = gate_ref[0, 0]
    parts = [act[:, k * EXPERT_HIDDEN:(k + 1) * EXPERT_HIDDEN] * gt[:, k:k + 1] for k in range(GROUP_SIZE)]
    acc_ref[...] += _dot(jnp.concatenate(parts, axis=1).astype(BF16), w2_ref[...])

    @pl.when(g == N_GROUPS - 1)
    def _():
        row = lax.broadcasted_iota(jnp.int32, (tm, 1), 0) + pl.program_id(1) * tm
        gate2 = jnp.where(row >= SEQ, modc_ref[0, 0, 5:6, :], modl_ref[0, 0, 5:6, :])
        y = DN_ALPHA * xmid_ref[0] + gate2 * acc_ref[...]
        o_ref[0] = _layer_norm(y) * lng_ref[...] + lnb_ref[...]


def _moe(h, gates, xmid, mod, w1c, w3c, w2c, sw1, sw3, sw2, lng, lnb, tm):
    bsz, s, d = h.shape
    gw = GROUP_SIZE * EXPERT_HIDDEN
    tok = pl.BlockSpec((1, tm, d), lambda b, i, g: (b, i, 0))
    full = lambda r, c: pl.BlockSpec((r, c), lambda b, i, g: (0, 0))
    return pl.pallas_call(
        functools.partial(_moe_body, tm=tm),
        grid=(bsz, s // tm, N_GROUPS),
        in_specs=[tok,
                  pl.BlockSpec((1, 1, tm, GROUP_SIZE), lambda b, i, g: (b, g, i, 0)),
                  tok,
                  pl.BlockSpec((1, 1, 6, d), lambda b, i, g: (b, 0, 0, 0)),
                  pl.BlockSpec((1, 1, 6, d), lambda b, i, g: (b, 1, 0, 0)),
                  pl.BlockSpec((d, gw), lambda b, i, g: (0, g)),
                  pl.BlockSpec((d, gw), lambda b, i, g: (0, g)),
                  pl.BlockSpec((gw, d), lambda b, i, g: (g, 0)),
                  full(d, SHARED_HIDDEN), full(d, SHARED_HIDDEN), full(SHARED_HIDDEN, d),
                  full(1, d), full(1, d)],
        out_specs=tok,
        out_shape=jax.ShapeDtypeStruct((bsz, s, d), F32),
        scratch_shapes=[pltpu.VMEM((tm, d), F32)],
        compiler_params=_params("parallel", "parallel", "arbitrary"),
        name="moe_ffn",
    )(h, gates, xmid, mod, mod, w1c, w3c, w2c, sw1, sw3, sw2, lng, lnb)


def _rope_tables():
    t = jnp.arange(SEQ)
    row = (t // GRID_W).astype(F32)
    col = (t % GRID_W).astype(F32)
    inv = ROPE_THETA ** (-jnp.arange(0, HALF, 2, dtype=F32) / HALF)
    ang = jnp.concatenate([row[:, None] * inv, col[:, None] * inv], -1)
    cos, sin = jnp.cos(ang), jnp.sin(ang)
    cos_h = jnp.concatenate([cos, cos], -1)
    sin_h = jnp.concatenate([-sin, sin], -1)
    n_heads = QK_W // HEAD_DIM
    cos_t = jnp.concatenate([jnp.tile(cos_h, (1, n_heads)), jnp.ones((CTX_LEN, QK_W), F32)], 0)
    sin_t = jnp.concatenate([jnp.tile(sin_h, (1, n_heads)), jnp.zeros((CTX_LEN, QK_W), F32)], 0)
    return cos_t, sin_t


def _modulation(cvec, w, b):
    return (jnp.dot(jax.nn.silu(cvec), w, precision=HIGHEST) + b).reshape(cvec.shape[:-1] + (6, D_MODEL))


def kernel(x, c, ctx, c_ctx, ada_w, ada_b, w_in, q_gain, k_gain, nat_rpb, hy_conv_w, hy_conv_b, hy_f_w1, hy_f_b1, hy_f_w2, hy_f_b2, hy_f_w3, hy_freq, hy_skip, w_out, ln1_g, ln1_b, router_w, router_b, exp_w1, exp_w3, exp_w2, sh_w1, sh_w3, sh_w2, ln2_g, ln2_b):
    bsz = x.shape[0]
    d = D_MODEL
    xs = jnp.concatenate([x, ctx], axis=1)

    head_perm = np.concatenate([np.arange(0, HEAD_DIM, 2), np.arange(1, HEAD_DIM, 2)])
    col_perm = np.arange(PROJ_W)
    for hd in range(QK_W // HEAD_DIM):
        col_perm[hd * HEAD_DIM:(hd + 1) * HEAD_DIM] = hd * HEAD_DIM + head_perm
    cos_t, sin_t = _rope_tables()
    bd = jnp.asarray(np.kron(np.eye(QK_W // HEAD_DIM), np.ones((HEAD_DIM, HEAD_DIM))), BF16)
    tables_lat = _dft_tables(SEQ)
    tables_ctx = _dft_tables(CTX_LEN)

    for l in range(DEPTH):
        last = l == DEPTH - 1
        mod_lat = _modulation(c, ada_w[l], ada_b[l])
        mod_ctx = _modulation(c_ctx, ada_w[l], ada_b[l])
        mod = jnp.stack([mod_lat, jnp.broadcast_to(mod_ctx, mod_lat.shape)], axis=1)

        w_bf = w_in[l][:, col_perm].astype(BF16)
        gain = jnp.concatenate([jnp.tile(q_gain[l][head_perm] * ATT_SCALE, A_HEADS),
                                jnp.tile(k_gain[l][head_perm], A_KV_HEADS)])[None].astype(F32)
        qa, ka, va, qb, kb, vb, z = _project(xs, mod, w_bf, bd, gain, cos_t, sin_t)

        s_out = SEQ if last else S_ALL
        oa = _gqa(qa, ka, va, s_out)
        ob = _na(qb, kb, vb, _na_bias_table(nat_rpb[l]), s_out)

        filt = (hy_f_w1[l], hy_f_b1[l], hy_f_w2[l], hy_f_b2[l], hy_f_w3[l], hy_freq[l])
        cw, cb, skip = hy_conv_w[l], hy_conv_b[l][None], hy_skip[l][None]
        kc, ks = _filter_spectrum(tables_lat[0], tables_lat[1], *_implicit_filters(SEQ, *filt))
        oc = _hyena(z, 0, SEQ, cw, cb, skip, tables_lat, kc, ks)
        if not last:
            kc, ks = _filter_spectrum(tables_ctx[0], tables_ctx[1], *_implicit_filters(CTX_LEN, *filt))
            oc = jnp.concatenate([oc, _hyena(z, SEQ, CTX_LEN, cw, cb, skip, tables_ctx, kc, ks)], axis=1)

        wo = w_out[l].astype(BF16)
        rw_hi, rw_lo = _split_bf16(router_w[l].T)
        rwt3 = jnp.concatenate([rw_hi, rw_hi, rw_lo], axis=1)
        xmid, h, gates = _out_proj(oa, ob, oc, xs, mod, wo[:A_W], wo[A_W:A_W + B_W], wo[A_W + B_W:],
                                   ln1_g[l][None], ln1_b[l][None], rwt3, router_b[l][:, None])

        w1c = jnp.transpose(exp_w1[l], (1, 0, 2)).reshape(d, N_EXPERTS * EXPERT_HIDDEN).astype(BF16)
        w3c = jnp.transpose(exp_w3[l], (1, 0, 2)).reshape(d, N_EXPERTS * EXPERT_HIDDEN).astype(BF16)
        w2c = exp_w2[l].reshape(N_EXPERTS * EXPERT_HIDDEN, d).astype(BF16)
        xs = _moe(h, gates, xmid, mod, w1c, w3c, w2c,
                  sh_w1[l].astype(BF16), sh_w3[l].astype(BF16), sh_w2[l].astype(BF16),
                  ln2_g[l][None], ln2_b[l][None], tm=512 if last else 768)
    return xs
```

```python
import functools
import math

import numpy as np
import jax
import jax.numpy as jnp
from jax import lax
from jax.experimental import pallas as pl
from jax.experimental.pallas import tpu as pltpu

F32 = jnp.float32
BF16 = jnp.bfloat16
HIGHEST = lax.Precision.HIGHEST

D_MODEL = 1024
SEQ = 2048
DEPTH = 2
GRID_W = 64
GRID_ROWS = SEQ // GRID_W
CTX_LEN = 256
S_ALL = SEQ + CTX_LEN
HEAD_DIM = 64
HALF = HEAD_DIM // 2
A_HEADS = 8
A_KV_HEADS = 2
A_REP = A_HEADS // A_KV_HEADS
ROPE_THETA = 10000.0
B_HEADS = 4
WIN_ROWS = 8
WIN_COLS = 16
C_CH = 256
C_POS_BANDS = 16
C_DECAY_TARGET = 1e-2
C_FAST_DECAY = 0.3
C_SLOW_DECAY = 1.5
A_W = A_HEADS * HEAD_DIM
A_KV_W = A_KV_HEADS * HEAD_DIM
B_W = B_HEADS * HEAD_DIM
A_K0 = A_W
A_V0 = A_K0 + A_KV_W
B_Q0 = A_V0 + A_KV_W
B_K0 = B_Q0 + B_W
B_V0 = B_K0 + B_W
C0 = B_V0 + B_W
PROJ_W = C0 + 3 * C_CH
QK_W = A_W + A_KV_W
N_EXPERTS = 64
TOP_K = 8
N_GROUPS = 8
GROUP_SIZE = N_EXPERTS // N_GROUPS
TOPK_GROUPS = 4
EXPERT_HIDDEN = 128
SHARED_HIDDEN = 256
ROUTED_SCALE = 2.5
DN_ALPHA = (2 * DEPTH) ** 0.25
LN_EPS = 1e-6
RMS_EPS = 1e-6
ATT_SCALE = HEAD_DIM ** -0.5
NEG_BIG = -1e30

VMEM_LIMIT_BYTES = 56 * 1024 * 1024
TOKEN_TILE = 256
GQA_Q_TILE = 256
GQA_CHAINS = 2
FREQ_TILE = 256


def _params(*sem):
    return pltpu.CompilerParams(dimension_semantics=sem, vmem_limit_bytes=VMEM_LIMIT_BYTES)


def _layer_norm(v):
    mu = jnp.mean(v, -1, keepdims=True)
    vc = v - mu
    var = jnp.mean(vc * vc, -1, keepdims=True)
    return vc * lax.rsqrt(var + LN_EPS)


def _sigmoid(v):
    return 1.0 / (1.0 + jnp.exp(-v))


def _dot(a, b):
    return jnp.dot(a, b, preferred_element_type=F32)


def _dot_nt(a, b):
    return lax.dot_general(a, b, (((1,), (1,)), ((), ())), preferred_element_type=F32)


def _proj_body(x_ref, mod_ref, w_ref, bd_ref, gain_ref, cos_ref, sin_ref,
               qa_ref, ka_ref, va_ref, qb_ref, kb_ref, vb_ref, z_ref):
    xn = _layer_norm(x_ref[0])
    shift = mod_ref[0, 0, 0:1, :]
    scale = mod_ref[0, 0, 1:2, :]
    hx = (xn * (1.0 + scale) + shift).astype(BF16)
    px = _dot(hx, w_ref[...])
    qk = px[:, :QK_W]
    ss = _dot((qk * qk).astype(BF16), bd_ref[...])
    qkn = qk * lax.rsqrt(ss * (1.0 / HEAD_DIM) + RMS_EPS) * gain_ref[...]
    lane = lax.broadcasted_iota(jnp.int32, qkn.shape, 1)
    partner = jnp.where((lane % HEAD_DIM) < HALF,
                        pltpu.roll(qkn, QK_W - HALF, 1), pltpu.roll(qkn, HALF, 1))
    qkr = (qkn * cos_ref[...] + partner * sin_ref[...]).astype(BF16)
    for h in range(A_HEADS):
        qa_ref[0, h] = qkr[:, h * HEAD_DIM:(h + 1) * HEAD_DIM]
    for g in range(A_KV_HEADS):
        ka_ref[0, g] = qkr[:, A_K0 + g * HEAD_DIM:A_K0 + (g + 1) * HEAD_DIM]
        va_ref[0, g] = px[:, A_V0 + g * HEAD_DIM:A_V0 + (g + 1) * HEAD_DIM].astype(BF16)
    for h in range(B_HEADS):
        qb_ref[0, h] = (px[:, B_Q0 + h * HEAD_DIM:B_Q0 + (h + 1) * HEAD_DIM] * ATT_SCALE).astype(BF16)
        kb_ref[0, h] = px[:, B_K0 + h * HEAD_DIM:B_K0 + (h + 1) * HEAD_DIM].astype(BF16)
        vb_ref[0, h] = px[:, B_V0 + h * HEAD_DIM:B_V0 + (h + 1) * HEAD_DIM].astype(BF16)
    z_ref[0] = px[:, C0:]


def _project(x, mod, w_bf, bd, gain, cos_t, sin_t):
    bsz, s, d = x.shape
    tm = TOKEN_TILE
    nt = s // tm
    lat_tiles = SEQ // tm
    heads = lambda n: pl.BlockSpec((1, n, tm, HEAD_DIM), lambda b, i: (b, 0, i, 0))
    hshape = lambda n: jax.ShapeDtypeStruct((bsz, n, s, HEAD_DIM), BF16)
    return pl.pallas_call(
        _proj_body,
        grid=(bsz, nt),
        in_specs=[
            pl.BlockSpec((1, tm, d), lambda b, i: (b, i, 0)),
            pl.BlockSpec((1, 1, 6, d), lambda b, i: (b, i // lat_tiles, 0, 0)),
            pl.BlockSpec((d, PROJ_W), lambda b, i: (0, 0)),
            pl.BlockSpec((QK_W, QK_W), lambda b, i: (0, 0)),
            pl.BlockSpec((1, QK_W), lambda b, i: (0, 0)),
            pl.BlockSpec((tm, QK_W), lambda b, i: (i, 0)),
            pl.BlockSpec((tm, QK_W), lambda b, i: (i, 0)),
        ],
        out_specs=[heads(A_HEADS), heads(A_KV_HEADS), heads(A_KV_HEADS),
                   heads(B_HEADS), heads(B_HEADS), heads(B_HEADS),
                   pl.BlockSpec((1, tm, 3 * C_CH), lambda b, i: (b, i, 0))],
        out_shape=[hshape(A_HEADS), hshape(A_KV_HEADS), hshape(A_KV_HEADS),
                   hshape(B_HEADS), hshape(B_HEADS), hshape(B_HEADS),
                   jax.ShapeDtypeStruct((bsz, s, 3 * C_CH), F32)],
        compiler_params=_params("parallel", "parallel"),
        name="proj_in",
    )(x, mod, w_bf, bd, gain, cos_t, sin_t)


def _gqa_body(q_ref, k_ref, v_ref, o_ref, *, tq, lat_tiles, with_ctx_queries):
    i = pl.program_id(2)

    def attend(k, v):
        for c in range(GQA_CHAINS):
            hc = A_REP // GQA_CHAINS
            q = q_ref[0, c * hc:(c + 1) * hc].reshape(hc * tq, HEAD_DIM)
            s = _dot_nt(q, k)
            m = jnp.max(s, -1, keepdims=True)
            p = jnp.exp(s - m)
            l = jnp.sum(p, -1, keepdims=True)
            o = _dot(p.astype(BF16), v) / l
            for h in range(hc):
                hh = c * hc + h
                o_ref[0, :, hh * HEAD_DIM:(hh + 1) * HEAD_DIM] = o[h * tq:(h + 1) * tq].astype(BF16)

    @pl.when(i < lat_tiles)
    def _():
        attend(k_ref[0, 0], v_ref[0, 0])

    if with_ctx_queries:
        @pl.when(i >= lat_tiles)
        def _():
            attend(k_ref[0, 0, SEQ:, :], v_ref[0, 0, SEQ:, :])


def _gqa(qa, ka, va, s_out):
    bsz, _, s, _ = qa.shape
    tq = GQA_Q_TILE
    body = functools.partial(_gqa_body, tq=tq, lat_tiles=SEQ // tq, with_ctx_queries=s_out > SEQ)
    return pl.pallas_call(
        body,
        grid=(bsz, A_KV_HEADS, s_out // tq),
        in_specs=[
            pl.BlockSpec((1, A_REP, tq, HEAD_DIM), lambda b, g, i: (b, g, i, 0)),
            pl.BlockSpec((1, 1, s, HEAD_DIM), lambda b, g, i: (b, g, 0, 0)),
            pl.BlockSpec((1, 1, s, HEAD_DIM), lambda b, g, i: (b, g, 0, 0)),
        ],
        out_specs=pl.BlockSpec((1, tq, A_REP * HEAD_DIM), lambda b, g, i: (b, i, g)),
        out_shape=jax.ShapeDtypeStruct((bsz, s_out, A_W), BF16),
        compiler_params=_params("parallel", "parallel", "parallel"),
        name="gqa_attn",
    )(qa, ka, va)


NA_HEADS_PER_STEP = 2
NA_Q_ROWS = 8
NA_K_ROWS = 16
NA_Q_TOKENS = NA_Q_ROWS * GRID_W
NA_K_TOKENS = NA_K_ROWS * GRID_W
NA_Q_BLOCKS = GRID_ROWS // NA_Q_ROWS
NA_KEY_ROW0 = (0, 4, 12, 16)
NA_PATTERN_BLOCKS = (0, 1, 3)


def _na_body(q_ref, k_ref, v_ref, bias_ref, o_ref):
    j = pl.program_id(2)
    key_row0 = jnp.where(j == 0, NA_KEY_ROW0[0],
                         jnp.where(j == 1, NA_KEY_ROW0[1], jnp.where(j == 2, NA_KEY_ROW0[2], NA_KEY_ROW0[3])))
    k0 = pl.multiple_of(key_row0 * GRID_W, 4 * GRID_W)
    for hh in range(NA_HEADS_PER_STEP):
        q = q_ref[0, hh]
        kw = k_ref[0, hh, pl.ds(k0, NA_K_TOKENS), :]
        vw = v_ref[0, hh, pl.ds(k0, NA_K_TOKENS), :]
        kc = k_ref[0, hh, SEQ:, :]
        vc = v_ref[0, hh, SEQ:, :]
        sw = _dot_nt(q, kw) + bias_ref[hh, 0]
        sc = _dot_nt(q, kc)
        m = jnp.maximum(jnp.max(sw, -1, keepdims=True), jnp.max(sc, -1, keepdims=True))
        pw = jnp.exp(sw - m)
        pc = jnp.exp(sc - m)
        l = jnp.sum(pw, -1, keepdims=True) + jnp.sum(pc, -1, keepdims=True)
        o = (_dot(pw.astype(BF16), vw) + _dot(pc.astype(BF16), vc)) / l
        o_ref[0, :, hh * HEAD_DIM:(hh + 1) * HEAD_DIM] = o.astype(BF16)


def _na(qb, kb, vb, bias):
    bsz, _, s, _ = qb.shape
    hp = NA_HEADS_PER_STEP
    kv = pl.BlockSpec((1, hp, s, HEAD_DIM), lambda b, h, j: (b, h, 0, 0))
    return pl.pallas_call(
        _na_body,
        grid=(bsz, B_HEADS // hp, NA_Q_BLOCKS),
        in_specs=[pl.BlockSpec((1, hp, NA_Q_TOKENS, HEAD_DIM), lambda b, h, j: (b, h, j, 0)), kv, kv,
                  pl.BlockSpec((hp, 1, NA_Q_TOKENS, NA_K_TOKENS), lambda b, h, j: (h, (j + 1) // 2, 0, 0))],
        out_specs=pl.BlockSpec((1, NA_Q_TOKENS, hp * HEAD_DIM), lambda b, h, j: (b, j, h)),
        out_shape=jax.ShapeDtypeStruct((bsz, SEQ, B_W), BF16),
        compiler_params=_params("parallel", "parallel", "arbitrary"),
        name="nbr_attn",
    )(qb, kb, vb, bias)


def _na_bias_table(rpb):
    col = np.arange(GRID_W)
    col_start = np.clip(col - WIN_COLS // 2, 0, GRID_W - WIN_COLS)
    in_cols = (col[None, :] >= col_start[:, None]) & (col[None, :] < col_start[:, None] + WIN_COLS)
    dc = np.clip(col[None, :] - col[:, None] + WIN_COLS - 1, 0, 2 * WIN_COLS - 2)
    blocks = np.array(NA_PATTERN_BLOCKS)
    qrow = blocks[:, None] * NA_Q_ROWS + np.arange(NA_Q_ROWS)[None, :]
    krow = np.array(NA_KEY_ROW0)[blocks][:, None] + np.arange(NA_K_ROWS)[None, :]
    rs = np.clip(qrow - WIN_ROWS // 2, 0, GRID_ROWS - WIN_ROWS)
    in_rows = (krow[:, None, :] >= rs[:, :, None]) & (krow[:, None, :] < rs[:, :, None] + WIN_ROWS)
    dr = np.clip(krow[:, None, :] - qrow[:, :, None] + WIN_ROWS - 1, 0, 2 * WIN_ROWS - 2)
    t = rpb.astype(F32)[:, dr]
    t = t[..., dc]
    valid = in_rows[None, :, :, :, None, None] & in_cols[None, None, None, None]
    t = jnp.where(valid, t, NEG_BIG)
    t = jnp.transpose(t, (0, 1, 2, 4, 3, 5))
    return t.reshape(B_HEADS, len(NA_PATTERN_BLOCKS), NA_Q_TOKENS, NA_K_TOKENS)


def _ctx_attn_body(q_ref, k_ref, v_ref, o_ref):
    for hh in range(NA_HEADS_PER_STEP):
        s = _dot_nt(q_ref[0, hh], k_ref[0, hh])
        m = jnp.max(s, -1, keepdims=True)
        p = jnp.exp(s - m)
        l = jnp.sum(p, -1, keepdims=True)
        o = _dot(p.astype(BF16), v_ref[0, hh]) / l
        o_ref[0, :, hh * HEAD_DIM:(hh + 1) * HEAD_DIM] = o.astype(BF16)


def _ctx_attn(qb, kb, vb):
    bsz = qb.shape[0]
    hp = NA_HEADS_PER_STEP
    seg = pl.BlockSpec((1, hp, CTX_LEN, HEAD_DIM), lambda b, h: (b, h, SEQ // CTX_LEN, 0))
    return pl.pallas_call(
        _ctx_attn_body,
        grid=(bsz, B_HEADS // hp),
        in_specs=[seg, seg, seg],
        out_specs=pl.BlockSpec((1, CTX_LEN, hp * HEAD_DIM), lambda b, h: (b, 0, h)),
        out_shape=jax.ShapeDtypeStruct((bsz, CTX_LEN, B_W), BF16),
        compiler_params=_params("parallel", "parallel"),
        name="ctx_attn",
    )(qb, kb, vb)


DFT_FINE = 16


def _dft_tables(n):
    s = jnp.arange(n, dtype=jnp.int32)[None, :]

    def cos_sin(f):
        ang = ((f * s) % (2 * n)).astype(F32) * (math.pi / n)
        return jnp.cos(ang), jnp.sin(ang)

    ch, sh = cos_sin(jnp.arange(n // DFT_FINE, dtype=jnp.int32)[:, None] * DFT_FINE)
    cl, sl = cos_sin(jnp.arange(DFT_FINE, dtype=jnp.int32)[:, None])
    fc = (ch[:, None] * cl[None] - sh[:, None] * sl[None]).reshape(n, n)
    sym = (sh[:, None] * cl[None] + ch[:, None] * sl[None]).reshape(n, n)
    alt = jnp.where(jnp.arange(n) % 2 == 0, 1.0, -1.0).astype(F32)
    first = jnp.arange(n) == 0
    fs = jnp.where(first[:, None], alt[None, :], sym)
    fst = jnp.where(first[None, :], alt[:, None], sym)
    return fc.astype(BF16), fs.astype(BF16), fst.astype(BF16)


def _split_bf16(v):
    hi = v.astype(BF16)
    lo = (v - hi.astype(F32)).astype(BF16)
    return hi, lo


def _spectrum_body(fc_ref, fs_ref, kp_ref, km_ref, kc_ref, ks_ref, *, tf):
    kp_hi, kp_lo = _split_bf16(kp_ref[...])
    km_hi, km_lo = _split_bf16(km_ref[...])
    fc = fc_ref[...]
    fs = fs_ref[...]
    kc_ref[...] = _dot(fc, kp_hi) + _dot(fc, kp_lo)
    ks_sin = _dot(fs, km_hi) + _dot(fs, km_lo)
    ks_nyq = _dot(fs, kp_hi) + _dot(fs, kp_lo)
    row = lax.broadcasted_iota(jnp.int32, ks_sin.shape, 0) + pl.program_id(0) * tf
    ks_ref[...] = jnp.where(row == 0, ks_nyq, ks_sin)


def _filter_spectrum(fc, fs, kplus, kminus):
    n = fc.shape[0]
    tf = min(FREQ_TILE, n)
    tab = pl.BlockSpec((tf, n), lambda j: (j, 0))
    filt = pl.BlockSpec((n, C_CH), lambda j: (0, 0))
    out = pl.BlockSpec((tf, C_CH), lambda j: (j, 0))
    return pl.pallas_call(
        functools.partial(_spectrum_body, tf=tf),
        grid=(n // tf,),
        in_specs=[tab, tab, filt, filt],
        out_specs=[out, out],
        out_shape=[jax.ShapeDtypeStruct((n, C_CH), F32)] * 2,
        compiler_params=_params("parallel"),
        name="hyena_spectrum",
    )(fc, fs, kplus, kminus)


def _hyena_body(z_ref, cw_ref, cb_ref, skip_ref, fc_ref, fs_ref, fct_ref, fst_ref, kc_ref, ks_ref,
                o_ref, ub_ref, uf_ref, x0_ref, y_ref, *, n, tf):
    j = pl.program_id(1)

    @pl.when(j == 0)
    def _():
        z = z_ref[0]
        t = lax.broadcasted_iota(jnp.int32, z.shape, 0)
        prev = jnp.where(t == 0, 0.0, pltpu.roll(z, 1, 0))
        nxt = jnp.where(t == n - 1, 0.0, pltpu.roll(z, n - 1, 0))
        zc = prev * cw_ref[0:1, :] + z * cw_ref[1:2, :] + nxt * cw_ref[2:3, :] + cb_ref[...]
        u = zc[:, 2 * C_CH:] * zc[:, C_CH:2 * C_CH]
        x0_ref[...] = zc[:, :C_CH]
        uf_ref[...] = u
        ub_ref[...] = u.astype(BF16)
        y_ref[...] = jnp.zeros_like(y_ref)

    ub = ub_ref[...]
    uc = _dot(fc_ref[...], ub)
    us = _dot(fs_ref[...], ub)
    kc = kc_ref[...]
    ks = ks_ref[...]
    row = lax.broadcasted_iota(jnp.int32, uc.shape, 0) + j * tf
    first = row == 0
    wgt = jnp.where(first, 1.0, 2.0) * (1.0 / (2 * n))
    re = jnp.where(first, uc * kc, uc * kc - us * ks) * wgt
    im = jnp.where(first, us * ks, uc * ks + us * kc) * wgt
    y_ref[...] += _dot(fct_ref[...], re.astype(BF16)) + _dot(fst_ref[...], im.astype(BF16))

    @pl.when(j == pl.num_programs(1) - 1)
    def _():
        o_ref[0] = ((y_ref[...] + uf_ref[...] * skip_ref[...]) * x0_ref[...]).astype(BF16)


def _hyena(z, seg_start, n, cw, cb, skip, tables, kc, ks):
    bsz = z.shape[0]
    fc, fs, fst = tables
    tf = min(FREQ_TILE, n)
    seg = seg_start // n
    row_blk = pl.BlockSpec((tf, n), lambda b, j: (j, 0))
    col_blk = pl.BlockSpec((n, tf), lambda b, j: (0, j))
    spec_blk = pl.BlockSpec((tf, C_CH), lambda b, j: (j, 0))
    small = lambda r, c: pl.BlockSpec((r, c), lambda b, j: (0, 0))
    return pl.pallas_call(
        functools.partial(_hyena_body, n=n, tf=tf),
        grid=(bsz, n // tf),
        in_specs=[pl.BlockSpec((1, n, 3 * C_CH), lambda b, j: (b, seg, 0)),
                  small(3, 3 * C_CH), small(1, 3 * C_CH), small(1, C_CH),
                  row_blk, row_blk, col_blk, col_blk, spec_blk, spec_blk],
        out_specs=pl.BlockSpec((1, n, C_CH), lambda b, j: (b, 0, 0)),
        out_shape=jax.ShapeDtypeStruct((bsz, n, C_CH), BF16),
        scratch_shapes=[pltpu.VMEM((n, C_CH), BF16), pltpu.VMEM((n, C_CH), F32),
                        pltpu.VMEM((n, C_CH), F32), pltpu.VMEM((n, C_CH), F32)],
        compiler_params=_params("parallel", "arbitrary"),
        name="hyena_conv",
    )(z, cw, cb, skip, fc, fs, fc, fst, kc, ks)


def _implicit_filters(n, w1, b1, w2, b2, w3, freq):
    t = jnp.linspace(0.0, 1.0, n, dtype=F32)[:, None]
    w = 2 * math.pi * jnp.arange(n, dtype=F32)[:, None] / n
    f = jnp.linspace(1e-4, C_POS_BANDS - 1, C_POS_BANDS, dtype=F32)[None]
    z = jnp.concatenate([t, jnp.cos(f * w), -jnp.sin(f * w)], -1)
    h = jnp.sin(freq[0] * (jnp.dot(z, w1, precision=HIGHEST) + b1))
    h = jnp.sin(freq[1] * (jnp.dot(h, w2, precision=HIGHEST) + b2))
    h = jnp.dot(h, w3, precision=HIGHEST)
    lo = math.log(C_DECAY_TARGET) / C_SLOW_DECAY
    hi = math.log(C_DECAY_TARGET) / C_FAST_DECAY
    deltas = jnp.abs(jnp.linspace(lo, hi, C_CH, dtype=F32))
    decay = jnp.exp(-t * deltas[None])
    hf, hb = h[:, :C_CH] * decay, h[:, C_CH:] * decay
    hb = hb.at[0].set(0.0)
    return hf + hb, hf - hb


def _router_gates(logits_t, rb):
    tm = logits_t.shape[1]
    scores = _sigmoid(logits_t)
    sel = scores + rb
    sel3 = sel.reshape(N_GROUPS, GROUP_SIZE, tm)
    kk = lax.broadcasted_iota(jnp.int32, sel3.shape, 1)
    m1 = jnp.max(sel3, 1, keepdims=True)
    i1 = jnp.min(jnp.where(sel3 == m1, kk, GROUP_SIZE), 1, keepdims=True)
    m2 = jnp.max(jnp.where(kk == i1, -jnp.inf, sel3), 1, keepdims=True)
    gscore = (m1 + m2).reshape(N_GROUPS, tm)
    gid = lax.broadcasted_iota(jnp.int32, gscore.shape, 0)
    rank = jnp.zeros(gscore.shape, jnp.int32)
    for g in range(N_GROUPS):
        other = gscore[g:g + 1, :]
        ahead = (other > gscore) | ((other == gscore) & (g < gid))
        rank = rank + ahead.astype(jnp.int32)
    gsel = (rank < TOPK_GROUPS).reshape(N_GROUPS, 1, tm)
    cand = jnp.where(gsel, sel3, -jnp.inf).reshape(N_EXPERTS, tm)
    eid = lax.broadcasted_iota(jnp.int32, cand.shape, 0)
    chosen = jnp.zeros(cand.shape, jnp.bool_)
    for _ in range(TOP_K):
        m = jnp.max(cand, 0, keepdims=True)
        pick = eid == jnp.min(jnp.where(cand == m, eid, N_EXPERTS), 0, keepdims=True)
        chosen = chosen | pick
        cand = jnp.where(pick, -jnp.inf, cand)
    wsel = jnp.where(chosen, scores, 0.0)
    return wsel / jnp.sum(wsel, 0, keepdims=True) * ROUTED_SCALE


def _out_body(oa_ref, ob_ref, oc_ref, x_ref, mod_ref, wa_ref, wb_ref, wc_ref, lng_ref, lnb_ref,
              rwt_ref, rb_ref, xmid_ref, h_ref, gate_ref):
    attn = _dot(oa_ref[0], wa_ref[...]) + _dot(ob_ref[0], wb_ref[...]) + _dot(oc_ref[0], wc_ref[...])
    gate1 = mod_ref[0, 0, 2:3, :]
    shift2 = mod_ref[0, 0, 3:4, :]
    scale2 = mod_ref[0, 0, 4:5, :]
    xm = _layer_norm(DN_ALPHA * x_ref[0] + gate1 * attn) * lng_ref[...] + lnb_ref[...]
    xmid_ref[0] = xm
    h = _layer_norm(xm) * (1.0 + scale2) + shift2
    hi, lo = _split_bf16(h)
    h_ref[0] = hi
    logits_t = _dot_nt(rwt_ref[...], jnp.concatenate([hi, lo, hi], axis=1))
    gates_t = _router_gates(logits_t, rb_ref[...])
    pad = jnp.zeros((128 - N_EXPERTS, gates_t.shape[1]), F32)
    gates = jnp.concatenate([gates_t, pad], axis=0).T
    for g in range(N_GROUPS):
        gate_ref[0, g] = gates[:, g * GROUP_SIZE:(g + 1) * GROUP_SIZE]


def _out_proj(oa, ob, oc, x, mod, wa, wb, wc, lng, lnb, rwt3, rb):
    bsz, s, _ = oa.shape
    d = D_MODEL
    tm = TOKEN_TILE
    lat_tiles = SEQ // tm
    tok = lambda w: pl.BlockSpec((1, tm, w), lambda b, i: (b, i, 0))
    full = lambda r, c: pl.BlockSpec((r, c), lambda b, i: (0, 0))
    return pl.pallas_call(
        _out_body,
        grid=(bsz, s // tm),
        in_specs=[tok(A_W), tok(B_W), tok(C_CH), tok(d),
                  pl.BlockSpec((1, 1, 6, d), lambda b, i: (b, i // lat_tiles, 0, 0)),
                  full(A_W, d), full(B_W, d), full(C_CH, d), full(1, d), full(1, d),
                  full(N_EXPERTS, 3 * d), full(N_EXPERTS, 1)],
        out_specs=[tok(d), tok(d),
                   pl.BlockSpec((1, N_GROUPS, tm, GROUP_SIZE), lambda b, i: (b, 0, i, 0))],
        out_shape=[jax.ShapeDtypeStruct((bsz, s, d), F32),
                   jax.ShapeDtypeStruct((bsz, s, d), BF16),
                   jax.ShapeDtypeStruct((bsz, N_GROUPS, s, GROUP_SIZE), F32)],
        compiler_params=_params("parallel", "parallel"),
        name="proj_out_router",
    )(oa, ob, oc, x, mod, wa, wb, wc, lng, lnb, rwt3, rb)


def _moe_body(h_ref, gate_ref, xmid_ref, modl_ref, modc_ref, w1_ref, w3_ref, w2_ref,
              sw1_ref, sw3_ref, sw2_ref, lng_ref, lnb_ref, o_ref, acc_ref, *, tm):
    g = pl.program_id(2)
    h = h_ref[0]

    @pl.when(g == 0)
    def _():
        a = _dot(h, sw1_ref[...])
        act = (a * _sigmoid(a) * _dot(h, sw3_ref[...])).astype(BF16)
        acc_ref[...] = _dot(act, sw2_ref[...])

    w1 = jnp.concatenate([w1_ref[k] for k in range(GROUP_SIZE)], axis=1)
    w3 = jnp.concatenate([w3_ref[k] for k in range(GROUP_SIZE)], axis=1)
    a = _dot(h, w1)
    act = a * _sigmoid(a) * _dot(h, w3)
    gt = gate_ref[0, 0]
    parts = [act[:, k * EXPERT_HIDDEN:(k + 1) * EXPERT_HIDDEN] * gt[:, k:k + 1] for k in range(GROUP_SIZE)]
    acc_ref[...] += _dot(jnp.concatenate(parts, axis=1).astype(BF16), w2_ref[...])

    @pl.when(g == N_GROUPS - 1)
    def _():
        row = lax.broadcasted_iota(jnp.int32, (tm, 1), 0) + pl.program_id(1) * tm
        gate2 = jnp.where(row >= SEQ, modc_ref[0, 0, 5:6, :], modl_ref[0, 0, 5:6, :])
        y = DN_ALPHA * xmid_ref[0] + gate2 * acc_ref[...]
        o_ref[0] = _layer_norm(y) * lng_ref[...] + lnb_ref[...]


def _moe(h, gates, xmid, mod, w1c, w3c, w2c, sw1, sw3, sw2, lng, lnb, tm):
    bsz, s, d = h.shape
    gw = GROUP_SIZE * EXPERT_HIDDEN
    tok = pl.BlockSpec((1, tm, d), lambda b, i, g: (b, i, 0))
    full = lambda r, c: pl.BlockSpec((r, c), lambda b, i, g: (0, 0))
    return pl.pallas_call(
        functools.partial(_moe_body, tm=tm),
        grid=(bsz, s // tm, N_GROUPS),
        in_specs=[tok,
                  pl.BlockSpec((1, 1, tm, GROUP_SIZE), lambda b, i, g: (b, g, i, 0)),
                  tok,
                  pl.BlockSpec((1, 1, 6, d), lambda b, i, g: (b, 0, 0, 0)),
                  pl.BlockSpec((1, 1, 6, d), lambda b, i, g: (b, 1, 0, 0)),
                  pl.BlockSpec((GROUP_SIZE, d, EXPERT_HIDDEN), lambda b, i, g: (g, 0, 0)),
                  pl.BlockSpec((GROUP_SIZE, d, EXPERT_HIDDEN), lambda b, i, g: (g, 0, 0)),
                  pl.BlockSpec((gw, d), lambda b, i, g: (g, 0)),
                  full(d, SHARED_HIDDEN), full(d, SHARED_HIDDEN), full(SHARED_HIDDEN, d),
                  full(1, d), full(1, d)],
        out_specs=tok,
        out_shape=jax.ShapeDtypeStruct((bsz, s, d), F32),
        scratch_shapes=[pltpu.VMEM((tm, d), F32)],
        compiler_params=_params("parallel", "parallel", "arbitrary"),
        name="moe_ffn",
    )(h, gates, xmid, mod, mod, w1c, w3c, w2c, sw1, sw3, sw2, lng, lnb)


def _rope_tables():
    t = jnp.arange(SEQ)
    row = (t // GRID_W).astype(F32)
    col = (t % GRID_W).astype(F32)
    inv = ROPE_THETA ** (-jnp.arange(0, HALF, 2, dtype=F32) / HALF)
    ang = jnp.concatenate([row[:, None] * inv, col[:, None] * inv], -1)
    cos, sin = jnp.cos(ang), jnp.sin(ang)
    cos_h = jnp.concatenate([cos, cos], -1)
    sin_h = jnp.concatenate([-sin, sin], -1)
    n_heads = QK_W // HEAD_DIM
    cos_t = jnp.concatenate([jnp.tile(cos_h, (1, n_heads)), jnp.ones((CTX_LEN, QK_W), F32)], 0)
    sin_t = jnp.concatenate([jnp.tile(sin_h, (1, n_heads)), jnp.zeros((CTX_LEN, QK_W), F32)], 0)
    return cos_t, sin_t


def _modulation(c, c_ctx, w, b):
    cvec = jnp.concatenate([c, c_ctx[None]], axis=0)
    m = (jnp.dot(jax.nn.silu(cvec), w, precision=HIGHEST) + b).reshape(-1, 6, D_MODEL)
    return jnp.stack([m[:-1], jnp.broadcast_to(m[-1], m[:-1].shape)], axis=1)


def kernel(x, c, ctx, c_ctx, ada_w, ada_b, w_in, q_gain, k_gain, nat_rpb, hy_conv_w, hy_conv_b, hy_f_w1, hy_f_b1, hy_f_w2, hy_f_b2, hy_f_w3, hy_freq, hy_skip, w_out, ln1_g, ln1_b, router_w, router_b, exp_w1, exp_w3, exp_w2, sh_w1, sh_w3, sh_w2, ln2_g, ln2_b):
    bsz = x.shape[0]
    d = D_MODEL
    xs = jnp.concatenate([x, ctx], axis=1)

    head_perm = np.concatenate([np.arange(0, HEAD_DIM, 2), np.arange(1, HEAD_DIM, 2)])
    cos_t, sin_t = _rope_tables()
    bd = jnp.asarray(np.kron(np.eye(QK_W // HEAD_DIM), np.ones((HEAD_DIM, HEAD_DIM))), BF16)
    tables_lat = _dft_tables(SEQ)
    tables_ctx = _dft_tables(CTX_LEN)
    w1_bf, w3_bf, w2_bf = exp_w1.astype(BF16), exp_w3.astype(BF16), exp_w2.astype(BF16)

    for l in range(DEPTH):
        last = l == DEPTH - 1
        mod = _modulation(c, c_ctx, ada_w[l], ada_b[l])

        w_qk = w_in[l][:, :QK_W].reshape(d, QK_W // HEAD_DIM, HALF, 2)
        w_qk = jnp.swapaxes(w_qk, 2, 3).reshape(d, QK_W)
        w_bf = jnp.concatenate([w_qk, w_in[l][:, QK_W:]], axis=1).astype(BF16)
        gain = jnp.concatenate([jnp.tile(q_gain[l][head_perm] * ATT_SCALE, A_HEADS),
                                jnp.tile(k_gain[l][head_perm], A_KV_HEADS)])[None].astype(F32)
        qa, ka, va, qb, kb, vb, z = _project(xs, mod, w_bf, bd, gain, cos_t, sin_t)

        s_out = SEQ if last else S_ALL
        oa = _gqa(qa, ka, va, s_out)
        ob = _na(qb, kb, vb, _na_bias_table(nat_rpb[l]))
        if not last:
            ob = jnp.concatenate([ob, _ctx_attn(qb, kb, vb)], axis=1)

        filt = (hy_f_w1[l], hy_f_b1[l], hy_f_w2[l], hy_f_b2[l], hy_f_w3[l], hy_freq[l])
        cw, cb, skip = hy_conv_w[l], hy_conv_b[l][None], hy_skip[l][None]
        kc, ks = _filter_spectrum(tables_lat[0], tables_lat[1], *_implicit_filters(SEQ, *filt))
        oc = _hyena(z, 0, SEQ, cw, cb, skip, tables_lat, kc, ks)
        if not last:
            kc, ks = _filter_spectrum(tables_ctx[0], tables_ctx[1], *_implicit_filters(CTX_LEN, *filt))
            oc = jnp.concatenate([oc, _hyena(z, SEQ, CTX_LEN, cw, cb, skip, tables_ctx, kc, ks)], axis=1)

        wo = w_out[l].astype(BF16)
        rw_hi, rw_lo = _split_bf16(router_w[l].T)
        rwt3 = jnp.concatenate([rw_hi, rw_hi, rw_lo], axis=1)
        xmid, h, gates = _out_proj(oa, ob, oc, xs, mod, wo[:A_W], wo[A_W:A_W + B_W], wo[A_W + B_W:],
                                   ln1_g[l][None], ln1_b[l][None], rwt3, router_b[l][:, None])

        w2c = w2_bf[l].reshape(N_EXPERTS * EXPERT_HIDDEN, d)
        xs = _moe(h, gates, xmid, mod, w1_bf[l], w3_bf[l], w2c,
                  sh_w1[l].astype(BF16), sh_w3[l].astype(BF16), sh_w2[l].astype(BF16),
                  ln2_g[l][None], ln2_b[l][None], tm=512 if last else 768)
    return xs
```

```python
import functools
import math

import numpy as np
import jax
import jax.numpy as jnp
from jax import lax
from jax.experimental import pallas as pl
from jax.experimental.pallas import tpu as pltpu

F32 = jnp.float32
BF16 = jnp.bfloat16
HIGHEST = lax.Precision.HIGHEST

D_MODEL = 1024
SEQ = 2048
DEPTH = 2
GRID_W = 64
GRID_ROWS = SEQ // GRID_W
CTX_LEN = 256
S_ALL = SEQ + CTX_LEN
HEAD_DIM = 64
HALF = HEAD_DIM // 2
A_HEADS = 8
A_KV_HEADS = 2
A_REP = A_HEADS // A_KV_HEADS
ROPE_THETA = 10000.0
B_HEADS = 4
WIN_ROWS = 8
WIN_COLS = 16
C_CH = 256
C_POS_BANDS = 16
C_DECAY_TARGET = 1e-2
C_FAST_DECAY = 0.3
C_SLOW_DECAY = 1.5
A_W = A_HEADS * HEAD_DIM
A_KV_W = A_KV_HEADS * HEAD_DIM
B_W = B_HEADS * HEAD_DIM
A_K0 = A_W
A_V0 = A_K0 + A_KV_W
B_Q0 = A_V0 + A_KV_W
B_K0 = B_Q0 + B_W
B_V0 = B_K0 + B_W
C0 = B_V0 + B_W
PROJ_W = C0 + 3 * C_CH
QK_W = A_W + A_KV_W
N_EXPERTS = 64
TOP_K = 8
N_GROUPS = 8
GROUP_SIZE = N_EXPERTS // N_GROUPS
TOPK_GROUPS = 4
EXPERT_HIDDEN = 128
SHARED_HIDDEN = 256
ROUTED_SCALE = 2.5
DN_ALPHA = (2 * DEPTH) ** 0.25
LN_EPS = 1e-6
RMS_EPS = 1e-6
ATT_SCALE = HEAD_DIM ** -0.5
NEG_BIG = -1e30

VMEM_LIMIT_BYTES = 56 * 1024 * 1024
TOKEN_TILE = 256
ROPE_TABLE_W = 128
GQA_Q_TILE = 256
GQA_CHAINS = 2
FREQ_TILE = 256


def _params(*sem):
    return pltpu.CompilerParams(dimension_semantics=sem, vmem_limit_bytes=VMEM_LIMIT_BYTES)


def _layer_norm(v):
    mu = jnp.mean(v, -1, keepdims=True)
    vc = v - mu
    var = jnp.mean(vc * vc, -1, keepdims=True)
    return vc * lax.rsqrt(var + LN_EPS)


def _sigmoid(v):
    return 1.0 / (1.0 + jnp.exp(-v))


def _dot(a, b):
    return jnp.dot(a, b, preferred_element_type=F32)


def _dot_nt(a, b):
    return lax.dot_general(a, b, (((1,), (1,)), ((), ())), preferred_element_type=F32)


def _mod_row(mod_ref, is_ctx, idx):
    return jnp.where(is_ctx, mod_ref[0, 1, idx:idx + 1, :], mod_ref[0, 0, idx:idx + 1, :])


def _sub_tiles(tm):
    for k in range(tm // TOKEN_TILE):
        rows = slice(k * TOKEN_TILE, (k + 1) * TOKEN_TILE)
        is_ctx = (pl.program_id(1) * tm + k * TOKEN_TILE) >= SEQ
        yield rows, is_ctx


def _proj_body(x_ref, mod_ref, w_ref, bd_ref, gain_ref, cos_ref, sin_ref,
               qa_ref, ka_ref, va_ref, qb_ref, kb_ref, vb_ref, z_ref, *, tm):
    n_rep = QK_W // ROPE_TABLE_W
    for rows, is_ctx in _sub_tiles(tm):
        xn = _layer_norm(x_ref[0, rows, :])
        hx = (xn * (1.0 + _mod_row(mod_ref, is_ctx, 1)) + _mod_row(mod_ref, is_ctx, 0)).astype(BF16)
        px = _dot(hx, w_ref[...])
        qk = px[:, :QK_W]
        ss = _dot((qk * qk).astype(BF16), bd_ref[...])
        qkn = qk * lax.rsqrt(ss * (1.0 / HEAD_DIM) + RMS_EPS) * gain_ref[...]
        lane = lax.broadcasted_iota(jnp.int32, qkn.shape, 1)
        partner = jnp.where((lane % HEAD_DIM) < HALF,
                            pltpu.roll(qkn, QK_W - HALF, 1), pltpu.roll(qkn, HALF, 1))
        cos = jnp.concatenate([cos_ref[rows, :]] * n_rep, axis=1)
        sin = jnp.concatenate([sin_ref[rows, :]] * n_rep, axis=1)
        qkr = (qkn * cos + partner * sin).astype(BF16)
        for h in range(A_HEADS):
            qa_ref[0, h, rows, :] = qkr[:, h * HEAD_DIM:(h + 1) * HEAD_DIM]
        for g in range(A_KV_HEADS):
            ka_ref[0, g, rows, :] = qkr[:, A_K0 + g * HEAD_DIM:A_K0 + (g + 1) * HEAD_DIM]
            va_ref[0, g, rows, :] = px[:, A_V0 + g * HEAD_DIM:A_V0 + (g + 1) * HEAD_DIM].astype(BF16)
        for h in range(B_HEADS):
            qb_ref[0, h, rows, :] = (px[:, B_Q0 + h * HEAD_DIM:B_Q0 + (h + 1) * HEAD_DIM] * ATT_SCALE).astype(BF16)
            kb_ref[0, h, rows, :] = px[:, B_K0 + h * HEAD_DIM:B_K0 + (h + 1) * HEAD_DIM].astype(BF16)
            vb_ref[0, h, rows, :] = px[:, B_V0 + h * HEAD_DIM:B_V0 + (h + 1) * HEAD_DIM].astype(BF16)
        z_ref[0, rows, :] = px[:, C0:]


def _step_tokens(s):
    return 3 * TOKEN_TILE if s % (3 * TOKEN_TILE) == 0 else 2 * TOKEN_TILE


def _project(x, mod, w_bf, bd, gain, cos_t, sin_t):
    bsz, s, d = x.shape
    tm = _step_tokens(s)
    nt = s // tm
    heads = lambda n: pl.BlockSpec((1, n, tm, HEAD_DIM), lambda b, i: (b, 0, i, 0))
    hshape = lambda n: jax.ShapeDtypeStruct((bsz, n, s, HEAD_DIM), BF16)
    return pl.pallas_call(
        functools.partial(_proj_body, tm=tm),
        grid=(bsz, nt),
        in_specs=[
            pl.BlockSpec((1, tm, d), lambda b, i: (b, i, 0)),
            pl.BlockSpec((1, 2, 6, d), lambda b, i: (b, 0, 0, 0)),
            pl.BlockSpec((d, PROJ_W), lambda b, i: (0, 0)),
            pl.BlockSpec((QK_W, QK_W), lambda b, i: (0, 0)),
            pl.BlockSpec((1, QK_W), lambda b, i: (0, 0)),
            pl.BlockSpec((tm, ROPE_TABLE_W), lambda b, i: (i, 0)),
            pl.BlockSpec((tm, ROPE_TABLE_W), lambda b, i: (i, 0)),
        ],
        out_specs=[heads(A_HEADS), heads(A_KV_HEADS), heads(A_KV_HEADS),
                   heads(B_HEADS), heads(B_HEADS), heads(B_HEADS),
                   pl.BlockSpec((1, tm, 3 * C_CH), lambda b, i: (b, i, 0))],
        out_shape=[hshape(A_HEADS), hshape(A_KV_HEADS), hshape(A_KV_HEADS),
                   hshape(B_HEADS), hshape(B_HEADS), hshape(B_HEADS),
                   jax.ShapeDtypeStruct((bsz, s, 3 * C_CH), F32)],
        compiler_params=_params("parallel", "parallel"),
        name="proj_in",
    )(x, mod, w_bf, bd, gain, cos_t, sin_t)


def _gqa_body(q_ref, k_ref, v_ref, o_ref, *, tq, lat_tiles, with_ctx_queries):
    i = pl.program_id(2)

    def attend(k, v):
        for c in range(GQA_CHAINS):
            hc = A_REP // GQA_CHAINS
            q = q_ref[0, c * hc:(c + 1) * hc].reshape(hc * tq, HEAD_DIM)
            s = _dot_nt(q, k)
            m = jnp.max(s, -1, keepdims=True)
            p = jnp.exp(s - m)
            l = jnp.sum(p, -1, keepdims=True)
            o = _dot(p.astype(BF16), v) / l
            for h in range(hc):
                hh = c * hc + h
                o_ref[0, :, hh * HEAD_DIM:(hh + 1) * HEAD_DIM] = o[h * tq:(h + 1) * tq].astype(BF16)

    @pl.when(i < lat_tiles)
    def _():
        attend(k_ref[0, 0], v_ref[0, 0])

    if with_ctx_queries:
        @pl.when(i >= lat_tiles)
        def _():
            attend(k_ref[0, 0, SEQ:, :], v_ref[0, 0, SEQ:, :])


def _gqa(qa, ka, va, s_out):
    bsz, _, s, _ = qa.shape
    tq = GQA_Q_TILE
    body = functools.partial(_gqa_body, tq=tq, lat_tiles=SEQ // tq, with_ctx_queries=s_out > SEQ)
    return pl.pallas_call(
        body,
        grid=(bsz, A_KV_HEADS, s_out // tq),
        in_specs=[
            pl.BlockSpec((1, A_REP, tq, HEAD_DIM), lambda b, g, i: (b, g, i, 0)),
            pl.BlockSpec((1, 1, s, HEAD_DIM), lambda b, g, i: (b, g, 0, 0)),
            pl.BlockSpec((1, 1, s, HEAD_DIM), lambda b, g, i: (b, g, 0, 0)),
        ],
        out_specs=pl.BlockSpec((1, tq, A_REP * HEAD_DIM), lambda b, g, i: (b, i, g)),
        out_shape=jax.ShapeDtypeStruct((bsz, s_out, A_W), BF16),
        compiler_params=_params("parallel", "parallel", "parallel"),
        name="gqa_attn",
    )(qa, ka, va)


NA_HEADS_PER_STEP = 2
NA_Q_ROWS = 8
NA_K_ROWS = 16
NA_Q_TOKENS = NA_Q_ROWS * GRID_W
NA_K_TOKENS = NA_K_ROWS * GRID_W
NA_Q_BLOCKS = GRID_ROWS // NA_Q_ROWS
NA_KEY_ROW0 = (0, 4, 12, 16)
NA_PATTERN_BLOCKS = (0, 1, 3)


def _na_body(q_ref, k_ref, v_ref, bias_ref, o_ref):
    j = pl.program_id(2)
    key_row0 = jnp.where(j == 0, NA_KEY_ROW0[0],
                         jnp.where(j == 1, NA_KEY_ROW0[1], jnp.where(j == 2, NA_KEY_ROW0[2], NA_KEY_ROW0[3])))
    k0 = pl.multiple_of(key_row0 * GRID_W, 4 * GRID_W)
    for hh in range(NA_HEADS_PER_STEP):
        q = q_ref[0, hh]
        kw = k_ref[0, hh, pl.ds(k0, NA_K_TOKENS), :]
        vw = v_ref[0, hh, pl.ds(k0, NA_K_TOKENS), :]
        kc = k_ref[0, hh, SEQ:, :]
        vc = v_ref[0, hh, SEQ:, :]
        sw = _dot_nt(q, kw) + bias_ref[hh, 0]
        sc = _dot_nt(q, kc)
        m = jnp.maximum(jnp.max(sw, -1, keepdims=True), jnp.max(sc, -1, keepdims=True))
        pw = jnp.exp(sw - m)
        pc = jnp.exp(sc - m)
        l = jnp.sum(pw, -1, keepdims=True) + jnp.sum(pc, -1, keepdims=True)
        o = (_dot(pw.astype(BF16), vw) + _dot(pc.astype(BF16), vc)) / l
        o_ref[0, :, hh * HEAD_DIM:(hh + 1) * HEAD_DIM] = o.astype(BF16)


def _na(qb, kb, vb, bias):
    bsz, _, s, _ = qb.shape
    hp = NA_HEADS_PER_STEP
    kv = pl.BlockSpec((1, hp, s, HEAD_DIM), lambda b, h, j: (b, h, 0, 0))
    return pl.pallas_call(
        _na_body,
        grid=(bsz, B_HEADS // hp, NA_Q_BLOCKS),
        in_specs=[pl.BlockSpec((1, hp, NA_Q_TOKENS, HEAD_DIM), lambda b, h, j: (b, h, j, 0)), kv, kv,
                  pl.BlockSpec((hp, 1, NA_Q_TOKENS, NA_K_TOKENS), lambda b, h, j: (h, (j + 1) // 2, 0, 0))],
        out_specs=pl.BlockSpec((1, NA_Q_TOKENS, hp * HEAD_DIM), lambda b, h, j: (b, j, h)),
        out_shape=jax.ShapeDtypeStruct((bsz, SEQ, B_W), BF16),
        compiler_params=_params("parallel", "parallel", "arbitrary"),
        name="nbr_attn",
    )(qb, kb, vb, bias)


def _na_bias_body(t_ref, o_ref):
    o_ref[...] = jnp.full(o_ref.shape, NEG_BIG, F32)
    for p, blk in enumerate(NA_PATTERN_BLOCKS):
        @pl.when(pl.program_id(1) == p)
        def _(blk=blk):
            for qi in range(NA_Q_ROWS):
                qrow = blk * NA_Q_ROWS + qi
                rs = min(max(qrow - WIN_ROWS // 2, 0), GRID_ROWS - WIN_ROWS)
                for krow in range(rs, rs + WIN_ROWS):
                    ki = krow - NA_KEY_ROW0[blk]
                    o_ref[0, 0, qi * GRID_W:(qi + 1) * GRID_W, ki * GRID_W:(ki + 1) * GRID_W] = (
                        t_ref[0, krow - qrow + WIN_ROWS - 1])


def _na_bias_table(rpb):
    col = np.arange(GRID_W)
    col_start = np.clip(col - WIN_COLS // 2, 0, GRID_W - WIN_COLS)
    in_cols = (col[None, :] >= col_start[:, None]) & (col[None, :] < col_start[:, None] + WIN_COLS)
    dc = np.clip(col[None, :] - col[:, None] + WIN_COLS - 1, 0, 2 * WIN_COLS - 2)
    tiles = jnp.where(in_cols[None, None], rpb.astype(F32)[:, :, dc], NEG_BIG)
    n_pat = len(NA_PATTERN_BLOCKS)
    return pl.pallas_call(
        _na_bias_body,
        grid=(B_HEADS, n_pat),
        in_specs=[pl.BlockSpec((1, 2 * WIN_ROWS - 1, GRID_W, GRID_W), lambda h, p: (h, 0, 0, 0))],
        out_specs=pl.BlockSpec((1, 1, NA_Q_TOKENS, NA_K_TOKENS), lambda h, p: (h, p, 0, 0)),
        out_shape=jax.ShapeDtypeStruct((B_HEADS, n_pat, NA_Q_TOKENS, NA_K_TOKENS), F32),
        compiler_params=_params("parallel", "arbitrary"),
        name="nbr_bias",
    )(tiles)


def _ctx_attn_body(q_ref, k_ref, v_ref, o_ref):
    for hh in range(NA_HEADS_PER_STEP):
        s = _dot_nt(q_ref[0, hh], k_ref[0, hh])
        m = jnp.max(s, -1, keepdims=True)
        p = jnp.exp(s - m)
        l = jnp.sum(p, -1, keepdims=True)
        o = _dot(p.astype(BF16), v_ref[0, hh]) / l
        o_ref[0, :, hh * HEAD_DIM:(hh + 1) * HEAD_DIM] = o.astype(BF16)


def _ctx_attn(qb, kb, vb):
    bsz = qb.shape[0]
    hp = NA_HEADS_PER_STEP
    seg = pl.BlockSpec((1, hp, CTX_LEN, HEAD_DIM), lambda b, h: (b, h, SEQ // CTX_LEN, 0))
    return pl.pallas_call(
        _ctx_attn_body,
        grid=(bsz, B_HEADS // hp),
        in_specs=[seg, seg, seg],
        out_specs=pl.BlockSpec((1, CTX_LEN, hp * HEAD_DIM), lambda b, h: (b, 0, h)),
        out_shape=jax.ShapeDtypeStruct((bsz, CTX_LEN, B_W), BF16),
        compiler_params=_params("parallel", "parallel"),
        name="ctx_attn",
    )(qb, kb, vb)


DFT_FINE = 16


def _dft_tables(n):
    s = jnp.arange(n, dtype=jnp.int32)[None, :]

    def cos_sin(f):
        ang = ((f * s) % (2 * n)).astype(F32) * (math.pi / n)
        return jnp.cos(ang), jnp.sin(ang)

    ch, sh = cos_sin(jnp.arange(n // DFT_FINE, dtype=jnp.int32)[:, None] * DFT_FINE)
    cl, sl = cos_sin(jnp.arange(DFT_FINE, dtype=jnp.int32)[:, None])
    fc = (ch[:, None] * cl[None] - sh[:, None] * sl[None]).reshape(n, n)
    sym = (sh[:, None] * cl[None] + ch[:, None] * sl[None]).reshape(n, n)
    alt = jnp.where(jnp.arange(n) % 2 == 0, 1.0, -1.0).astype(F32)
    first = jnp.arange(n) == 0
    fs = jnp.where(first[:, None], alt[None, :], sym)
    fst = jnp.where(first[None, :], alt[:, None], sym)
    return fc.astype(BF16), fs.astype(BF16), fst.astype(BF16)


def _split_bf16(v):
    hi = v.astype(BF16)
    lo = (v - hi.astype(F32)).astype(BF16)
    return hi, lo


def _spectrum_body(fc_ref, fs_ref, kp_ref, km_ref, kc_ref, ks_ref, *, tf):
    kp_hi, kp_lo = _split_bf16(kp_ref[...])
    km_hi, km_lo = _split_bf16(km_ref[...])
    fc = fc_ref[...]
    fs = fs_ref[...]
    kc_ref[...] = _dot(fc, kp_hi) + _dot(fc, kp_lo)
    ks_sin = _dot(fs, km_hi) + _dot(fs, km_lo)
    ks_nyq = _dot(fs, kp_hi) + _dot(fs, kp_lo)
    row = lax.broadcasted_iota(jnp.int32, ks_sin.shape, 0) + pl.program_id(0) * tf
    ks_ref[...] = jnp.where(row == 0, ks_nyq, ks_sin)


def _filter_spectrum(fc, fs, kplus, kminus):
    n = fc.shape[0]
    tf = min(FREQ_TILE, n)
    tab = pl.BlockSpec((tf, n), lambda j: (j, 0))
    filt = pl.BlockSpec((n, C_CH), lambda j: (0, 0))
    out = pl.BlockSpec((tf, C_CH), lambda j: (j, 0))
    return pl.pallas_call(
        functools.partial(_spectrum_body, tf=tf),
        grid=(n // tf,),
        in_specs=[tab, tab, filt, filt],
        out_specs=[out, out],
        out_shape=[jax.ShapeDtypeStruct((n, C_CH), F32)] * 2,
        compiler_params=_params("parallel"),
        name="hyena_spectrum",
    )(fc, fs, kplus, kminus)


def _hyena_body(z_ref, cw_ref, cb_ref, skip_ref, fc_ref, fs_ref, fct_ref, fst_ref, kc_ref, ks_ref,
                o_ref, ub_ref, uf_ref, x0_ref, y_ref, *, n, tf):
    j = pl.program_id(1)

    @pl.when(j == 0)
    def _():
        z = z_ref[0]
        t = lax.broadcasted_iota(jnp.int32, z.shape, 0)
        prev = jnp.where(t == 0, 0.0, pltpu.roll(z, 1, 0))
        nxt = jnp.where(t == n - 1, 0.0, pltpu.roll(z, n - 1, 0))
        zc = prev * cw_ref[0:1, :] + z * cw_ref[1:2, :] + nxt * cw_ref[2:3, :] + cb_ref[...]
        u = zc[:, 2 * C_CH:] * zc[:, C_CH:2 * C_CH]
        x0_ref[...] = zc[:, :C_CH]
        uf_ref[...] = u
        ub_ref[...] = u.astype(BF16)
        y_ref[...] = jnp.zeros_like(y_ref)

    ub = ub_ref[...]
    uc = _dot(fc_ref[...], ub)
    us = _dot(fs_ref[...], ub)
    kc = kc_ref[...]
    ks = ks_ref[...]
    row = lax.broadcasted_iota(jnp.int32, uc.shape, 0) + j * tf
    first = row == 0
    wgt = jnp.where(first, 1.0, 2.0) * (1.0 / (2 * n))
    re = jnp.where(first, uc * kc, uc * kc - us * ks) * wgt
    im = jnp.where(first, us * ks, uc * ks + us * kc) * wgt
    y_ref[...] += _dot(fct_ref[...], re.astype(BF16)) + _dot(fst_ref[...], im.astype(BF16))

    @pl.when(j == pl.num_programs(1) - 1)
    def _():
        o_ref[0] = ((y_ref[...] + uf_ref[...] * skip_ref[...]) * x0_ref[...]).astype(BF16)


def _hyena(z, seg_start, n, cw, cb, skip, tables, kc, ks):
    bsz = z.shape[0]
    fc, fs, fst = tables
    tf = min(FREQ_TILE, n)
    seg = seg_start // n
    row_blk = pl.BlockSpec((tf, n), lambda b, j: (j, 0))
    col_blk = pl.BlockSpec((n, tf), lambda b, j: (0, j))
    spec_blk = pl.BlockSpec((tf, C_CH), lambda b, j: (j, 0))
    small = lambda r, c: pl.BlockSpec((r, c), lambda b, j: (0, 0))
    return pl.pallas_call(
        functools.partial(_hyena_body, n=n, tf=tf),
        grid=(bsz, n // tf),
        in_specs=[pl.BlockSpec((1, n, 3 * C_CH), lambda b, j: (b, seg, 0)),
                  small(3, 3 * C_CH), small(1, 3 * C_CH), small(1, C_CH),
                  row_blk, row_blk, col_blk, col_blk, spec_blk, spec_blk],
        out_specs=pl.BlockSpec((1, n, C_CH), lambda b, j: (b, 0, 0)),
        out_shape=jax.ShapeDtypeStruct((bsz, n, C_CH), BF16),
        scratch_shapes=[pltpu.VMEM((n, C_CH), BF16), pltpu.VMEM((n, C_CH), F32),
                        pltpu.VMEM((n, C_CH), F32), pltpu.VMEM((n, C_CH), F32)],
        compiler_params=_params("parallel", "arbitrary"),
        name="hyena_conv",
    )(z, cw, cb, skip, fc, fs, fc, fst, kc, ks)


def _implicit_filters(n, w1, b1, w2, b2, w3, freq):
    t = jnp.linspace(0.0, 1.0, n, dtype=F32)[:, None]
    w = 2 * math.pi * jnp.arange(n, dtype=F32)[:, None] / n
    f = jnp.linspace(1e-4, C_POS_BANDS - 1, C_POS_BANDS, dtype=F32)[None]
    z = jnp.concatenate([t, jnp.cos(f * w), -jnp.sin(f * w)], -1)
    h = jnp.sin(freq[0] * (jnp.dot(z, w1, precision=HIGHEST) + b1))
    h = jnp.sin(freq[1] * (jnp.dot(h, w2, precision=HIGHEST) + b2))
    h = jnp.dot(h, w3, precision=HIGHEST)
    lo = math.log(C_DECAY_TARGET) / C_SLOW_DECAY
    hi = math.log(C_DECAY_TARGET) / C_FAST_DECAY
    deltas = jnp.abs(jnp.linspace(lo, hi, C_CH, dtype=F32))
    decay = jnp.exp(-t * deltas[None])
    hf, hb = h[:, :C_CH] * decay, h[:, C_CH:] * decay
    hb = hb.at[0].set(0.0)
    return hf + hb, hf - hb


def _router_gates(logits_t, rb):
    tm = logits_t.shape[1]
    scores = _sigmoid(logits_t)
    sel = scores + rb
    sel3 = sel.reshape(N_GROUPS, GROUP_SIZE, tm)
    kk = lax.broadcasted_iota(jnp.int32, sel3.shape, 1)
    m1 = jnp.max(sel3, 1, keepdims=True)
    i1 = jnp.min(jnp.where(sel3 == m1, kk, GROUP_SIZE), 1, keepdims=True)
    m2 = jnp.max(jnp.where(kk == i1, -jnp.inf, sel3), 1, keepdims=True)
    gscore = (m1 + m2).reshape(N_GROUPS, tm)
    gid = lax.broadcasted_iota(jnp.int32, gscore.shape, 0)
    rank = jnp.zeros(gscore.shape, jnp.int32)
    for g in range(N_GROUPS):
        other = gscore[g:g + 1, :]
        ahead = (other > gscore) | ((other == gscore) & (g < gid))
        rank = rank + ahead.astype(jnp.int32)
    gsel = (rank < TOPK_GROUPS).reshape(N_GROUPS, 1, tm)
    cand = jnp.where(gsel, sel3, -jnp.inf).reshape(N_EXPERTS, tm)
    eid = lax.broadcasted_iota(jnp.int32, cand.shape, 0)
    chosen = jnp.zeros(cand.shape, jnp.bool_)
    for _ in range(TOP_K):
        m = jnp.max(cand, 0, keepdims=True)
        pick = eid == jnp.min(jnp.where(cand == m, eid, N_EXPERTS), 0, keepdims=True)
        chosen = chosen | pick
        cand = jnp.where(pick, -jnp.inf, cand)
    wsel = jnp.where(chosen, scores, 0.0)
    return wsel / jnp.sum(wsel, 0, keepdims=True) * ROUTED_SCALE


def _out_body(oa_ref, ob_ref, oc_ref, x_ref, mod_ref, wa_ref, wb_ref, wc_ref, lng_ref, lnb_ref,
              rwt_ref, rb_ref, xmid_ref, h_ref, gate_ref, *, tm):
    for rows, is_ctx in _sub_tiles(tm):
        attn = (_dot(oa_ref[0, rows, :], wa_ref[...]) + _dot(ob_ref[0, rows, :], wb_ref[...])
                + _dot(oc_ref[0, rows, :], wc_ref[...]))
        y = DN_ALPHA * x_ref[0, rows, :] + _mod_row(mod_ref, is_ctx, 2) * attn
        xm = _layer_norm(y) * lng_ref[...] + lnb_ref[...]
        xmid_ref[0, rows, :] = xm
        h = _layer_norm(xm) * (1.0 + _mod_row(mod_ref, is_ctx, 4)) + _mod_row(mod_ref, is_ctx, 3)
        hi, lo = _split_bf16(h)
        h_ref[0, rows, :] = hi
        logits_t = _dot_nt(rwt_ref[...], jnp.concatenate([hi, lo, hi], axis=1))
        gates_t = _router_gates(logits_t, rb_ref[...])
        pad = jnp.zeros((128 - N_EXPERTS, gates_t.shape[1]), F32)
        gates = jnp.concatenate([gates_t, pad], axis=0).T
        for g in range(N_GROUPS):
            gate_ref[0, g, rows, :] = gates[:, g * GROUP_SIZE:(g + 1) * GROUP_SIZE]


def _out_proj(oa, ob, oc, x, mod, wa, wb, wc, lng, lnb, rwt3, rb):
    bsz, s, _ = oa.shape
    d = D_MODEL
    tm = _step_tokens(s)
    tok = lambda w: pl.BlockSpec((1, tm, w), lambda b, i: (b, i, 0))
    full = lambda r, c: pl.BlockSpec((r, c), lambda b, i: (0, 0))
    return pl.pallas_call(
        functools.partial(_out_body, tm=tm),
        grid=(bsz, s // tm),
        in_specs=[tok(A_W), tok(B_W), tok(C_CH), tok(d),
                  pl.BlockSpec((1, 2, 6, d), lambda b, i: (b, 0, 0, 0)),
                  full(A_W, d), full(B_W, d), full(C_CH, d), full(1, d), full(1, d),
                  full(N_EXPERTS, 3 * d), full(N_EXPERTS, 1)],
        out_specs=[tok(d), tok(d),
                   pl.BlockSpec((1, N_GROUPS, tm, GROUP_SIZE), lambda b, i: (b, 0, i, 0))],
        out_shape=[jax.ShapeDtypeStruct((bsz, s, d), F32),
                   jax.ShapeDtypeStruct((bsz, s, d), BF16),
                   jax.ShapeDtypeStruct((bsz, N_GROUPS, s, GROUP_SIZE), F32)],
        compiler_params=_params("parallel", "parallel"),
        name="proj_out_router",
    )(oa, ob, oc, x, mod, wa, wb, wc, lng, lnb, rwt3, rb)


MOE_CHAINS = 2


def _moe_body(h_ref, gate_ref, xmid_ref, mod_ref, w1_ref, w3_ref, w2_ref,
              sw1_ref, sw3_ref, sw2_ref, lng_ref, lnb_ref, o_ref, acc_ref, *, tm):
    g = pl.program_id(2)
    rc = tm // MOE_CHAINS

    @pl.when(g == 0)
    def _():
        h = h_ref[0]
        a = _dot(h, sw1_ref[...])
        act = (a * _sigmoid(a) * _dot(h, sw3_ref[...])).astype(BF16)
        acc_ref[...] = _dot(act, sw2_ref[...])

    w1 = jnp.concatenate([w1_ref[k] for k in range(GROUP_SIZE)], axis=1)
    w3 = jnp.concatenate([w3_ref[k] for k in range(GROUP_SIZE)], axis=1)
    w2 = w2_ref[...].reshape(GROUP_SIZE * EXPERT_HIDDEN, D_MODEL)
    for c in range(MOE_CHAINS):
        rows = slice(c * rc, (c + 1) * rc)
        h = h_ref[0, rows, :]
        a = _dot(h, w1)
        act = a * _sigmoid(a) * _dot(h, w3)
        gt = gate_ref[0, 0, rows, :]
        parts = [act[:, k * EXPERT_HIDDEN:(k + 1) * EXPERT_HIDDEN] * gt[:, k:k + 1]
                 for k in range(GROUP_SIZE)]
        acc_ref[rows, :] += _dot(jnp.concatenate(parts, axis=1).astype(BF16), w2)

    @pl.when(g == N_GROUPS - 1)
    def _():
        row = lax.broadcasted_iota(jnp.int32, (tm, 1), 0) + pl.program_id(1) * tm
        gate2 = jnp.where(row >= SEQ, mod_ref[0, 1, 5:6, :], mod_ref[0, 0, 5:6, :])
        y = DN_ALPHA * xmid_ref[0] + gate2 * acc_ref[...]
        o_ref[0] = _layer_norm(y) * lng_ref[...] + lnb_ref[...]


def _moe(h, gates, xmid, mod, layer, w1_all, w3_all, w2_all, sw1, sw3, sw2, lng, lnb, tm):
    bsz, s, d = h.shape
    tok = pl.BlockSpec((1, tm, d), lambda b, i, g: (b, i, 0))
    full = lambda r, c: pl.BlockSpec((r, c), lambda b, i, g: (0, 0))
    up = pl.BlockSpec((None, GROUP_SIZE, d, EXPERT_HIDDEN), lambda b, i, g: (layer, g, 0, 0))
    return pl.pallas_call(
        functools.partial(_moe_body, tm=tm),
        grid=(bsz, s // tm, N_GROUPS),
        in_specs=[tok,
                  pl.BlockSpec((1, 1, tm, GROUP_SIZE), lambda b, i, g: (b, g, i, 0)),
                  tok,
                  pl.BlockSpec((1, 2, 6, d), lambda b, i, g: (b, 0, 0, 0)),
                  up, up,
                  pl.BlockSpec((None, GROUP_SIZE, EXPERT_HIDDEN, d), lambda b, i, g: (layer, g, 0, 0)),
                  full(d, SHARED_HIDDEN), full(d, SHARED_HIDDEN), full(SHARED_HIDDEN, d),
                  full(1, d), full(1, d)],
        out_specs=tok,
        out_shape=jax.ShapeDtypeStruct((bsz, s, d), F32),
        scratch_shapes=[pltpu.VMEM((tm, d), F32)],
        compiler_params=_params("parallel", "parallel", "arbitrary"),
        name="moe_ffn",
    )(h, gates, xmid, mod, w1_all, w3_all, w2_all, sw1, sw3, sw2, lng, lnb)


def _rope_tables():
    t = jnp.arange(SEQ)
    row = (t // GRID_W).astype(F32)
    col = (t % GRID_W).astype(F32)
    inv = ROPE_THETA ** (-jnp.arange(0, HALF, 2, dtype=F32) / HALF)
    ang = jnp.concatenate([row[:, None] * inv, col[:, None] * inv], -1)
    cos, sin = jnp.cos(ang), jnp.sin(ang)
    n_heads = ROPE_TABLE_W // HEAD_DIM
    cos_t = jnp.concatenate([jnp.tile(jnp.concatenate([cos, cos], -1), (1, n_heads)),
                             jnp.ones((CTX_LEN, ROPE_TABLE_W), F32)], 0)
    sin_t = jnp.concatenate([jnp.tile(jnp.concatenate([-sin, sin], -1), (1, n_heads)),
                             jnp.zeros((CTX_LEN, ROPE_TABLE_W), F32)], 0)
    return cos_t, sin_t


def _modulation(c, c_ctx, w, b):
    cvec = jnp.concatenate([c, c_ctx[None]], axis=0)
    m = (jnp.dot(jax.nn.silu(cvec), w, precision=HIGHEST) + b).reshape(-1, 6, D_MODEL)
    return jnp.stack([m[:-1], jnp.broadcast_to(m[-1], m[:-1].shape)], axis=1)


def kernel(x, c, ctx, c_ctx, ada_w, ada_b, w_in, q_gain, k_gain, nat_rpb, hy_conv_w, hy_conv_b, hy_f_w1, hy_f_b1, hy_f_w2, hy_f_b2, hy_f_w3, hy_freq, hy_skip, w_out, ln1_g, ln1_b, router_w, router_b, exp_w1, exp_w3, exp_w2, sh_w1, sh_w3, sh_w2, ln2_g, ln2_b):
    bsz = x.shape[0]
    d = D_MODEL
    xs = jnp.concatenate([x, ctx], axis=1)

    head_perm = np.concatenate([np.arange(0, HEAD_DIM, 2), np.arange(1, HEAD_DIM, 2)])
    qk_perm = (np.arange(QK_W // HEAD_DIM)[:, None] * HEAD_DIM + head_perm[None, :]).reshape(-1)
    perm_mat = jnp.asarray(np.eye(QK_W)[:, qk_perm], BF16)
    cos_t, sin_t = _rope_tables()
    bd = jnp.asarray(np.kron(np.eye(QK_W // HEAD_DIM), np.ones((HEAD_DIM, HEAD_DIM))), BF16)
    tables_lat = _dft_tables(SEQ)
    tables_ctx = _dft_tables(CTX_LEN)
    w1_bf, w3_bf, w2_bf = exp_w1.astype(BF16), exp_w3.astype(BF16), exp_w2.astype(BF16)

    for l in range(DEPTH):
        last = l == DEPTH - 1
        mod = _modulation(c, c_ctx, ada_w[l], ada_b[l])

        w_bf = w_in[l].astype(BF16)
        w_qk = jnp.dot(w_bf[:, :QK_W], perm_mat, preferred_element_type=F32).astype(BF16)
        w_bf = jnp.concatenate([w_qk, w_bf[:, QK_W:]], axis=1)
        gain = jnp.concatenate([jnp.tile(q_gain[l][head_perm] * ATT_SCALE, A_HEADS),
                                jnp.tile(k_gain[l][head_perm], A_KV_HEADS)])[None].astype(F32)
        qa, ka, va, qb, kb, vb, z = _project(xs, mod, w_bf, bd, gain, cos_t, sin_t)

        s_out = SEQ if last else S_ALL
        oa = _gqa(qa, ka, va, s_out)
        ob = _na(qb, kb, vb, _na_bias_table(nat_rpb[l]))
        if not last:
            ob = jnp.concatenate([ob, _ctx_attn(qb, kb, vb)], axis=1)

        filt = (hy_f_w1[l], hy_f_b1[l], hy_f_w2[l], hy_f_b2[l], hy_f_w3[l], hy_freq[l])
        cw, cb, skip = hy_conv_w[l], hy_conv_b[l][None], hy_skip[l][None]
        kc, ks = _filter_spectrum(tables_lat[0], tables_lat[1], *_implicit_filters(SEQ, *filt))
        oc = _hyena(z, 0, SEQ, cw, cb, skip, tables_lat, kc, ks)
        if not last:
            kc, ks = _filter_spectrum(tables_ctx[0], tables_ctx[1], *_implicit_filters(CTX_LEN, *filt))
            oc = jnp.concatenate([oc, _hyena(z, SEQ, CTX_LEN, cw, cb, skip, tables_ctx, kc, ks)], axis=1)

        wo = w_out[l].astype(BF16)
        rw_hi, rw_lo = _split_bf16(router_w[l].T)
        rwt3 = jnp.concatenate([rw_hi, rw_hi, rw_lo], axis=1)
        xmid, h, gates = _out_proj(oa, ob, oc, xs, mod, wo[:A_W], wo[A_W:A_W + B_W], wo[A_W + B_W:],
                                   ln1_g[l][None], ln1_b[l][None], rwt3, router_b[l][:, None])

        xs = _moe(h, gates, xmid, mod, l, w1_bf, w3_bf, w2_bf,
                  sh_w1[l].astype(BF16), sh_w3[l].astype(BF16), sh_w2[l].astype(BF16),
                  ln2_g[l][None], ln2_b[l][None], tm=512 if last else 768)
    return xs
```

```python
import functools
import math

import numpy as np
import jax
import jax.numpy as jnp
from jax import lax
from jax.experimental import pallas as pl
from jax.experimental.pallas import tpu as pltpu

F32 = jnp.float32
BF16 = jnp.bfloat16
HIGHEST = lax.Precision.HIGHEST

D_MODEL = 1024
SEQ = 2048
DEPTH = 2
GRID_W = 64
GRID_ROWS = SEQ // GRID_W
CTX_LEN = 256
S_ALL = SEQ + CTX_LEN
HEAD_DIM = 64
HALF = HEAD_DIM // 2
A_HEADS = 8
A_KV_HEADS = 2
A_REP = A_HEADS // A_KV_HEADS
ROPE_THETA = 10000.0
B_HEADS = 4
WIN_ROWS = 8
WIN_COLS = 16
C_CH = 256
C_POS_BANDS = 16
C_DECAY_TARGET = 1e-2
C_FAST_DECAY = 0.3
C_SLOW_DECAY = 1.5
A_W = A_HEADS * HEAD_DIM
A_KV_W = A_KV_HEADS * HEAD_DIM
B_W = B_HEADS * HEAD_DIM
A_K0 = A_W
A_V0 = A_K0 + A_KV_W
B_Q0 = A_V0 + A_KV_W
B_K0 = B_Q0 + B_W
B_V0 = B_K0 + B_W
C0 = B_V0 + B_W
PROJ_W = C0 + 3 * C_CH
QK_W = A_W + A_KV_W
N_EXPERTS = 64
TOP_K = 8
N_GROUPS = 8
GROUP_SIZE = N_EXPERTS // N_GROUPS
TOPK_GROUPS = 4
EXPERT_HIDDEN = 128
SHARED_HIDDEN = 256
ROUTED_SCALE = 2.5
DN_ALPHA = (2 * DEPTH) ** 0.25
LN_EPS = 1e-6
RMS_EPS = 1e-6
ATT_SCALE = HEAD_DIM ** -0.5
NEG_BIG = -1e30

VMEM_LIMIT_BYTES = 56 * 1024 * 1024
TOKEN_TILE = 256
ROPE_TABLE_W = 128
GQA_Q_TILE = 256
GQA_CHAINS = 2
GQA_KEY_CHUNK = 768
FREQ_TILE = 256


def _params(*sem):
    return pltpu.CompilerParams(dimension_semantics=sem, vmem_limit_bytes=VMEM_LIMIT_BYTES)


def _layer_norm(v):
    mu = jnp.mean(v, -1, keepdims=True)
    vc = v - mu
    var = jnp.mean(vc * vc, -1, keepdims=True)
    return vc * lax.rsqrt(var + LN_EPS)


def _sigmoid(v):
    return 1.0 / (1.0 + jnp.exp(-v))


def _dot(a, b):
    return jnp.dot(a, b, preferred_element_type=F32)


def _dot_nt(a, b):
    return lax.dot_general(a, b, (((1,), (1,)), ((), ())), preferred_element_type=F32)


def _mod_row(mod_ref, is_ctx, idx):
    return jnp.where(is_ctx, mod_ref[0, 1, idx:idx + 1, :], mod_ref[0, 0, idx:idx + 1, :])


def _sub_tiles(tm):
    for k in range(tm // TOKEN_TILE):
        rows = slice(k * TOKEN_TILE, (k + 1) * TOKEN_TILE)
        is_ctx = (pl.program_id(1) * tm + k * TOKEN_TILE) >= SEQ
        yield rows, is_ctx


def _proj_body(x_ref, mod_ref, w_ref, bd_ref, gain_ref, cos_ref, sin_ref,
               qa_ref, ka_ref, va_ref, qb_ref, kb_ref, vb_ref, z_ref, *, tm):
    n_rep = QK_W // ROPE_TABLE_W
    for rows, is_ctx in _sub_tiles(tm):
        xn = _layer_norm(x_ref[0, rows, :])
        hx = (xn * (1.0 + _mod_row(mod_ref, is_ctx, 1)) + _mod_row(mod_ref, is_ctx, 0)).astype(BF16)
        px = _dot(hx, w_ref[...])
        qk = px[:, :QK_W]
        ss = _dot((qk * qk).astype(BF16), bd_ref[...])
        qkn = qk * lax.rsqrt(ss * (1.0 / HEAD_DIM) + RMS_EPS) * gain_ref[...]
        lane = lax.broadcasted_iota(jnp.int32, qkn.shape, 1)
        partner = jnp.where((lane % HEAD_DIM) < HALF,
                            pltpu.roll(qkn, QK_W - HALF, 1), pltpu.roll(qkn, HALF, 1))
        cos = jnp.concatenate([cos_ref[rows, :]] * n_rep, axis=1)
        sin = jnp.concatenate([sin_ref[rows, :]] * n_rep, axis=1)
        qkr = (qkn * cos + partner * sin).astype(BF16)
        for h in range(A_HEADS):
            qa_ref[0, h, rows, :] = qkr[:, h * HEAD_DIM:(h + 1) * HEAD_DIM]
        for g in range(A_KV_HEADS):
            ka_ref[0, g, rows, :] = qkr[:, A_K0 + g * HEAD_DIM:A_K0 + (g + 1) * HEAD_DIM]
            va_ref[0, g, rows, :] = px[:, A_V0 + g * HEAD_DIM:A_V0 + (g + 1) * HEAD_DIM].astype(BF16)
        for h in range(B_HEADS):
            qb_ref[0, h, rows, :] = (px[:, B_Q0 + h * HEAD_DIM:B_Q0 + (h + 1) * HEAD_DIM] * ATT_SCALE).astype(BF16)
            kb_ref[0, h, rows, :] = px[:, B_K0 + h * HEAD_DIM:B_K0 + (h + 1) * HEAD_DIM].astype(BF16)
            vb_ref[0, h, rows, :] = px[:, B_V0 + h * HEAD_DIM:B_V0 + (h + 1) * HEAD_DIM].astype(BF16)
        z_ref[0, rows, :] = px[:, C0:]


def _step_tokens(s):
    return 3 * TOKEN_TILE if s % (3 * TOKEN_TILE) == 0 else 2 * TOKEN_TILE


def _project(x, mod, w_bf, bd, gain, cos_t, sin_t):
    bsz, s, d = x.shape
    tm = _step_tokens(s)
    nt = s // tm
    heads = lambda n: pl.BlockSpec((1, n, tm, HEAD_DIM), lambda b, i: (b, 0, i, 0))
    hshape = lambda n: jax.ShapeDtypeStruct((bsz, n, s, HEAD_DIM), BF16)
    return pl.pallas_call(
        functools.partial(_proj_body, tm=tm),
        grid=(bsz, nt),
        in_specs=[
            pl.BlockSpec((1, tm, d), lambda b, i: (b, i, 0)),
            pl.BlockSpec((1, 2, 6, d), lambda b, i: (b, 0, 0, 0)),
            pl.BlockSpec((d, PROJ_W), lambda b, i: (0, 0)),
            pl.BlockSpec((QK_W, QK_W), lambda b, i: (0, 0)),
            pl.BlockSpec((1, QK_W), lambda b, i: (0, 0)),
            pl.BlockSpec((tm, ROPE_TABLE_W), lambda b, i: (i, 0)),
            pl.BlockSpec((tm, ROPE_TABLE_W), lambda b, i: (i, 0)),
        ],
        out_specs=[heads(A_HEADS), heads(A_KV_HEADS), heads(A_KV_HEADS),
                   heads(B_HEADS), heads(B_HEADS), heads(B_HEADS),
                   pl.BlockSpec((1, tm, 3 * C_CH), lambda b, i: (b, i, 0))],
        out_shape=[hshape(A_HEADS), hshape(A_KV_HEADS), hshape(A_KV_HEADS),
                   hshape(B_HEADS), hshape(B_HEADS), hshape(B_HEADS),
                   jax.ShapeDtypeStruct((bsz, s, 3 * C_CH), F32)],
        compiler_params=_params("parallel", "parallel"),
        name="proj_in",
    )(x, mod, w_bf, bd, gain, cos_t, sin_t)


def _gqa_body(q_ref, k_ref, v_ref, o_ref, *, tq, lat_tiles, with_ctx_queries):
    i = pl.program_id(2)

    def attend(key0, n_keys):
        for c in range(GQA_CHAINS):
            hc = A_REP // GQA_CHAINS
            q = q_ref[0, c * hc:(c + 1) * hc].reshape(hc * tq, HEAD_DIM)
            kc = min(GQA_KEY_CHUNK, n_keys)
            m = l = acc = None
            for j in range(n_keys // kc):
                keys = slice(key0 + j * kc, key0 + (j + 1) * kc)
                s = _dot_nt(q, k_ref[0, 0, keys, :])
                mj = jnp.max(s, -1, keepdims=True)
                if j == 0:
                    m = mj
                    p = jnp.exp(s - m)
                    l = jnp.sum(p, -1, keepdims=True)
                    acc = _dot(p.astype(BF16), v_ref[0, 0, keys, :])
                else:
                    m_new = jnp.maximum(m, mj)
                    alpha = jnp.exp(m - m_new)
                    p = jnp.exp(s - m_new)
                    l = alpha * l + jnp.sum(p, -1, keepdims=True)
                    acc = alpha * acc + _dot(p.astype(BF16), v_ref[0, 0, keys, :])
                    m = m_new
            o = acc / l
            for h in range(hc):
                hh = c * hc + h
                o_ref[0, :, hh * HEAD_DIM:(hh + 1) * HEAD_DIM] = o[h * tq:(h + 1) * tq].astype(BF16)

    @pl.when(i < lat_tiles)
    def _():
        attend(0, k_ref.shape[2])

    if with_ctx_queries:
        @pl.when(i >= lat_tiles)
        def _():
            attend(SEQ, CTX_LEN)


def _gqa(qa, ka, va, s_out):
    bsz, _, s, _ = qa.shape
    tq = GQA_Q_TILE
    body = functools.partial(_gqa_body, tq=tq, lat_tiles=SEQ // tq, with_ctx_queries=s_out > SEQ)
    return pl.pallas_call(
        body,
        grid=(bsz, A_KV_HEADS, s_out // tq),
        in_specs=[
            pl.BlockSpec((1, A_REP, tq, HEAD_DIM), lambda b, g, i: (b, g, i, 0)),
            pl.BlockSpec((1, 1, s, HEAD_DIM), lambda b, g, i: (b, g, 0, 0)),
            pl.BlockSpec((1, 1, s, HEAD_DIM), lambda b, g, i: (b, g, 0, 0)),
        ],
        out_specs=pl.BlockSpec((1, tq, A_REP * HEAD_DIM), lambda b, g, i: (b, i, g)),
        out_shape=jax.ShapeDtypeStruct((bsz, s_out, A_W), BF16),
        compiler_params=_params("parallel", "parallel", "parallel"),
        name="gqa_attn",
    )(qa, ka, va)


NA_HEADS_PER_STEP = 2
NA_Q_ROWS = 8
NA_K_ROWS = 16
NA_Q_TOKENS = NA_Q_ROWS * GRID_W
NA_K_TOKENS = NA_K_ROWS * GRID_W
NA_Q_BLOCKS = GRID_ROWS // NA_Q_ROWS
NA_KEY_ROW0 = (0, 4, 12, 16)
NA_PATTERN_BLOCKS = (0, 1, 3)


def _na_body(q_ref, k_ref, v_ref, bias_ref, o_ref):
    j = pl.program_id(2)
    key_row0 = jnp.where(j == 0, NA_KEY_ROW0[0],
                         jnp.where(j == 1, NA_KEY_ROW0[1], jnp.where(j == 2, NA_KEY_ROW0[2], NA_KEY_ROW0[3])))
    k0 = pl.multiple_of(key_row0 * GRID_W, 4 * GRID_W)
    for hh in range(NA_HEADS_PER_STEP):
        q = q_ref[0, hh]
        kw = k_ref[0, hh, pl.ds(k0, NA_K_TOKENS), :]
        vw = v_ref[0, hh, pl.ds(k0, NA_K_TOKENS), :]
        kc = k_ref[0, hh, SEQ:, :]
        vc = v_ref[0, hh, SEQ:, :]
        sw = _dot_nt(q, kw) + bias_ref[hh, 0]
        sc = _dot_nt(q, kc)
        m = jnp.maximum(jnp.max(sw, -1, keepdims=True), jnp.max(sc, -1, keepdims=True))
        pw = jnp.exp(sw - m)
        pc = jnp.exp(sc - m)
        l = jnp.sum(pw, -1, keepdims=True) + jnp.sum(pc, -1, keepdims=True)
        o = (_dot(pw.astype(BF16), vw) + _dot(pc.astype(BF16), vc)) / l
        o_ref[0, :, hh * HEAD_DIM:(hh + 1) * HEAD_DIM] = o.astype(BF16)


def _na(qb, kb, vb, bias):
    bsz, _, s, _ = qb.shape
    hp = NA_HEADS_PER_STEP
    kv = pl.BlockSpec((1, hp, s, HEAD_DIM), lambda b, h, j: (b, h, 0, 0))
    return pl.pallas_call(
        _na_body,
        grid=(bsz, B_HEADS // hp, NA_Q_BLOCKS),
        in_specs=[pl.BlockSpec((1, hp, NA_Q_TOKENS, HEAD_DIM), lambda b, h, j: (b, h, j, 0)), kv, kv,
                  pl.BlockSpec((hp, 1, NA_Q_TOKENS, NA_K_TOKENS), lambda b, h, j: (h, (j + 1) // 2, 0, 0))],
        out_specs=pl.BlockSpec((1, NA_Q_TOKENS, hp * HEAD_DIM), lambda b, h, j: (b, j, h)),
        out_shape=jax.ShapeDtypeStruct((bsz, SEQ, B_W), BF16),
        compiler_params=_params("parallel", "parallel", "arbitrary"),
        name="nbr_attn",
    )(qb, kb, vb, bias)


def _na_bias_body(t_ref, o_ref):
    o_ref[...] = jnp.full(o_ref.shape, NEG_BIG, F32)
    for p, blk in enumerate(NA_PATTERN_BLOCKS):
        @pl.when(pl.program_id(1) == p)
        def _(blk=blk):
            for qi in range(NA_Q_ROWS):
                qrow = blk * NA_Q_ROWS + qi
                rs = min(max(qrow - WIN_ROWS // 2, 0), GRID_ROWS - WIN_ROWS)
                for krow in range(rs, rs + WIN_ROWS):
                    ki = krow - NA_KEY_ROW0[blk]
                    o_ref[0, 0, qi * GRID_W:(qi + 1) * GRID_W, ki * GRID_W:(ki + 1) * GRID_W] = (
                        t_ref[0, krow - qrow + WIN_ROWS - 1])


def _na_bias_table(rpb):
    col = np.arange(GRID_W)
    col_start = np.clip(col - WIN_COLS // 2, 0, GRID_W - WIN_COLS)
    in_cols = (col[None, :] >= col_start[:, None]) & (col[None, :] < col_start[:, None] + WIN_COLS)
    dc = np.clip(col[None, :] - col[:, None] + WIN_COLS - 1, 0, 2 * WIN_COLS - 2)
    onehot = jnp.asarray(dc[None] == np.arange(2 * WIN_COLS - 1)[:, None, None], F32)
    tiles = jnp.einsum('hdj,jqk->hdqk', rpb.astype(F32), onehot, precision=HIGHEST)
    tiles = jnp.where(in_cols[None, None], tiles, NEG_BIG)
    n_pat = len(NA_PATTERN_BLOCKS)
    return pl.pallas_call(
        _na_bias_body,
        grid=(B_HEADS, n_pat),
        in_specs=[pl.BlockSpec((1, 2 * WIN_ROWS - 1, GRID_W, GRID_W), lambda h, p: (h, 0, 0, 0))],
        out_specs=pl.BlockSpec((1, 1, NA_Q_TOKENS, NA_K_TOKENS), lambda h, p: (h, p, 0, 0)),
        out_shape=jax.ShapeDtypeStruct((B_HEADS, n_pat, NA_Q_TOKENS, NA_K_TOKENS), F32),
        compiler_params=_params("parallel", "arbitrary"),
        name="nbr_bias",
    )(tiles)


def _ctx_attn_body(q_ref, k_ref, v_ref, o_ref):
    for hh in range(NA_HEADS_PER_STEP):
        s = _dot_nt(q_ref[0, hh], k_ref[0, hh])
        m = jnp.max(s, -1, keepdims=True)
        p = jnp.exp(s - m)
        l = jnp.sum(p, -1, keepdims=True)
        o = _dot(p.astype(BF16), v_ref[0, hh]) / l
        o_ref[0, :, hh * HEAD_DIM:(hh + 1) * HEAD_DIM] = o.astype(BF16)


def _ctx_attn(qb, kb, vb):
    bsz = qb.shape[0]
    hp = NA_HEADS_PER_STEP
    seg = pl.BlockSpec((1, hp, CTX_LEN, HEAD_DIM), lambda b, h: (b, h, SEQ // CTX_LEN, 0))
    return pl.pallas_call(
        _ctx_attn_body,
        grid=(bsz, B_HEADS // hp),
        in_specs=[seg, seg, seg],
        out_specs=pl.BlockSpec((1, CTX_LEN, hp * HEAD_DIM), lambda b, h: (b, 0, h)),
        out_shape=jax.ShapeDtypeStruct((bsz, CTX_LEN, B_W), BF16),
        compiler_params=_params("parallel", "parallel"),
        name="ctx_attn",
    )(qb, kb, vb)


DFT_FINE = 16


def _dft_tables(n):
    s = jnp.arange(n, dtype=jnp.int32)[None, :]

    def cos_sin(f):
        ang = ((f * s) % (2 * n)).astype(F32) * (math.pi / n)
        return jnp.cos(ang), jnp.sin(ang)

    ch, sh = cos_sin(jnp.arange(n // DFT_FINE, dtype=jnp.int32)[:, None] * DFT_FINE)
    cl, sl = cos_sin(jnp.arange(DFT_FINE, dtype=jnp.int32)[:, None])
    fc = (ch[:, None] * cl[None] - sh[:, None] * sl[None]).reshape(n, n)
    sym = (sh[:, None] * cl[None] + ch[:, None] * sl[None]).reshape(n, n)
    alt = jnp.where(jnp.arange(n) % 2 == 0, 1.0, -1.0).astype(F32)
    first = jnp.arange(n) == 0
    fs = jnp.where(first[:, None], alt[None, :], sym)
    fst = jnp.where(first[None, :], alt[:, None], sym)
    return fc.astype(BF16), fs.astype(BF16), fst.astype(BF16)


def _split_bf16(v):
    hi = v.astype(BF16)
    lo = (v - hi.astype(F32)).astype(BF16)
    return hi, lo


def _spectrum_body(fc_ref, fs_ref, kp_ref, km_ref, kc_ref, ks_ref, *, tf):
    kp_hi, kp_lo = _split_bf16(kp_ref[...])
    km_hi, km_lo = _split_bf16(km_ref[...])
    fc = fc_ref[...]
    fs = fs_ref[...]
    kc_ref[...] = _dot(fc, kp_hi) + _dot(fc, kp_lo)
    ks_sin = _dot(fs, km_hi) + _dot(fs, km_lo)
    ks_nyq = _dot(fs, kp_hi) + _dot(fs, kp_lo)
    row = lax.broadcasted_iota(jnp.int32, ks_sin.shape, 0) + pl.program_id(0) * tf
    ks_ref[...] = jnp.where(row == 0, ks_nyq, ks_sin)


def _filter_spectrum(fc, fs, kplus, kminus):
    n = fc.shape[0]
    tf = min(FREQ_TILE, n)
    tab = pl.BlockSpec((tf, n), lambda j: (j, 0))
    filt = pl.BlockSpec((n, C_CH), lambda j: (0, 0))
    out = pl.BlockSpec((tf, C_CH), lambda j: (j, 0))
    return pl.pallas_call(
        functools.partial(_spectrum_body, tf=tf),
        grid=(n // tf,),
        in_specs=[tab, tab, filt, filt],
        out_specs=[out, out],
        out_shape=[jax.ShapeDtypeStruct((n, C_CH), F32)] * 2,
        compiler_params=_params("parallel"),
        name="hyena_spectrum",
    )(fc, fs, kplus, kminus)


def _hyena_body(z_ref, cw_ref, cb_ref, skip_ref, fc_ref, fs_ref, fct_ref, fst_ref, kc_ref, ks_ref,
                o_ref, ub_ref, uf_ref, x0_ref, y_ref, *, n, tf):
    j = pl.program_id(1)

    @pl.when(j == 0)
    def _():
        z = z_ref[0]
        t = lax.broadcasted_iota(jnp.int32, z.shape, 0)
        prev = jnp.where(t == 0, 0.0, pltpu.roll(z, 1, 0))
        nxt = jnp.where(t == n - 1, 0.0, pltpu.roll(z, n - 1, 0))
        zc = prev * cw_ref[0:1, :] + z * cw_ref[1:2, :] + nxt * cw_ref[2:3, :] + cb_ref[...]
        u = zc[:, 2 * C_CH:] * zc[:, C_CH:2 * C_CH]
        x0_ref[...] = zc[:, :C_CH]
        uf_ref[...] = u
        ub_ref[...] = u.astype(BF16)
        y_ref[...] = jnp.zeros_like(y_ref)

    ub = ub_ref[...]
    uc = _dot(fc_ref[...], ub)
    us = _dot(fs_ref[...], ub)
    kc = kc_ref[...]
    ks = ks_ref[...]
    row = lax.broadcasted_iota(jnp.int32, uc.shape, 0) + j * tf
    first = row == 0
    wgt = jnp.where(first, 1.0, 2.0) * (1.0 / (2 * n))
    re = jnp.where(first, uc * kc, uc * kc - us * ks) * wgt
    im = jnp.where(first, us * ks, uc * ks + us * kc) * wgt
    y_ref[...] += _dot(fct_ref[...], re.astype(BF16)) + _dot(fst_ref[...], im.astype(BF16))

    @pl.when(j == pl.num_programs(1) - 1)
    def _():
        o_ref[0] = ((y_ref[...] + uf_ref[...] * skip_ref[...]) * x0_ref[...]).astype(BF16)


def _hyena(z, seg_start, n, cw, cb, skip, tables, kc, ks):
    bsz = z.shape[0]
    fc, fs, fst = tables
    tf = min(FREQ_TILE, n)
    seg = seg_start // n
    row_blk = pl.BlockSpec((tf, n), lambda b, j: (j, 0))
    col_blk = pl.BlockSpec((n, tf), lambda b, j: (0, j))
    spec_blk = pl.BlockSpec((tf, C_CH), lambda b, j: (j, 0))
    small = lambda r, c: pl.BlockSpec((r, c), lambda b, j: (0, 0))
    return pl.pallas_call(
        functools.partial(_hyena_body, n=n, tf=tf),
        grid=(bsz, n // tf),
        in_specs=[pl.BlockSpec((1, n, 3 * C_CH), lambda b, j: (b, seg, 0)),
                  small(3, 3 * C_CH), small(1, 3 * C_CH), small(1, C_CH),
                  row_blk, row_blk, col_blk, col_blk, spec_blk, spec_blk],
        out_specs=pl.BlockSpec((1, n, C_CH), lambda b, j: (b, 0, 0)),
        out_shape=jax.ShapeDtypeStruct((bsz, n, C_CH), BF16),
        scratch_shapes=[pltpu.VMEM((n, C_CH), BF16), pltpu.VMEM((n, C_CH), F32),
                        pltpu.VMEM((n, C_CH), F32), pltpu.VMEM((n, C_CH), F32)],
        compiler_params=_params("parallel", "arbitrary"),
        name="hyena_conv",
    )(z, cw, cb, skip, fc, fs, fc, fst, kc, ks)


def _implicit_filters(n, w1, b1, w2, b2, w3, freq):
    t = jnp.linspace(0.0, 1.0, n, dtype=F32)[:, None]
    w = 2 * math.pi * jnp.arange(n, dtype=F32)[:, None] / n
    f = jnp.linspace(1e-4, C_POS_BANDS - 1, C_POS_BANDS, dtype=F32)[None]
    z = jnp.concatenate([t, jnp.cos(f * w), -jnp.sin(f * w)], -1)
    h = jnp.sin(freq[0] * (jnp.dot(z, w1, precision=HIGHEST) + b1))
    h = jnp.sin(freq[1] * (jnp.dot(h, w2, precision=HIGHEST) + b2))
    h = jnp.dot(h, w3, precision=HIGHEST)
    lo = math.log(C_DECAY_TARGET) / C_SLOW_DECAY
    hi = math.log(C_DECAY_TARGET) / C_FAST_DECAY
    deltas = jnp.abs(jnp.linspace(lo, hi, C_CH, dtype=F32))
    decay = jnp.exp(-t * deltas[None])
    hf, hb = h[:, :C_CH] * decay, h[:, C_CH:] * decay
    hb = hb.at[0].set(0.0)
    return hf + hb, hf - hb


def _router_gates(logits_t, rb):
    tm = logits_t.shape[1]
    scores = _sigmoid(logits_t)
    sel = scores + rb
    sel3 = sel.reshape(N_GROUPS, GROUP_SIZE, tm)
    kk = lax.broadcasted_iota(jnp.int32, sel3.shape, 1)
    m1 = jnp.max(sel3, 1, keepdims=True)
    i1 = jnp.min(jnp.where(sel3 == m1, kk, GROUP_SIZE), 1, keepdims=True)
    m2 = jnp.max(jnp.where(kk == i1, -jnp.inf, sel3), 1, keepdims=True)
    gscore = (m1 + m2).reshape(N_GROUPS, tm)
    gid = lax.broadcasted_iota(jnp.int32, gscore.shape, 0)
    rank = jnp.zeros(gscore.shape, jnp.int32)
    for g in range(N_GROUPS):
        other = gscore[g:g + 1, :]
        ahead = (other > gscore) | ((other == gscore) & (g < gid))
        rank = rank + ahead.astype(jnp.int32)
    gsel = (rank < TOPK_GROUPS).reshape(N_GROUPS, 1, tm)
    cand = jnp.where(gsel, sel3, -jnp.inf).reshape(N_EXPERTS, tm)
    eid = lax.broadcasted_iota(jnp.int32, cand.shape, 0)
    chosen = jnp.zeros(cand.shape, jnp.bool_)
    for _ in range(TOP_K):
        m = jnp.max(cand, 0, keepdims=True)
        pick = eid == jnp.min(jnp.where(cand == m, eid, N_EXPERTS), 0, keepdims=True)
        chosen = chosen | pick
        cand = jnp.where(pick, -jnp.inf, cand)
    wsel = jnp.where(chosen, scores, 0.0)
    return wsel / jnp.sum(wsel, 0, keepdims=True) * ROUTED_SCALE


def _out_body(oa_ref, ob_ref, oc_ref, x_ref, mod_ref, wa_ref, wb_ref, wc_ref, lng_ref, lnb_ref,
              rwt_ref, rb_ref, xmid_ref, h_ref, gate_ref, *, tm):
    for rows, is_ctx in _sub_tiles(tm):
        attn = (_dot(oa_ref[0, rows, :], wa_ref[...]) + _dot(ob_ref[0, rows, :], wb_ref[...])
                + _dot(oc_ref[0, rows, :], wc_ref[...]))
        y = DN_ALPHA * x_ref[0, rows, :] + _mod_row(mod_ref, is_ctx, 2) * attn
        xm = _layer_norm(y) * lng_ref[...] + lnb_ref[...]
        xmid_ref[0, rows, :] = xm
        h = _layer_norm(xm) * (1.0 + _mod_row(mod_ref, is_ctx, 4)) + _mod_row(mod_ref, is_ctx, 3)
        hi, lo = _split_bf16(h)
        h_ref[0, rows, :] = hi
        logits_t = _dot_nt(rwt_ref[...], jnp.concatenate([hi, lo, hi], axis=1))
        gates_t = _router_gates(logits_t, rb_ref[...])
        pad = jnp.zeros((128 - N_EXPERTS, gates_t.shape[1]), F32)
        gates = jnp.concatenate([gates_t, pad], axis=0).T
        for g in range(N_GROUPS):
            gate_ref[0, g, rows, :] = gates[:, g * GROUP_SIZE:(g + 1) * GROUP_SIZE]


def _out_proj(oa, ob, oc, x, mod, wa, wb, wc, lng, lnb, rwt3, rb):
    bsz, s, _ = oa.shape
    d = D_MODEL
    tm = _step_tokens(s)
    tok = lambda w: pl.BlockSpec((1, tm, w), lambda b, i: (b, i, 0))
    full = lambda r, c: pl.BlockSpec((r, c), lambda b, i: (0, 0))
    return pl.pallas_call(
        functools.partial(_out_body, tm=tm),
        grid=(bsz, s // tm),
        in_specs=[tok(A_W), tok(B_W), tok(C_CH), tok(d),
                  pl.BlockSpec((1, 2, 6, d), lambda b, i: (b, 0, 0, 0)),
                  full(A_W, d), full(B_W, d), full(C_CH, d), full(1, d), full(1, d),
                  full(N_EXPERTS, 3 * d), full(N_EXPERTS, 1)],
        out_specs=[tok(d), tok(d),
                   pl.BlockSpec((1, N_GROUPS, tm, GROUP_SIZE), lambda b, i: (b, 0, i, 0))],
        out_shape=[jax.ShapeDtypeStruct((bsz, s, d), F32),
                   jax.ShapeDtypeStruct((bsz, s, d), BF16),
                   jax.ShapeDtypeStruct((bsz, N_GROUPS, s, GROUP_SIZE), F32)],
        compiler_params=_params("parallel", "parallel"),
        name="proj_out_router",
    )(oa, ob, oc, x, mod, wa, wb, wc, lng, lnb, rwt3, rb)


MOE_CHAINS = 2


def _moe_body(h_ref, gate_ref, xmid_ref, mod_ref, w1_ref, w3_ref, w2_ref,
              sw1_ref, sw3_ref, sw2_ref, lng_ref, lnb_ref, o_ref, acc_ref, *, tm):
    g = pl.program_id(2)
    rc = tm // MOE_CHAINS

    @pl.when(g == 0)
    def _():
        h = h_ref[0]
        a = _dot(h, sw1_ref[...])
        act = (a * _sigmoid(a) * _dot(h, sw3_ref[...])).astype(BF16)
        acc_ref[...] = _dot(act, sw2_ref[...])

    w1 = jnp.concatenate([w1_ref[k] for k in range(GROUP_SIZE)], axis=1)
    w3 = jnp.concatenate([w3_ref[k] for k in range(GROUP_SIZE)], axis=1)
    w2 = w2_ref[...].reshape(GROUP_SIZE * EXPERT_HIDDEN, D_MODEL)
    for c in range(MOE_CHAINS):
        rows = slice(c * rc, (c + 1) * rc)
        h = h_ref[0, rows, :]
        a = _dot(h, w1)
        act = a * _sigmoid(a) * _dot(h, w3)
        gt = gate_ref[0, 0, rows, :]
        parts = [act[:, k * EXPERT_HIDDEN:(k + 1) * EXPERT_HIDDEN] * gt[:, k:k + 1]
                 for k in range(GROUP_SIZE)]
        acc_ref[rows, :] += _dot(jnp.concatenate(parts, axis=1).astype(BF16), w2)

    @pl.when(g == N_GROUPS - 1)
    def _():
        row = lax.broadcasted_iota(jnp.int32, (tm, 1), 0) + pl.program_id(1) * tm
        gate2 = jnp.where(row >= SEQ, mod_ref[0, 1, 5:6, :], mod_ref[0, 0, 5:6, :])
        y = DN_ALPHA * xmid_ref[0] + gate2 * acc_ref[...]
        o_ref[0] = _layer_norm(y) * lng_ref[...] + lnb_ref[...]


def _moe(h, gates, xmid, mod, layer, w1_all, w3_all, w2_all, sw1, sw3, sw2, lng, lnb, tm):
    bsz, s, d = h.shape
    tok = pl.BlockSpec((1, tm, d), lambda b, i, g: (b, i, 0))
    full = lambda r, c: pl.BlockSpec((r, c), lambda b, i, g: (0, 0))
    up = pl.BlockSpec((None, GROUP_SIZE, d, EXPERT_HIDDEN), lambda b, i, g: (layer, g, 0, 0))
    return pl.pallas_call(
        functools.partial(_moe_body, tm=tm),
        grid=(bsz, s // tm, N_GROUPS),
        in_specs=[tok,
                  pl.BlockSpec((1, 1, tm, GROUP_SIZE), lambda b, i, g: (b, g, i, 0)),
                  tok,
                  pl.BlockSpec((1, 2, 6, d), lambda b, i, g: (b, 0, 0, 0)),
                  up, up,
                  pl.BlockSpec((None, GROUP_SIZE, EXPERT_HIDDEN, d), lambda b, i, g: (layer, g, 0, 0)),
                  full(d, SHARED_HIDDEN), full(d, SHARED_HIDDEN), full(SHARED_HIDDEN, d),
                  full(1, d), full(1, d)],
        out_specs=tok,
        out_shape=jax.ShapeDtypeStruct((bsz, s, d), F32),
        scratch_shapes=[pltpu.VMEM((tm, d), F32)],
        compiler_params=_params("parallel", "parallel", "arbitrary"),
        name="moe_ffn",
    )(h, gates, xmid, mod, w1_all, w3_all, w2_all, sw1, sw3, sw2, lng, lnb)


def _rope_tables():
    t = jnp.arange(SEQ)
    row = (t // GRID_W).astype(F32)
    col = (t % GRID_W).astype(F32)
    inv = ROPE_THETA ** (-jnp.arange(0, HALF, 2, dtype=F32) / HALF)
    ang = jnp.concatenate([row[:, None] * inv, col[:, None] * inv], -1)
    cos, sin = jnp.cos(ang), jnp.sin(ang)
    n_heads = ROPE_TABLE_W // HEAD_DIM
    cos_t = jnp.concatenate([jnp.tile(jnp.concatenate([cos, cos], -1), (1, n_heads)),
                             jnp.ones((CTX_LEN, ROPE_TABLE_W), F32)], 0)
    sin_t = jnp.concatenate([jnp.tile(jnp.concatenate([-sin, sin], -1), (1, n_heads)),
                             jnp.zeros((CTX_LEN, ROPE_TABLE_W), F32)], 0)
    return cos_t, sin_t


def _modulation(c, c_ctx, w, b):
    cvec = jnp.concatenate([c, c_ctx[None]], axis=0)
    m = (jnp.dot(jax.nn.silu(cvec), w, precision=HIGHEST) + b).reshape(-1, 6, D_MODEL)
    return jnp.stack([m[:-1], jnp.broadcast_to(m[-1], m[:-1].shape)], axis=1)


def kernel(x, c, ctx, c_ctx, ada_w, ada_b, w_in, q_gain, k_gain, nat_rpb, hy_conv_w, hy_conv_b, hy_f_w1, hy_f_b1, hy_f_w2, hy_f_b2, hy_f_w3, hy_freq, hy_skip, w_out, ln1_g, ln1_b, router_w, router_b, exp_w1, exp_w3, exp_w2, sh_w1, sh_w3, sh_w2, ln2_g, ln2_b):
    bsz = x.shape[0]
    d = D_MODEL
    xs = jnp.concatenate([x, ctx], axis=1)

    head_perm = np.concatenate([np.arange(0, HEAD_DIM, 2), np.arange(1, HEAD_DIM, 2)])
    qk_perm = (np.arange(QK_W // HEAD_DIM)[:, None] * HEAD_DIM + head_perm[None, :]).reshape(-1)
    perm_mat = jnp.asarray(np.eye(QK_W)[:, qk_perm], BF16)
    cos_t, sin_t = _rope_tables()
    bd = jnp.asarray(np.kron(np.eye(QK_W // HEAD_DIM), np.ones((HEAD_DIM, HEAD_DIM))), BF16)
    tables_lat = _dft_tables(SEQ)
    tables_ctx = _dft_tables(CTX_LEN)
    w1_bf, w3_bf, w2_bf = exp_w1.astype(BF16), exp_w3.astype(BF16), exp_w2.astype(BF16)

    for l in range(DEPTH):
        last = l == DEPTH - 1
        mod = _modulation(c, c_ctx, ada_w[l], ada_b[l])

        w_bf = w_in[l].astype(BF16)
        w_qk = jnp.dot(w_bf[:, :QK_W], perm_mat, preferred_element_type=F32).astype(BF16)
        w_bf = jnp.concatenate([w_qk, w_bf[:, QK_W:]], axis=1)
        gain = jnp.concatenate([jnp.tile(q_gain[l][head_perm] * ATT_SCALE, A_HEADS),
                                jnp.tile(k_gain[l][head_perm], A_KV_HEADS)])[None].astype(F32)
        qa, ka, va, qb, kb, vb, z = _project(xs, mod, w_bf, bd, gain, cos_t, sin_t)

        s_out = SEQ if last else S_ALL
        oa = _gqa(qa, ka, va, s_out)
        ob = _na(qb, kb, vb, _na_bias_table(nat_rpb[l]))
        if not last:
            ob = jnp.concatenate([ob, _ctx_attn(qb, kb, vb)], axis=1)

        filt = (hy_f_w1[l], hy_f_b1[l], hy_f_w2[l], hy_f_b2[l], hy_f_w3[l], hy_freq[l])
        cw, cb, skip = hy_conv_w[l], hy_conv_b[l][None], hy_skip[l][None]
        kc, ks = _filter_spectrum(tables_lat[0], tables_lat[1], *_implicit_filters(SEQ, *filt))
        oc = _hyena(z, 0, SEQ, cw, cb, skip, tables_lat, kc, ks)
        if not last:
            kc, ks = _filter_spectrum(tables_ctx[0], tables_ctx[1], *_implicit_filters(CTX_LEN, *filt))
            oc = jnp.concatenate([oc, _hyena(z, SEQ, CTX_LEN, cw, cb, skip, tables_ctx, kc, ks)], axis=1)

        wo = w_out[l].astype(BF16)
        rw_hi, rw_lo = _split_bf16(router_w[l].T)
        rwt3 = jnp.concatenate([rw_hi, rw_hi, rw_lo], axis=1)
        xmid, h, gates = _out_proj(oa, ob, oc, xs, mod, wo[:A_W], wo[A_W:A_W + B_W], wo[A_W + B_W:],
                                   ln1_g[l][None], ln1_b[l][None], rwt3, router_b[l][:, None])

        xs = _moe(h, gates, xmid, mod, l, w1_bf, w3_bf, w2_bf,
                  sh_w1[l].astype(BF16), sh_w3[l].astype(BF16), sh_w2[l].astype(BF16),
                  ln2_g[l][None], ln2_b[l][None], tm=512 if last else 768)
    return xs
```

```python
import functools
import math

import numpy as np
import jax
import jax.numpy as jnp
from jax import lax
from jax.experimental import pallas as pl
from jax.experimental.pallas import tpu as pltpu

F32 = jnp.float32
BF16 = jnp.bfloat16
HIGHEST = lax.Precision.HIGHEST

D_MODEL = 1024
SEQ = 2048
DEPTH = 2
GRID_W = 64
GRID_ROWS = SEQ // GRID_W
CTX_LEN = 256
S_ALL = SEQ + CTX_LEN
HEAD_DIM = 64
HALF = HEAD_DIM // 2
A_HEADS = 8
A_KV_HEADS = 2
A_REP = A_HEADS // A_KV_HEADS
ROPE_THETA = 10000.0
B_HEADS = 4
WIN_ROWS = 8
WIN_COLS = 16
C_CH = 256
C_POS_BANDS = 16
C_DECAY_TARGET = 1e-2
C_FAST_DECAY = 0.3
C_SLOW_DECAY = 1.5
A_W = A_HEADS * HEAD_DIM
A_KV_W = A_KV_HEADS * HEAD_DIM
B_W = B_HEADS * HEAD_DIM
A_K0 = A_W
A_V0 = A_K0 + A_KV_W
B_Q0 = A_V0 + A_KV_W
B_K0 = B_Q0 + B_W
B_V0 = B_K0 + B_W
C0 = B_V0 + B_W
PROJ_W = C0 + 3 * C_CH
QK_W = A_W + A_KV_W
N_EXPERTS = 64
TOP_K = 8
N_GROUPS = 8
GROUP_SIZE = N_EXPERTS // N_GROUPS
TOPK_GROUPS = 4
EXPERT_HIDDEN = 128
SHARED_HIDDEN = 256
ROUTED_SCALE = 2.5
DN_ALPHA = (2 * DEPTH) ** 0.25
LN_EPS = 1e-6
RMS_EPS = 1e-6
ATT_SCALE = HEAD_DIM ** -0.5
NEG_BIG = -1e30

VMEM_LIMIT_BYTES = 56 * 1024 * 1024
TOKEN_TILE = 256
ROPE_TABLE_W = 128
GQA_Q_TILE = 256
GQA_CHAINS = 2
GQA_KEY_CHUNK = 768
FREQ_TILE = 256


def _params(*sem):
    return pltpu.CompilerParams(dimension_semantics=sem, vmem_limit_bytes=VMEM_LIMIT_BYTES)


def _layer_norm(v):
    mu = jnp.mean(v, -1, keepdims=True)
    vc = v - mu
    var = jnp.mean(vc * vc, -1, keepdims=True)
    return vc * lax.rsqrt(var + LN_EPS)


def _sigmoid(v):
    return 1.0 / (1.0 + jnp.exp(-v))


def _dot(a, b):
    return jnp.dot(a, b, preferred_element_type=F32)


def _dot_nt(a, b):
    return lax.dot_general(a, b, (((1,), (1,)), ((), ())), preferred_element_type=F32)


def _mod_row(mod_ref, is_ctx, idx):
    return jnp.where(is_ctx, mod_ref[0, 1, idx:idx + 1, :], mod_ref[0, 0, idx:idx + 1, :])


def _sub_tiles(tm):
    for k in range(tm // TOKEN_TILE):
        rows = slice(k * TOKEN_TILE, (k + 1) * TOKEN_TILE)
        is_ctx = (pl.program_id(1) * tm + k * TOKEN_TILE) >= SEQ
        yield rows, is_ctx


def _proj_body(x_ref, mod_ref, w_ref, bd_ref, gain_ref, cos_ref, sin_ref,
               qa_ref, ka_ref, va_ref, qb_ref, kb_ref, vb_ref, z_ref, *, tm):
    n_rep = QK_W // ROPE_TABLE_W
    for rows, is_ctx in _sub_tiles(tm):
        xn = _layer_norm(x_ref[0, rows, :])
        hx = (xn * (1.0 + _mod_row(mod_ref, is_ctx, 1)) + _mod_row(mod_ref, is_ctx, 0)).astype(BF16)
        px = _dot(hx, w_ref[...])
        qk = px[:, :QK_W]
        ss = _dot((qk * qk).astype(BF16), bd_ref[...])
        qkn = qk * lax.rsqrt(ss * (1.0 / HEAD_DIM) + RMS_EPS) * gain_ref[...]
        lane = lax.broadcasted_iota(jnp.int32, qkn.shape, 1)
        partner = jnp.where((lane % HEAD_DIM) < HALF,
                            pltpu.roll(qkn, QK_W - HALF, 1), pltpu.roll(qkn, HALF, 1))
        cos = jnp.concatenate([cos_ref[rows, :]] * n_rep, axis=1)
        sin = jnp.concatenate([sin_ref[rows, :]] * n_rep, axis=1)
        qkr = (qkn * cos + partner * sin).astype(BF16)
        for h in range(A_HEADS):
            qa_ref[0, h, rows, :] = qkr[:, h * HEAD_DIM:(h + 1) * HEAD_DIM]
        for g in range(A_KV_HEADS):
            ka_ref[0, g, rows, :] = qkr[:, A_K0 + g * HEAD_DIM:A_K0 + (g + 1) * HEAD_DIM]
            va_ref[0, g, rows, :] = px[:, A_V0 + g * HEAD_DIM:A_V0 + (g + 1) * HEAD_DIM].astype(BF16)
        for h in range(B_HEADS):
            qb_ref[0, h, rows, :] = (px[:, B_Q0 + h * HEAD_DIM:B_Q0 + (h + 1) * HEAD_DIM] * ATT_SCALE).astype(BF16)
            kb_ref[0, h, rows, :] = px[:, B_K0 + h * HEAD_DIM:B_K0 + (h + 1) * HEAD_DIM].astype(BF16)
            vb_ref[0, h, rows, :] = px[:, B_V0 + h * HEAD_DIM:B_V0 + (h + 1) * HEAD_DIM].astype(BF16)
        z_ref[0, rows, :] = px[:, C0:]


def _step_tokens(s):
    return 3 * TOKEN_TILE if s % (3 * TOKEN_TILE) == 0 else 2 * TOKEN_TILE


def _project(x, mod, w_bf, bd, gain, cos_t, sin_t):
    bsz, s, d = x.shape
    tm = _step_tokens(s)
    nt = s // tm
    heads = lambda n: pl.BlockSpec((1, n, tm, HEAD_DIM), lambda b, i: (b, 0, i, 0))
    hshape = lambda n: jax.ShapeDtypeStruct((bsz, n, s, HEAD_DIM), BF16)
    return pl.pallas_call(
        functools.partial(_proj_body, tm=tm),
        grid=(bsz, nt),
        in_specs=[
            pl.BlockSpec((1, tm, d), lambda b, i: (b, i, 0)),
            pl.BlockSpec((1, 2, 6, d), lambda b, i: (b, 0, 0, 0)),
            pl.BlockSpec((d, PROJ_W), lambda b, i: (0, 0)),
            pl.BlockSpec((QK_W, QK_W), lambda b, i: (0, 0)),
            pl.BlockSpec((1, QK_W), lambda b, i: (0, 0)),
            pl.BlockSpec((tm, ROPE_TABLE_W), lambda b, i: (i, 0)),
            pl.BlockSpec((tm, ROPE_TABLE_W), lambda b, i: (i, 0)),
        ],
        out_specs=[heads(A_HEADS), heads(A_KV_HEADS), heads(A_KV_HEADS),
                   heads(B_HEADS), heads(B_HEADS), heads(B_HEADS),
                   pl.BlockSpec((1, tm, 3 * C_CH), lambda b, i: (b, i, 0))],
        out_shape=[hshape(A_HEADS), hshape(A_KV_HEADS), hshape(A_KV_HEADS),
                   hshape(B_HEADS), hshape(B_HEADS), hshape(B_HEADS),
                   jax.ShapeDtypeStruct((bsz, s, 3 * C_CH), F32)],
        compiler_params=_params("parallel", "parallel"),
        name="proj_in",
    )(x, mod, w_bf, bd, gain, cos_t, sin_t)


def _gqa_body(q_ref, k_ref, v_ref, o_ref, *, tq, chains, lat_tiles, with_ctx_queries):
    i = pl.program_id(2)

    def attend(key0, n_keys):
        for c in range(chains):
            hc = A_REP // chains
            q = q_ref[0, c * hc:(c + 1) * hc].reshape(hc * tq, HEAD_DIM)
            kc = min(GQA_KEY_CHUNK, n_keys)
            m = l = acc = None
            for j in range(n_keys // kc):
                keys = slice(key0 + j * kc, key0 + (j + 1) * kc)
                s = _dot_nt(q, k_ref[0, 0, keys, :])
                mj = jnp.max(s, -1, keepdims=True)
                if j == 0:
                    m = mj
                    p = jnp.exp(s - m)
                    l = jnp.sum(p, -1, keepdims=True)
                    acc = _dot(p.astype(BF16), v_ref[0, 0, keys, :])
                else:
                    m_new = jnp.maximum(m, mj)
                    alpha = jnp.exp(m - m_new)
                    p = jnp.exp(s - m_new)
                    l = alpha * l + jnp.sum(p, -1, keepdims=True)
                    acc = alpha * acc + _dot(p.astype(BF16), v_ref[0, 0, keys, :])
                    m = m_new
            o = acc / l
            for h in range(hc):
                hh = c * hc + h
                o_ref[0, :, hh * HEAD_DIM:(hh + 1) * HEAD_DIM] = o[h * tq:(h + 1) * tq].astype(BF16)

    @pl.when(i < lat_tiles)
    def _():
        attend(0, k_ref.shape[2])

    if with_ctx_queries:
        @pl.when(i >= lat_tiles)
        def _():
            attend(SEQ, CTX_LEN)


def _gqa(qa, ka, va, s_out):
    bsz, _, s, _ = qa.shape
    tq, chains = (512, 4) if s_out == SEQ else (GQA_Q_TILE, GQA_CHAINS)
    body = functools.partial(_gqa_body, tq=tq, chains=chains, lat_tiles=SEQ // tq, with_ctx_queries=s_out > SEQ)
    return pl.pallas_call(
        body,
        grid=(bsz, A_KV_HEADS, s_out // tq),
        in_specs=[
            pl.BlockSpec((1, A_REP, tq, HEAD_DIM), lambda b, g, i: (b, g, i, 0)),
            pl.BlockSpec((1, 1, s, HEAD_DIM), lambda b, g, i: (b, g, 0, 0)),
            pl.BlockSpec((1, 1, s, HEAD_DIM), lambda b, g, i: (b, g, 0, 0)),
        ],
        out_specs=pl.BlockSpec((1, tq, A_REP * HEAD_DIM), lambda b, g, i: (b, i, g)),
        out_shape=jax.ShapeDtypeStruct((bsz, s_out, A_W), BF16),
        compiler_params=_params("parallel", "parallel", "parallel"),
        name="gqa_attn",
    )(qa, ka, va)


NA_HEADS_PER_STEP = 2
NA_Q_ROWS = 8
NA_K_ROWS = 16
NA_Q_TOKENS = NA_Q_ROWS * GRID_W
NA_K_TOKENS = NA_K_ROWS * GRID_W
NA_Q_BLOCKS = GRID_ROWS // NA_Q_ROWS
NA_KEY_ROW0 = (0, 4, 12, 16)
NA_PATTERN_BLOCKS = (0, 1, 3)


def _na_body(q_ref, k_ref, v_ref, bias_ref, o_ref):
    j = pl.program_id(2)
    key_row0 = jnp.where(j == 0, NA_KEY_ROW0[0],
                         jnp.where(j == 1, NA_KEY_ROW0[1], jnp.where(j == 2, NA_KEY_ROW0[2], NA_KEY_ROW0[3])))
    k0 = pl.multiple_of(key_row0 * GRID_W, 4 * GRID_W)
    for hh in range(NA_HEADS_PER_STEP):
        q = q_ref[0, hh]
        kw = k_ref[0, hh, pl.ds(k0, NA_K_TOKENS), :]
        vw = v_ref[0, hh, pl.ds(k0, NA_K_TOKENS), :]
        kc = k_ref[0, hh, SEQ:, :]
        vc = v_ref[0, hh, SEQ:, :]
        sw = _dot_nt(q, kw) + bias_ref[hh, 0]
        sc = _dot_nt(q, kc)
        m = jnp.maximum(jnp.max(sw, -1, keepdims=True), jnp.max(sc, -1, keepdims=True))
        pw = jnp.exp(sw - m)
        pc = jnp.exp(sc - m)
        l = jnp.sum(pw, -1, keepdims=True) + jnp.sum(pc, -1, keepdims=True)
        o = (_dot(pw.astype(BF16), vw) + _dot(pc.astype(BF16), vc)) / l
        o_ref[0, :, hh * HEAD_DIM:(hh + 1) * HEAD_DIM] = o.astype(BF16)


def _na(qb, kb, vb, bias):
    bsz, _, s, _ = qb.shape
    hp = NA_HEADS_PER_STEP
    kv = pl.BlockSpec((1, hp, s, HEAD_DIM), lambda b, h, j: (b, h, 0, 0))
    return pl.pallas_call(
        _na_body,
        grid=(bsz, B_HEADS // hp, NA_Q_BLOCKS),
        in_specs=[pl.BlockSpec((1, hp, NA_Q_TOKENS, HEAD_DIM), lambda b, h, j: (b, h, j, 0)), kv, kv,
                  pl.BlockSpec((hp, 1, NA_Q_TOKENS, NA_K_TOKENS), lambda b, h, j: (h, (j + 1) // 2, 0, 0))],
        out_specs=pl.BlockSpec((1, NA_Q_TOKENS, hp * HEAD_DIM), lambda b, h, j: (b, j, h)),
        out_shape=jax.ShapeDtypeStruct((bsz, SEQ, B_W), BF16),
        compiler_params=_params("parallel", "parallel", "arbitrary"),
        name="nbr_attn",
    )(qb, kb, vb, bias)


def _na_bias_body(t_ref, o_ref):
    o_ref[...] = jnp.full(o_ref.shape, NEG_BIG, F32)
    for p, blk in enumerate(NA_PATTERN_BLOCKS):
        @pl.when(pl.program_id(1) == p)
        def _(blk=blk):
            for qi in range(NA_Q_ROWS):
                qrow = blk * NA_Q_ROWS + qi
                rs = min(max(qrow - WIN_ROWS // 2, 0), GRID_ROWS - WIN_ROWS)
                for krow in range(rs, rs + WIN_ROWS):
                    ki = krow - NA_KEY_ROW0[blk]
                    o_ref[0, 0, qi * GRID_W:(qi + 1) * GRID_W, ki * GRID_W:(ki + 1) * GRID_W] = (
                        t_ref[0, krow - qrow + WIN_ROWS - 1])


def _na_bias_table(rpb):
    col = np.arange(GRID_W)
    col_start = np.clip(col - WIN_COLS // 2, 0, GRID_W - WIN_COLS)
    in_cols = (col[None, :] >= col_start[:, None]) & (col[None, :] < col_start[:, None] + WIN_COLS)
    dc = np.clip(col[None, :] - col[:, None] + WIN_COLS - 1, 0, 2 * WIN_COLS - 2)
    onehot = jnp.asarray(dc[None] == np.arange(2 * WIN_COLS - 1)[:, None, None], F32)
    tiles = jnp.einsum('hdj,jqk->hdqk', rpb.astype(F32), onehot, precision=HIGHEST)
    tiles = jnp.where(in_cols[None, None], tiles, NEG_BIG)
    n_pat = len(NA_PATTERN_BLOCKS)
    return pl.pallas_call(
        _na_bias_body,
        grid=(B_HEADS, n_pat),
        in_specs=[pl.BlockSpec((1, 2 * WIN_ROWS - 1, GRID_W, GRID_W), lambda h, p: (h, 0, 0, 0))],
        out_specs=pl.BlockSpec((1, 1, NA_Q_TOKENS, NA_K_TOKENS), lambda h, p: (h, p, 0, 0)),
        out_shape=jax.ShapeDtypeStruct((B_HEADS, n_pat, NA_Q_TOKENS, NA_K_TOKENS), F32),
        compiler_params=_params("parallel", "arbitrary"),
        name="nbr_bias",
    )(tiles)


def _ctx_attn_body(q_ref, k_ref, v_ref, o_ref):
    for hh in range(NA_HEADS_PER_STEP):
        s = _dot_nt(q_ref[0, hh], k_ref[0, hh])
        m = jnp.max(s, -1, keepdims=True)
        p = jnp.exp(s - m)
        l = jnp.sum(p, -1, keepdims=True)
        o = _dot(p.astype(BF16), v_ref[0, hh]) / l
        o_ref[0, :, hh * HEAD_DIM:(hh + 1) * HEAD_DIM] = o.astype(BF16)


def _ctx_attn(qb, kb, vb):
    bsz = qb.shape[0]
    hp = NA_HEADS_PER_STEP
    seg = pl.BlockSpec((1, hp, CTX_LEN, HEAD_DIM), lambda b, h: (b, h, SEQ // CTX_LEN, 0))
    return pl.pallas_call(
        _ctx_attn_body,
        grid=(bsz, B_HEADS // hp),
        in_specs=[seg, seg, seg],
        out_specs=pl.BlockSpec((1, CTX_LEN, hp * HEAD_DIM), lambda b, h: (b, 0, h)),
        out_shape=jax.ShapeDtypeStruct((bsz, CTX_LEN, B_W), BF16),
        compiler_params=_params("parallel", "parallel"),
        name="ctx_attn",
    )(qb, kb, vb)


DFT_FINE = 16


def _dft_tables(n):
    s = jnp.arange(n, dtype=jnp.int32)[None, :]

    def cos_sin(f):
        ang = ((f * s) % (2 * n)).astype(F32) * (math.pi / n)
        return jnp.cos(ang), jnp.sin(ang)

    ch, sh = cos_sin(jnp.arange(n // DFT_FINE, dtype=jnp.int32)[:, None] * DFT_FINE)
    cl, sl = cos_sin(jnp.arange(DFT_FINE, dtype=jnp.int32)[:, None])
    fc = (ch[:, None] * cl[None] - sh[:, None] * sl[None]).reshape(n, n)
    sym = (sh[:, None] * cl[None] + ch[:, None] * sl[None]).reshape(n, n)
    alt = jnp.where(jnp.arange(n) % 2 == 0, 1.0, -1.0).astype(F32)
    first = jnp.arange(n) == 0
    fs = jnp.where(first[:, None], alt[None, :], sym)
    fst = jnp.where(first[None, :], alt[:, None], sym)
    return fc.astype(BF16), fs.astype(BF16), fst.astype(BF16)


def _split_bf16(v):
    hi = v.astype(BF16)
    lo = (v - hi.astype(F32)).astype(BF16)
    return hi, lo


def _spectrum_body(fc_ref, fs_ref, kp_ref, km_ref, kc_ref, ks_ref, *, tf):
    kp_hi, kp_lo = _split_bf16(kp_ref[...])
    km_hi, km_lo = _split_bf16(km_ref[...])
    fc = fc_ref[...]
    fs = fs_ref[...]
    kc_ref[...] = _dot(fc, kp_hi) + _dot(fc, kp_lo)
    ks_sin = _dot(fs, km_hi) + _dot(fs, km_lo)
    ks_nyq = _dot(fs, kp_hi) + _dot(fs, kp_lo)
    row = lax.broadcasted_iota(jnp.int32, ks_sin.shape, 0) + pl.program_id(0) * tf
    ks_ref[...] = jnp.where(row == 0, ks_nyq, ks_sin)


def _filter_spectrum(fc, fs, kplus, kminus):
    n = fc.shape[0]
    tf = min(FREQ_TILE, n)
    tab = pl.BlockSpec((tf, n), lambda j: (j, 0))
    filt = pl.BlockSpec((n, C_CH), lambda j: (0, 0))
    out = pl.BlockSpec((tf, C_CH), lambda j: (j, 0))
    return pl.pallas_call(
        functools.partial(_spectrum_body, tf=tf),
        grid=(n // tf,),
        in_specs=[tab, tab, filt, filt],
        out_specs=[out, out],
        out_shape=[jax.ShapeDtypeStruct((n, C_CH), F32)] * 2,
        compiler_params=_params("parallel"),
        name="hyena_spectrum",
    )(fc, fs, kplus, kminus)


def _hyena_body(z_ref, cw_ref, cb_ref, skip_ref, fc_ref, fs_ref, fct_ref, fst_ref, kc_ref, ks_ref,
                o_ref, ub_ref, uf_ref, x0_ref, y_ref, *, n, tf):
    j = pl.program_id(1)

    @pl.when(j == 0)
    def _():
        z = z_ref[0]
        t = lax.broadcasted_iota(jnp.int32, z.shape, 0)
        prev = jnp.where(t == 0, 0.0, pltpu.roll(z, 1, 0))
        nxt = jnp.where(t == n - 1, 0.0, pltpu.roll(z, n - 1, 0))
        zc = prev * cw_ref[0:1, :] + z * cw_ref[1:2, :] + nxt * cw_ref[2:3, :] + cb_ref[...]
        u = zc[:, 2 * C_CH:] * zc[:, C_CH:2 * C_CH]
        x0_ref[...] = zc[:, :C_CH]
        uf_ref[...] = u
        ub_ref[...] = u.astype(BF16)
        y_ref[...] = jnp.zeros_like(y_ref)

    ub = ub_ref[...]
    uc = _dot(fc_ref[...], ub)
    us = _dot(fs_ref[...], ub)
    kc = kc_ref[...]
    ks = ks_ref[...]
    row = lax.broadcasted_iota(jnp.int32, uc.shape, 0) + j * tf
    first = row == 0
    wgt = jnp.where(first, 1.0, 2.0) * (1.0 / (2 * n))
    re = jnp.where(first, uc * kc, uc * kc - us * ks) * wgt
    im = jnp.where(first, us * ks, uc * ks + us * kc) * wgt
    y_ref[...] += _dot(fct_ref[...], re.astype(BF16)) + _dot(fst_ref[...], im.astype(BF16))

    @pl.when(j == pl.num_programs(1) - 1)
    def _():
        o_ref[0] = ((y_ref[...] + uf_ref[...] * skip_ref[...]) * x0_ref[...]).astype(BF16)


def _hyena(z, seg_start, n, cw, cb, skip, tables, kc, ks):
    bsz = z.shape[0]
    fc, fs, fst = tables
    tf = min(FREQ_TILE, n)
    seg = seg_start // n
    row_blk = pl.BlockSpec((tf, n), lambda b, j: (j, 0))
    col_blk = pl.BlockSpec((n, tf), lambda b, j: (0, j))
    spec_blk = pl.BlockSpec((tf, C_CH), lambda b, j: (j, 0))
    small = lambda r, c: pl.BlockSpec((r, c), lambda b, j: (0, 0))
    return pl.pallas_call(
        functools.partial(_hyena_body, n=n, tf=tf),
        grid=(bsz, n // tf),
        in_specs=[pl.BlockSpec((1, n, 3 * C_CH), lambda b, j: (b, seg, 0)),
                  small(3, 3 * C_CH), small(1, 3 * C_CH), small(1, C_CH),
                  row_blk, row_blk, col_blk, col_blk, spec_blk, spec_blk],
        out_specs=pl.BlockSpec((1, n, C_CH), lambda b, j: (b, 0, 0)),
        out_shape=jax.ShapeDtypeStruct((bsz, n, C_CH), BF16),
        scratch_shapes=[pltpu.VMEM((n, C_CH), BF16), pltpu.VMEM((n, C_CH), F32),
                        pltpu.VMEM((n, C_CH), F32), pltpu.VMEM((n, C_CH), F32)],
        compiler_params=_params("parallel", "arbitrary"),
        name="hyena_conv",
    )(z, cw, cb, skip, fc, fs, fc, fst, kc, ks)


def _implicit_filters(n, w1, b1, w2, b2, w3, freq):
    t = jnp.linspace(0.0, 1.0, n, dtype=F32)[:, None]
    w = 2 * math.pi * jnp.arange(n, dtype=F32)[:, None] / n
    f = jnp.linspace(1e-4, C_POS_BANDS - 1, C_POS_BANDS, dtype=F32)[None]
    z = jnp.concatenate([t, jnp.cos(f * w), -jnp.sin(f * w)], -1)
    h = jnp.sin(freq[0] * (jnp.dot(z, w1, precision=HIGHEST) + b1))
    h = jnp.sin(freq[1] * (jnp.dot(h, w2, precision=HIGHEST) + b2))
    h = jnp.dot(h, w3, precision=HIGHEST)
    lo = math.log(C_DECAY_TARGET) / C_SLOW_DECAY
    hi = math.log(C_DECAY_TARGET) / C_FAST_DECAY
    deltas = jnp.abs(jnp.linspace(lo, hi, C_CH, dtype=F32))
    decay = jnp.exp(-t * deltas[None])
    hf, hb = h[:, :C_CH] * decay, h[:, C_CH:] * decay
    hb = hb.at[0].set(0.0)
    return hf + hb, hf - hb


def _router_gates(logits_t, rb):
    tm = logits_t.shape[1]
    scores = _sigmoid(logits_t)
    sel = scores + rb
    sel3 = sel.reshape(N_GROUPS, GROUP_SIZE, tm)
    kk = lax.broadcasted_iota(jnp.int32, sel3.shape, 1)
    m1 = jnp.max(sel3, 1, keepdims=True)
    i1 = jnp.min(jnp.where(sel3 == m1, kk, GROUP_SIZE), 1, keepdims=True)
    m2 = jnp.max(jnp.where(kk == i1, -jnp.inf, sel3), 1, keepdims=True)
    gscore = (m1 + m2).reshape(N_GROUPS, tm)
    gid = lax.broadcasted_iota(jnp.int32, gscore.shape, 0)
    rank = jnp.zeros(gscore.shape, jnp.int32)
    for g in range(N_GROUPS):
        other = gscore[g:g + 1, :]
        ahead = (other > gscore) | ((other == gscore) & (g < gid))
        rank = rank + ahead.astype(jnp.int32)
    gsel = (rank < TOPK_GROUPS).reshape(N_GROUPS, 1, tm)
    cand = jnp.where(gsel, sel3, -jnp.inf).reshape(N_EXPERTS, tm)
    eid = lax.broadcasted_iota(jnp.int32, cand.shape, 0)
    chosen = jnp.zeros(cand.shape, jnp.bool_)
    for _ in range(TOP_K):
        m = jnp.max(cand, 0, keepdims=True)
        pick = eid == jnp.min(jnp.where(cand == m, eid, N_EXPERTS), 0, keepdims=True)
        chosen = chosen | pick
        cand = jnp.where(pick, -jnp.inf, cand)
    wsel = jnp.where(chosen, scores, 0.0)
    return wsel / jnp.sum(wsel, 0, keepdims=True) * ROUTED_SCALE


def _out_body(oa_ref, ob_ref, oc_ref, x_ref, mod_ref, wa_ref, wb_ref, wc_ref, lng_ref, lnb_ref,
              rwt_ref, rb_ref, xmid_ref, h_ref, gate_ref, *, tm):
    for rows, is_ctx in _sub_tiles(tm):
        attn = (_dot(oa_ref[0, rows, :], wa_ref[...]) + _dot(ob_ref[0, rows, :], wb_ref[...])
                + _dot(oc_ref[0, rows, :], wc_ref[...]))
        y = DN_ALPHA * x_ref[0, rows, :] + _mod_row(mod_ref, is_ctx, 2) * attn
        xm = _layer_norm(y) * lng_ref[...] + lnb_ref[...]
        xmid_ref[0, rows, :] = xm
        h = _layer_norm(xm) * (1.0 + _mod_row(mod_ref, is_ctx, 4)) + _mod_row(mod_ref, is_ctx, 3)
        hi, lo = _split_bf16(h)
        h_ref[0, rows, :] = hi
        logits_t = _dot_nt(rwt_ref[...], jnp.concatenate([hi, lo, hi], axis=1))
        gates_t = _router_gates(logits_t, rb_ref[...])
        pad = jnp.zeros((128 - N_EXPERTS, gates_t.shape[1]), F32)
        gates = jnp.concatenate([gates_t, pad], axis=0).T
        for g in range(N_GROUPS):
            gate_ref[0, g, rows, :] = gates[:, g * GROUP_SIZE:(g + 1) * GROUP_SIZE]


def _out_proj(oa, ob, oc, x, mod, wa, wb, wc, lng, lnb, rwt3, rb):
    bsz, s, _ = oa.shape
    d = D_MODEL
    tm = _step_tokens(s)
    tok = lambda w: pl.BlockSpec((1, tm, w), lambda b, i: (b, i, 0))
    full = lambda r, c: pl.BlockSpec((r, c), lambda b, i: (0, 0))
    return pl.pallas_call(
        functools.partial(_out_body, tm=tm),
        grid=(bsz, s // tm),
        in_specs=[tok(A_W), tok(B_W), tok(C_CH), tok(d),
                  pl.BlockSpec((1, 2, 6, d), lambda b, i: (b, 0, 0, 0)),
                  full(A_W, d), full(B_W, d), full(C_CH, d), full(1, d), full(1, d),
                  full(N_EXPERTS, 3 * d), full(N_EXPERTS, 1)],
        out_specs=[tok(d), tok(d),
                   pl.BlockSpec((1, N_GROUPS, tm, GROUP_SIZE), lambda b, i: (b, 0, i, 0))],
        out_shape=[jax.ShapeDtypeStruct((bsz, s, d), F32),
                   jax.ShapeDtypeStruct((bsz, s, d), BF16),
                   jax.ShapeDtypeStruct((bsz, N_GROUPS, s, GROUP_SIZE), F32)],
        compiler_params=_params("parallel", "parallel"),
        name="proj_out_router",
    )(oa, ob, oc, x, mod, wa, wb, wc, lng, lnb, rwt3, rb)


MOE_CHAINS = 2


def _moe_body(h_ref, gate_ref, xmid_ref, mod_ref, w1_ref, w3_ref, w2_ref,
              sw1_ref, sw3_ref, sw2_ref, lng_ref, lnb_ref, o_ref, acc_ref, *, tm):
    g = pl.program_id(2)
    rc = tm // MOE_CHAINS

    @pl.when(g == 0)
    def _():
        h = h_ref[0]
        a = _dot(h, sw1_ref[...])
        act = (a * _sigmoid(a) * _dot(h, sw3_ref[...])).astype(BF16)
        acc_ref[...] = _dot(act, sw2_ref[...])

    w1 = jnp.concatenate([w1_ref[k] for k in range(GROUP_SIZE)], axis=1)
    w3 = jnp.concatenate([w3_ref[k] for k in range(GROUP_SIZE)], axis=1)
    w2 = w2_ref[...].reshape(GROUP_SIZE * EXPERT_HIDDEN, D_MODEL)
    for c in range(MOE_CHAINS):
        rows = slice(c * rc, (c + 1) * rc)
        h = h_ref[0, rows, :]
        a = _dot(h, w1)
        act = a * _sigmoid(a) * _dot(h, w3)
        gt = gate_ref[0, 0, rows, :]
        parts = [act[:, k * EXPERT_HIDDEN:(k + 1) * EXPERT_HIDDEN] * gt[:, k:k + 1]
                 for k in range(GROUP_SIZE)]
        acc_ref[rows, :] += _dot(jnp.concatenate(parts, axis=1).astype(BF16), w2)

    @pl.when(g == N_GROUPS - 1)
    def _():
        row = lax.broadcasted_iota(jnp.int32, (tm, 1), 0) + pl.program_id(1) * tm
        gate2 = jnp.where(row >= SEQ, mod_ref[0, 1, 5:6, :], mod_ref[0, 0, 5:6, :])
        y = DN_ALPHA * xmid_ref[0] + gate2 * acc_ref[...]
        o_ref[0] = _layer_norm(y) * lng_ref[...] + lnb_ref[...]


def _moe(h, gates, xmid, mod, layer, w1_all, w3_all, w2_all, sw1, sw3, sw2, lng, lnb, tm):
    bsz, s, d = h.shape
    tok = pl.BlockSpec((1, tm, d), lambda b, i, g: (b, i, 0))
    tok_once = pl.BlockSpec((1, tm, d), lambda b, i, g: (b, i, 0), pipeline_mode=pl.Buffered(1))
    full = lambda r, c: pl.BlockSpec((r, c), lambda b, i, g: (0, 0))
    up = pl.BlockSpec((None, GROUP_SIZE, d, EXPERT_HIDDEN), lambda b, i, g: (layer, g, 0, 0))
    return pl.pallas_call(
        functools.partial(_moe_body, tm=tm),
        grid=(bsz, s // tm, N_GROUPS),
        in_specs=[tok,
                  pl.BlockSpec((1, 1, tm, GROUP_SIZE), lambda b, i, g: (b, g, i, 0)),
                  tok_once,
                  pl.BlockSpec((1, 2, 6, d), lambda b, i, g: (b, 0, 0, 0)),
                  up, up,
                  pl.BlockSpec((None, GROUP_SIZE, EXPERT_HIDDEN, d), lambda b, i, g: (layer, g, 0, 0)),
                  full(d, SHARED_HIDDEN), full(d, SHARED_HIDDEN), full(SHARED_HIDDEN, d),
                  full(1, d), full(1, d)],
        out_specs=tok,
        out_shape=jax.ShapeDtypeStruct((bsz, s, d), F32),
        scratch_shapes=[pltpu.VMEM((tm, d), F32)],
        compiler_params=_params("parallel", "parallel", "arbitrary"),
        name="moe_ffn",
    )(h, gates, xmid, mod, w1_all, w3_all, w2_all, sw1, sw3, sw2, lng, lnb)


def _rope_tables():
    t = jnp.arange(SEQ)
    row = (t // GRID_W).astype(F32)
    col = (t % GRID_W).astype(F32)
    inv = ROPE_THETA ** (-jnp.arange(0, HALF, 2, dtype=F32) / HALF)
    ang = jnp.concatenate([row[:, None] * inv, col[:, None] * inv], -1)
    cos, sin = jnp.cos(ang), jnp.sin(ang)
    n_heads = ROPE_TABLE_W // HEAD_DIM
    cos_t = jnp.concatenate([jnp.tile(jnp.concatenate([cos, cos], -1), (1, n_heads)),
                             jnp.ones((CTX_LEN, ROPE_TABLE_W), F32)], 0)
    sin_t = jnp.concatenate([jnp.tile(jnp.concatenate([-sin, sin], -1), (1, n_heads)),
                             jnp.zeros((CTX_LEN, ROPE_TABLE_W), F32)], 0)
    return cos_t, sin_t


def _modulation(c, c_ctx, w, b):
    cvec = jnp.concatenate([c, c_ctx[None]], axis=0)
    m = (jnp.dot(jax.nn.silu(cvec), w, precision=HIGHEST) + b).reshape(-1, 6, D_MODEL)
    return jnp.stack([m[:-1], jnp.broadcast_to(m[-1], m[:-1].shape)], axis=1)


def kernel(x, c, ctx, c_ctx, ada_w, ada_b, w_in, q_gain, k_gain, nat_rpb, hy_conv_w, hy_conv_b, hy_f_w1, hy_f_b1, hy_f_w2, hy_f_b2, hy_f_w3, hy_freq, hy_skip, w_out, ln1_g, ln1_b, router_w, router_b, exp_w1, exp_w3, exp_w2, sh_w1, sh_w3, sh_w2, ln2_g, ln2_b):
    bsz = x.shape[0]
    d = D_MODEL
    xs = jnp.concatenate([x, ctx], axis=1)

    head_perm = np.concatenate([np.arange(0, HEAD_DIM, 2), np.arange(1, HEAD_DIM, 2)])
    qk_perm = (np.arange(QK_W // HEAD_DIM)[:, None] * HEAD_DIM + head_perm[None, :]).reshape(-1)
    perm_mat = jnp.asarray(np.eye(QK_W)[:, qk_perm], BF16)
    cos_t, sin_t = _rope_tables()
    bd = jnp.asarray(np.kron(np.eye(QK_W // HEAD_DIM), np.ones((HEAD_DIM, HEAD_DIM))), BF16)
    tables_lat = _dft_tables(SEQ)
    tables_ctx = _dft_tables(CTX_LEN)
    w1_bf, w3_bf, w2_bf = exp_w1.astype(BF16), exp_w3.astype(BF16), exp_w2.astype(BF16)

    for l in range(DEPTH):
        last = l == DEPTH - 1
        mod = _modulation(c, c_ctx, ada_w[l], ada_b[l])

        w_bf = w_in[l].astype(BF16)
        w_qk = jnp.dot(w_bf[:, :QK_W], perm_mat, preferred_element_type=F32).astype(BF16)
        w_bf = jnp.concatenate([w_qk, w_bf[:, QK_W:]], axis=1)
        gain = jnp.concatenate([jnp.tile(q_gain[l][head_perm] * ATT_SCALE, A_HEADS),
                                jnp.tile(k_gain[l][head_perm], A_KV_HEADS)])[None].astype(F32)
        qa, ka, va, qb, kb, vb, z = _project(xs, mod, w_bf, bd, gain, cos_t, sin_t)

        s_out = SEQ if last else S_ALL
        oa = _gqa(qa, ka, va, s_out)
        ob = _na(qb, kb, vb, _na_bias_table(nat_rpb[l]))
        if not last:
            ob = jnp.concatenate([ob, _ctx_attn(qb, kb, vb)], axis=1)

        filt = (hy_f_w1[l], hy_f_b1[l], hy_f_w2[l], hy_f_b2[l], hy_f_w3[l], hy_freq[l])
        cw, cb, skip = hy_conv_w[l], hy_conv_b[l][None], hy_skip[l][None]
        kc, ks = _filter_spectrum(tables_lat[0], tables_lat[1], *_implicit_filters(SEQ, *filt))
        oc = _hyena(z, 0, SEQ, cw, cb, skip, tables_lat, kc, ks)
        if not last:
            kc, ks = _filter_spectrum(tables_ctx[0], tables_ctx[1], *_implicit_filters(CTX_LEN, *filt))
            oc = jnp.concatenate([oc, _hyena(z, SEQ, CTX_LEN, cw, cb, skip, tables_ctx, kc, ks)], axis=1)

        wo = w_out[l].astype(BF16)
        rw_hi, rw_lo = _split_bf16(router_w[l].T)
        rwt3 = jnp.concatenate([rw_hi, rw_hi, rw_lo], axis=1)
        xmid, h, gates = _out_proj(oa, ob, oc, xs, mod, wo[:A_W], wo[A_W:A_W + B_W], wo[A_W + B_W:],
                                   ln1_g[l][None], ln1_b[l][None], rwt3, router_b[l][:, None])

        xs = _moe(h, gates, xmid, mod, l, w1_bf, w3_bf, w2_bf,
                  sh_w1[l].astype(BF16), sh_w3[l].astype(BF16), sh_w2[l].astype(BF16),
                  ln2_g[l][None], ln2_b[l][None], tm=h.shape[1] // 2)
    return xs
```

```python
import functools
import math

import numpy as np
import jax
import jax.numpy as jnp
from jax import lax
from jax.experimental import pallas as pl
from jax.experimental.pallas import tpu as pltpu

F32 = jnp.float32
BF16 = jnp.bfloat16
HIGHEST = lax.Precision.HIGHEST

D_MODEL = 1024
SEQ = 2048
DEPTH = 2
GRID_W = 64
GRID_ROWS = SEQ // GRID_W
CTX_LEN = 256
S_ALL = SEQ + CTX_LEN
HEAD_DIM = 64
HALF = HEAD_DIM // 2
A_HEADS = 8
A_KV_HEADS = 2
A_REP = A_HEADS // A_KV_HEADS
ROPE_THETA = 10000.0
B_HEADS = 4
WIN_ROWS = 8
WIN_COLS = 16
C_CH = 256
C_POS_BANDS = 16
C_DECAY_TARGET = 1e-2
C_FAST_DECAY = 0.3
C_SLOW_DECAY = 1.5
A_W = A_HEADS * HEAD_DIM
A_KV_W = A_KV_HEADS * HEAD_DIM
B_W = B_HEADS * HEAD_DIM
A_K0 = A_W
A_V0 = A_K0 + A_KV_W
B_Q0 = A_V0 + A_KV_W
B_K0 = B_Q0 + B_W
B_V0 = B_K0 + B_W
C0 = B_V0 + B_W
PROJ_W = C0 + 3 * C_CH
QK_W = A_W + A_KV_W
N_EXPERTS = 64
TOP_K = 8
N_GROUPS = 8
GROUP_SIZE = N_EXPERTS // N_GROUPS
TOPK_GROUPS = 4
EXPERT_HIDDEN = 128
SHARED_HIDDEN = 256
ROUTED_SCALE = 2.5
DN_ALPHA = (2 * DEPTH) ** 0.25
LN_EPS = 1e-6
RMS_EPS = 1e-6
ATT_SCALE = HEAD_DIM ** -0.5
NEG_BIG = -1e30

VMEM_LIMIT_BYTES = 56 * 1024 * 1024
TOKEN_TILE = 256
ROPE_TABLE_W = 128
GQA_Q_TILE = 256
GQA_CHAINS = 2
GQA_KEY_CHUNK = 768
FREQ_TILE = 256


def _params(*sem):
    return pltpu.CompilerParams(dimension_semantics=sem, vmem_limit_bytes=VMEM_LIMIT_BYTES)


def _layer_norm(v):
    mu = jnp.mean(v, -1, keepdims=True)
    vc = v - mu
    var = jnp.mean(vc * vc, -1, keepdims=True)
    return vc * lax.rsqrt(var + LN_EPS)


def _sigmoid(v):
    return 1.0 / (1.0 + jnp.exp(-v))


def _dot(a, b):
    return jnp.dot(a, b, preferred_element_type=F32)


def _dot_nt(a, b):
    return lax.dot_general(a, b, (((1,), (1,)), ((), ())), preferred_element_type=F32)


def _mod_row(mod_ref, is_ctx, idx):
    return jnp.where(is_ctx, mod_ref[0, 1, idx:idx + 1, :], mod_ref[0, 0, idx:idx + 1, :])


def _sub_tiles(tm):
    for k in range(tm // TOKEN_TILE):
        rows = slice(k * TOKEN_TILE, (k + 1) * TOKEN_TILE)
        is_ctx = (pl.program_id(1) * tm + k * TOKEN_TILE) >= SEQ
        yield rows, is_ctx


def _proj_body(x_ref, mod_ref, w_ref, bd_ref, gain_ref, cos_ref, sin_ref,
               qa_ref, ka_ref, va_ref, qb_ref, kb_ref, vb_ref, z_ref, *, tm):
    n_rep = QK_W // ROPE_TABLE_W
    for rows, is_ctx in _sub_tiles(tm):
        xn = _layer_norm(x_ref[0, rows, :])
        hx = (xn * (1.0 + _mod_row(mod_ref, is_ctx, 1)) + _mod_row(mod_ref, is_ctx, 0)).astype(BF16)
        px = _dot(hx, w_ref[...])
        qk = px[:, :QK_W]
        ss = _dot((qk * qk).astype(BF16), bd_ref[...])
        qkn = qk * lax.rsqrt(ss * (1.0 / HEAD_DIM) + RMS_EPS) * gain_ref[...]
        lane = lax.broadcasted_iota(jnp.int32, qkn.shape, 1)
        partner = jnp.where((lane % HEAD_DIM) < HALF,
                            pltpu.roll(qkn, QK_W - HALF, 1), pltpu.roll(qkn, HALF, 1))
        cos = jnp.concatenate([cos_ref[rows, :]] * n_rep, axis=1)
        sin = jnp.concatenate([sin_ref[rows, :]] * n_rep, axis=1)
        qkr = (qkn * cos + partner * sin).astype(BF16)
        for h in range(A_HEADS):
            qa_ref[0, h, rows, :] = qkr[:, h * HEAD_DIM:(h + 1) * HEAD_DIM]
        for g in range(A_KV_HEADS):
            ka_ref[0, g, rows, :] = qkr[:, A_K0 + g * HEAD_DIM:A_K0 + (g + 1) * HEAD_DIM]
            va_ref[0, g, rows, :] = px[:, A_V0 + g * HEAD_DIM:A_V0 + (g + 1) * HEAD_DIM].astype(BF16)
        for h in range(B_HEADS):
            qb_ref[0, h, rows, :] = (px[:, B_Q0 + h * HEAD_DIM:B_Q0 + (h + 1) * HEAD_DIM] * ATT_SCALE).astype(BF16)
            kb_ref[0, h, rows, :] = px[:, B_K0 + h * HEAD_DIM:B_K0 + (h + 1) * HEAD_DIM].astype(BF16)
            vb_ref[0, h, rows, :] = px[:, B_V0 + h * HEAD_DIM:B_V0 + (h + 1) * HEAD_DIM].astype(BF16)
        z_ref[0, rows, :] = px[:, C0:]


def _step_tokens(s):
    return 3 * TOKEN_TILE if s % (3 * TOKEN_TILE) == 0 else 2 * TOKEN_TILE


def _project(x, mod, w_bf, bd, gain, cos_t, sin_t):
    bsz, s, d = x.shape
    tm = _step_tokens(s)
    nt = s // tm
    heads = lambda n: pl.BlockSpec((1, n, tm, HEAD_DIM), lambda b, i: (b, 0, i, 0))
    hshape = lambda n: jax.ShapeDtypeStruct((bsz, n, s, HEAD_DIM), BF16)
    return pl.pallas_call(
        functools.partial(_proj_body, tm=tm),
        grid=(bsz, nt),
        in_specs=[
            pl.BlockSpec((1, tm, d), lambda b, i: (b, i, 0)),
            pl.BlockSpec((1, 2, 6, d), lambda b, i: (b, 0, 0, 0)),
            pl.BlockSpec((d, PROJ_W), lambda b, i: (0, 0)),
            pl.BlockSpec((QK_W, QK_W), lambda b, i: (0, 0)),
            pl.BlockSpec((1, QK_W), lambda b, i: (0, 0)),
            pl.BlockSpec((tm, ROPE_TABLE_W), lambda b, i: (i, 0)),
            pl.BlockSpec((tm, ROPE_TABLE_W), lambda b, i: (i, 0)),
        ],
        out_specs=[heads(A_HEADS), heads(A_KV_HEADS), heads(A_KV_HEADS),
                   heads(B_HEADS), heads(B_HEADS), heads(B_HEADS),
                   pl.BlockSpec((1, tm, 3 * C_CH), lambda b, i: (b, i, 0))],
        out_shape=[hshape(A_HEADS), hshape(A_KV_HEADS), hshape(A_KV_HEADS),
                   hshape(B_HEADS), hshape(B_HEADS), hshape(B_HEADS),
                   jax.ShapeDtypeStruct((bsz, s, 3 * C_CH), F32)],
        compiler_params=_params("parallel", "parallel"),
        name="proj_in",
    )(x, mod, w_bf, bd, gain, cos_t, sin_t)


def _gqa_body(q_ref, k_ref, v_ref, o_ref, *, tq, chains, lat_tiles, with_ctx_queries):
    i = pl.program_id(2)

    def attend(key0, n_keys):
        for c in range(chains):
            hc = A_REP // chains
            q = q_ref[0, c * hc:(c + 1) * hc].reshape(hc * tq, HEAD_DIM)
            kc = min(GQA_KEY_CHUNK, n_keys)
            m = l = acc = None
            for j in range(n_keys // kc):
                keys = slice(key0 + j * kc, key0 + (j + 1) * kc)
                s = _dot_nt(q, k_ref[0, 0, keys, :])
                mj = jnp.max(s, -1, keepdims=True)
                if j == 0:
                    m = mj
                    p = jnp.exp(s - m)
                    l = jnp.sum(p, -1, keepdims=True)
                    acc = _dot(p.astype(BF16), v_ref[0, 0, keys, :])
                else:
                    m_new = jnp.maximum(m, mj)
                    alpha = jnp.exp(m - m_new)
                    p = jnp.exp(s - m_new)
                    l = alpha * l + jnp.sum(p, -1, keepdims=True)
                    acc = alpha * acc + _dot(p.astype(BF16), v_ref[0, 0, keys, :])
                    m = m_new
            o = acc / l
            for h in range(hc):
                hh = c * hc + h
                o_ref[0, :, hh * HEAD_DIM:(hh + 1) * HEAD_DIM] = o[h * tq:(h + 1) * tq].astype(BF16)

    @pl.when(i < lat_tiles)
    def _():
        attend(0, k_ref.shape[2])

    if with_ctx_queries:
        @pl.when(i >= lat_tiles)
        def _():
            attend(SEQ, CTX_LEN)


def _gqa(qa, ka, va, s_out):
    bsz, _, s, _ = qa.shape
    tq, chains = (512, 4) if s_out == SEQ else (GQA_Q_TILE, GQA_CHAINS)
    body = functools.partial(_gqa_body, tq=tq, chains=chains, lat_tiles=SEQ // tq, with_ctx_queries=s_out > SEQ)
    return pl.pallas_call(
        body,
        grid=(bsz, A_KV_HEADS, s_out // tq),
        in_specs=[
            pl.BlockSpec((1, A_REP, tq, HEAD_DIM), lambda b, g, i: (b, g, i, 0)),
            pl.BlockSpec((1, 1, s, HEAD_DIM), lambda b, g, i: (b, g, 0, 0)),
            pl.BlockSpec((1, 1, s, HEAD_DIM), lambda b, g, i: (b, g, 0, 0)),
        ],
        out_specs=pl.BlockSpec((1, tq, A_REP * HEAD_DIM), lambda b, g, i: (b, i, g)),
        out_shape=jax.ShapeDtypeStruct((bsz, s_out, A_W), BF16),
        compiler_params=_params("parallel", "parallel", "parallel"),
        name="gqa_attn",
    )(qa, ka, va)


NA_HEADS_PER_STEP = 2
NA_Q_ROWS = 8
NA_K_ROWS = 16
NA_Q_TOKENS = NA_Q_ROWS * GRID_W
NA_K_TOKENS = NA_K_ROWS * GRID_W
NA_Q_BLOCKS = GRID_ROWS // NA_Q_ROWS
NA_KEY_ROW0 = (0, 4, 12, 16)
NA_PATTERN_BLOCKS = (0, 1, 3)


def _na_body(q_ref, k_ref, v_ref, bias_ref, o_ref):
    j = pl.program_id(2)
    key_row0 = jnp.where(j == 0, NA_KEY_ROW0[0],
                         jnp.where(j == 1, NA_KEY_ROW0[1], jnp.where(j == 2, NA_KEY_ROW0[2], NA_KEY_ROW0[3])))
    k0 = pl.multiple_of(key_row0 * GRID_W, 4 * GRID_W)
    for hh in range(NA_HEADS_PER_STEP):
        q = q_ref[0, hh]
        kw = k_ref[0, hh, pl.ds(k0, NA_K_TOKENS), :]
        vw = v_ref[0, hh, pl.ds(k0, NA_K_TOKENS), :]
        kc = k_ref[0, hh, SEQ:, :]
        vc = v_ref[0, hh, SEQ:, :]
        sw = _dot_nt(q, kw) + bias_ref[hh, 0]
        sc = _dot_nt(q, kc)
        m = jnp.maximum(jnp.max(sw, -1, keepdims=True), jnp.max(sc, -1, keepdims=True))
        pw = jnp.exp(sw - m)
        pc = jnp.exp(sc - m)
        l = jnp.sum(pw, -1, keepdims=True) + jnp.sum(pc, -1, keepdims=True)
        o = (_dot(pw.astype(BF16), vw) + _dot(pc.astype(BF16), vc)) / l
        o_ref[0, :, hh * HEAD_DIM:(hh + 1) * HEAD_DIM] = o.astype(BF16)


def _na(qb, kb, vb, bias):
    bsz, _, s, _ = qb.shape
    hp = NA_HEADS_PER_STEP
    kv = pl.BlockSpec((1, hp, s, HEAD_DIM), lambda b, h, j: (b, h, 0, 0))
    return pl.pallas_call(
        _na_body,
        grid=(bsz, B_HEADS // hp, NA_Q_BLOCKS),
        in_specs=[pl.BlockSpec((1, hp, NA_Q_TOKENS, HEAD_DIM), lambda b, h, j: (b, h, j, 0)), kv, kv,
                  pl.BlockSpec((hp, 1, NA_Q_TOKENS, NA_K_TOKENS), lambda b, h, j: (h, (j + 1) // 2, 0, 0))],
        out_specs=pl.BlockSpec((1, NA_Q_TOKENS, hp * HEAD_DIM), lambda b, h, j: (b, j, h)),
        out_shape=jax.ShapeDtypeStruct((bsz, SEQ, B_W), BF16),
        compiler_params=_params("parallel", "parallel", "arbitrary"),
        name="nbr_attn",
    )(qb, kb, vb, bias)


def _na_bias_body(t_ref, o_ref):
    o_ref[...] = jnp.full(o_ref.shape, NEG_BIG, F32)
    for p, blk in enumerate(NA_PATTERN_BLOCKS):
        @pl.when(pl.program_id(1) == p)
        def _(blk=blk):
            for qi in range(NA_Q_ROWS):
                qrow = blk * NA_Q_ROWS + qi
                rs = min(max(qrow - WIN_ROWS // 2, 0), GRID_ROWS - WIN_ROWS)
                for krow in range(rs, rs + WIN_ROWS):
                    ki = krow - NA_KEY_ROW0[blk]
                    o_ref[0, 0, qi * GRID_W:(qi + 1) * GRID_W, ki * GRID_W:(ki + 1) * GRID_W] = (
                        t_ref[0, krow - qrow + WIN_ROWS - 1])


def _na_bias_table(rpb):
    col = np.arange(GRID_W)
    col_start = np.clip(col - WIN_COLS // 2, 0, GRID_W - WIN_COLS)
    in_cols = (col[None, :] >= col_start[:, None]) & (col[None, :] < col_start[:, None] + WIN_COLS)
    dc = np.clip(col[None, :] - col[:, None] + WIN_COLS - 1, 0, 2 * WIN_COLS - 2)
    onehot = jnp.asarray(dc[None] == np.arange(2 * WIN_COLS - 1)[:, None, None], F32)
    tiles = jnp.einsum('hdj,jqk->hdqk', rpb.astype(F32), onehot, precision=HIGHEST)
    tiles = jnp.where(in_cols[None, None], tiles, NEG_BIG)
    n_pat = len(NA_PATTERN_BLOCKS)
    return pl.pallas_call(
        _na_bias_body,
        grid=(B_HEADS, n_pat),
        in_specs=[pl.BlockSpec((1, 2 * WIN_ROWS - 1, GRID_W, GRID_W), lambda h, p: (h, 0, 0, 0))],
        out_specs=pl.BlockSpec((1, 1, NA_Q_TOKENS, NA_K_TOKENS), lambda h, p: (h, p, 0, 0)),
        out_shape=jax.ShapeDtypeStruct((B_HEADS, n_pat, NA_Q_TOKENS, NA_K_TOKENS), F32),
        compiler_params=_params("parallel", "arbitrary"),
        name="nbr_bias",
    )(tiles)


def _ctx_attn_body(q_ref, k_ref, v_ref, o_ref):
    for hh in range(NA_HEADS_PER_STEP):
        s = _dot_nt(q_ref[0, hh], k_ref[0, hh])
        m = jnp.max(s, -1, keepdims=True)
        p = jnp.exp(s - m)
        l = jnp.sum(p, -1, keepdims=True)
        o = _dot(p.astype(BF16), v_ref[0, hh]) / l
        o_ref[0, :, hh * HEAD_DIM:(hh + 1) * HEAD_DIM] = o.astype(BF16)


def _ctx_attn(qb, kb, vb):
    bsz = qb.shape[0]
    hp = NA_HEADS_PER_STEP
    seg = pl.BlockSpec((1, hp, CTX_LEN, HEAD_DIM), lambda b, h: (b, h, SEQ // CTX_LEN, 0))
    return pl.pallas_call(
        _ctx_attn_body,
        grid=(bsz, B_HEADS // hp),
        in_specs=[seg, seg, seg],
        out_specs=pl.BlockSpec((1, CTX_LEN, hp * HEAD_DIM), lambda b, h: (b, 0, h)),
        out_shape=jax.ShapeDtypeStruct((bsz, CTX_LEN, B_W), BF16),
        compiler_params=_params("parallel", "parallel"),
        name="ctx_attn",
    )(qb, kb, vb)


DFT_FINE = 16


def _dft_tables(n):
    s = jnp.arange(n, dtype=jnp.int32)[None, :]

    def cos_sin(f):
        ang = ((f * s) % (2 * n)).astype(F32) * (math.pi / n)
        return jnp.cos(ang), jnp.sin(ang)

    ch, sh = cos_sin(jnp.arange(n // DFT_FINE, dtype=jnp.int32)[:, None] * DFT_FINE)
    cl, sl = cos_sin(jnp.arange(DFT_FINE, dtype=jnp.int32)[:, None])
    fc = (ch[:, None] * cl[None] - sh[:, None] * sl[None]).reshape(n, n)
    sym = (sh[:, None] * cl[None] + ch[:, None] * sl[None]).reshape(n, n)
    alt = jnp.where(jnp.arange(n) % 2 == 0, 1.0, -1.0).astype(F32)
    first = jnp.arange(n) == 0
    fs = jnp.where(first[:, None], alt[None, :], sym)
    fst = jnp.where(first[None, :], alt[:, None], sym)
    fc, fs, fst = fc.astype(BF16), fs.astype(BF16), fst.astype(BF16)
    tf = min(FREQ_TILE, n)
    nf = n // tf
    fwd = jnp.concatenate([fc.reshape(nf, tf, n), fs.reshape(nf, tf, n)], axis=1)
    inv = jnp.concatenate([jnp.transpose(fc.reshape(n, nf, tf), (1, 0, 2)),
                           jnp.transpose(fst.reshape(n, nf, tf), (1, 0, 2))], axis=2)
    return fc, fs, fwd, inv


def _split_bf16(v):
    hi = v.astype(BF16)
    lo = (v - hi.astype(F32)).astype(BF16)
    return hi, lo


def _spectrum_body(fc_ref, fs_ref, kp_ref, km_ref, kc_ref, ks_ref, *, tf):
    kp_hi, kp_lo = _split_bf16(kp_ref[...])
    km_hi, km_lo = _split_bf16(km_ref[...])
    fc = fc_ref[...]
    fs = fs_ref[...]
    kc_ref[...] = _dot(fc, kp_hi) + _dot(fc, kp_lo)
    ks_sin = _dot(fs, km_hi) + _dot(fs, km_lo)
    ks_nyq = _dot(fs, kp_hi) + _dot(fs, kp_lo)
    row = lax.broadcasted_iota(jnp.int32, ks_sin.shape, 0) + pl.program_id(0) * tf
    ks_ref[...] = jnp.where(row == 0, ks_nyq, ks_sin)


def _filter_spectrum(fc, fs, kplus, kminus):
    n = fc.shape[0]
    tf = min(FREQ_TILE, n)
    tab = pl.BlockSpec((tf, n), lambda j: (j, 0))
    filt = pl.BlockSpec((n, C_CH), lambda j: (0, 0))
    out = pl.BlockSpec((tf, C_CH), lambda j: (j, 0))
    return pl.pallas_call(
        functools.partial(_spectrum_body, tf=tf),
        grid=(n // tf,),
        in_specs=[tab, tab, filt, filt],
        out_specs=[out, out],
        out_shape=[jax.ShapeDtypeStruct((n, C_CH), F32)] * 2,
        compiler_params=_params("parallel"),
        name="hyena_spectrum",
    )(fc, fs, kplus, kminus)


def _hyena_gate_body(z_ref, cw_ref, cb_ref, skip_ref, u_ref, x0_ref, t_ref, *, n):
    z = z_ref[0]
    pos = lax.broadcasted_iota(jnp.int32, z.shape, 0)
    prev = jnp.where(pos == 0, 0.0, pltpu.roll(z, 1, 0))
    nxt = jnp.where(pos == n - 1, 0.0, pltpu.roll(z, n - 1, 0))
    zc = prev * cw_ref[0:1, :] + z * cw_ref[1:2, :] + nxt * cw_ref[2:3, :] + cb_ref[...]
    x0 = zc[:, :C_CH]
    u = zc[:, 2 * C_CH:] * zc[:, C_CH:2 * C_CH]
    u_ref[0] = u.astype(BF16)
    x0_ref[0] = x0
    t_ref[0] = u * skip_ref[...] * x0


def _hyena_gate(z, seg_start, n, cw, cb, skip):
    bsz = z.shape[0]
    seg = seg_start // n
    small = lambda r, c: pl.BlockSpec((r, c), lambda b: (0, 0))
    out = pl.BlockSpec((1, n, C_CH), lambda b: (b, 0, 0))
    return pl.pallas_call(
        functools.partial(_hyena_gate_body, n=n),
        grid=(bsz,),
        in_specs=[pl.BlockSpec((1, n, 3 * C_CH), lambda b: (b, seg, 0)),
                  small(3, 3 * C_CH), small(1, 3 * C_CH), small(1, C_CH)],
        out_specs=[out, out, out],
        out_shape=[jax.ShapeDtypeStruct((bsz, n, C_CH), BF16),
                   jax.ShapeDtypeStruct((bsz, n, C_CH), F32),
                   jax.ShapeDtypeStruct((bsz, n, C_CH), F32)],
        compiler_params=_params("parallel"),
        name="hyena_gate",
    )(z, cw, cb, skip)


HYENA_BATCH = 2


def _hyena_body(u_ref, x0_ref, t_ref, fwd_ref, inv_ref, kc_ref, ks_ref, o_ref, y_ref, *, n, tf):
    j = pl.program_id(1)

    @pl.when(j == 0)
    def _():
        y_ref[...] = jnp.zeros_like(y_ref)

    kc = kc_ref[...]
    ks = ks_ref[...]
    row = lax.broadcasted_iota(jnp.int32, kc.shape, 0) + j * tf
    first = row == 0
    wgt = jnp.where(first, 1.0, 2.0) * (1.0 / (2 * n))
    for b in range(HYENA_BATCH):
        spec = _dot(fwd_ref[0], u_ref[b])
        uc, us = spec[:tf], spec[tf:]
        re = jnp.where(first, uc * kc, uc * kc - us * ks) * wgt
        im = jnp.where(first, us * ks, uc * ks + us * kc) * wgt
        y_ref[b] += _dot(inv_ref[0], jnp.concatenate([re, im], axis=0).astype(BF16))

    @pl.when(j == pl.num_programs(1) - 1)
    def _():
        o_ref[...] = (y_ref[...] * x0_ref[...] + t_ref[...]).astype(BF16)


def _hyena(u, x0, t, n, fwd, inv, kc, ks):
    bsz = u.shape[0]
    assert bsz % HYENA_BATCH == 0, bsz
    tf = min(FREQ_TILE, n)
    nf = n // tf
    seq = pl.BlockSpec((HYENA_BATCH, n, C_CH), lambda g, j: (g, 0, 0))
    spec_blk = pl.BlockSpec((tf, C_CH), lambda g, j: (j, 0))
    return pl.pallas_call(
        functools.partial(_hyena_body, n=n, tf=tf),
        grid=(bsz // HYENA_BATCH, nf),
        in_specs=[seq, seq, seq,
                  pl.BlockSpec((1, 2 * tf, n), lambda g, j: (j, 0, 0)),
                  pl.BlockSpec((1, n, 2 * tf), lambda g, j: (j, 0, 0)),
                  spec_blk, spec_blk],
        out_specs=seq,
        out_shape=jax.ShapeDtypeStruct((bsz, n, C_CH), BF16),
        scratch_shapes=[pltpu.VMEM((HYENA_BATCH, n, C_CH), F32)],
        compiler_params=_params("parallel", "arbitrary"),
        name="hyena_conv",
    )(u, x0, t, fwd, inv, kc, ks)


def _implicit_filters(n, w1, b1, w2, b2, w3, freq):
    t = jnp.linspace(0.0, 1.0, n, dtype=F32)[:, None]
    w = 2 * math.pi * jnp.arange(n, dtype=F32)[:, None] / n
    f = jnp.linspace(1e-4, C_POS_BANDS - 1, C_POS_BANDS, dtype=F32)[None]
    z = jnp.concatenate([t, jnp.cos(f * w), -jnp.sin(f * w)], -1)
    h = jnp.sin(freq[0] * (jnp.dot(z, w1, precision=HIGHEST) + b1))
    h = jnp.sin(freq[1] * (jnp.dot(h, w2, precision=HIGHEST) + b2))
    h = jnp.dot(h, w3, precision=HIGHEST)
    lo = math.log(C_DECAY_TARGET) / C_SLOW_DECAY
    hi = math.log(C_DECAY_TARGET) / C_FAST_DECAY
    deltas = jnp.abs(jnp.linspace(lo, hi, C_CH, dtype=F32))
    decay = jnp.exp(-t * deltas[None])
    hf, hb = h[:, :C_CH] * decay, h[:, C_CH:] * decay
    hb = hb.at[0].set(0.0)
    return hf + hb, hf - hb


def _router_gates(logits_t, rb):
    tm = logits_t.shape[1]
    scores = _sigmoid(logits_t)
    sel = scores + rb
    sel3 = sel.reshape(N_GROUPS, GROUP_SIZE, tm)
    kk = lax.broadcasted_iota(jnp.int32, sel3.shape, 1)
    m1 = jnp.max(sel3, 1, keepdims=True)
    i1 = jnp.min(jnp.where(sel3 == m1, kk, GROUP_SIZE), 1, keepdims=True)
    m2 = jnp.max(jnp.where(kk == i1, -jnp.inf, sel3), 1, keepdims=True)
    gscore = (m1 + m2).reshape(N_GROUPS, tm)
    gid = lax.broadcasted_iota(jnp.int32, gscore.shape, 0)
    rank = jnp.zeros(gscore.shape, jnp.int32)
    for g in range(N_GROUPS):
        other = gscore[g:g + 1, :]
        ahead = (other > gscore) | ((other == gscore) & (g < gid))
        rank = rank + ahead.astype(jnp.int32)
    gsel = (rank < TOPK_GROUPS).reshape(N_GROUPS, 1, tm)
    cand = jnp.where(gsel, sel3, -jnp.inf).reshape(N_EXPERTS, tm)
    eid = lax.broadcasted_iota(jnp.int32, cand.shape, 0)
    chosen = jnp.zeros(cand.shape, jnp.bool_)
    for _ in range(TOP_K):
        m = jnp.max(cand, 0, keepdims=True)
        pick = eid == jnp.min(jnp.where(cand == m, eid, N_EXPERTS), 0, keepdims=True)
        chosen = chosen | pick
        cand = jnp.where(pick, -jnp.inf, cand)
    wsel = jnp.where(chosen, scores, 0.0)
    return wsel / jnp.sum(wsel, 0, keepdims=True) * ROUTED_SCALE


def _out_body(oa_ref, ob_ref, oc_ref, x_ref, mod_ref, wa_ref, wb_ref, wc_ref, lng_ref, lnb_ref,
              rwt_ref, rb_ref, xmid_ref, h_ref, gate_ref, *, tm):
    for rows, is_ctx in _sub_tiles(tm):
        attn = (_dot(oa_ref[0, rows, :], wa_ref[...]) + _dot(ob_ref[0, rows, :], wb_ref[...])
                + _dot(oc_ref[0, rows, :], wc_ref[...]))
        y = DN_ALPHA * x_ref[0, rows, :] + _mod_row(mod_ref, is_ctx, 2) * attn
        xm = _layer_norm(y) * lng_ref[...] + lnb_ref[...]
        xmid_ref[0, rows, :] = xm
        h = _layer_norm(xm) * (1.0 + _mod_row(mod_ref, is_ctx, 4)) + _mod_row(mod_ref, is_ctx, 3)
        hi, lo = _split_bf16(h)
        h_ref[0, rows, :] = hi
        logits_t = _dot_nt(rwt_ref[...], jnp.concatenate([hi, lo, hi], axis=1))
        gates_t = _router_gates(logits_t, rb_ref[...])
        pad = jnp.zeros((128 - N_EXPERTS, gates_t.shape[1]), F32)
        gates = jnp.concatenate([gates_t, pad], axis=0).T
        for g in range(N_GROUPS):
            gate_ref[0, g, rows, :] = gates[:, g * GROUP_SIZE:(g + 1) * GROUP_SIZE]


def _out_proj(oa, ob, oc, x, mod, wa, wb, wc, lng, lnb, rwt3, rb):
    bsz, s, _ = oa.shape
    d = D_MODEL
    tm = _step_tokens(s)
    tok = lambda w: pl.BlockSpec((1, tm, w), lambda b, i: (b, i, 0))
    full = lambda r, c: pl.BlockSpec((r, c), lambda b, i: (0, 0))
    return pl.pallas_call(
        functools.partial(_out_body, tm=tm),
        grid=(bsz, s // tm),
        in_specs=[tok(A_W), tok(B_W), tok(C_CH), tok(d),
                  pl.BlockSpec((1, 2, 6, d), lambda b, i: (b, 0, 0, 0)),
                  full(A_W, d), full(B_W, d), full(C_CH, d), full(1, d), full(1, d),
                  full(N_EXPERTS, 3 * d), full(N_EXPERTS, 1)],
        out_specs=[tok(d), tok(d),
                   pl.BlockSpec((1, N_GROUPS, tm, GROUP_SIZE), lambda b, i: (b, 0, i, 0))],
        out_shape=[jax.ShapeDtypeStruct((bsz, s, d), F32),
                   jax.ShapeDtypeStruct((bsz, s, d), BF16),
                   jax.ShapeDtypeStruct((bsz, N_GROUPS, s, GROUP_SIZE), F32)],
        compiler_params=_params("parallel", "parallel"),
        name="proj_out_router",
    )(oa, ob, oc, x, mod, wa, wb, wc, lng, lnb, rwt3, rb)


MOE_CHAINS = 2


def _moe_body(h_ref, gate_ref, xmid_ref, mod_ref, w1_ref, w3_ref, w2_ref,
              sw1_ref, sw3_ref, sw2_ref, lng_ref, lnb_ref, o_ref, acc_ref, *, tm):
    g = pl.program_id(2)
    rc = tm // MOE_CHAINS

    @pl.when(g == 0)
    def _():
        h = h_ref[0]
        a = _dot(h, sw1_ref[...])
        act = (a * _sigmoid(a) * _dot(h, sw3_ref[...])).astype(BF16)
        acc_ref[...] = _dot(act, sw2_ref[...])

    w1 = jnp.concatenate([w1_ref[k] for k in range(GROUP_SIZE)], axis=1)
    w3 = jnp.concatenate([w3_ref[k] for k in range(GROUP_SIZE)], axis=1)
    w2 = w2_ref[...].reshape(GROUP_SIZE * EXPERT_HIDDEN, D_MODEL)
    for c in range(MOE_CHAINS):
        rows = slice(c * rc, (c + 1) * rc)
        h = h_ref[0, rows, :]
        a = _dot(h, w1)
        act = a * _sigmoid(a) * _dot(h, w3)
        gt = gate_ref[0, 0, rows, :]
        parts = [act[:, k * EXPERT_HIDDEN:(k + 1) * EXPERT_HIDDEN] * gt[:, k:k + 1]
                 for k in range(GROUP_SIZE)]
        acc_ref[rows, :] += _dot(jnp.concatenate(parts, axis=1).astype(BF16), w2)

    @pl.when(g == N_GROUPS - 1)
    def _():
        row = lax.broadcasted_iota(jnp.int32, (tm, 1), 0) + pl.program_id(1) * tm
        gate2 = jnp.where(row >= SEQ, mod_ref[0, 1, 5:6, :], mod_ref[0, 0, 5:6, :])
        y = DN_ALPHA * xmid_ref[0] + gate2 * acc_ref[...]
        o_ref[0] = _layer_norm(y) * lng_ref[...] + lnb_ref[...]


MOE_TILE_2BUF = 768
MOE_TILE_1BUF = 1024


def _moe(h, gates, xmid, mod, layer, w1_all, w3_all, w2_all, sw1, sw3, sw2, lng, lnb):
    bsz, s, d = h.shape
    tm = MOE_TILE_1BUF if s % MOE_TILE_1BUF == 0 else MOE_TILE_2BUF
    tok = pl.BlockSpec((1, tm, d), lambda b, i, g: (b, i, 0))
    tok_once = pl.BlockSpec((1, tm, d), lambda b, i, g: (b, i, 0),
                            pipeline_mode=pl.Buffered(1 if tm > MOE_TILE_2BUF else 2))
    full = lambda r, c: pl.BlockSpec((r, c), lambda b, i, g: (0, 0))
    up = pl.BlockSpec((None, GROUP_SIZE, d, EXPERT_HIDDEN), lambda b, i, g: (layer, g, 0, 0))
    return pl.pallas_call(
        functools.partial(_moe_body, tm=tm),
        grid=(bsz, s // tm, N_GROUPS),
        in_specs=[tok,
                  pl.BlockSpec((1, 1, tm, GROUP_SIZE), lambda b, i, g: (b, g, i, 0)),
                  tok_once,
                  pl.BlockSpec((1, 2, 6, d), lambda b, i, g: (b, 0, 0, 0)),
                  up, up,
                  pl.BlockSpec((None, GROUP_SIZE, EXPERT_HIDDEN, d), lambda b, i, g: (layer, g, 0, 0)),
                  full(d, SHARED_HIDDEN), full(d, SHARED_HIDDEN), full(SHARED_HIDDEN, d),
                  full(1, d), full(1, d)],
        out_specs=tok,
        out_shape=jax.ShapeDtypeStruct((bsz, s, d), F32),
        scratch_shapes=[pltpu.VMEM((tm, d), F32)],
        compiler_params=_params("parallel", "parallel", "arbitrary"),
        name="moe_ffn",
    )(h, gates, xmid, mod, w1_all, w3_all, w2_all, sw1, sw3, sw2, lng, lnb)


def _rope_tables():
    t = jnp.arange(SEQ)
    row = (t // GRID_W).astype(F32)
    col = (t % GRID_W).astype(F32)
    inv = ROPE_THETA ** (-jnp.arange(0, HALF, 2, dtype=F32) / HALF)
    ang = jnp.concatenate([row[:, None] * inv, col[:, None] * inv], -1)
    cos, sin = jnp.cos(ang), jnp.sin(ang)
    n_heads = ROPE_TABLE_W // HEAD_DIM
    cos_t = jnp.concatenate([jnp.tile(jnp.concatenate([cos, cos], -1), (1, n_heads)),
                             jnp.ones((CTX_LEN, ROPE_TABLE_W), F32)], 0)
    sin_t = jnp.concatenate([jnp.tile(jnp.concatenate([-sin, sin], -1), (1, n_heads)),
                             jnp.zeros((CTX_LEN, ROPE_TABLE_W), F32)], 0)
    return cos_t, sin_t


def _modulation(c, c_ctx, w, b):
    cvec = jnp.concatenate([c, c_ctx[None]], axis=0)
    m = (jnp.dot(jax.nn.silu(cvec), w, precision=HIGHEST) + b).reshape(-1, 6, D_MODEL)
    return jnp.stack([m[:-1], jnp.broadcast_to(m[-1], m[:-1].shape)], axis=1)


def kernel(x, c, ctx, c_ctx, ada_w, ada_b, w_in, q_gain, k_gain, nat_rpb, hy_conv_w, hy_conv_b, hy_f_w1, hy_f_b1, hy_f_w2, hy_f_b2, hy_f_w3, hy_freq, hy_skip, w_out, ln1_g, ln1_b, router_w, router_b, exp_w1, exp_w3, exp_w2, sh_w1, sh_w3, sh_w2, ln2_g, ln2_b):
    bsz = x.shape[0]
    d = D_MODEL
    xs = jnp.concatenate([x, ctx], axis=1)

    head_perm = np.concatenate([np.arange(0, HEAD_DIM, 2), np.arange(1, HEAD_DIM, 2)])
    qk_perm = (np.arange(QK_W // HEAD_DIM)[:, None] * HEAD_DIM + head_perm[None, :]).reshape(-1)
    perm_mat = jnp.asarray(np.eye(QK_W)[:, qk_perm], BF16)
    cos_t, sin_t = _rope_tables()
    bd = jnp.asarray(np.kron(np.eye(QK_W // HEAD_DIM), np.ones((HEAD_DIM, HEAD_DIM))), BF16)
    tables_lat = _dft_tables(SEQ)
    tables_ctx = _dft_tables(CTX_LEN)
    w1_bf, w3_bf, w2_bf = exp_w1.astype(BF16), exp_w3.astype(BF16), exp_w2.astype(BF16)

    for l in range(DEPTH):
        last = l == DEPTH - 1
        mod = _modulation(c, c_ctx, ada_w[l], ada_b[l])

        w_bf = w_in[l].astype(BF16)
        w_qk = jnp.dot(w_bf[:, :QK_W], perm_mat, preferred_element_type=F32).astype(BF16)
        w_bf = jnp.concatenate([w_qk, w_bf[:, QK_W:]], axis=1)
        gain = jnp.concatenate([jnp.tile(q_gain[l][head_perm] * ATT_SCALE, A_HEADS),
                                jnp.tile(k_gain[l][head_perm], A_KV_HEADS)])[None].astype(F32)
        qa, ka, va, qb, kb, vb, z = _project(xs, mod, w_bf, bd, gain, cos_t, sin_t)

        s_out = SEQ if last else S_ALL
        oa = _gqa(qa, ka, va, s_out)
        ob = _na(qb, kb, vb, _na_bias_table(nat_rpb[l]))
        if not last:
            ob = jnp.concatenate([ob, _ctx_attn(qb, kb, vb)], axis=1)

        filt = (hy_f_w1[l], hy_f_b1[l], hy_f_w2[l], hy_f_b2[l], hy_f_w3[l], hy_freq[l])
        cw, cb, skip = hy_conv_w[l], hy_conv_b[l][None], hy_skip[l][None]
        segments = [(0, SEQ, tables_lat)] + ([] if last else [(SEQ, CTX_LEN, tables_ctx)])
        oc = []
        for seg_start, n, (fc, fs, fwd, inv) in segments:
            kc, ks = _filter_spectrum(fc, fs, *_implicit_filters(n, *filt))
            u, x0, t = _hyena_gate(z, seg_start, n, cw, cb, skip)
            oc.append(_hyena(u, x0, t, n, fwd, inv, kc, ks))
        oc = oc[0] if last else jnp.concatenate(oc, axis=1)

        wo = w_out[l].astype(BF16)
        rw_hi, rw_lo = _split_bf16(router_w[l].T)
        rwt3 = jnp.concatenate([rw_hi, rw_hi, rw_lo], axis=1)
        xmid, h, gates = _out_proj(oa, ob, oc, xs, mod, wo[:A_W], wo[A_W:A_W + B_W], wo[A_W + B_W:],
                                   ln1_g[l][None], ln1_b[l][None], rwt3, router_b[l][:, None])

        xs = _moe(h, gates, xmid, mod, l, w1_bf, w3_bf, w2_bf,
                  sh_w1[l].astype(BF16), sh_w3[l].astype(BF16), sh_w2[l].astype(BF16),
                  ln2_g[l][None], ln2_b[l][None])
    return xs
```

```python
import functools
import math

import numpy as np
import jax
import jax.numpy as jnp
from jax import lax
from jax.experimental import pallas as pl
from jax.experimental.pallas import tpu as pltpu

F32 = jnp.float32
BF16 = jnp.bfloat16
HIGHEST = lax.Precision.HIGHEST

D_MODEL = 1024
SEQ = 2048
DEPTH = 2
GRID_W = 64
GRID_ROWS = SEQ // GRID_W
CTX_LEN = 256
S_ALL = SEQ + CTX_LEN
HEAD_DIM = 64
HALF = HEAD_DIM // 2
A_HEADS = 8
A_KV_HEADS = 2
A_REP = A_HEADS // A_KV_HEADS
ROPE_THETA = 10000.0
B_HEADS = 4
WIN_ROWS = 8
WIN_COLS = 16
C_CH = 256
C_POS_BANDS = 16
C_DECAY_TARGET = 1e-2
C_FAST_DECAY = 0.3
C_SLOW_DECAY = 1.5
A_W = A_HEADS * HEAD_DIM
A_KV_W = A_KV_HEADS * HEAD_DIM
B_W = B_HEADS * HEAD_DIM
A_K0 = A_W
A_V0 = A_K0 + A_KV_W
B_Q0 = A_V0 + A_KV_W
B_K0 = B_Q0 + B_W
B_V0 = B_K0 + B_W
C0 = B_V0 + B_W
PROJ_W = C0 + 3 * C_CH
QK_W = A_W + A_KV_W
N_EXPERTS = 64
TOP_K = 8
N_GROUPS = 8
GROUP_SIZE = N_EXPERTS // N_GROUPS
TOPK_GROUPS = 4
EXPERT_HIDDEN = 128
SHARED_HIDDEN = 256
ROUTED_SCALE = 2.5
DN_ALPHA = (2 * DEPTH) ** 0.25
LN_EPS = 1e-6
RMS_EPS = 1e-6
ATT_SCALE = HEAD_DIM ** -0.5
NEG_BIG = -1e30

VMEM_LIMIT_BYTES = 56 * 1024 * 1024
TOKEN_TILE = 256
ROPE_TABLE_W = 128
GQA_Q_TILE = 512
GQA_KEY_CHUNK = 768
FREQ_TILE = 256


def _params(*sem):
    return pltpu.CompilerParams(dimension_semantics=sem, vmem_limit_bytes=VMEM_LIMIT_BYTES)


def _layer_norm(v):
    mu = jnp.mean(v, -1, keepdims=True)
    vc = v - mu
    var = jnp.mean(vc * vc, -1, keepdims=True)
    return vc * lax.rsqrt(var + LN_EPS)


def _sigmoid(v):
    return 1.0 / (1.0 + jnp.exp(-v))


def _dot(a, b):
    return jnp.dot(a, b, preferred_element_type=F32)


def _dot_nt(a, b):
    return lax.dot_general(a, b, (((1,), (1,)), ((), ())), preferred_element_type=F32)


def _mod_row(mod_ref, is_ctx, idx):
    return jnp.where(is_ctx, mod_ref[0, 1, idx:idx + 1, :], mod_ref[0, 0, idx:idx + 1, :])


def _sub_tiles(tm):
    for k in range(tm // TOKEN_TILE):
        rows = slice(k * TOKEN_TILE, (k + 1) * TOKEN_TILE)
        is_ctx = (pl.program_id(1) * tm + k * TOKEN_TILE) >= SEQ
        yield rows, is_ctx


def _proj_body(x_ref, mod_ref, w_ref, bd_ref, gain_ref, cos_ref, sin_ref,
               qa_ref, ka_ref, va_ref, qb_ref, kb_ref, vb_ref, z_ref, *, tm):
    n_rep = QK_W // ROPE_TABLE_W
    for rows, is_ctx in _sub_tiles(tm):
        xn = _layer_norm(x_ref[0, rows, :])
        hx = (xn * (1.0 + _mod_row(mod_ref, is_ctx, 1)) + _mod_row(mod_ref, is_ctx, 0)).astype(BF16)
        px = _dot(hx, w_ref[...])
        qk = px[:, :QK_W]
        ss = _dot((qk * qk).astype(BF16), bd_ref[...])
        qkn = qk * lax.rsqrt(ss * (1.0 / HEAD_DIM) + RMS_EPS) * gain_ref[...]
        lane = lax.broadcasted_iota(jnp.int32, qkn.shape, 1)
        partner = jnp.where((lane % HEAD_DIM) < HALF,
                            pltpu.roll(qkn, QK_W - HALF, 1), pltpu.roll(qkn, HALF, 1))
        cos = jnp.concatenate([cos_ref[rows, :]] * n_rep, axis=1)
        sin = jnp.concatenate([sin_ref[rows, :]] * n_rep, axis=1)
        qkr = (qkn * cos + partner * sin).astype(BF16)
        for h in range(A_HEADS):
            qa_ref[0, h, rows, :] = qkr[:, h * HEAD_DIM:(h + 1) * HEAD_DIM]
        for g in range(A_KV_HEADS):
            ka_ref[0, g, rows, :] = qkr[:, A_K0 + g * HEAD_DIM:A_K0 + (g + 1) * HEAD_DIM]
            va_ref[0, g, rows, :] = px[:, A_V0 + g * HEAD_DIM:A_V0 + (g + 1) * HEAD_DIM].astype(BF16)
        for h in range(B_HEADS):
            qb_ref[0, h, rows, :] = (px[:, B_Q0 + h * HEAD_DIM:B_Q0 + (h + 1) * HEAD_DIM] * ATT_SCALE).astype(BF16)
            kb_ref[0, h, rows, :] = px[:, B_K0 + h * HEAD_DIM:B_K0 + (h + 1) * HEAD_DIM].astype(BF16)
            vb_ref[0, h, rows, :] = px[:, B_V0 + h * HEAD_DIM:B_V0 + (h + 1) * HEAD_DIM].astype(BF16)
        z_ref[0, rows, :] = px[:, C0:]


def _step_tokens(s):
    return 3 * TOKEN_TILE if s % (3 * TOKEN_TILE) == 0 else 2 * TOKEN_TILE


def _project(x, mod, w_bf, bd, gain, cos_t, sin_t):
    bsz, s, d = x.shape
    tm = _step_tokens(s)
    nt = s // tm
    heads = lambda n: pl.BlockSpec((1, n, tm, HEAD_DIM), lambda b, i: (b, 0, i, 0))
    hshape = lambda n: jax.ShapeDtypeStruct((bsz, n, s, HEAD_DIM), BF16)
    return pl.pallas_call(
        functools.partial(_proj_body, tm=tm),
        grid=(bsz, nt),
        in_specs=[
            pl.BlockSpec((1, tm, d), lambda b, i: (b, i, 0)),
            pl.BlockSpec((1, 2, 6, d), lambda b, i: (b, 0, 0, 0)),
            pl.BlockSpec((d, PROJ_W), lambda b, i: (0, 0)),
            pl.BlockSpec((QK_W, QK_W), lambda b, i: (0, 0)),
            pl.BlockSpec((1, QK_W), lambda b, i: (0, 0)),
            pl.BlockSpec((tm, ROPE_TABLE_W), lambda b, i: (i, 0)),
            pl.BlockSpec((tm, ROPE_TABLE_W), lambda b, i: (i, 0)),
        ],
        out_specs=[heads(A_HEADS), heads(A_KV_HEADS), heads(A_KV_HEADS),
                   heads(B_HEADS), heads(B_HEADS), heads(B_HEADS),
                   pl.BlockSpec((1, tm, 3 * C_CH), lambda b, i: (b, i, 0))],
        out_shape=[hshape(A_HEADS), hshape(A_KV_HEADS), hshape(A_KV_HEADS),
                   hshape(B_HEADS), hshape(B_HEADS), hshape(B_HEADS),
                   jax.ShapeDtypeStruct((bsz, s, 3 * C_CH), F32)],
        compiler_params=_params("parallel", "parallel"),
        name="proj_in",
    )(x, mod, w_bf, bd, gain, cos_t, sin_t)


def _gqa_body(q_ref, k_ref, v_ref, o_ref):
    tq = q_ref.shape[2]
    n_keys = k_ref.shape[2]
    for h in range(A_REP):
        q = q_ref[0, h]
        m = l = acc = None
        for j in range(n_keys // GQA_KEY_CHUNK):
            keys = slice(j * GQA_KEY_CHUNK, (j + 1) * GQA_KEY_CHUNK)
            s = _dot_nt(q, k_ref[0, 0, keys, :])
            mj = jnp.max(s, -1, keepdims=True)
            if j == 0:
                m = mj
                p = jnp.exp(s - m)
                l = jnp.sum(p, -1, keepdims=True)
                acc = _dot(p.astype(BF16), v_ref[0, 0, keys, :])
            else:
                m_new = jnp.maximum(m, mj)
                alpha = jnp.exp(m - m_new)
                p = jnp.exp(s - m_new)
                l = alpha * l + jnp.sum(p, -1, keepdims=True)
                acc = alpha * acc + _dot(p.astype(BF16), v_ref[0, 0, keys, :])
                m = m_new
        o_ref[0, :, h * HEAD_DIM:(h + 1) * HEAD_DIM] = (acc / l).astype(BF16)


def _gqa(qa, ka, va):
    bsz, _, s, _ = qa.shape
    tq = GQA_Q_TILE
    return pl.pallas_call(
        _gqa_body,
        grid=(bsz, A_KV_HEADS, SEQ // tq),
        in_specs=[
            pl.BlockSpec((1, A_REP, tq, HEAD_DIM), lambda b, g, i: (b, g, i, 0)),
            pl.BlockSpec((1, 1, s, HEAD_DIM), lambda b, g, i: (b, g, 0, 0)),
            pl.BlockSpec((1, 1, s, HEAD_DIM), lambda b, g, i: (b, g, 0, 0)),
        ],
        out_specs=pl.BlockSpec((1, tq, A_REP * HEAD_DIM), lambda b, g, i: (b, i, g)),
        out_shape=jax.ShapeDtypeStruct((bsz, SEQ, A_W), BF16),
        compiler_params=_params("parallel", "parallel", "parallel"),
        name="gqa_attn",
    )(qa, ka, va)


NA_HEADS_PER_STEP = 2
NA_Q_ROWS = 8
NA_K_ROWS = 16
NA_Q_TOKENS = NA_Q_ROWS * GRID_W
NA_K_TOKENS = NA_K_ROWS * GRID_W
NA_Q_BLOCKS = GRID_ROWS // NA_Q_ROWS
NA_KEY_ROW0 = (0, 4, 12, 16)
NA_PATTERN_BLOCKS = (0, 1, 3)


def _na_body(q_ref, k_ref, v_ref, bias_ref, o_ref):
    j = pl.program_id(2)
    key_row0 = jnp.where(j == 0, NA_KEY_ROW0[0],
                         jnp.where(j == 1, NA_KEY_ROW0[1], jnp.where(j == 2, NA_KEY_ROW0[2], NA_KEY_ROW0[3])))
    k0 = pl.multiple_of(key_row0 * GRID_W, 4 * GRID_W)
    for hh in range(NA_HEADS_PER_STEP):
        q = q_ref[0, hh]
        kw = k_ref[0, hh, pl.ds(k0, NA_K_TOKENS), :]
        vw = v_ref[0, hh, pl.ds(k0, NA_K_TOKENS), :]
        kc = k_ref[0, hh, SEQ:, :]
        vc = v_ref[0, hh, SEQ:, :]
        sw = _dot_nt(q, kw) + bias_ref[hh, 0]
        sc = _dot_nt(q, kc)
        m = jnp.maximum(jnp.max(sw, -1, keepdims=True), jnp.max(sc, -1, keepdims=True))
        pw = jnp.exp(sw - m)
        pc = jnp.exp(sc - m)
        l = jnp.sum(pw, -1, keepdims=True) + jnp.sum(pc, -1, keepdims=True)
        o = (_dot(pw.astype(BF16), vw) + _dot(pc.astype(BF16), vc)) / l
        o_ref[0, :, hh * HEAD_DIM:(hh + 1) * HEAD_DIM] = o.astype(BF16)


def _na(qb, kb, vb, bias):
    bsz, _, s, _ = qb.shape
    hp = NA_HEADS_PER_STEP
    kv = pl.BlockSpec((1, hp, s, HEAD_DIM), lambda b, h, j: (b, h, 0, 0))
    return pl.pallas_call(
        _na_body,
        grid=(bsz, B_HEADS // hp, NA_Q_BLOCKS),
        in_specs=[pl.BlockSpec((1, hp, NA_Q_TOKENS, HEAD_DIM), lambda b, h, j: (b, h, j, 0)), kv, kv,
                  pl.BlockSpec((hp, 1, NA_Q_TOKENS, NA_K_TOKENS), lambda b, h, j: (h, (j + 1) // 2, 0, 0))],
        out_specs=pl.BlockSpec((1, NA_Q_TOKENS, hp * HEAD_DIM), lambda b, h, j: (b, j, h)),
        out_shape=jax.ShapeDtypeStruct((bsz, SEQ, B_W), BF16),
        compiler_params=_params("parallel", "parallel", "arbitrary"),
        name="nbr_attn",
    )(qb, kb, vb, bias)


def _na_bias_body(t_ref, o_ref):
    o_ref[...] = jnp.full(o_ref.shape, NEG_BIG, F32)
    for p, blk in enumerate(NA_PATTERN_BLOCKS):
        @pl.when(pl.program_id(1) == p)
        def _(blk=blk):
            for qi in range(NA_Q_ROWS):
                qrow = blk * NA_Q_ROWS + qi
                rs = min(max(qrow - WIN_ROWS // 2, 0), GRID_ROWS - WIN_ROWS)
                for krow in range(rs, rs + WIN_ROWS):
                    ki = krow - NA_KEY_ROW0[blk]
                    o_ref[0, 0, qi * GRID_W:(qi + 1) * GRID_W, ki * GRID_W:(ki + 1) * GRID_W] = (
                        t_ref[0, krow - qrow + WIN_ROWS - 1])


def _na_bias_table(rpb):
    col = np.arange(GRID_W)
    col_start = np.clip(col - WIN_COLS // 2, 0, GRID_W - WIN_COLS)
    in_cols = (col[None, :] >= col_start[:, None]) & (col[None, :] < col_start[:, None] + WIN_COLS)
    dc = np.clip(col[None, :] - col[:, None] + WIN_COLS - 1, 0, 2 * WIN_COLS - 2)
    onehot = jnp.asarray(dc[None] == np.arange(2 * WIN_COLS - 1)[:, None, None], F32)
    tiles = jnp.einsum('hdj,jqk->hdqk', rpb.astype(F32), onehot, precision=HIGHEST)
    tiles = jnp.where(in_cols[None, None], tiles, NEG_BIG)
    n_pat = len(NA_PATTERN_BLOCKS)
    return pl.pallas_call(
        _na_bias_body,
        grid=(B_HEADS, n_pat),
        in_specs=[pl.BlockSpec((1, 2 * WIN_ROWS - 1, GRID_W, GRID_W), lambda h, p: (h, 0, 0, 0))],
        out_specs=pl.BlockSpec((1, 1, NA_Q_TOKENS, NA_K_TOKENS), lambda h, p: (h, p, 0, 0)),
        out_shape=jax.ShapeDtypeStruct((B_HEADS, n_pat, NA_Q_TOKENS, NA_K_TOKENS), F32),
        compiler_params=_params("parallel", "arbitrary"),
        name="nbr_bias",
    )(tiles)


def _ctx_attn_body(q_ref, k_ref, v_ref, o_ref, *, rep):
    for g in range(k_ref.shape[1]):
        for r in range(rep):
            hh = g * rep + r
            s = _dot_nt(q_ref[0, hh], k_ref[0, g])
            m = jnp.max(s, -1, keepdims=True)
            p = jnp.exp(s - m)
            l = jnp.sum(p, -1, keepdims=True)
            o = _dot(p.astype(BF16), v_ref[0, g]) / l
            o_ref[0, :, hh * HEAD_DIM:(hh + 1) * HEAD_DIM] = o.astype(BF16)


def _ctx_attn(q, k, v, kv_per_step):
    bsz, n_q, _, _ = q.shape
    n_kv = k.shape[1]
    rep = n_q // n_kv
    seg = lambda n: pl.BlockSpec((1, n, CTX_LEN, HEAD_DIM), lambda b, h: (b, h, SEQ // CTX_LEN, 0))
    return pl.pallas_call(
        functools.partial(_ctx_attn_body, rep=rep),
        grid=(bsz, n_kv // kv_per_step),
        in_specs=[seg(kv_per_step * rep), seg(kv_per_step), seg(kv_per_step)],
        out_specs=pl.BlockSpec((1, CTX_LEN, kv_per_step * rep * HEAD_DIM), lambda b, h: (b, 0, h)),
        out_shape=jax.ShapeDtypeStruct((bsz, CTX_LEN, n_q * HEAD_DIM), BF16),
        compiler_params=_params("parallel", "parallel"),
        name="ctx_attn",
    )(q, k, v)


DFT_FINE = 16


def _dft_tables(n):
    s = jnp.arange(n, dtype=jnp.int32)[None, :]

    def cos_sin(f):
        ang = ((f * s) % (2 * n)).astype(F32) * (math.pi / n)
        return jnp.cos(ang), jnp.sin(ang)

    ch, sh = cos_sin(jnp.arange(n // DFT_FINE, dtype=jnp.int32)[:, None] * DFT_FINE)
    cl, sl = cos_sin(jnp.arange(DFT_FINE, dtype=jnp.int32)[:, None])
    fc = (ch[:, None] * cl[None] - sh[:, None] * sl[None]).reshape(n, n)
    sym = (sh[:, None] * cl[None] + ch[:, None] * sl[None]).reshape(n, n)
    alt = jnp.where(jnp.arange(n) % 2 == 0, 1.0, -1.0).astype(F32)
    first = jnp.arange(n) == 0
    fs = jnp.where(first[:, None], alt[None, :], sym)
    fst = jnp.where(first[None, :], alt[:, None], sym)
    fc, fs, fst = fc.astype(BF16), fs.astype(BF16), fst.astype(BF16)
    tf = min(FREQ_TILE, n)
    nf = n // tf
    fwd = jnp.concatenate([fc.reshape(nf, tf, n), fs.reshape(nf, tf, n)], axis=1)
    inv = jnp.concatenate([jnp.transpose(fc.reshape(n, nf, tf), (1, 0, 2)),
                           jnp.transpose(fst.reshape(n, nf, tf), (1, 0, 2))], axis=2)
    return fc, fs, fwd, inv


def _split_bf16(v):
    hi = v.astype(BF16)
    lo = (v - hi.astype(F32)).astype(BF16)
    return hi, lo


def _spectrum_body(fc_ref, fs_ref, kp_ref, km_ref, kc_ref, ks_ref, *, tf):
    kp_hi, kp_lo = _split_bf16(kp_ref[...])
    km_hi, km_lo = _split_bf16(km_ref[...])
    fc = fc_ref[...]
    fs = fs_ref[...]
    kc_ref[...] = _dot(fc, kp_hi) + _dot(fc, kp_lo)
    ks_sin = _dot(fs, km_hi) + _dot(fs, km_lo)
    ks_nyq = _dot(fs, kp_hi) + _dot(fs, kp_lo)
    row = lax.broadcasted_iota(jnp.int32, ks_sin.shape, 0) + pl.program_id(0) * tf
    ks_ref[...] = jnp.where(row == 0, ks_nyq, ks_sin)


def _filter_spectrum(fc, fs, kplus, kminus):
    n = fc.shape[0]
    tf = min(FREQ_TILE, n)
    tab = pl.BlockSpec((tf, n), lambda j: (j, 0))
    filt = pl.BlockSpec((n, C_CH), lambda j: (0, 0))
    out = pl.BlockSpec((tf, C_CH), lambda j: (j, 0))
    return pl.pallas_call(
        functools.partial(_spectrum_body, tf=tf),
        grid=(n // tf,),
        in_specs=[tab, tab, filt, filt],
        out_specs=[out, out],
        out_shape=[jax.ShapeDtypeStruct((n, C_CH), F32)] * 2,
        compiler_params=_params("parallel"),
        name="hyena_spectrum",
    )(fc, fs, kplus, kminus)


def _short_conv(z, cw_ref, cb_ref, cols, n):
    pos = lax.broadcasted_iota(jnp.int32, z.shape, 0)
    prev = jnp.where(pos == 0, 0.0, pltpu.roll(z, 1, 0))
    nxt = jnp.where(pos == n - 1, 0.0, pltpu.roll(z, n - 1, 0))
    return prev * cw_ref[0:1, cols] + z * cw_ref[1:2, cols] + nxt * cw_ref[2:3, cols] + cb_ref[:, cols]


def _hyena_gate_body(z_ref, cw_ref, cb_ref, skip_ref, u_ref, us_ref, *, n):
    zc = _short_conv(z_ref[0, :, C_CH:], cw_ref, cb_ref, slice(C_CH, 3 * C_CH), n)
    u = zc[:, C_CH:] * zc[:, :C_CH]
    u_ref[0] = u.astype(BF16)
    us_ref[0] = u * skip_ref[...]


def _hyena_gate(z, seg_start, n, cw, cb, skip):
    bsz = z.shape[0]
    seg = seg_start // n
    small = lambda r, c: pl.BlockSpec((r, c), lambda b: (0, 0))
    out = pl.BlockSpec((1, n, C_CH), lambda b: (b, 0, 0))
    return pl.pallas_call(
        functools.partial(_hyena_gate_body, n=n),
        grid=(bsz,),
        in_specs=[pl.BlockSpec((1, n, 3 * C_CH), lambda b: (b, seg, 0)),
                  small(3, 3 * C_CH), small(1, 3 * C_CH), small(1, C_CH)],
        out_specs=[out, out],
        out_shape=[jax.ShapeDtypeStruct((bsz, n, C_CH), BF16),
                   jax.ShapeDtypeStruct((bsz, n, C_CH), F32)],
        compiler_params=_params("parallel"),
        name="hyena_gate",
    )(z, cw, cb, skip)


HYENA_BATCH = 4


def _hyena_body(u_ref, us_ref, z_ref, cw_ref, cb_ref, fwd_ref, inv_ref, kc_ref, ks_ref, o_ref, y_ref, *, n, tf):
    j = pl.program_id(1)

    @pl.when(j == 0)
    def _():
        y_ref[...] = jnp.zeros_like(y_ref)

    kc = kc_ref[...]
    ks = ks_ref[...]
    row = lax.broadcasted_iota(jnp.int32, kc.shape, 0) + j * tf
    first = row == 0
    wgt = jnp.where(first, 1.0, 2.0) * (1.0 / (2 * n))
    for b in range(HYENA_BATCH):
        spec = _dot(fwd_ref[0], u_ref[b])
        uc, us = spec[:tf], spec[tf:]
        re = jnp.where(first, uc * kc, uc * kc - us * ks) * wgt
        im = jnp.where(first, us * ks, uc * ks + us * kc) * wgt
        y_ref[b] += _dot(inv_ref[0], jnp.concatenate([re, im], axis=0).astype(BF16))

    @pl.when(j == pl.num_programs(1) - 1)
    def _():
        for b in range(HYENA_BATCH):
            x0 = _short_conv(z_ref[b], cw_ref, cb_ref, slice(0, C_CH), n)
            o_ref[b] = ((y_ref[b] + us_ref[b]) * x0).astype(BF16)


def _hyena(u, us, z, seg_start, n, cw, cb, fwd, inv, kc, ks):
    bsz = u.shape[0]
    assert bsz % HYENA_BATCH == 0, bsz
    tf = min(FREQ_TILE, n)
    nf = n // tf
    seg = seg_start // n
    seq = pl.BlockSpec((HYENA_BATCH, n, C_CH), lambda g, j: (g, 0, 0))
    once = pl.Buffered(1)
    small = lambda r, c: pl.BlockSpec((r, c), lambda g, j: (0, 0))
    spec_blk = pl.BlockSpec((tf, C_CH), lambda g, j: (j, 0))
    return pl.pallas_call(
        functools.partial(_hyena_body, n=n, tf=tf),
        grid=(bsz // HYENA_BATCH, nf),
        in_specs=[seq,
                  pl.BlockSpec((HYENA_BATCH, n, C_CH), lambda g, j: (g, 0, 0), pipeline_mode=once),
                  pl.BlockSpec((HYENA_BATCH, n, C_CH), lambda g, j: (g, seg, 0), pipeline_mode=once),
                  small(3, 3 * C_CH), small(1, 3 * C_CH),
                  pl.BlockSpec((1, 2 * tf, n), lambda g, j: (j, 0, 0)),
                  pl.BlockSpec((1, n, 2 * tf), lambda g, j: (j, 0, 0)),
                  spec_blk, spec_blk],
        out_specs=seq,
        out_shape=jax.ShapeDtypeStruct((bsz, n, C_CH), BF16),
        scratch_shapes=[pltpu.VMEM((HYENA_BATCH, n, C_CH), F32)],
        compiler_params=_params("parallel", "arbitrary"),
        name="hyena_conv",
    )(u, us, z, cw, cb, fwd, inv, kc, ks)


def _implicit_filters(n, w1, b1, w2, b2, w3, freq):
    t = jnp.linspace(0.0, 1.0, n, dtype=F32)[:, None]
    w = 2 * math.pi * jnp.arange(n, dtype=F32)[:, None] / n
    f = jnp.linspace(1e-4, C_POS_BANDS - 1, C_POS_BANDS, dtype=F32)[None]
    z = jnp.concatenate([t, jnp.cos(f * w), -jnp.sin(f * w)], -1)
    h = jnp.sin(freq[0] * (jnp.dot(z, w1, precision=HIGHEST) + b1))
    h = jnp.sin(freq[1] * (jnp.dot(h, w2, precision=HIGHEST) + b2))
    h = jnp.dot(h, w3, precision=HIGHEST)
    lo = math.log(C_DECAY_TARGET) / C_SLOW_DECAY
    hi = math.log(C_DECAY_TARGET) / C_FAST_DECAY
    deltas = jnp.abs(jnp.linspace(lo, hi, C_CH, dtype=F32))
    decay = jnp.exp(-t * deltas[None])
    hf, hb = h[:, :C_CH] * decay, h[:, C_CH:] * decay
    hb = hb.at[0].set(0.0)
    return hf + hb, hf - hb


def _router_gates(logits_t, rb):
    tm = logits_t.shape[1]
    scores = _sigmoid(logits_t)
    sel = scores + rb
    sel3 = sel.reshape(N_GROUPS, GROUP_SIZE, tm)
    kk = lax.broadcasted_iota(jnp.int32, sel3.shape, 1)
    m1 = jnp.max(sel3, 1, keepdims=True)
    i1 = jnp.min(jnp.where(sel3 == m1, kk, GROUP_SIZE), 1, keepdims=True)
    m2 = jnp.max(jnp.where(kk == i1, -jnp.inf, sel3), 1, keepdims=True)
    gscore = (m1 + m2).reshape(N_GROUPS, tm)
    gid = lax.broadcasted_iota(jnp.int32, gscore.shape, 0)
    rank = jnp.zeros(gscore.shape, jnp.int32)
    for g in range(N_GROUPS):
        other = gscore[g:g + 1, :]
        ahead = (other > gscore) | ((other == gscore) & (g < gid))
        rank = rank + ahead.astype(jnp.int32)
    gsel = (rank < TOPK_GROUPS).reshape(N_GROUPS, 1, tm)
    cand = jnp.where(gsel, sel3, -jnp.inf).reshape(N_EXPERTS, tm)
    eid = lax.broadcasted_iota(jnp.int32, cand.shape, 0)
    chosen = jnp.zeros(cand.shape, jnp.bool_)
    for _ in range(TOP_K):
        m = jnp.max(cand, 0, keepdims=True)
        pick = eid == jnp.min(jnp.where(cand == m, eid, N_EXPERTS), 0, keepdims=True)
        chosen = chosen | pick
        cand = jnp.where(pick, -jnp.inf, cand)
    wsel = jnp.where(chosen, scores, 0.0)
    return wsel / jnp.sum(wsel, 0, keepdims=True) * ROUTED_SCALE


def _out_body(oa_ref, ob_ref, oc_ref, x_ref, mod_ref, wa_ref, wb_ref, wc_ref, lng_ref, lnb_ref,
              rwt_ref, rb_ref, xmid_ref, h_ref, gate_ref, *, tm):
    for rows, is_ctx in _sub_tiles(tm):
        attn = (_dot(oa_ref[0, rows, :], wa_ref[...]) + _dot(ob_ref[0, rows, :], wb_ref[...])
                + _dot(oc_ref[0, rows, :], wc_ref[...]))
        y = DN_ALPHA * x_ref[0, rows, :] + _mod_row(mod_ref, is_ctx, 2) * attn
        xm = _layer_norm(y) * lng_ref[...] + lnb_ref[...]
        xmid_ref[0, rows, :] = xm
        h = _layer_norm(xm) * (1.0 + _mod_row(mod_ref, is_ctx, 4)) + _mod_row(mod_ref, is_ctx, 3)
        hi, lo = _split_bf16(h)
        h_ref[0, rows, :] = hi
        logits_t = _dot_nt(rwt_ref[...], jnp.concatenate([hi, lo, hi], axis=1))
        gates_t = _router_gates(logits_t, rb_ref[...])
        pad = jnp.zeros((128 - N_EXPERTS, gates_t.shape[1]), F32)
        gates = jnp.concatenate([gates_t, pad], axis=0).T
        for g in range(N_GROUPS):
            gate_ref[0, g, rows, :] = gates[:, g * GROUP_SIZE:(g + 1) * GROUP_SIZE]


def _out_proj(oa, ob, oc, x, mod, wa, wb, wc, lng, lnb, rwt3, rb):
    bsz, s, _ = oa.shape
    d = D_MODEL
    tm = _step_tokens(s)
    tok = lambda w: pl.BlockSpec((1, tm, w), lambda b, i: (b, i, 0))
    full = lambda r, c: pl.BlockSpec((r, c), lambda b, i: (0, 0))
    return pl.pallas_call(
        functools.partial(_out_body, tm=tm),
        grid=(bsz, s // tm),
        in_specs=[tok(A_W), tok(B_W), tok(C_CH), tok(d),
                  pl.BlockSpec((1, 2, 6, d), lambda b, i: (b, 0, 0, 0)),
                  full(A_W, d), full(B_W, d), full(C_CH, d), full(1, d), full(1, d),
                  full(N_EXPERTS, 3 * d), full(N_EXPERTS, 1)],
        out_specs=[tok(d), tok(d),
                   pl.BlockSpec((1, N_GROUPS, tm, GROUP_SIZE), lambda b, i: (b, 0, i, 0))],
        out_shape=[jax.ShapeDtypeStruct((bsz, s, d), F32),
                   jax.ShapeDtypeStruct((bsz, s, d), BF16),
                   jax.ShapeDtypeStruct((bsz, N_GROUPS, s, GROUP_SIZE), F32)],
        compiler_params=_params("parallel", "parallel"),
        name="proj_out_router",
    )(oa, ob, oc, x, mod, wa, wb, wc, lng, lnb, rwt3, rb)


MOE_CHAINS = 2


def _moe_body(h_ref, gate_ref, xmid_ref, mod_ref, w1_ref, w3_ref, w2_ref,
              sw1_ref, sw3_ref, sw2_ref, lng_ref, lnb_ref, o_ref, acc_ref, *, tm):
    g = pl.program_id(2)
    rc = tm // MOE_CHAINS

    @pl.when(g == 0)
    def _():
        h = h_ref[0]
        a = _dot(h, sw1_ref[...])
        act = (a * _sigmoid(a) * _dot(h, sw3_ref[...])).astype(BF16)
        acc_ref[...] = _dot(act, sw2_ref[...])

    w1 = jnp.concatenate([w1_ref[k] for k in range(GROUP_SIZE)], axis=1)
    w3 = jnp.concatenate([w3_ref[k] for k in range(GROUP_SIZE)], axis=1)
    w2 = w2_ref[...].reshape(GROUP_SIZE * EXPERT_HIDDEN, D_MODEL)
    for c in range(MOE_CHAINS):
        rows = slice(c * rc, (c + 1) * rc)
        h = h_ref[0, rows, :]
        a = _dot(h, w1)
        act = a * _sigmoid(a) * _dot(h, w3)
        gt = gate_ref[0, 0, rows, :]
        parts = [act[:, k * EXPERT_HIDDEN:(k + 1) * EXPERT_HIDDEN] * gt[:, k:k + 1]
                 for k in range(GROUP_SIZE)]
        acc_ref[rows, :] += _dot(jnp.concatenate(parts, axis=1).astype(BF16), w2)

    @pl.when(g == N_GROUPS - 1)
    def _():
        row = lax.broadcasted_iota(jnp.int32, (tm, 1), 0) + pl.program_id(1) * tm
        gate2 = jnp.where(row >= SEQ, mod_ref[0, 1, 5:6, :], mod_ref[0, 0, 5:6, :])
        y = DN_ALPHA * xmid_ref[0] + gate2 * acc_ref[...]
        o_ref[0] = _layer_norm(y) * lng_ref[...] + lnb_ref[...]


MOE_TILE_2BUF = 768
MOE_TILE_1BUF = 1024


def _moe(h, gates, xmid, mod, layer, w1_all, w3_all, w2_all, sw1, sw3, sw2, lng, lnb):
    bsz, s, d = h.shape
    tm = MOE_TILE_1BUF if s % MOE_TILE_1BUF == 0 else MOE_TILE_2BUF
    tok = pl.BlockSpec((1, tm, d), lambda b, i, g: (b, i, 0))
    tok_once = pl.BlockSpec((1, tm, d), lambda b, i, g: (b, i, 0),
                            pipeline_mode=pl.Buffered(1 if tm > MOE_TILE_2BUF else 2))
    full = lambda r, c: pl.BlockSpec((r, c), lambda b, i, g: (0, 0))
    up = pl.BlockSpec((None, GROUP_SIZE, d, EXPERT_HIDDEN), lambda b, i, g: (layer, g, 0, 0))
    return pl.pallas_call(
        functools.partial(_moe_body, tm=tm),
        grid=(bsz, s // tm, N_GROUPS),
        in_specs=[tok,
                  pl.BlockSpec((1, 1, tm, GROUP_SIZE), lambda b, i, g: (b, g, i, 0)),
                  tok_once,
                  pl.BlockSpec((1, 2, 6, d), lambda b, i, g: (b, 0, 0, 0)),
                  up, up,
                  pl.BlockSpec((None, GROUP_SIZE, EXPERT_HIDDEN, d), lambda b, i, g: (layer, g, 0, 0)),
                  full(d, SHARED_HIDDEN), full(d, SHARED_HIDDEN), full(SHARED_HIDDEN, d),
                  full(1, d), full(1, d)],
        out_specs=tok,
        out_shape=jax.ShapeDtypeStruct((bsz, s, d), F32),
        scratch_shapes=[pltpu.VMEM((tm, d), F32)],
        compiler_params=_params("parallel", "parallel", "arbitrary"),
        name="moe_ffn",
    )(h, gates, xmid, mod, w1_all, w3_all, w2_all, sw1, sw3, sw2, lng, lnb)


def _rope_tables():
    t = jnp.arange(SEQ)
    row = (t // GRID_W).astype(F32)
    col = (t % GRID_W).astype(F32)
    inv = ROPE_THETA ** (-jnp.arange(0, HALF, 2, dtype=F32) / HALF)
    ang = jnp.concatenate([row[:, None] * inv, col[:, None] * inv], -1)
    cos, sin = jnp.cos(ang), jnp.sin(ang)
    n_heads = ROPE_TABLE_W // HEAD_DIM
    cos_t = jnp.concatenate([jnp.tile(jnp.concatenate([cos, cos], -1), (1, n_heads)),
                             jnp.ones((CTX_LEN, ROPE_TABLE_W), F32)], 0)
    sin_t = jnp.concatenate([jnp.tile(jnp.concatenate([-sin, sin], -1), (1, n_heads)),
                             jnp.zeros((CTX_LEN, ROPE_TABLE_W), F32)], 0)
    return cos_t, sin_t


def _modulation(c, c_ctx, w, b):
    cvec = jnp.concatenate([c, c_ctx[None]], axis=0)
    m = (jnp.dot(jax.nn.silu(cvec), w, precision=HIGHEST) + b).reshape(-1, 6, D_MODEL)
    return jnp.stack([m[:-1], jnp.broadcast_to(m[-1], m[:-1].shape)], axis=1)


def kernel(x, c, ctx, c_ctx, ada_w, ada_b, w_in, q_gain, k_gain, nat_rpb, hy_conv_w, hy_conv_b, hy_f_w1, hy_f_b1, hy_f_w2, hy_f_b2, hy_f_w3, hy_freq, hy_skip, w_out, ln1_g, ln1_b, router_w, router_b, exp_w1, exp_w3, exp_w2, sh_w1, sh_w3, sh_w2, ln2_g, ln2_b):
    bsz = x.shape[0]
    d = D_MODEL
    xs = jnp.concatenate([x, ctx], axis=1)

    head_perm = np.concatenate([np.arange(0, HEAD_DIM, 2), np.arange(1, HEAD_DIM, 2)])
    qk_perm = (np.arange(QK_W // HEAD_DIM)[:, None] * HEAD_DIM + head_perm[None, :]).reshape(-1)
    perm_mat = jnp.asarray(np.eye(QK_W)[:, qk_perm], BF16)
    cos_t, sin_t = _rope_tables()
    bd = jnp.asarray(np.kron(np.eye(QK_W // HEAD_DIM), np.ones((HEAD_DIM, HEAD_DIM))), BF16)
    tables_lat = _dft_tables(SEQ)
    tables_ctx = _dft_tables(CTX_LEN)
    w1_bf, w3_bf, w2_bf = exp_w1.astype(BF16), exp_w3.astype(BF16), exp_w2.astype(BF16)

    for l in range(DEPTH):
        last = l == DEPTH - 1
        mod = _modulation(c, c_ctx, ada_w[l], ada_b[l])

        w_bf = w_in[l].astype(BF16)
        w_qk = jnp.dot(w_bf[:, :QK_W], perm_mat, preferred_element_type=F32).astype(BF16)
        w_bf = jnp.concatenate([w_qk, w_bf[:, QK_W:]], axis=1)
        gain = jnp.concatenate([jnp.tile(q_gain[l][head_perm] * ATT_SCALE, A_HEADS),
                                jnp.tile(k_gain[l][head_perm], A_KV_HEADS)])[None].astype(F32)
        qa, ka, va, qb, kb, vb, z = _project(xs, mod, w_bf, bd, gain, cos_t, sin_t)

        oa = _gqa(qa, ka, va)
        ob = _na(qb, kb, vb, _na_bias_table(nat_rpb[l]))
        if not last:
            oa = jnp.concatenate([oa, _ctx_attn(qa, ka, va, 1)], axis=1)
            ob = jnp.concatenate([ob, _ctx_attn(qb, kb, vb, NA_HEADS_PER_STEP)], axis=1)

        filt = (hy_f_w1[l], hy_f_b1[l], hy_f_w2[l], hy_f_b2[l], hy_f_w3[l], hy_freq[l])
        cw, cb, skip = hy_conv_w[l], hy_conv_b[l][None], hy_skip[l][None]
        segments = [(0, SEQ, tables_lat)] + ([] if last else [(SEQ, CTX_LEN, tables_ctx)])
        oc = []
        for seg_start, n, (fc, fs, fwd, inv) in segments:
            kc, ks = _filter_spectrum(fc, fs, *_implicit_filters(n, *filt))
            u, us = _hyena_gate(z, seg_start, n, cw, cb, skip)
            oc.append(_hyena(u, us, z, seg_start, n, cw, cb, fwd, inv, kc, ks))
        oc = oc[0] if last else jnp.concatenate(oc, axis=1)

        wo = w_out[l].astype(BF16)
        rw_hi, rw_lo = _split_bf16(router_w[l].T)
        rwt3 = jnp.concatenate([rw_hi, rw_hi, rw_lo], axis=1)
        xmid, h, gates = _out_proj(oa, ob, oc, xs, mod, wo[:A_W], wo[A_W:A_W + B_W], wo[A_W + B_W:],
                                   ln1_g[l][None], ln1_b[l][None], rwt3, router_b[l][:, None])

        xs = _moe(h, gates, xmid, mod, l, w1_bf, w3_bf, w2_bf,
                  sh_w1[l].astype(BF16), sh_w3[l].astype(BF16), sh_w2[l].astype(BF16),
                  ln2_g[l][None], ln2_b[l][None])
    return xs
```

```python
import functools
import math

import numpy as np
import jax
import jax.numpy as jnp
from jax import lax
from jax.experimental import pallas as pl
from jax.experimental.pallas import tpu as pltpu

F32 = jnp.float32
BF16 = jnp.bfloat16
HIGHEST = lax.Precision.HIGHEST

D_MODEL = 1024
SEQ = 2048
DEPTH = 2
GRID_W = 64
GRID_ROWS = SEQ // GRID_W
CTX_LEN = 256
S_ALL = SEQ + CTX_LEN
HEAD_DIM = 64
HALF = HEAD_DIM // 2
A_HEADS = 8
A_KV_HEADS = 2
A_REP = A_HEADS // A_KV_HEADS
ROPE_THETA = 10000.0
B_HEADS = 4
WIN_ROWS = 8
WIN_COLS = 16
C_CH = 256
C_POS_BANDS = 16
C_DECAY_TARGET = 1e-2
C_FAST_DECAY = 0.3
C_SLOW_DECAY = 1.5
A_W = A_HEADS * HEAD_DIM
A_KV_W = A_KV_HEADS * HEAD_DIM
B_W = B_HEADS * HEAD_DIM
A_K0 = A_W
A_V0 = A_K0 + A_KV_W
B_Q0 = A_V0 + A_KV_W
B_K0 = B_Q0 + B_W
B_V0 = B_K0 + B_W
C0 = B_V0 + B_W
PROJ_W = C0 + 3 * C_CH
QK_W = A_W + A_KV_W
N_EXPERTS = 64
TOP_K = 8
N_GROUPS = 8
GROUP_SIZE = N_EXPERTS // N_GROUPS
TOPK_GROUPS = 4
EXPERT_HIDDEN = 128
SHARED_HIDDEN = 256
ROUTED_SCALE = 2.5
DN_ALPHA = (2 * DEPTH) ** 0.25
LN_EPS = 1e-6
RMS_EPS = 1e-6
ATT_SCALE = HEAD_DIM ** -0.5
NEG_BIG = -1e30

VMEM_LIMIT_BYTES = 56 * 1024 * 1024
TOKEN_TILE = 256
ROPE_TABLE_W = 128
GQA_Q_TILE = 512
GQA_KEY_CHUNK = 768
FREQ_TILE = 256


def _params(*sem):
    return pltpu.CompilerParams(dimension_semantics=sem, vmem_limit_bytes=VMEM_LIMIT_BYTES)


def _layer_norm(v):
    mu = jnp.mean(v, -1, keepdims=True)
    vc = v - mu
    var = jnp.mean(vc * vc, -1, keepdims=True)
    return vc * lax.rsqrt(var + LN_EPS)


def _sigmoid(v):
    return 1.0 / (1.0 + jnp.exp(-v))


def _dot(a, b):
    return jnp.dot(a, b, preferred_element_type=F32)


def _dot_nt(a, b):
    return lax.dot_general(a, b, (((1,), (1,)), ((), ())), preferred_element_type=F32)


def _mod_row(mod_ref, is_ctx, idx):
    return jnp.where(is_ctx, mod_ref[0, 1, idx:idx + 1, :], mod_ref[0, 0, idx:idx + 1, :])


def _sub_tiles(tm):
    for k in range(tm // TOKEN_TILE):
        rows = slice(k * TOKEN_TILE, (k + 1) * TOKEN_TILE)
        is_ctx = (pl.program_id(1) * tm + k * TOKEN_TILE) >= SEQ
        yield k, rows, is_ctx


def _stream_specs(tm, ctx_block):
    n_sub = tm // TOKEN_TILE
    last_lat = SEQ // TOKEN_TILE - 1
    lat = [pl.BlockSpec((1, TOKEN_TILE, D_MODEL),
                        lambda b, i, k=k: (b, jnp.minimum(i * n_sub + k, last_lat), 0)) for k in range(n_sub)]
    return lat + [pl.BlockSpec((1, TOKEN_TILE, D_MODEL), lambda b, i: (b, ctx_block, 0))]


def _stream_tile(x_refs, k, is_ctx):
    return jnp.where(is_ctx, x_refs[-1][0], x_refs[k][0])


def _proj_body(*refs, tm):
    n_x = tm // TOKEN_TILE + 1
    x_refs = refs[:n_x]
    (mod_ref, w_ref, bd_ref, gain_ref, cos_ref, sin_ref,
     qa_ref, ka_ref, va_ref, qb_ref, kb_ref, vb_ref, z_ref) = refs[n_x:]
    n_rep = QK_W // ROPE_TABLE_W
    for k, rows, is_ctx in _sub_tiles(tm):
        xn = _layer_norm(_stream_tile(x_refs, k, is_ctx))
        hx = (xn * (1.0 + _mod_row(mod_ref, is_ctx, 1)) + _mod_row(mod_ref, is_ctx, 0)).astype(BF16)
        px = _dot(hx, w_ref[...])
        qk = px[:, :QK_W]
        ss = _dot((qk * qk).astype(BF16), bd_ref[...])
        qkn = qk * lax.rsqrt(ss * (1.0 / HEAD_DIM) + RMS_EPS) * gain_ref[...]
        lane = lax.broadcasted_iota(jnp.int32, qkn.shape, 1)
        partner = jnp.where((lane % HEAD_DIM) < HALF,
                            pltpu.roll(qkn, QK_W - HALF, 1), pltpu.roll(qkn, HALF, 1))
        cos = jnp.concatenate([cos_ref[rows, :]] * n_rep, axis=1)
        sin = jnp.concatenate([sin_ref[rows, :]] * n_rep, axis=1)
        qkr = (qkn * cos + partner * sin).astype(BF16)
        for h in range(A_HEADS):
            qa_ref[0, h, rows, :] = qkr[:, h * HEAD_DIM:(h + 1) * HEAD_DIM]
        for g in range(A_KV_HEADS):
            ka_ref[0, g, rows, :] = qkr[:, A_K0 + g * HEAD_DIM:A_K0 + (g + 1) * HEAD_DIM]
            va_ref[0, g, rows, :] = px[:, A_V0 + g * HEAD_DIM:A_V0 + (g + 1) * HEAD_DIM].astype(BF16)
        for h in range(B_HEADS):
            qb_ref[0, h, rows, :] = (px[:, B_Q0 + h * HEAD_DIM:B_Q0 + (h + 1) * HEAD_DIM] * ATT_SCALE).astype(BF16)
            kb_ref[0, h, rows, :] = px[:, B_K0 + h * HEAD_DIM:B_K0 + (h + 1) * HEAD_DIM].astype(BF16)
            vb_ref[0, h, rows, :] = px[:, B_V0 + h * HEAD_DIM:B_V0 + (h + 1) * HEAD_DIM].astype(BF16)
        z_ref[0, rows, :] = px[:, C0:]


def _step_tokens(s):
    return 3 * TOKEN_TILE if s % (3 * TOKEN_TILE) == 0 else 2 * TOKEN_TILE


def _project(x_lat, x_ctx, ctx_block, mod, w_bf, bd, gain, cos_t, sin_t):
    bsz, d, s = x_lat.shape[0], D_MODEL, S_ALL
    tm = _step_tokens(s)
    nt = s // tm
    heads = lambda n: pl.BlockSpec((1, n, tm, HEAD_DIM), lambda b, i: (b, 0, i, 0))
    hshape = lambda n: jax.ShapeDtypeStruct((bsz, n, s, HEAD_DIM), BF16)
    return pl.pallas_call(
        functools.partial(_proj_body, tm=tm),
        grid=(bsz, nt),
        in_specs=_stream_specs(tm, ctx_block) + [
            pl.BlockSpec((1, 2, 6, d), lambda b, i: (b, 0, 0, 0)),
            pl.BlockSpec((d, PROJ_W), lambda b, i: (0, 0)),
            pl.BlockSpec((QK_W, QK_W), lambda b, i: (0, 0)),
            pl.BlockSpec((1, QK_W), lambda b, i: (0, 0)),
            pl.BlockSpec((tm, ROPE_TABLE_W), lambda b, i: (i, 0)),
            pl.BlockSpec((tm, ROPE_TABLE_W), lambda b, i: (i, 0)),
        ],
        out_specs=[heads(A_HEADS), heads(A_KV_HEADS), heads(A_KV_HEADS),
                   heads(B_HEADS), heads(B_HEADS), heads(B_HEADS),
                   pl.BlockSpec((1, tm, 3 * C_CH), lambda b, i: (b, i, 0))],
        out_shape=[hshape(A_HEADS), hshape(A_KV_HEADS), hshape(A_KV_HEADS),
                   hshape(B_HEADS), hshape(B_HEADS), hshape(B_HEADS),
                   jax.ShapeDtypeStruct((bsz, s, 3 * C_CH), F32)],
        compiler_params=_params("parallel", "parallel"),
        name="proj_in",
    )(*([x_lat] * (tm // TOKEN_TILE)), x_ctx, mod, w_bf, bd, gain, cos_t, sin_t)


def _gqa_body(q_ref, k_ref, v_ref, o_ref):
    tq = q_ref.shape[2]
    n_keys = k_ref.shape[2]
    for h in range(A_REP):
        q = q_ref[0, h]
        m = l = acc = None
        for j in range(n_keys // GQA_KEY_CHUNK):
            keys = slice(j * GQA_KEY_CHUNK, (j + 1) * GQA_KEY_CHUNK)
            s = _dot_nt(q, k_ref[0, 0, keys, :])
            mj = jnp.max(s, -1, keepdims=True)
            if j == 0:
                m = mj
                p = jnp.exp(s - m)
                l = jnp.sum(p, -1, keepdims=True)
                acc = _dot(p.astype(BF16), v_ref[0, 0, keys, :])
            else:
                m_new = jnp.maximum(m, mj)
                alpha = jnp.exp(m - m_new)
                p = jnp.exp(s - m_new)
                l = alpha * l + jnp.sum(p, -1, keepdims=True)
                acc = alpha * acc + _dot(p.astype(BF16), v_ref[0, 0, keys, :])
                m = m_new
        o_ref[0, :, h * HEAD_DIM:(h + 1) * HEAD_DIM] = (acc / l).astype(BF16)


def _gqa(qa, ka, va, out_rows):
    bsz, _, s, _ = qa.shape
    tq = GQA_Q_TILE
    return pl.pallas_call(
        _gqa_body,
        grid=(bsz, A_KV_HEADS, SEQ // tq),
        in_specs=[
            pl.BlockSpec((1, A_REP, tq, HEAD_DIM), lambda b, g, i: (b, g, i, 0)),
            pl.BlockSpec((1, 1, s, HEAD_DIM), lambda b, g, i: (b, g, 0, 0)),
            pl.BlockSpec((1, 1, s, HEAD_DIM), lambda b, g, i: (b, g, 0, 0)),
        ],
        out_specs=pl.BlockSpec((1, tq, A_REP * HEAD_DIM), lambda b, g, i: (b, i, g)),
        out_shape=jax.ShapeDtypeStruct((bsz, out_rows, A_W), BF16),
        compiler_params=_params("parallel", "parallel", "parallel"),
        name="gqa_attn",
    )(qa, ka, va)


NA_HEADS_PER_STEP = 2
NA_Q_ROWS = 8
NA_K_ROWS = 16
NA_Q_TOKENS = NA_Q_ROWS * GRID_W
NA_K_TOKENS = NA_K_ROWS * GRID_W
NA_Q_BLOCKS = GRID_ROWS // NA_Q_ROWS
NA_KEY_ROW0 = (0, 4, 12, 16)
NA_PATTERN_BLOCKS = (0, 1, 3)


def _na_body(q_ref, k_ref, v_ref, bias_ref, o_ref):
    j = pl.program_id(2)
    key_row0 = jnp.where(j == 0, NA_KEY_ROW0[0],
                         jnp.where(j == 1, NA_KEY_ROW0[1], jnp.where(j == 2, NA_KEY_ROW0[2], NA_KEY_ROW0[3])))
    k0 = pl.multiple_of(key_row0 * GRID_W, 4 * GRID_W)
    for hh in range(NA_HEADS_PER_STEP):
        q = q_ref[0, hh]
        kw = k_ref[0, hh, pl.ds(k0, NA_K_TOKENS), :]
        vw = v_ref[0, hh, pl.ds(k0, NA_K_TOKENS), :]
        kc = k_ref[0, hh, SEQ:, :]
        vc = v_ref[0, hh, SEQ:, :]
        sw = _dot_nt(q, kw) + bias_ref[hh, 0]
        sc = _dot_nt(q, kc)
        m = jnp.maximum(jnp.max(sw, -1, keepdims=True), jnp.max(sc, -1, keepdims=True))
        pw = jnp.exp(sw - m)
        pc = jnp.exp(sc - m)
        l = jnp.sum(pw, -1, keepdims=True) + jnp.sum(pc, -1, keepdims=True)
        o = (_dot(pw.astype(BF16), vw) + _dot(pc.astype(BF16), vc)) / l
        o_ref[0, :, hh * HEAD_DIM:(hh + 1) * HEAD_DIM] = o.astype(BF16)


def _na(qb, kb, vb, bias, out_rows):
    bsz, _, s, _ = qb.shape
    hp = NA_HEADS_PER_STEP
    kv = pl.BlockSpec((1, hp, s, HEAD_DIM), lambda b, h, j: (b, h, 0, 0))
    return pl.pallas_call(
        _na_body,
        grid=(bsz, B_HEADS // hp, NA_Q_BLOCKS),
        in_specs=[pl.BlockSpec((1, hp, NA_Q_TOKENS, HEAD_DIM), lambda b, h, j: (b, h, j, 0)), kv, kv,
                  pl.BlockSpec((hp, 1, NA_Q_TOKENS, NA_K_TOKENS), lambda b, h, j: (h, (j + 1) // 2, 0, 0))],
        out_specs=pl.BlockSpec((1, NA_Q_TOKENS, hp * HEAD_DIM), lambda b, h, j: (b, j, h)),
        out_shape=jax.ShapeDtypeStruct((bsz, out_rows, B_W), BF16),
        compiler_params=_params("parallel", "parallel", "arbitrary"),
        name="nbr_attn",
    )(qb, kb, vb, bias)


def _na_bias_body(t_ref, o_ref):
    o_ref[...] = jnp.full(o_ref.shape, NEG_BIG, F32)
    for p, blk in enumerate(NA_PATTERN_BLOCKS):
        @pl.when(pl.program_id(1) == p)
        def _(blk=blk):
            for qi in range(NA_Q_ROWS):
                qrow = blk * NA_Q_ROWS + qi
                rs = min(max(qrow - WIN_ROWS // 2, 0), GRID_ROWS - WIN_ROWS)
                for krow in range(rs, rs + WIN_ROWS):
                    ki = krow - NA_KEY_ROW0[blk]
                    o_ref[0, 0, qi * GRID_W:(qi + 1) * GRID_W, ki * GRID_W:(ki + 1) * GRID_W] = (
                        t_ref[0, krow - qrow + WIN_ROWS - 1])


def _na_bias_table(rpb):
    col = np.arange(GRID_W)
    col_start = np.clip(col - WIN_COLS // 2, 0, GRID_W - WIN_COLS)
    in_cols = (col[None, :] >= col_start[:, None]) & (col[None, :] < col_start[:, None] + WIN_COLS)
    dc = np.clip(col[None, :] - col[:, None] + WIN_COLS - 1, 0, 2 * WIN_COLS - 2)
    onehot = jnp.asarray(dc[None] == np.arange(2 * WIN_COLS - 1)[:, None, None], F32)
    tiles = jnp.einsum('hdj,jqk->hdqk', rpb.astype(F32), onehot, precision=HIGHEST)
    tiles = jnp.where(in_cols[None, None], tiles, NEG_BIG)
    n_pat = len(NA_PATTERN_BLOCKS)
    return pl.pallas_call(
        _na_bias_body,
        grid=(B_HEADS, n_pat),
        in_specs=[pl.BlockSpec((1, 2 * WIN_ROWS - 1, GRID_W, GRID_W), lambda h, p: (h, 0, 0, 0))],
        out_specs=pl.BlockSpec((1, 1, NA_Q_TOKENS, NA_K_TOKENS), lambda h, p: (h, p, 0, 0)),
        out_shape=jax.ShapeDtypeStruct((B_HEADS, n_pat, NA_Q_TOKENS, NA_K_TOKENS), F32),
        compiler_params=_params("parallel", "arbitrary"),
        name="nbr_bias",
    )(tiles)


def _ctx_attn_body(q_ref, k_ref, v_ref, _, o_ref, *, rep):
    for g in range(k_ref.shape[1]):
        for r in range(rep):
            hh = g * rep + r
            s = _dot_nt(q_ref[0, hh], k_ref[0, g])
            m = jnp.max(s, -1, keepdims=True)
            p = jnp.exp(s - m)
            l = jnp.sum(p, -1, keepdims=True)
            o = _dot(p.astype(BF16), v_ref[0, g]) / l
            o_ref[0, :, hh * HEAD_DIM:(hh + 1) * HEAD_DIM] = o.astype(BF16)


def _ctx_attn(q, k, v, kv_per_step, into):
    bsz, n_q, _, _ = q.shape
    n_kv = k.shape[1]
    rep = n_q // n_kv
    seg = lambda n: pl.BlockSpec((1, n, CTX_LEN, HEAD_DIM), lambda b, h: (b, h, SEQ // CTX_LEN, 0))
    return pl.pallas_call(
        functools.partial(_ctx_attn_body, rep=rep),
        grid=(bsz, n_kv // kv_per_step),
        in_specs=[seg(kv_per_step * rep), seg(kv_per_step), seg(kv_per_step),
                  pl.BlockSpec(memory_space=pl.ANY)],
        out_specs=pl.BlockSpec((1, CTX_LEN, kv_per_step * rep * HEAD_DIM),
                               lambda b, h: (b, SEQ // CTX_LEN, h)),
        out_shape=jax.ShapeDtypeStruct(into.shape, into.dtype),
        input_output_aliases={3: 0},
        compiler_params=_params("parallel", "parallel"),
        name="ctx_attn",
    )(q, k, v, into)


DFT_FINE = 16


def _dft_table_body(chr_ref, shr_ref, clr_ref, slr_ref, chc_ref, shc_ref, clc_ref, slc_ref,
                    fwd_ref, inv_ref, *, n, tf):
    j = pl.program_id(0)

    def expand(c_hi, s_hi, c_lo, s_lo, rows):
        cos = (c_hi[:, None, :] * c_lo[None, :, :] - s_hi[:, None, :] * s_lo[None, :, :])
        sin = (s_hi[:, None, :] * c_lo[None, :, :] + c_hi[:, None, :] * s_lo[None, :, :])
        return cos.reshape(rows, -1), sin.reshape(rows, -1)

    cos_r, sin_r = expand(chr_ref[...], shr_ref[...], clr_ref[...], slr_ref[...], tf)
    cos_c, sin_c = expand(chc_ref[...], shc_ref[...], clc_ref[...], slc_ref[...], n)
    fr = lax.broadcasted_iota(jnp.int32, (tf, n), 0) + j * tf
    tr = lax.broadcasted_iota(jnp.int32, (tf, n), 1)
    sin_r = jnp.where(fr == 0, jnp.where(tr % 2 == 0, 1.0, -1.0), sin_r)
    tc = lax.broadcasted_iota(jnp.int32, (n, tf), 0)
    fcol = lax.broadcasted_iota(jnp.int32, (n, tf), 1) + j * tf
    sin_c = jnp.where(fcol == 0, jnp.where(tc % 2 == 0, 1.0, -1.0), sin_c)
    fwd_ref[0, :tf, :] = cos_r.astype(BF16)
    fwd_ref[0, tf:, :] = sin_r.astype(BF16)
    inv_ref[0, :, :tf] = cos_c.astype(BF16)
    inv_ref[0, :, tf:] = sin_c.astype(BF16)


def _dft_tables(n):
    s = jnp.arange(n, dtype=jnp.int32)[None, :]

    def cos_sin(f):
        ang = ((f * s) % (2 * n)).astype(F32) * (math.pi / n)
        return jnp.cos(ang), jnp.sin(ang)

    ch, sh = cos_sin(jnp.arange(n // DFT_FINE, dtype=jnp.int32)[:, None] * DFT_FINE)
    cl, sl = cos_sin(jnp.arange(DFT_FINE, dtype=jnp.int32)[:, None])
    tf = min(FREQ_TILE, n)
    nf = n // tf
    hi_rows = pl.BlockSpec((tf // DFT_FINE, n), lambda j: (j, 0))
    lo_rows = pl.BlockSpec((DFT_FINE, n), lambda j: (0, 0))
    hi_cols = pl.BlockSpec((n // DFT_FINE, tf), lambda j: (0, j))
    lo_cols = pl.BlockSpec((DFT_FINE, tf), lambda j: (0, j))
    return pl.pallas_call(
        functools.partial(_dft_table_body, n=n, tf=tf),
        grid=(nf,),
        in_specs=[hi_rows, hi_rows, lo_rows, lo_rows, hi_cols, hi_cols, lo_cols, lo_cols],
        out_specs=[pl.BlockSpec((1, 2 * tf, n), lambda j: (j, 0, 0)),
                   pl.BlockSpec((1, n, 2 * tf), lambda j: (j, 0, 0))],
        out_shape=[jax.ShapeDtypeStruct((nf, 2 * tf, n), BF16),
                   jax.ShapeDtypeStruct((nf, n, 2 * tf), BF16)],
        compiler_params=_params("parallel"),
        name="dft_tables",
    )(ch, sh, cl, sl, ch, sh, cl, sl)


def _split_bf16(v):
    hi = v.astype(BF16)
    lo = (v - hi.astype(F32)).astype(BF16)
    return hi, lo


def _spectrum_body(fwd_ref, kp_ref, km_ref, kc_ref, ks_ref, *, tf):
    kp_hi, kp_lo = _split_bf16(kp_ref[...])
    km_hi, km_lo = _split_bf16(km_ref[...])
    fc = fwd_ref[0, :tf, :]
    fs = fwd_ref[0, tf:, :]
    kc_ref[...] = _dot(fc, kp_hi) + _dot(fc, kp_lo)
    ks_sin = _dot(fs, km_hi) + _dot(fs, km_lo)
    ks_nyq = _dot(fs, kp_hi) + _dot(fs, kp_lo)
    row = lax.broadcasted_iota(jnp.int32, ks_sin.shape, 0) + pl.program_id(0) * tf
    ks_ref[...] = jnp.where(row == 0, ks_nyq, ks_sin)


def _filter_spectrum(fwd, kplus, kminus):
    nf, tf2, n = fwd.shape
    tf = tf2 // 2
    filt = pl.BlockSpec((n, C_CH), lambda j: (0, 0))
    out = pl.BlockSpec((tf, C_CH), lambda j: (j, 0))
    return pl.pallas_call(
        functools.partial(_spectrum_body, tf=tf),
        grid=(nf,),
        in_specs=[pl.BlockSpec((1, tf2, n), lambda j: (j, 0, 0)), filt, filt],
        out_specs=[out, out],
        out_shape=[jax.ShapeDtypeStruct((n, C_CH), F32)] * 2,
        compiler_params=_params("parallel"),
        name="hyena_spectrum",
    )(fwd, kplus, kminus)


def _short_conv(z, cw_ref, cb_ref, cols, n):
    pos = lax.broadcasted_iota(jnp.int32, z.shape, 0)
    prev = jnp.where(pos == 0, 0.0, pltpu.roll(z, 1, 0))
    nxt = jnp.where(pos == n - 1, 0.0, pltpu.roll(z, n - 1, 0))
    return prev * cw_ref[0:1, cols] + z * cw_ref[1:2, cols] + nxt * cw_ref[2:3, cols] + cb_ref[:, cols]


def _hyena_gate_body(z_ref, cw_ref, cb_ref, skip_ref, u_ref, us_ref, *, n):
    zc = _short_conv(z_ref[0, :, C_CH:], cw_ref, cb_ref, slice(C_CH, 3 * C_CH), n)
    u = zc[:, C_CH:] * zc[:, :C_CH]
    u_ref[0] = u.astype(BF16)
    us_ref[0] = u * skip_ref[...]


def _hyena_gate(z, seg_start, n, cw, cb, skip):
    bsz = z.shape[0]
    seg = seg_start // n
    small = lambda r, c: pl.BlockSpec((r, c), lambda b: (0, 0))
    out = pl.BlockSpec((1, n, C_CH), lambda b: (b, 0, 0))
    return pl.pallas_call(
        functools.partial(_hyena_gate_body, n=n),
        grid=(bsz,),
        in_specs=[pl.BlockSpec((1, n, 3 * C_CH), lambda b: (b, seg, 0)),
                  small(3, 3 * C_CH), small(1, 3 * C_CH), small(1, C_CH)],
        out_specs=[out, out],
        out_shape=[jax.ShapeDtypeStruct((bsz, n, C_CH), BF16),
                   jax.ShapeDtypeStruct((bsz, n, C_CH), F32)],
        compiler_params=_params("parallel"),
        name="hyena_gate",
    )(z, cw, cb, skip)


HYENA_BATCH = 4


def _hyena_body(u_ref, us_ref, z_ref, cw_ref, cb_ref, fwd_ref, inv_ref, kc_ref, ks_ref, *rest, n, tf):
    o_ref, y_ref = rest[-2:]
    j = pl.program_id(1)

    @pl.when(j == 0)
    def _():
        y_ref[...] = jnp.zeros_like(y_ref)

    kc = kc_ref[...]
    ks = ks_ref[...]
    row = lax.broadcasted_iota(jnp.int32, kc.shape, 0) + j * tf
    first = row == 0
    wgt = jnp.where(first, 1.0, 2.0) * (1.0 / (2 * n))
    for b in range(HYENA_BATCH):
        spec = _dot(fwd_ref[0], u_ref[b])
        uc, us = spec[:tf], spec[tf:]
        re = jnp.where(first, uc * kc, uc * kc - us * ks) * wgt
        im = jnp.where(first, us * ks, uc * ks + us * kc) * wgt
        y_ref[b] += _dot(inv_ref[0], jnp.concatenate([re, im], axis=0).astype(BF16))

    @pl.when(j == pl.num_programs(1) - 1)
    def _():
        for b in range(HYENA_BATCH):
            x0 = _short_conv(z_ref[b], cw_ref, cb_ref, slice(0, C_CH), n)
            o_ref[b] = ((y_ref[b] + us_ref[b]) * x0).astype(BF16)


def _hyena(u, us, z, seg_start, n, cw, cb, fwd, inv, kc, ks, out_rows=None, into=None):
    bsz = u.shape[0]
    assert bsz % HYENA_BATCH == 0, bsz
    tf = min(FREQ_TILE, n)
    nf = n // tf
    seg = seg_start // n
    seq = pl.BlockSpec((HYENA_BATCH, n, C_CH), lambda g, j: (g, 0, 0))
    out_shape = jax.ShapeDtypeStruct((bsz, out_rows, C_CH), BF16) if into is None else \
        jax.ShapeDtypeStruct(into.shape, into.dtype)
    extra_in = [] if into is None else [into]
    extra_specs = [] if into is None else [pl.BlockSpec(memory_space=pl.ANY)]
    aliases = {} if into is None else {9: 0}
    once = pl.Buffered(1)
    small = lambda r, c: pl.BlockSpec((r, c), lambda g, j: (0, 0))
    spec_blk = pl.BlockSpec((tf, C_CH), lambda g, j: (j, 0))
    return pl.pallas_call(
        functools.partial(_hyena_body, n=n, tf=tf),
        grid=(bsz // HYENA_BATCH, nf),
        in_specs=[seq,
                  pl.BlockSpec((HYENA_BATCH, n, C_CH), lambda g, j: (g, 0, 0), pipeline_mode=once),
                  pl.BlockSpec((HYENA_BATCH, n, C_CH), lambda g, j: (g, seg, 0), pipeline_mode=once),
                  small(3, 3 * C_CH), small(1, 3 * C_CH),
                  pl.BlockSpec((1, 2 * tf, n), lambda g, j: (j, 0, 0)),
                  pl.BlockSpec((1, n, 2 * tf), lambda g, j: (j, 0, 0)),
                  spec_blk, spec_blk] + extra_specs,
        out_specs=pl.BlockSpec((HYENA_BATCH, n, C_CH), lambda g, j: (g, seg, 0)),
        out_shape=out_shape,
        input_output_aliases=aliases,
        scratch_shapes=[pltpu.VMEM((HYENA_BATCH, n, C_CH), F32)],
        compiler_params=_params("parallel", "arbitrary"),
        name="hyena_conv",
    )(u, us, z, cw, cb, fwd, inv, kc, ks, *extra_in)


def _implicit_filters(n, w1, b1, w2, b2, w3, freq):
    t = jnp.linspace(0.0, 1.0, n, dtype=F32)[:, None]
    w = 2 * math.pi * jnp.arange(n, dtype=F32)[:, None] / n
    f = jnp.linspace(1e-4, C_POS_BANDS - 1, C_POS_BANDS, dtype=F32)[None]
    z = jnp.concatenate([t, jnp.cos(f * w), -jnp.sin(f * w)], -1)
    h = jnp.sin(freq[0] * (jnp.dot(z, w1, precision=HIGHEST) + b1))
    h = jnp.sin(freq[1] * (jnp.dot(h, w2, precision=HIGHEST) + b2))
    h = jnp.dot(h, w3, precision=HIGHEST)
    lo = math.log(C_DECAY_TARGET) / C_SLOW_DECAY
    hi = math.log(C_DECAY_TARGET) / C_FAST_DECAY
    deltas = jnp.abs(jnp.linspace(lo, hi, C_CH, dtype=F32))
    decay = jnp.exp(-t * deltas[None])
    hf, hb = h[:, :C_CH] * decay, h[:, C_CH:] * decay
    hb = hb.at[0].set(0.0)
    return hf + hb, hf - hb


def _router_gates(logits_t, rb):
    tm = logits_t.shape[1]
    scores = _sigmoid(logits_t)
    sel = scores + rb
    sel3 = sel.reshape(N_GROUPS, GROUP_SIZE, tm)
    kk = lax.broadcasted_iota(jnp.int32, sel3.shape, 1)
    m1 = jnp.max(sel3, 1, keepdims=True)
    i1 = jnp.min(jnp.where(sel3 == m1, kk, GROUP_SIZE), 1, keepdims=True)
    m2 = jnp.max(jnp.where(kk == i1, -jnp.inf, sel3), 1, keepdims=True)
    gscore = (m1 + m2).reshape(N_GROUPS, tm)
    gid = lax.broadcasted_iota(jnp.int32, gscore.shape, 0)
    rank = jnp.zeros(gscore.shape, jnp.int32)
    for g in range(N_GROUPS):
        other = gscore[g:g + 1, :]
        ahead = (other > gscore) | ((other == gscore) & (g < gid))
        rank = rank + ahead.astype(jnp.int32)
    gsel = (rank < TOPK_GROUPS).reshape(N_GROUPS, 1, tm)
    cand = jnp.where(gsel, sel3, -jnp.inf).reshape(N_EXPERTS, tm)
    eid = lax.broadcasted_iota(jnp.int32, cand.shape, 0)
    chosen = jnp.zeros(cand.shape, jnp.bool_)
    for _ in range(TOP_K):
        m = jnp.max(cand, 0, keepdims=True)
        pick = eid == jnp.min(jnp.where(cand == m, eid, N_EXPERTS), 0, keepdims=True)
        chosen = chosen | pick
        cand = jnp.where(pick, -jnp.inf, cand)
    wsel = jnp.where(chosen, scores, 0.0)
    return wsel / jnp.sum(wsel, 0, keepdims=True) * ROUTED_SCALE


def _out_body(*refs, tm):
    n_x = tm // TOKEN_TILE + 1
    x_refs = refs[:n_x]
    (oa_ref, ob_ref, oc_ref, mod_ref, wa_ref, wb_ref, wc_ref, lng_ref, lnb_ref,
     rwt_ref, rb_ref, xmid_ref, h_ref, gate_ref) = refs[n_x:]
    for k, rows, is_ctx in _sub_tiles(tm):
        attn = (_dot(oa_ref[0, rows, :], wa_ref[...]) + _dot(ob_ref[0, rows, :], wb_ref[...])
                + _dot(oc_ref[0, rows, :], wc_ref[...]))
        y = DN_ALPHA * _stream_tile(x_refs, k, is_ctx) + _mod_row(mod_ref, is_ctx, 2) * attn
        xm = _layer_norm(y) * lng_ref[...] + lnb_ref[...]
        xmid_ref[0, rows, :] = xm
        h = _layer_norm(xm) * (1.0 + _mod_row(mod_ref, is_ctx, 4)) + _mod_row(mod_ref, is_ctx, 3)
        hi, lo = _split_bf16(h)
        h_ref[0, rows, :] = hi
        logits_t = _dot_nt(rwt_ref[...], jnp.concatenate([hi, lo, hi], axis=1))
        gates_t = _router_gates(logits_t, rb_ref[...])
        pad = jnp.zeros((128 - N_EXPERTS, gates_t.shape[1]), F32)
        gates = jnp.concatenate([gates_t, pad], axis=0).T
        for g in range(N_GROUPS):
            gate_ref[0, g, rows, :] = gates[:, g * GROUP_SIZE:(g + 1) * GROUP_SIZE]


def _out_proj(x_lat, x_ctx, ctx_block, oa, ob, oc, mod, wa, wb, wc, lng, lnb, rwt3, rb):
    bsz, s, _ = oa.shape
    d = D_MODEL
    tm = _step_tokens(s)
    tok = lambda w: pl.BlockSpec((1, tm, w), lambda b, i: (b, i, 0))
    full = lambda r, c: pl.BlockSpec((r, c), lambda b, i: (0, 0))
    return pl.pallas_call(
        functools.partial(_out_body, tm=tm),
        grid=(bsz, s // tm),
        in_specs=_stream_specs(tm, ctx_block) + [
                  tok(A_W), tok(B_W), tok(C_CH),
                  pl.BlockSpec((1, 2, 6, d), lambda b, i: (b, 0, 0, 0)),
                  full(A_W, d), full(B_W, d), full(C_CH, d), full(1, d), full(1, d),
                  full(N_EXPERTS, 3 * d), full(N_EXPERTS, 1)],
        out_specs=[tok(d), tok(d),
                   pl.BlockSpec((1, N_GROUPS, tm, GROUP_SIZE), lambda b, i: (b, 0, i, 0))],
        out_shape=[jax.ShapeDtypeStruct((bsz, s, d), F32),
                   jax.ShapeDtypeStruct((bsz, s, d), BF16),
                   jax.ShapeDtypeStruct((bsz, N_GROUPS, s, GROUP_SIZE), F32)],
        compiler_params=_params("parallel", "parallel"),
        name="proj_out_router",
    )(*([x_lat] * (tm // TOKEN_TILE)), x_ctx, oa, ob, oc, mod, wa, wb, wc, lng, lnb, rwt3, rb)


MOE_CHAINS = 2


def _moe_body(h_ref, gate_ref, xmid_ref, mod_ref, w1_ref, w3_ref, w2_ref,
              sw1_ref, sw3_ref, sw2_ref, lng_ref, lnb_ref, o_ref, acc_ref, *, tm):
    g = pl.program_id(2)
    rc = tm // MOE_CHAINS

    @pl.when(g == 0)
    def _():
        h = h_ref[0]
        a = _dot(h, sw1_ref[...])
        act = (a * _sigmoid(a) * _dot(h, sw3_ref[...])).astype(BF16)
        acc_ref[...] = _dot(act, sw2_ref[...])

    w1 = jnp.concatenate([w1_ref[k] for k in range(GROUP_SIZE)], axis=1)
    w3 = jnp.concatenate([w3_ref[k] for k in range(GROUP_SIZE)], axis=1)
    w2 = w2_ref[...].reshape(GROUP_SIZE * EXPERT_HIDDEN, D_MODEL)
    for c in range(MOE_CHAINS):
        rows = slice(c * rc, (c + 1) * rc)
        h = h_ref[0, rows, :]
        a = _dot(h, w1)
        act = a * _sigmoid(a) * _dot(h, w3)
        gt = gate_ref[0, 0, rows, :]
        parts = [act[:, k * EXPERT_HIDDEN:(k + 1) * EXPERT_HIDDEN] * gt[:, k:k + 1]
                 for k in range(GROUP_SIZE)]
        acc_ref[rows, :] += _dot(jnp.concatenate(parts, axis=1).astype(BF16), w2)

    @pl.when(g == N_GROUPS - 1)
    def _():
        row = lax.broadcasted_iota(jnp.int32, (tm, 1), 0) + pl.program_id(1) * tm
        gate2 = jnp.where(row >= SEQ, mod_ref[0, 1, 5:6, :], mod_ref[0, 0, 5:6, :])
        y = DN_ALPHA * xmid_ref[0] + gate2 * acc_ref[...]
        o_ref[0] = _layer_norm(y) * lng_ref[...] + lnb_ref[...]


MOE_TILE_2BUF = 768
MOE_TILE_1BUF = 1024


def _moe(h, gates, xmid, mod, layer, w1_all, w3_all, w2_all, sw1, sw3, sw2, lng, lnb):
    bsz, s, d = h.shape
    tm = MOE_TILE_1BUF if s % MOE_TILE_1BUF == 0 else MOE_TILE_2BUF
    tok = pl.BlockSpec((1, tm, d), lambda b, i, g: (b, i, 0))
    tok_once = pl.BlockSpec((1, tm, d), lambda b, i, g: (b, i, 0),
                            pipeline_mode=pl.Buffered(1 if tm > MOE_TILE_2BUF else 2))
    full = lambda r, c: pl.BlockSpec((r, c), lambda b, i, g: (0, 0))
    up = pl.BlockSpec((None, GROUP_SIZE, d, EXPERT_HIDDEN), lambda b, i, g: (layer, g, 0, 0))
    return pl.pallas_call(
        functools.partial(_moe_body, tm=tm),
        grid=(bsz, s // tm, N_GROUPS),
        in_specs=[tok,
                  pl.BlockSpec((1, 1, tm, GROUP_SIZE), lambda b, i, g: (b, g, i, 0)),
                  tok_once,
                  pl.BlockSpec((1, 2, 6, d), lambda b, i, g: (b, 0, 0, 0)),
                  up, up,
                  pl.BlockSpec((None, GROUP_SIZE, EXPERT_HIDDEN, d), lambda b, i, g: (layer, g, 0, 0)),
                  full(d, SHARED_HIDDEN), full(d, SHARED_HIDDEN), full(SHARED_HIDDEN, d),
                  full(1, d), full(1, d)],
        out_specs=tok,
        out_shape=jax.ShapeDtypeStruct((bsz, s, d), F32),
        scratch_shapes=[pltpu.VMEM((tm, d), F32)],
        compiler_params=_params("parallel", "parallel", "arbitrary"),
        name="moe_ffn",
    )(h, gates, xmid, mod, w1_all, w3_all, w2_all, sw1, sw3, sw2, lng, lnb)


def _rope_tables():
    t = jnp.arange(SEQ)
    row = (t // GRID_W).astype(F32)
    col = (t % GRID_W).astype(F32)
    inv = ROPE_THETA ** (-jnp.arange(0, HALF, 2, dtype=F32) / HALF)
    ang = jnp.concatenate([row[:, None] * inv, col[:, None] * inv], -1)
    cos, sin = jnp.cos(ang), jnp.sin(ang)
    n_heads = ROPE_TABLE_W // HEAD_DIM
    cos_t = jnp.concatenate([jnp.tile(jnp.concatenate([cos, cos], -1), (1, n_heads)),
                             jnp.ones((CTX_LEN, ROPE_TABLE_W), F32)], 0)
    sin_t = jnp.concatenate([jnp.tile(jnp.concatenate([-sin, sin], -1), (1, n_heads)),
                             jnp.zeros((CTX_LEN, ROPE_TABLE_W), F32)], 0)
    return cos_t, sin_t


def _modulation(c, c_ctx, w, b):
    cvec = jnp.concatenate([c, c_ctx[None]], axis=0)
    m = (jnp.dot(jax.nn.silu(cvec), w, precision=HIGHEST) + b).reshape(-1, 6, D_MODEL)
    return jnp.stack([m[:-1], jnp.broadcast_to(m[-1], m[:-1].shape)], axis=1)


def kernel(x, c, ctx, c_ctx, ada_w, ada_b, w_in, q_gain, k_gain, nat_rpb, hy_conv_w, hy_conv_b, hy_f_w1, hy_f_b1, hy_f_w2, hy_f_b2, hy_f_w3, hy_freq, hy_skip, w_out, ln1_g, ln1_b, router_w, router_b, exp_w1, exp_w3, exp_w2, sh_w1, sh_w3, sh_w2, ln2_g, ln2_b):
    bsz = x.shape[0]
    d = D_MODEL
    x_lat, x_ctx, ctx_block = x, ctx, 0

    head_perm = np.concatenate([np.arange(0, HEAD_DIM, 2), np.arange(1, HEAD_DIM, 2)])
    qk_perm = (np.arange(QK_W // HEAD_DIM)[:, None] * HEAD_DIM + head_perm[None, :]).reshape(-1)
    perm_mat = jnp.asarray(np.eye(QK_W)[:, qk_perm], BF16)
    cos_t, sin_t = _rope_tables()
    bd = jnp.asarray(np.kron(np.eye(QK_W // HEAD_DIM), np.ones((HEAD_DIM, HEAD_DIM))), BF16)
    tables_lat = _dft_tables(SEQ)
    tables_ctx = _dft_tables(CTX_LEN)
    w1_bf, w3_bf, w2_bf = exp_w1.astype(BF16), exp_w3.astype(BF16), exp_w2.astype(BF16)

    for l in range(DEPTH):
        last = l == DEPTH - 1
        mod = _modulation(c, c_ctx, ada_w[l], ada_b[l])

        w_bf = w_in[l].astype(BF16)
        w_qk = jnp.dot(w_bf[:, :QK_W], perm_mat, preferred_element_type=F32).astype(BF16)
        w_bf = jnp.concatenate([w_qk, w_bf[:, QK_W:]], axis=1)
        gain = jnp.concatenate([jnp.tile(q_gain[l][head_perm] * ATT_SCALE, A_HEADS),
                                jnp.tile(k_gain[l][head_perm], A_KV_HEADS)])[None].astype(F32)
        qa, ka, va, qb, kb, vb, z = _project(x_lat, x_ctx, ctx_block, mod, w_bf, bd, gain, cos_t, sin_t)

        out_rows = SEQ if last else S_ALL
        oa = _gqa(qa, ka, va, out_rows)
        ob = _na(qb, kb, vb, _na_bias_table(nat_rpb[l]), out_rows)
        if not last:
            oa = _ctx_attn(qa, ka, va, 1, oa)
            ob = _ctx_attn(qb, kb, vb, NA_HEADS_PER_STEP, ob)

        filt = (hy_f_w1[l], hy_f_b1[l], hy_f_w2[l], hy_f_b2[l], hy_f_w3[l], hy_freq[l])
        cw, cb, skip = hy_conv_w[l], hy_conv_b[l][None], hy_skip[l][None]
        segments = [(0, SEQ, tables_lat)] + ([] if last else [(SEQ, CTX_LEN, tables_ctx)])
        oc = None
        for seg_start, n, (fwd, inv) in segments:
            kc, ks = _filter_spectrum(fwd, *_implicit_filters(n, *filt))
            u, us = _hyena_gate(z, seg_start, n, cw, cb, skip)
            oc = _hyena(u, us, z, seg_start, n, cw, cb, fwd, inv, kc, ks, out_rows=out_rows, into=oc)

        wo = w_out[l].astype(BF16)
        rw_hi, rw_lo = _split_bf16(router_w[l].T)
        rwt3 = jnp.concatenate([rw_hi, rw_hi, rw_lo], axis=1)
        xmid, h, gates = _out_proj(x_lat, x_ctx, ctx_block, oa, ob, oc, mod,
                                   wo[:A_W], wo[A_W:A_W + B_W], wo[A_W + B_W:],
                                   ln1_g[l][None], ln1_b[l][None], rwt3, router_b[l][:, None])

        xs = _moe(h, gates, xmid, mod, l, w1_bf, w3_bf, w2_bf,
                  sh_w1[l].astype(BF16), sh_w3[l].astype(BF16), sh_w2[l].astype(BF16),
                  ln2_g[l][None], ln2_b[l][None])
        x_lat, x_ctx, ctx_block = xs, xs, SEQ // TOKEN_TILE
    return xs
```

```python
import functools
import math

import numpy as np
import jax
import jax.numpy as jnp
from jax import lax
from jax.experimental import pallas as pl
from jax.experimental.pallas import tpu as pltpu

F32 = jnp.float32
BF16 = jnp.bfloat16
HIGHEST = lax.Precision.HIGHEST

D_MODEL = 1024
SEQ = 2048
DEPTH = 2
GRID_W = 64
GRID_ROWS = SEQ // GRID_W
CTX_LEN = 256
S_ALL = SEQ + CTX_LEN
HEAD_DIM = 64
HALF = HEAD_DIM // 2
A_HEADS = 8
A_KV_HEADS = 2
A_REP = A_HEADS // A_KV_HEADS
ROPE_THETA = 10000.0
B_HEADS = 4
WIN_ROWS = 8
WIN_COLS = 16
C_CH = 256
C_POS_BANDS = 16
C_DECAY_TARGET = 1e-2
C_FAST_DECAY = 0.3
C_SLOW_DECAY = 1.5
A_W = A_HEADS * HEAD_DIM
A_KV_W = A_KV_HEADS * HEAD_DIM
B_W = B_HEADS * HEAD_DIM
A_K0 = A_W
A_V0 = A_K0 + A_KV_W
B_Q0 = A_V0 + A_KV_W
B_K0 = B_Q0 + B_W
B_V0 = B_K0 + B_W
C0 = B_V0 + B_W
PROJ_W = C0 + 3 * C_CH
QK_W = A_W + A_KV_W
N_EXPERTS = 64
TOP_K = 8
N_GROUPS = 8
GROUP_SIZE = N_EXPERTS // N_GROUPS
TOPK_GROUPS = 4
EXPERT_HIDDEN = 128
SHARED_HIDDEN = 256
ROUTED_SCALE = 2.5
DN_ALPHA = (2 * DEPTH) ** 0.25
LN_EPS = 1e-6
RMS_EPS = 1e-6
ATT_SCALE = HEAD_DIM ** -0.5
NEG_BIG = -1e30

VMEM_LIMIT_BYTES = 56 * 1024 * 1024
TOKEN_TILE = 256
ROPE_TABLE_W = 128
GQA_Q_TILE = 512
GQA_KEY_CHUNK = 768
FREQ_TILE = 256


def _params(*sem):
    return pltpu.CompilerParams(dimension_semantics=sem, vmem_limit_bytes=VMEM_LIMIT_BYTES)


def _layer_norm(v):
    mu = jnp.mean(v, -1, keepdims=True)
    vc = v - mu
    var = jnp.mean(vc * vc, -1, keepdims=True)
    return vc * lax.rsqrt(var + LN_EPS)


def _sigmoid(v):
    return 1.0 / (1.0 + jnp.exp(-v))


def _dot(a, b):
    return jnp.dot(a, b, preferred_element_type=F32)


def _dot_nt(a, b):
    return lax.dot_general(a, b, (((1,), (1,)), ((), ())), preferred_element_type=F32)


def _mod_row(mod_ref, is_ctx, idx):
    return jnp.where(is_ctx, mod_ref[0, 1, idx:idx + 1, :], mod_ref[0, 0, idx:idx + 1, :])


def _sub_tiles(tm):
    for k in range(tm // TOKEN_TILE):
        rows = slice(k * TOKEN_TILE, (k + 1) * TOKEN_TILE)
        is_ctx = (pl.program_id(1) * tm + k * TOKEN_TILE) >= SEQ
        yield k, rows, is_ctx


def _stream_specs(tm, ctx_block, width=D_MODEL):
    n_sub = tm // TOKEN_TILE
    last_lat = SEQ // TOKEN_TILE - 1
    lat = [pl.BlockSpec((1, TOKEN_TILE, width),
                        lambda b, i, k=k: (b, jnp.minimum(i * n_sub + k, last_lat), 0)) for k in range(n_sub)]
    return lat + [pl.BlockSpec((1, TOKEN_TILE, width), lambda b, i: (b, ctx_block, 0))]


def _stream_args(lat, ctx, tm):
    return [lat] * (tm // TOKEN_TILE) + [ctx]


def _stream_tile(x_refs, k, is_ctx):
    return jnp.where(is_ctx, x_refs[-1][0], x_refs[k][0])


def _proj_body(*refs, tm):
    n_x = tm // TOKEN_TILE + 1
    x_refs = refs[:n_x]
    (mod_ref, w_ref, wvt_ref, bd_ref, gain_ref, cos_ref, sin_ref,
     qa_ref, ka_ref, va_ref, vat_ref, qb_ref, kb_ref, vb_ref, z_ref) = refs[n_x:]
    n_rep = QK_W // ROPE_TABLE_W
    for k, rows, is_ctx in _sub_tiles(tm):
        xn = _layer_norm(_stream_tile(x_refs, k, is_ctx))
        hx = (xn * (1.0 + _mod_row(mod_ref, is_ctx, 1)) + _mod_row(mod_ref, is_ctx, 0)).astype(BF16)
        vt = _dot_nt(wvt_ref[...], hx)
        for g in range(A_KV_HEADS):
            vat_ref[0, g, :, rows] = vt[g * HEAD_DIM:(g + 1) * HEAD_DIM].astype(BF16)
        px = _dot(hx, w_ref[...])
        qk = px[:, :QK_W]
        ss = _dot((qk * qk).astype(BF16), bd_ref[...])
        qkn = qk * lax.rsqrt(ss * (1.0 / HEAD_DIM) + RMS_EPS) * gain_ref[...]
        lane = lax.broadcasted_iota(jnp.int32, qkn.shape, 1)
        partner = jnp.where((lane % HEAD_DIM) < HALF,
                            pltpu.roll(qkn, QK_W - HALF, 1), pltpu.roll(qkn, HALF, 1))
        cos = jnp.concatenate([cos_ref[rows, :]] * n_rep, axis=1)
        sin = jnp.concatenate([sin_ref[rows, :]] * n_rep, axis=1)
        qkr = (qkn * cos + partner * sin).astype(BF16)
        for h in range(A_HEADS):
            qa_ref[0, h, rows, :] = qkr[:, h * HEAD_DIM:(h + 1) * HEAD_DIM]
        for g in range(A_KV_HEADS):
            ka_ref[0, g, rows, :] = qkr[:, A_K0 + g * HEAD_DIM:A_K0 + (g + 1) * HEAD_DIM]
            va_ref[0, g, rows, :] = px[:, A_V0 + g * HEAD_DIM:A_V0 + (g + 1) * HEAD_DIM].astype(BF16)
        for h in range(B_HEADS):
            qb_ref[0, h, rows, :] = (px[:, B_Q0 + h * HEAD_DIM:B_Q0 + (h + 1) * HEAD_DIM] * ATT_SCALE).astype(BF16)
            kb_ref[0, h, rows, :] = px[:, B_K0 + h * HEAD_DIM:B_K0 + (h + 1) * HEAD_DIM].astype(BF16)
            vb_ref[0, h, rows, :] = px[:, B_V0 + h * HEAD_DIM:B_V0 + (h + 1) * HEAD_DIM].astype(BF16)
        z_ref[0, rows, :] = px[:, C0:]


def _step_tokens(s):
    return 3 * TOKEN_TILE if s % (3 * TOKEN_TILE) == 0 else 2 * TOKEN_TILE


def _project(x_lat, x_ctx, ctx_block, mod, w_bf, bd, gain, cos_t, sin_t):
    bsz, d, s = x_lat.shape[0], D_MODEL, S_ALL
    tm = _step_tokens(s)
    nt = s // tm
    heads = lambda n: pl.BlockSpec((1, n, tm, HEAD_DIM), lambda b, i: (b, 0, i, 0))
    hshape = lambda n: jax.ShapeDtypeStruct((bsz, n, s, HEAD_DIM), BF16)
    return pl.pallas_call(
        functools.partial(_proj_body, tm=tm),
        grid=(bsz, nt),
        in_specs=_stream_specs(tm, ctx_block) + [
            pl.BlockSpec((1, 2, 6, d), lambda b, i: (b, 0, 0, 0)),
            pl.BlockSpec((d, PROJ_W), lambda b, i: (0, 0)),
            pl.BlockSpec((A_KV_W, d), lambda b, i: (0, 0)),
            pl.BlockSpec((QK_W, QK_W), lambda b, i: (0, 0)),
            pl.BlockSpec((1, QK_W), lambda b, i: (0, 0)),
            pl.BlockSpec((tm, ROPE_TABLE_W), lambda b, i: (i, 0)),
            pl.BlockSpec((tm, ROPE_TABLE_W), lambda b, i: (i, 0)),
        ],
        out_specs=[heads(A_HEADS), heads(A_KV_HEADS), heads(A_KV_HEADS),
                   pl.BlockSpec((1, A_KV_HEADS, HEAD_DIM, tm), lambda b, i: (b, 0, 0, i)),
                   heads(B_HEADS), heads(B_HEADS), heads(B_HEADS),
                   pl.BlockSpec((1, tm, 3 * C_CH), lambda b, i: (b, i, 0))],
        out_shape=[hshape(A_HEADS), hshape(A_KV_HEADS), hshape(A_KV_HEADS),
                   jax.ShapeDtypeStruct((bsz, A_KV_HEADS, HEAD_DIM, s), BF16),
                   hshape(B_HEADS), hshape(B_HEADS), hshape(B_HEADS),
                   jax.ShapeDtypeStruct((bsz, s, 3 * C_CH), F32)],
        compiler_params=_params("parallel", "parallel"),
        name="proj_in",
    )(*_stream_args(x_lat, x_ctx, tm), mod, w_bf, w_bf[:, A_V0:A_V0 + A_KV_W].T, bd, gain, cos_t, sin_t)


def _gqa_body(q_ref, k_ref, vt_ref, o_ref):
    n_keys = k_ref.shape[2]
    m, l, acc = [None] * A_REP, [None] * A_REP, [None] * A_REP
    for j in range(n_keys // GQA_KEY_CHUNK):
        keys = slice(j * GQA_KEY_CHUNK, (j + 1) * GQA_KEY_CHUNK)
        sts = [_dot_nt(k_ref[0, 0, keys, :], q_ref[0, h]) for h in range(A_REP)]
        for h in range(A_REP):
            st = sts[h]
            mj = jnp.max(st, 0, keepdims=True)
            if j == 0:
                m[h] = mj
                p = jnp.exp(st - mj)
                l[h] = jnp.sum(p, 0, keepdims=True)
                acc[h] = _dot(vt_ref[0, 0, :, keys], p.astype(BF16))
            else:
                m_new = jnp.maximum(m[h], mj)
                alpha = jnp.exp(m[h] - m_new)
                p = jnp.exp(st - m_new)
                l[h] = alpha * l[h] + jnp.sum(p, 0, keepdims=True)
                acc[h] = alpha * acc[h] + _dot(vt_ref[0, 0, :, keys], p.astype(BF16))
                m[h] = m_new
    for h in range(A_REP):
        o_ref[0, :, h * HEAD_DIM:(h + 1) * HEAD_DIM] = (acc[h] / l[h]).T.astype(BF16)


def _gqa(qa, ka, vat):
    bsz, _, s, _ = qa.shape
    tq = GQA_Q_TILE
    return pl.pallas_call(
        _gqa_body,
        grid=(bsz, A_KV_HEADS, SEQ // tq),
        in_specs=[
            pl.BlockSpec((1, A_REP, tq, HEAD_DIM), lambda b, g, i: (b, g, i, 0)),
            pl.BlockSpec((1, 1, s, HEAD_DIM), lambda b, g, i: (b, g, 0, 0)),
            pl.BlockSpec((1, 1, HEAD_DIM, s), lambda b, g, i: (b, g, 0, 0)),
        ],
        out_specs=pl.BlockSpec((1, tq, A_REP * HEAD_DIM), lambda b, g, i: (b, i, g)),
        out_shape=jax.ShapeDtypeStruct((bsz, SEQ, A_W), BF16),
        compiler_params=_params("parallel", "parallel", "parallel"),
        name="gqa_attn",
    )(qa, ka, vat)


NA_HEADS_PER_STEP = 2
NA_Q_ROWS = 8
NA_K_ROWS = 16
NA_Q_TOKENS = NA_Q_ROWS * GRID_W
NA_K_TOKENS = NA_K_ROWS * GRID_W
NA_Q_BLOCKS = GRID_ROWS // NA_Q_ROWS
NA_KEY_ROW0 = (0, 4, 12, 16)
NA_PATTERN_BLOCKS = (0, 1, 3)


def _na_body(q_ref, k_ref, v_ref, bias_ref, o_ref):
    j = pl.program_id(2)
    key_row0 = jnp.where(j == 0, NA_KEY_ROW0[0],
                         jnp.where(j == 1, NA_KEY_ROW0[1], jnp.where(j == 2, NA_KEY_ROW0[2], NA_KEY_ROW0[3])))
    k0 = pl.multiple_of(key_row0 * GRID_W, 4 * GRID_W)
    for hh in range(NA_HEADS_PER_STEP):
        q = q_ref[0, hh]
        kw = k_ref[0, hh, pl.ds(k0, NA_K_TOKENS), :]
        vw = v_ref[0, hh, pl.ds(k0, NA_K_TOKENS), :]
        kc = k_ref[0, hh, SEQ:, :]
        vc = v_ref[0, hh, SEQ:, :]
        sw = _dot_nt(q, kw) + bias_ref[hh, 0]
        sc = _dot_nt(q, kc)
        m = jnp.maximum(jnp.max(sw, -1, keepdims=True), jnp.max(sc, -1, keepdims=True))
        pw = jnp.exp(sw - m)
        pc = jnp.exp(sc - m)
        l = jnp.sum(pw, -1, keepdims=True) + jnp.sum(pc, -1, keepdims=True)
        o = (_dot(pw.astype(BF16), vw) + _dot(pc.astype(BF16), vc)) / l
        o_ref[0, :, hh * HEAD_DIM:(hh + 1) * HEAD_DIM] = o.astype(BF16)


def _na(qb, kb, vb, bias):
    bsz, _, s, _ = qb.shape
    hp = NA_HEADS_PER_STEP
    kv = pl.BlockSpec((1, hp, s, HEAD_DIM), lambda b, h, j: (b, h, 0, 0))
    return pl.pallas_call(
        _na_body,
        grid=(bsz, B_HEADS // hp, NA_Q_BLOCKS),
        in_specs=[pl.BlockSpec((1, hp, NA_Q_TOKENS, HEAD_DIM), lambda b, h, j: (b, h, j, 0)), kv, kv,
                  pl.BlockSpec((hp, 1, NA_Q_TOKENS, NA_K_TOKENS), lambda b, h, j: (h, (j + 1) // 2, 0, 0))],
        out_specs=pl.BlockSpec((1, NA_Q_TOKENS, hp * HEAD_DIM), lambda b, h, j: (b, j, h)),
        out_shape=jax.ShapeDtypeStruct((bsz, SEQ, B_W), BF16),
        compiler_params=_params("parallel", "parallel", "arbitrary"),
        name="nbr_attn",
    )(qb, kb, vb, bias)


def _na_bias_body(t_ref, o_ref):
    o_ref[...] = jnp.full(o_ref.shape, NEG_BIG, F32)
    for p, blk in enumerate(NA_PATTERN_BLOCKS):
        @pl.when(pl.program_id(1) == p)
        def _(blk=blk):
            for qi in range(NA_Q_ROWS):
                qrow = blk * NA_Q_ROWS + qi
                rs = min(max(qrow - WIN_ROWS // 2, 0), GRID_ROWS - WIN_ROWS)
                for krow in range(rs, rs + WIN_ROWS):
                    ki = krow - NA_KEY_ROW0[blk]
                    o_ref[0, 0, qi * GRID_W:(qi + 1) * GRID_W, ki * GRID_W:(ki + 1) * GRID_W] = (
                        t_ref[0, krow - qrow + WIN_ROWS - 1])


def _na_bias_table(rpb):
    col = np.arange(GRID_W)
    col_start = np.clip(col - WIN_COLS // 2, 0, GRID_W - WIN_COLS)
    in_cols = (col[None, :] >= col_start[:, None]) & (col[None, :] < col_start[:, None] + WIN_COLS)
    dc = np.clip(col[None, :] - col[:, None] + WIN_COLS - 1, 0, 2 * WIN_COLS - 2)
    onehot = jnp.asarray(dc[None] == np.arange(2 * WIN_COLS - 1)[:, None, None], F32)
    tiles = jnp.einsum('hdj,jqk->hdqk', rpb.astype(F32), onehot, precision=HIGHEST)
    tiles = jnp.where(in_cols[None, None], tiles, NEG_BIG)
    n_pat = len(NA_PATTERN_BLOCKS)
    return pl.pallas_call(
        _na_bias_body,
        grid=(B_HEADS, n_pat),
        in_specs=[pl.BlockSpec((1, 2 * WIN_ROWS - 1, GRID_W, GRID_W), lambda h, p: (h, 0, 0, 0))],
        out_specs=pl.BlockSpec((1, 1, NA_Q_TOKENS, NA_K_TOKENS), lambda h, p: (h, p, 0, 0)),
        out_shape=jax.ShapeDtypeStruct((B_HEADS, n_pat, NA_Q_TOKENS, NA_K_TOKENS), F32),
        compiler_params=_params("parallel", "arbitrary"),
        name="nbr_bias",
    )(tiles)


def _ctx_attn_body(q_ref, k_ref, v_ref, o_ref, *, rep):
    for g in range(k_ref.shape[1]):
        for r in range(rep):
            hh = g * rep + r
            s = _dot_nt(q_ref[0, hh], k_ref[0, g])
            m = jnp.max(s, -1, keepdims=True)
            p = jnp.exp(s - m)
            l = jnp.sum(p, -1, keepdims=True)
            o = _dot(p.astype(BF16), v_ref[0, g]) / l
            o_ref[0, :, hh * HEAD_DIM:(hh + 1) * HEAD_DIM] = o.astype(BF16)


def _ctx_attn(q, k, v, kv_per_step):
    bsz, n_q, _, _ = q.shape
    n_kv = k.shape[1]
    rep = n_q // n_kv
    seg = lambda n: pl.BlockSpec((1, n, CTX_LEN, HEAD_DIM), lambda b, h: (b, h, SEQ // CTX_LEN, 0))
    return pl.pallas_call(
        functools.partial(_ctx_attn_body, rep=rep),
        grid=(bsz, n_kv // kv_per_step),
        in_specs=[seg(kv_per_step * rep), seg(kv_per_step), seg(kv_per_step)],
        out_specs=pl.BlockSpec((1, CTX_LEN, kv_per_step * rep * HEAD_DIM), lambda b, h: (b, 0, h)),
        out_shape=jax.ShapeDtypeStruct((bsz, CTX_LEN, n_q * HEAD_DIM), BF16),
        compiler_params=_params("parallel", "parallel"),
        name="ctx_attn",
    )(q, k, v)


DFT_FINE = 16


def _dft_table_body(chr_ref, shr_ref, clr_ref, slr_ref, chc_ref, shc_ref, clc_ref, slc_ref,
                    fwd_ref, inv_ref, *, n, tf):
    j = pl.program_id(0)

    def expand(c_hi, s_hi, c_lo, s_lo, rows):
        cos = (c_hi[:, None, :] * c_lo[None, :, :] - s_hi[:, None, :] * s_lo[None, :, :])
        sin = (s_hi[:, None, :] * c_lo[None, :, :] + c_hi[:, None, :] * s_lo[None, :, :])
        return cos.reshape(rows, -1), sin.reshape(rows, -1)

    cos_r, sin_r = expand(chr_ref[...], shr_ref[...], clr_ref[...], slr_ref[...], tf)
    cos_c, sin_c = expand(chc_ref[...], shc_ref[...], clc_ref[...], slc_ref[...], n)
    fr = lax.broadcasted_iota(jnp.int32, (tf, n), 0) + j * tf
    tr = lax.broadcasted_iota(jnp.int32, (tf, n), 1)
    sin_r = jnp.where(fr == 0, jnp.where(tr % 2 == 0, 1.0, -1.0), sin_r)
    tc = lax.broadcasted_iota(jnp.int32, (n, tf), 0)
    fcol = lax.broadcasted_iota(jnp.int32, (n, tf), 1) + j * tf
    sin_c = jnp.where(fcol == 0, jnp.where(tc % 2 == 0, 1.0, -1.0), sin_c)
    fwd_ref[0, :tf, :] = cos_r.astype(BF16)
    fwd_ref[0, tf:, :] = sin_r.astype(BF16)
    inv_ref[0, :, :tf] = cos_c.astype(BF16)
    inv_ref[0, :, tf:] = sin_c.astype(BF16)


def _dft_tables(n):
    s = jnp.arange(n, dtype=jnp.int32)[None, :]

    def cos_sin(f):
        ang = ((f * s) % (2 * n)).astype(F32) * (math.pi / n)
        return jnp.cos(ang), jnp.sin(ang)

    ch, sh = cos_sin(jnp.arange(n // DFT_FINE, dtype=jnp.int32)[:, None] * DFT_FINE)
    cl, sl = cos_sin(jnp.arange(DFT_FINE, dtype=jnp.int32)[:, None])
    tf = min(FREQ_TILE, n)
    nf = n // tf
    hi_rows = pl.BlockSpec((tf // DFT_FINE, n), lambda j: (j, 0))
    lo_rows = pl.BlockSpec((DFT_FINE, n), lambda j: (0, 0))
    hi_cols = pl.BlockSpec((n // DFT_FINE, tf), lambda j: (0, j))
    lo_cols = pl.BlockSpec((DFT_FINE, tf), lambda j: (0, j))
    return pl.pallas_call(
        functools.partial(_dft_table_body, n=n, tf=tf),
        grid=(nf,),
        in_specs=[hi_rows, hi_rows, lo_rows, lo_rows, hi_cols, hi_cols, lo_cols, lo_cols],
        out_specs=[pl.BlockSpec((1, 2 * tf, n), lambda j: (j, 0, 0)),
                   pl.BlockSpec((1, n, 2 * tf), lambda j: (j, 0, 0))],
        out_shape=[jax.ShapeDtypeStruct((nf, 2 * tf, n), BF16),
                   jax.ShapeDtypeStruct((nf, n, 2 * tf), BF16)],
        compiler_params=_params("parallel"),
        name="dft_tables",
    )(ch, sh, cl, sl, ch, sh, cl, sl)


def _split_bf16(v):
    hi = v.astype(BF16)
    lo = (v - hi.astype(F32)).astype(BF16)
    return hi, lo


def _spectrum_body(fwd_ref, kp_ref, km_ref, kc_ref, ks_ref, *, tf):
    kp_hi, kp_lo = _split_bf16(kp_ref[...])
    km_hi, km_lo = _split_bf16(km_ref[...])
    fc = fwd_ref[0, :tf, :]
    fs = fwd_ref[0, tf:, :]
    kc_ref[...] = _dot(fc, kp_hi) + _dot(fc, kp_lo)
    ks_sin = _dot(fs, km_hi) + _dot(fs, km_lo)
    ks_nyq = _dot(fs, kp_hi) + _dot(fs, kp_lo)
    row = lax.broadcasted_iota(jnp.int32, ks_sin.shape, 0) + pl.program_id(0) * tf
    ks_ref[...] = jnp.where(row == 0, ks_nyq, ks_sin)


def _filter_spectrum(fwd, kplus, kminus):
    nf, tf2, n = fwd.shape
    tf = tf2 // 2
    filt = pl.BlockSpec((n, C_CH), lambda j: (0, 0))
    out = pl.BlockSpec((tf, C_CH), lambda j: (j, 0))
    return pl.pallas_call(
        functools.partial(_spectrum_body, tf=tf),
        grid=(nf,),
        in_specs=[pl.BlockSpec((1, tf2, n), lambda j: (j, 0, 0)), filt, filt],
        out_specs=[out, out],
        out_shape=[jax.ShapeDtypeStruct((n, C_CH), F32)] * 2,
        compiler_params=_params("parallel"),
        name="hyena_spectrum",
    )(fwd, kplus, kminus)


def _short_conv(z, cw_ref, cb_ref, cols, n):
    pos = lax.broadcasted_iota(jnp.int32, z.shape, 0)
    prev = jnp.where(pos == 0, 0.0, pltpu.roll(z, 1, 0))
    nxt = jnp.where(pos == n - 1, 0.0, pltpu.roll(z, n - 1, 0))
    return prev * cw_ref[0:1, cols] + z * cw_ref[1:2, cols] + nxt * cw_ref[2:3, cols] + cb_ref[:, cols]


def _hyena_gate_body(z_ref, cw_ref, cb_ref, skip_ref, u_ref, us_ref, *, n):
    zc = _short_conv(z_ref[0, :, C_CH:], cw_ref, cb_ref, slice(C_CH, 3 * C_CH), n)
    u = zc[:, C_CH:] * zc[:, :C_CH]
    u_ref[0] = u.astype(BF16)
    us_ref[0] = u * skip_ref[...]


def _hyena_gate(z, seg_start, n, cw, cb, skip):
    bsz = z.shape[0]
    seg = seg_start // n
    small = lambda r, c: pl.BlockSpec((r, c), lambda b: (0, 0))
    out = pl.BlockSpec((1, n, C_CH), lambda b: (b, 0, 0))
    return pl.pallas_call(
        functools.partial(_hyena_gate_body, n=n),
        grid=(bsz,),
        in_specs=[pl.BlockSpec((1, n, 3 * C_CH), lambda b: (b, seg, 0)),
                  small(3, 3 * C_CH), small(1, 3 * C_CH), small(1, C_CH)],
        out_specs=[out, out],
        out_shape=[jax.ShapeDtypeStruct((bsz, n, C_CH), BF16),
                   jax.ShapeDtypeStruct((bsz, n, C_CH), F32)],
        compiler_params=_params("parallel"),
        name="hyena_gate",
    )(z, cw, cb, skip)


HYENA_BATCH = 4


def _hyena_body(u_ref, us_ref, z_ref, cw_ref, cb_ref, fwd_ref, inv_ref, kc_ref, ks_ref, o_ref, y_ref, *, n, tf):
    j = pl.program_id(1)

    @pl.when(j == 0)
    def _():
        y_ref[...] = jnp.zeros_like(y_ref)

    kc = kc_ref[...]
    ks = ks_ref[...]
    row = lax.broadcasted_iota(jnp.int32, kc.shape, 0) + j * tf
    first = row == 0
    wgt = jnp.where(first, 1.0, 2.0) * (1.0 / (2 * n))
    for b in range(HYENA_BATCH):
        spec = _dot(fwd_ref[0], u_ref[b])
        uc, us = spec[:tf], spec[tf:]
        re = jnp.where(first, uc * kc, uc * kc - us * ks) * wgt
        im = jnp.where(first, us * ks, uc * ks + us * kc) * wgt
        y_ref[b] += _dot(inv_ref[0], jnp.concatenate([re, im], axis=0).astype(BF16))

    @pl.when(j == pl.num_programs(1) - 1)
    def _():
        for b in range(HYENA_BATCH):
            x0 = _short_conv(z_ref[b], cw_ref, cb_ref, slice(0, C_CH), n)
            o_ref[b] = ((y_ref[b] + us_ref[b]) * x0).astype(BF16)


def _hyena(u, us, z, seg_start, n, cw, cb, fwd, inv, kc, ks):
    bsz = u.shape[0]
    assert bsz % HYENA_BATCH == 0, bsz
    tf = min(FREQ_TILE, n)
    nf = n // tf
    seg = seg_start // n
    seq = pl.BlockSpec((HYENA_BATCH, n, C_CH), lambda g, j: (g, 0, 0))
    once = pl.Buffered(1)
    small = lambda r, c: pl.BlockSpec((r, c), lambda g, j: (0, 0))
    spec_blk = pl.BlockSpec((tf, C_CH), lambda g, j: (j, 0))
    return pl.pallas_call(
        functools.partial(_hyena_body, n=n, tf=tf),
        grid=(bsz // HYENA_BATCH, nf),
        in_specs=[seq,
                  pl.BlockSpec((HYENA_BATCH, n, C_CH), lambda g, j: (g, 0, 0), pipeline_mode=once),
                  pl.BlockSpec((HYENA_BATCH, n, C_CH), lambda g, j: (g, seg, 0), pipeline_mode=once),
                  small(3, 3 * C_CH), small(1, 3 * C_CH),
                  pl.BlockSpec((1, 2 * tf, n), lambda g, j: (j, 0, 0)),
                  pl.BlockSpec((1, n, 2 * tf), lambda g, j: (j, 0, 0)),
                  spec_blk, spec_blk],
        out_specs=seq,
        out_shape=jax.ShapeDtypeStruct((bsz, n, C_CH), BF16),
        scratch_shapes=[pltpu.VMEM((HYENA_BATCH, n, C_CH), F32)],
        compiler_params=_params("parallel", "arbitrary"),
        name="hyena_conv",
    )(u, us, z, cw, cb, fwd, inv, kc, ks)


def _implicit_filters(n, w1, b1, w2, b2, w3, freq):
    t = jnp.linspace(0.0, 1.0, n, dtype=F32)[:, None]
    w = 2 * math.pi * jnp.arange(n, dtype=F32)[:, None] / n
    f = jnp.linspace(1e-4, C_POS_BANDS - 1, C_POS_BANDS, dtype=F32)[None]
    z = jnp.concatenate([t, jnp.cos(f * w), -jnp.sin(f * w)], -1)
    h = jnp.sin(freq[0] * (jnp.dot(z, w1, precision=HIGHEST) + b1))
    h = jnp.sin(freq[1] * (jnp.dot(h, w2, precision=HIGHEST) + b2))
    h = jnp.dot(h, w3, precision=HIGHEST)
    lo = math.log(C_DECAY_TARGET) / C_SLOW_DECAY
    hi = math.log(C_DECAY_TARGET) / C_FAST_DECAY
    deltas = jnp.abs(jnp.linspace(lo, hi, C_CH, dtype=F32))
    decay = jnp.exp(-t * deltas[None])
    hf, hb = h[:, :C_CH] * decay, h[:, C_CH:] * decay
    hb = hb.at[0].set(0.0)
    return hf + hb, hf - hb


def _router_gates(logits_t, rb):
    tm = logits_t.shape[1]
    scores = _sigmoid(logits_t)
    sel = scores + rb
    sel3 = sel.reshape(N_GROUPS, GROUP_SIZE, tm)
    kk = lax.broadcasted_iota(jnp.int32, sel3.shape, 1)
    m1 = jnp.max(sel3, 1, keepdims=True)
    i1 = jnp.min(jnp.where(sel3 == m1, kk, GROUP_SIZE), 1, keepdims=True)
    m2 = jnp.max(jnp.where(kk == i1, -jnp.inf, sel3), 1, keepdims=True)
    gscore = (m1 + m2).reshape(N_GROUPS, tm)
    gid = lax.broadcasted_iota(jnp.int32, gscore.shape, 0)
    rank = jnp.zeros(gscore.shape, jnp.int32)
    for g in range(N_GROUPS):
        other = gscore[g:g + 1, :]
        ahead = (other > gscore) | ((other == gscore) & (g < gid))
        rank = rank + ahead.astype(jnp.int32)
    gsel = (rank < TOPK_GROUPS).reshape(N_GROUPS, 1, tm)
    cand = jnp.where(gsel, sel3, -jnp.inf).reshape(N_EXPERTS, tm)
    eid = lax.broadcasted_iota(jnp.int32, cand.shape, 0)
    chosen = jnp.zeros(cand.shape, jnp.bool_)
    for _ in range(TOP_K):
        m = jnp.max(cand, 0, keepdims=True)
        pick = eid == jnp.min(jnp.where(cand == m, eid, N_EXPERTS), 0, keepdims=True)
        chosen = chosen | pick
        cand = jnp.where(pick, -jnp.inf, cand)
    wsel = jnp.where(chosen, scores, 0.0)
    return wsel / jnp.sum(wsel, 0, keepdims=True) * ROUTED_SCALE


def _out_body(*refs, tm):
    n_x = tm // TOKEN_TILE + 1
    x_refs, oa_refs, ob_refs, oc_refs = (refs[j * n_x:(j + 1) * n_x] for j in range(4))
    (mod_ref, wa_ref, wb_ref, wc_ref, lng_ref, lnb_ref,
     rwt_ref, rb_ref, xmid_ref, h_ref, gate_ref) = refs[4 * n_x:]
    for k, rows, is_ctx in _sub_tiles(tm):
        attn = (_dot(_stream_tile(oa_refs, k, is_ctx), wa_ref[...])
                + _dot(_stream_tile(ob_refs, k, is_ctx), wb_ref[...])
                + _dot(_stream_tile(oc_refs, k, is_ctx), wc_ref[...]))
        y = DN_ALPHA * _stream_tile(x_refs, k, is_ctx) + _mod_row(mod_ref, is_ctx, 2) * attn
        xm = _layer_norm(y) * lng_ref[...] + lnb_ref[...]
        xmid_ref[0, rows, :] = xm
        h = _layer_norm(xm) * (1.0 + _mod_row(mod_ref, is_ctx, 4)) + _mod_row(mod_ref, is_ctx, 3)
        hi, lo = _split_bf16(h)
        h_ref[0, rows, :] = hi
        logits_t = _dot_nt(rwt_ref[...], jnp.concatenate([hi, lo, hi], axis=1))
        gates_t = _router_gates(logits_t, rb_ref[...])
        pad = jnp.zeros((128 - N_EXPERTS, gates_t.shape[1]), F32)
        gates = jnp.concatenate([gates_t, pad], axis=0).T
        for g in range(N_GROUPS):
            gate_ref[0, g, rows, :] = gates[:, g * GROUP_SIZE:(g + 1) * GROUP_SIZE]


def _out_proj(s, x_lat, x_ctx, ctx_block, oa, ob, oc, mod, wa, wb, wc, lng, lnb, rwt3, rb):
    bsz = x_lat.shape[0]
    d = D_MODEL
    tm = _step_tokens(s)
    tok = lambda w: pl.BlockSpec((1, tm, w), lambda b, i: (b, i, 0))
    full = lambda r, c: pl.BlockSpec((r, c), lambda b, i: (0, 0))
    return pl.pallas_call(
        functools.partial(_out_body, tm=tm),
        grid=(bsz, s // tm),
        in_specs=_stream_specs(tm, ctx_block) + _stream_specs(tm, 0, A_W) + _stream_specs(tm, 0, B_W)
        + _stream_specs(tm, 0, C_CH) + [
                  pl.BlockSpec((1, 2, 6, d), lambda b, i: (b, 0, 0, 0)),
                  full(A_W, d), full(B_W, d), full(C_CH, d), full(1, d), full(1, d),
                  full(N_EXPERTS, 3 * d), full(N_EXPERTS, 1)],
        out_specs=[tok(d), tok(d),
                   pl.BlockSpec((1, N_GROUPS, tm, GROUP_SIZE), lambda b, i: (b, 0, i, 0))],
        out_shape=[jax.ShapeDtypeStruct((bsz, s, d), F32),
                   jax.ShapeDtypeStruct((bsz, s, d), BF16),
                   jax.ShapeDtypeStruct((bsz, N_GROUPS, s, GROUP_SIZE), F32)],
        compiler_params=_params("parallel", "parallel"),
        name="proj_out_router",
    )(*_stream_args(x_lat, x_ctx, tm), *_stream_args(*oa, tm), *_stream_args(*ob, tm), *_stream_args(*oc, tm),
      mod, wa, wb, wc, lng, lnb, rwt3, rb)


MOE_CHAINS = 2


def _moe_body(h_ref, gate_ref, xmid_ref, mod_ref, w1_ref, w3_ref, w2_ref,
              sw1_ref, sw3_ref, sw2_ref, lng_ref, lnb_ref, o_ref, acc_ref, *, tm):
    g = pl.program_id(2)
    rc = tm // MOE_CHAINS

    @pl.when(g == 0)
    def _():
        h = h_ref[0]
        a = _dot(h, sw1_ref[...])
        act = (a * _sigmoid(a) * _dot(h, sw3_ref[...])).astype(BF16)
        acc_ref[...] = _dot(act, sw2_ref[...])

    w1 = jnp.concatenate([w1_ref[k] for k in range(GROUP_SIZE)], axis=1)
    w3 = jnp.concatenate([w3_ref[k] for k in range(GROUP_SIZE)], axis=1)
    w2 = w2_ref[...].reshape(GROUP_SIZE * EXPERT_HIDDEN, D_MODEL)
    for c in range(MOE_CHAINS):
        rows = slice(c * rc, (c + 1) * rc)
        h = h_ref[0, rows, :]
        a = _dot(h, w1)
        act = a * _sigmoid(a) * _dot(h, w3)
        gt = gate_ref[0, 0, rows, :]
        parts = [act[:, k * EXPERT_HIDDEN:(k + 1) * EXPERT_HIDDEN] * gt[:, k:k + 1]
                 for k in range(GROUP_SIZE)]
        acc_ref[rows, :] += _dot(jnp.concatenate(parts, axis=1).astype(BF16), w2)

    @pl.when(g == N_GROUPS - 1)
    def _():
        row = lax.broadcasted_iota(jnp.int32, (tm, 1), 0) + pl.program_id(1) * tm
        gate2 = jnp.where(row >= SEQ, mod_ref[0, 1, 5:6, :], mod_ref[0, 0, 5:6, :])
        y = DN_ALPHA * xmid_ref[0] + gate2 * acc_ref[...]
        o_ref[0] = _layer_norm(y) * lng_ref[...] + lnb_ref[...]


MOE_TILE_2BUF = 768
MOE_TILE_1BUF = 1024


def _moe(h, gates, xmid, mod, layer, w1_all, w3_all, w2_all, sw1, sw3, sw2, lng, lnb):
    bsz, s, d = h.shape
    tm = MOE_TILE_1BUF if s % MOE_TILE_1BUF == 0 else MOE_TILE_2BUF
    tok = pl.BlockSpec((1, tm, d), lambda b, i, g: (b, i, 0))
    tok_once = pl.BlockSpec((1, tm, d), lambda b, i, g: (b, i, 0),
                            pipeline_mode=pl.Buffered(1 if tm > MOE_TILE_2BUF else 2))
    full = lambda r, c: pl.BlockSpec((r, c), lambda b, i, g: (0, 0))
    up = pl.BlockSpec((None, GROUP_SIZE, d, EXPERT_HIDDEN), lambda b, i, g: (layer, g, 0, 0))
    return pl.pallas_call(
        functools.partial(_moe_body, tm=tm),
        grid=(bsz, s // tm, N_GROUPS),
        in_specs=[tok,
                  pl.BlockSpec((1, 1, tm, GROUP_SIZE), lambda b, i, g: (b, g, i, 0)),
                  tok_once,
                  pl.BlockSpec((1, 2, 6, d), lambda b, i, g: (b, 0, 0, 0)),
                  up, up,
                  pl.BlockSpec((None, GROUP_SIZE, EXPERT_HIDDEN, d), lambda b, i, g: (layer, g, 0, 0)),
                  full(d, SHARED_HIDDEN), full(d, SHARED_HIDDEN), full(SHARED_HIDDEN, d),
                  full(1, d), full(1, d)],
        out_specs=tok,
        out_shape=jax.ShapeDtypeStruct((bsz, s, d), F32),
        scratch_shapes=[pltpu.VMEM((tm, d), F32)],
        compiler_params=_params("parallel", "parallel", "arbitrary"),
        name="moe_ffn",
    )(h, gates, xmid, mod, w1_all, w3_all, w2_all, sw1, sw3, sw2, lng, lnb)


def _rope_tables():
    t = jnp.arange(SEQ)
    row = (t // GRID_W).astype(F32)
    col = (t % GRID_W).astype(F32)
    inv = ROPE_THETA ** (-jnp.arange(0, HALF, 2, dtype=F32) / HALF)
    ang = jnp.concatenate([row[:, None] * inv, col[:, None] * inv], -1)
    cos, sin = jnp.cos(ang), jnp.sin(ang)
    n_heads = ROPE_TABLE_W // HEAD_DIM
    cos_t = jnp.concatenate([jnp.tile(jnp.concatenate([cos, cos], -1), (1, n_heads)),
                             jnp.ones((CTX_LEN, ROPE_TABLE_W), F32)], 0)
    sin_t = jnp.concatenate([jnp.tile(jnp.concatenate([-sin, sin], -1), (1, n_heads)),
                             jnp.zeros((CTX_LEN, ROPE_TABLE_W), F32)], 0)
    return cos_t, sin_t


def _modulation(c, c_ctx, w, b):
    cvec = jnp.concatenate([c, c_ctx[None]], axis=0)
    m = (jnp.dot(jax.nn.silu(cvec), w, precision=HIGHEST) + b).reshape(-1, 6, D_MODEL)
    return jnp.stack([m[:-1], jnp.broadcast_to(m[-1], m[:-1].shape)], axis=1)


def kernel(x, c, ctx, c_ctx, ada_w, ada_b, w_in, q_gain, k_gain, nat_rpb, hy_conv_w, hy_conv_b, hy_f_w1, hy_f_b1, hy_f_w2, hy_f_b2, hy_f_w3, hy_freq, hy_skip, w_out, ln1_g, ln1_b, router_w, router_b, exp_w1, exp_w3, exp_w2, sh_w1, sh_w3, sh_w2, ln2_g, ln2_b):
    bsz = x.shape[0]
    d = D_MODEL
    x_lat, x_ctx, ctx_block = x, ctx, 0

    head_perm = np.concatenate([np.arange(0, HEAD_DIM, 2), np.arange(1, HEAD_DIM, 2)])
    qk_perm = (np.arange(QK_W // HEAD_DIM)[:, None] * HEAD_DIM + head_perm[None, :]).reshape(-1)
    perm_mat = jnp.asarray(np.eye(QK_W)[:, qk_perm], BF16)
    cos_t, sin_t = _rope_tables()
    bd = jnp.asarray(np.kron(np.eye(QK_W // HEAD_DIM), np.ones((HEAD_DIM, HEAD_DIM))), BF16)
    tables_lat = _dft_tables(SEQ)
    tables_ctx = _dft_tables(CTX_LEN)
    w1_bf, w3_bf, w2_bf = exp_w1.astype(BF16), exp_w3.astype(BF16), exp_w2.astype(BF16)

    for l in range(DEPTH):
        last = l == DEPTH - 1
        mod = _modulation(c, c_ctx, ada_w[l], ada_b[l])

        w_bf = w_in[l].astype(BF16)
        w_qk = jnp.dot(w_bf[:, :QK_W], perm_mat, preferred_element_type=F32).astype(BF16)
        w_bf = jnp.concatenate([w_qk, w_bf[:, QK_W:]], axis=1)
        gain = jnp.concatenate([jnp.tile(q_gain[l][head_perm] * ATT_SCALE, A_HEADS),
                                jnp.tile(k_gain[l][head_perm], A_KV_HEADS)])[None].astype(F32)
        qa, ka, va, vat, qb, kb, vb, z = _project(x_lat, x_ctx, ctx_block, mod, w_bf, bd, gain, cos_t, sin_t)

        oa = [_gqa(qa, ka, vat)]
        ob = [_na(qb, kb, vb, _na_bias_table(nat_rpb[l]))]
        if not last:
            oa.append(_ctx_attn(qa, ka, va, 1))
            ob.append(_ctx_attn(qb, kb, vb, NA_HEADS_PER_STEP))

        filt = (hy_f_w1[l], hy_f_b1[l], hy_f_w2[l], hy_f_b2[l], hy_f_w3[l], hy_freq[l])
        cw, cb, skip = hy_conv_w[l], hy_conv_b[l][None], hy_skip[l][None]
        segments = [(0, SEQ, tables_lat)] + ([] if last else [(SEQ, CTX_LEN, tables_ctx)])
        oc = []
        for seg_start, n, (fwd, inv) in segments:
            kc, ks = _filter_spectrum(fwd, *_implicit_filters(n, *filt))
            u, us = _hyena_gate(z, seg_start, n, cw, cb, skip)
            oc.append(_hyena(u, us, z, seg_start, n, cw, cb, fwd, inv, kc, ks))
        oa, ob, oc = ((o[0], o[-1]) for o in (oa, ob, oc))

        wo = w_out[l].astype(BF16)
        rw_hi, rw_lo = _split_bf16(router_w[l].T)
        rwt3 = jnp.concatenate([rw_hi, rw_hi, rw_lo], axis=1)
        xmid, h, gates = _out_proj(SEQ if last else S_ALL, x_lat, x_ctx, ctx_block, oa, ob, oc, mod,
                                   wo[:A_W], wo[A_W:A_W + B_W], wo[A_W + B_W:],
                                   ln1_g[l][None], ln1_b[l][None], rwt3, router_b[l][:, None])

        xs = _moe(h, gates, xmid, mod, l, w1_bf, w3_bf, w2_bf,
                  sh_w1[l].astype(BF16), sh_w3[l].astype(BF16), sh_w2[l].astype(BF16),
                  ln2_g[l][None], ln2_b[l][None])
        x_lat, x_ctx, ctx_block = xs, xs, SEQ // TOKEN_TILE
    return xs
```

```python
import functools
import math

import numpy as np
import jax
import jax.numpy as jnp
from jax import lax
from jax.experimental import pallas as pl
from jax.experimental.pallas import tpu as pltpu

F32 = jnp.float32
BF16 = jnp.bfloat16
HIGHEST = lax.Precision.HIGHEST

D_MODEL = 1024
SEQ = 2048
DEPTH = 2
GRID_W = 64
GRID_ROWS = SEQ // GRID_W
CTX_LEN = 256
S_ALL = SEQ + CTX_LEN
HEAD_DIM = 64
HALF = HEAD_DIM // 2
A_HEADS = 8
A_KV_HEADS = 2
A_REP = A_HEADS // A_KV_HEADS
ROPE_THETA = 10000.0
B_HEADS = 4
WIN_ROWS = 8
WIN_COLS = 16
C_CH = 256
C_POS_BANDS = 16
C_DECAY_TARGET = 1e-2
C_FAST_DECAY = 0.3
C_SLOW_DECAY = 1.5
A_W = A_HEADS * HEAD_DIM
A_KV_W = A_KV_HEADS * HEAD_DIM
B_W = B_HEADS * HEAD_DIM
A_K0 = A_W
A_V0 = A_K0 + A_KV_W
B_Q0 = A_V0 + A_KV_W
B_K0 = B_Q0 + B_W
B_V0 = B_K0 + B_W
C0 = B_V0 + B_W
PROJ_W = C0 + 3 * C_CH
QK_W = A_W + A_KV_W
N_EXPERTS = 64
TOP_K = 8
N_GROUPS = 8
GROUP_SIZE = N_EXPERTS // N_GROUPS
TOPK_GROUPS = 4
EXPERT_HIDDEN = 128
SHARED_HIDDEN = 256
ROUTED_SCALE = 2.5
DN_ALPHA = (2 * DEPTH) ** 0.25
LN_EPS = 1e-6
RMS_EPS = 1e-6
ATT_SCALE = HEAD_DIM ** -0.5
NEG_BIG = -1e30

VMEM_LIMIT_BYTES = 56 * 1024 * 1024
TOKEN_TILE = 256
ROPE_TABLE_W = 128
GQA_Q_TILE = 512
GQA_KEY_CHUNK = 768
FREQ_TILE = 256


def _params(*sem):
    return pltpu.CompilerParams(dimension_semantics=sem, vmem_limit_bytes=VMEM_LIMIT_BYTES)


def _layer_norm(v):
    mu = jnp.mean(v, -1, keepdims=True)
    vc = v - mu
    var = jnp.mean(vc * vc, -1, keepdims=True)
    return vc * lax.rsqrt(var + LN_EPS)


def _sigmoid(v):
    return 1.0 / (1.0 + jnp.exp(-v))


def _dot(a, b):
    return jnp.dot(a, b, preferred_element_type=F32)


def _dot_nt(a, b):
    return lax.dot_general(a, b, (((1,), (1,)), ((), ())), preferred_element_type=F32)


def _mod_row(mod_ref, is_ctx, idx):
    return jnp.where(is_ctx, mod_ref[0, 1, idx:idx + 1, :], mod_ref[0, 0, idx:idx + 1, :])


def _sub_tiles(tm):
    for k in range(tm // TOKEN_TILE):
        rows = slice(k * TOKEN_TILE, (k + 1) * TOKEN_TILE)
        is_ctx = (pl.program_id(1) * tm + k * TOKEN_TILE) >= SEQ
        yield k, rows, is_ctx


def _stream_specs(tm, ctx_block, width=D_MODEL):
    n_sub = tm // TOKEN_TILE
    last_lat = SEQ // TOKEN_TILE - 1
    lat = [pl.BlockSpec((1, TOKEN_TILE, width),
                        lambda b, i, k=k: (b, jnp.minimum(i * n_sub + k, last_lat), 0)) for k in range(n_sub)]
    return lat + [pl.BlockSpec((1, TOKEN_TILE, width), lambda b, i: (b, ctx_block, 0))]


def _stream_args(lat, ctx, tm):
    return [lat] * (tm // TOKEN_TILE) + [ctx]


def _stream_tile(x_refs, k, is_ctx):
    return jnp.where(is_ctx, x_refs[-1][0], x_refs[k][0])


def _proj_body(*refs, tm):
    n_x = tm // TOKEN_TILE + 1
    x_refs = refs[:n_x]
    (mod_ref, w_ref, bd_ref, gain_ref, cos_ref, sin_ref,
     qa_ref, ka_ref, va_ref, qb_ref, kb_ref, vb_ref, z_ref) = refs[n_x:]
    n_rep = QK_W // ROPE_TABLE_W
    for k, rows, is_ctx in _sub_tiles(tm):
        xn = _layer_norm(_stream_tile(x_refs, k, is_ctx))
        hx = (xn * (1.0 + _mod_row(mod_ref, is_ctx, 1)) + _mod_row(mod_ref, is_ctx, 0)).astype(BF16)
        px = _dot(hx, w_ref[...])
        qk = px[:, :QK_W]
        ss = _dot((qk * qk).astype(BF16), bd_ref[...])
        qkn = qk * lax.rsqrt(ss * (1.0 / HEAD_DIM) + RMS_EPS) * gain_ref[...]
        lane = lax.broadcasted_iota(jnp.int32, qkn.shape, 1)
        partner = jnp.where((lane % HEAD_DIM) < HALF,
                            pltpu.roll(qkn, QK_W - HALF, 1), pltpu.roll(qkn, HALF, 1))
        cos = jnp.concatenate([cos_ref[rows, :]] * n_rep, axis=1)
        sin = jnp.concatenate([sin_ref[rows, :]] * n_rep, axis=1)
        qkr = (qkn * cos + partner * sin).astype(BF16)
        for h in range(A_HEADS):
            qa_ref[0, h, rows, :] = qkr[:, h * HEAD_DIM:(h + 1) * HEAD_DIM]
        for g in range(A_KV_HEADS):
            ka_ref[0, g, rows, :] = qkr[:, A_K0 + g * HEAD_DIM:A_K0 + (g + 1) * HEAD_DIM]
            va_ref[0, g, rows, :] = px[:, A_V0 + g * HEAD_DIM:A_V0 + (g + 1) * HEAD_DIM].astype(BF16)
        for h in range(B_HEADS):
            qb_ref[0, h, rows, :] = (px[:, B_Q0 + h * HEAD_DIM:B_Q0 + (h + 1) * HEAD_DIM] * ATT_SCALE).astype(BF16)
            kb_ref[0, h, rows, :] = px[:, B_K0 + h * HEAD_DIM:B_K0 + (h + 1) * HEAD_DIM].astype(BF16)
            vb_ref[0, h, rows, :] = px[:, B_V0 + h * HEAD_DIM:B_V0 + (h + 1) * HEAD_DIM].astype(BF16)
        z_ref[0, rows, :] = px[:, C0:]


def _step_tokens(s):
    return 3 * TOKEN_TILE if s % (3 * TOKEN_TILE) == 0 else 2 * TOKEN_TILE


def _project(x_lat, x_ctx, ctx_block, mod, w_bf, bd, gain, cos_t, sin_t):
    bsz, d, s = x_lat.shape[0], D_MODEL, S_ALL
    tm = _step_tokens(s)
    nt = s // tm
    heads = lambda n: pl.BlockSpec((1, n, tm, HEAD_DIM), lambda b, i: (b, 0, i, 0))
    hshape = lambda n: jax.ShapeDtypeStruct((bsz, n, s, HEAD_DIM), BF16)
    return pl.pallas_call(
        functools.partial(_proj_body, tm=tm),
        grid=(bsz, nt),
        in_specs=_stream_specs(tm, ctx_block) + [
            pl.BlockSpec((1, 2, 6, d), lambda b, i: (b, 0, 0, 0)),
            pl.BlockSpec((d, PROJ_W), lambda b, i: (0, 0)),
            pl.BlockSpec((QK_W, QK_W), lambda b, i: (0, 0)),
            pl.BlockSpec((1, QK_W), lambda b, i: (0, 0)),
            pl.BlockSpec((tm, ROPE_TABLE_W), lambda b, i: (i, 0)),
            pl.BlockSpec((tm, ROPE_TABLE_W), lambda b, i: (i, 0)),
        ],
        out_specs=[heads(A_HEADS), heads(A_KV_HEADS), heads(A_KV_HEADS),
                   heads(B_HEADS), heads(B_HEADS), heads(B_HEADS),
                   pl.BlockSpec((1, tm, 3 * C_CH), lambda b, i: (b, i, 0))],
        out_shape=[hshape(A_HEADS), hshape(A_KV_HEADS), hshape(A_KV_HEADS),
                   hshape(B_HEADS), hshape(B_HEADS), hshape(B_HEADS),
                   jax.ShapeDtypeStruct((bsz, s, 3 * C_CH), F32)],
        compiler_params=_params("parallel", "parallel"),
        name="proj_in",
    )(*_stream_args(x_lat, x_ctx, tm), mod, w_bf, bd, gain, cos_t, sin_t)


def _gqa_body(q_ref, k_ref, v_ref, o_ref):
    n_keys = k_ref.shape[2]
    for h in range(A_REP):
        q = q_ref[0, h]
        m = l = acc = None
        for j in range(n_keys // GQA_KEY_CHUNK):
            keys = slice(j * GQA_KEY_CHUNK, (j + 1) * GQA_KEY_CHUNK)
            s = _dot_nt(q, k_ref[0, 0, keys, :])
            mj = jnp.max(s, -1, keepdims=True)
            if j == 0:
                m = mj
                p = jnp.exp(s - m)
                l = jnp.sum(p, -1, keepdims=True)
                acc = _dot(p.astype(BF16), v_ref[0, 0, keys, :])
            else:
                m_new = jnp.maximum(m, mj)
                alpha = jnp.exp(m - m_new)
                p = jnp.exp(s - m_new)
                l = alpha * l + jnp.sum(p, -1, keepdims=True)
                acc = alpha * acc + _dot(p.astype(BF16), v_ref[0, 0, keys, :])
                m = m_new
        o_ref[0, :, h * HEAD_DIM:(h + 1) * HEAD_DIM] = (acc / l).astype(BF16)


def _gqa(qa, ka, va):
    bsz, _, s, _ = qa.shape
    tq = GQA_Q_TILE
    return pl.pallas_call(
        _gqa_body,
        grid=(bsz, A_KV_HEADS, SEQ // tq),
        in_specs=[
            pl.BlockSpec((1, A_REP, tq, HEAD_DIM), lambda b, g, i: (b, g, i, 0)),
            pl.BlockSpec((1, 1, s, HEAD_DIM), lambda b, g, i: (b, g, 0, 0)),
            pl.BlockSpec((1, 1, s, HEAD_DIM), lambda b, g, i: (b, g, 0, 0)),
        ],
        out_specs=pl.BlockSpec((1, tq, A_REP * HEAD_DIM), lambda b, g, i: (b, i, g)),
        out_shape=jax.ShapeDtypeStruct((bsz, SEQ, A_W), BF16),
        compiler_params=_params("parallel", "parallel", "parallel"),
        name="gqa_attn",
    )(qa, ka, va)


NA_HEADS_PER_STEP = 2
NA_Q_ROWS = 8
NA_K_ROWS = 16
NA_Q_TOKENS = NA_Q_ROWS * GRID_W
NA_K_TOKENS = NA_K_ROWS * GRID_W
NA_Q_BLOCKS = GRID_ROWS // NA_Q_ROWS
NA_KEY_ROW0 = (0, 4, 12, 16)
NA_PATTERN_BLOCKS = (0, 1, 3)


def _na_body(q_ref, k_ref, v_ref, bias_ref, o_ref):
    j = pl.program_id(2)
    key_row0 = jnp.where(j == 0, NA_KEY_ROW0[0],
                         jnp.where(j == 1, NA_KEY_ROW0[1], jnp.where(j == 2, NA_KEY_ROW0[2], NA_KEY_ROW0[3])))
    k0 = pl.multiple_of(key_row0 * GRID_W, 4 * GRID_W)
    for hh in range(NA_HEADS_PER_STEP):
        q = q_ref[0, hh]
        kw = k_ref[0, hh, pl.ds(k0, NA_K_TOKENS), :]
        vw = v_ref[0, hh, pl.ds(k0, NA_K_TOKENS), :]
        kc = k_ref[0, hh, SEQ:, :]
        vc = v_ref[0, hh, SEQ:, :]
        sw = _dot_nt(q, kw) + bias_ref[hh, 0]
        sc = _dot_nt(q, kc)
        m = jnp.maximum(jnp.max(sw, -1, keepdims=True), jnp.max(sc, -1, keepdims=True))
        pw = jnp.exp(sw - m)
        pc = jnp.exp(sc - m)
        l = jnp.sum(pw, -1, keepdims=True) + jnp.sum(pc, -1, keepdims=True)
        o = (_dot(pw.astype(BF16), vw) + _dot(pc.astype(BF16), vc)) / l
        o_ref[0, :, hh * HEAD_DIM:(hh + 1) * HEAD_DIM] = o.astype(BF16)


def _na(qb, kb, vb, bias):
    bsz, _, s, _ = qb.shape
    hp = NA_HEADS_PER_STEP
    kv = pl.BlockSpec((1, hp, s, HEAD_DIM), lambda b, h, j: (b, h, 0, 0))
    return pl.pallas_call(
        _na_body,
        grid=(bsz, B_HEADS // hp, NA_Q_BLOCKS),
        in_specs=[pl.BlockSpec((1, hp, NA_Q_TOKENS, HEAD_DIM), lambda b, h, j: (b, h, j, 0)), kv, kv,
                  pl.BlockSpec((hp, 1, NA_Q_TOKENS, NA_K_TOKENS), lambda b, h, j: (h, (j + 1) // 2, 0, 0))],
        out_specs=pl.BlockSpec((1, NA_Q_TOKENS, hp * HEAD_DIM), lambda b, h, j: (b, j, h)),
        out_shape=jax.ShapeDtypeStruct((bsz, SEQ, B_W), BF16),
        compiler_params=_params("parallel", "parallel", "arbitrary"),
        name="nbr_attn",
    )(qb, kb, vb, bias)


def _na_bias_body(t_ref, o_ref):
    o_ref[...] = jnp.full(o_ref.shape, NEG_BIG, F32)
    for p, blk in enumerate(NA_PATTERN_BLOCKS):
        @pl.when(pl.program_id(1) == p)
        def _(blk=blk):
            for qi in range(NA_Q_ROWS):
                qrow = blk * NA_Q_ROWS + qi
                rs = min(max(qrow - WIN_ROWS // 2, 0), GRID_ROWS - WIN_ROWS)
                for krow in range(rs, rs + WIN_ROWS):
                    ki = krow - NA_KEY_ROW0[blk]
                    o_ref[0, 0, qi * GRID_W:(qi + 1) * GRID_W, ki * GRID_W:(ki + 1) * GRID_W] = (
                        t_ref[0, krow - qrow + WIN_ROWS - 1])


def _na_bias_table(rpb):
    col = np.arange(GRID_W)
    col_start = np.clip(col - WIN_COLS // 2, 0, GRID_W - WIN_COLS)
    in_cols = (col[None, :] >= col_start[:, None]) & (col[None, :] < col_start[:, None] + WIN_COLS)
    dc = np.clip(col[None, :] - col[:, None] + WIN_COLS - 1, 0, 2 * WIN_COLS - 2)
    onehot = jnp.asarray(dc[None] == np.arange(2 * WIN_COLS - 1)[:, None, None], F32)
    tiles = jnp.einsum('hdj,jqk->hdqk', rpb.astype(F32), onehot, precision=HIGHEST)
    tiles = jnp.where(in_cols[None, None], tiles, NEG_BIG)
    n_pat = len(NA_PATTERN_BLOCKS)
    return pl.pallas_call(
        _na_bias_body,
        grid=(B_HEADS, n_pat),
        in_specs=[pl.BlockSpec((1, 2 * WIN_ROWS - 1, GRID_W, GRID_W), lambda h, p: (h, 0, 0, 0))],
        out_specs=pl.BlockSpec((1, 1, NA_Q_TOKENS, NA_K_TOKENS), lambda h, p: (h, p, 0, 0)),
        out_shape=jax.ShapeDtypeStruct((B_HEADS, n_pat, NA_Q_TOKENS, NA_K_TOKENS), F32),
        compiler_params=_params("parallel", "arbitrary"),
        name="nbr_bias",
    )(tiles)


def _ctx_attn_body(q_ref, k_ref, v_ref, o_ref, *, rep):
    for g in range(k_ref.shape[1]):
        for r in range(rep):
            hh = g * rep + r
            s = _dot_nt(q_ref[0, hh], k_ref[0, g])
            m = jnp.max(s, -1, keepdims=True)
            p = jnp.exp(s - m)
            l = jnp.sum(p, -1, keepdims=True)
            o = _dot(p.astype(BF16), v_ref[0, g]) / l
            o_ref[0, :, hh * HEAD_DIM:(hh + 1) * HEAD_DIM] = o.astype(BF16)


def _ctx_attn(q, k, v, kv_per_step):
    bsz, n_q, _, _ = q.shape
    n_kv = k.shape[1]
    rep = n_q // n_kv
    seg = lambda n: pl.BlockSpec((1, n, CTX_LEN, HEAD_DIM), lambda b, h: (b, h, SEQ // CTX_LEN, 0))
    return pl.pallas_call(
        functools.partial(_ctx_attn_body, rep=rep),
        grid=(bsz, n_kv // kv_per_step),
        in_specs=[seg(kv_per_step * rep), seg(kv_per_step), seg(kv_per_step)],
        out_specs=pl.BlockSpec((1, CTX_LEN, kv_per_step * rep * HEAD_DIM), lambda b, h: (b, 0, h)),
        out_shape=jax.ShapeDtypeStruct((bsz, CTX_LEN, n_q * HEAD_DIM), BF16),
        compiler_params=_params("parallel", "parallel"),
        name="ctx_attn",
    )(q, k, v)


DFT_FINE = 16


def _dft_table_body(chr_ref, shr_ref, clr_ref, slr_ref, chc_ref, shc_ref, clc_ref, slc_ref,
                    fwd_ref, inv_ref, *, n, tf):
    j = pl.program_id(0)

    def expand(c_hi, s_hi, c_lo, s_lo, rows):
        cos = (c_hi[:, None, :] * c_lo[None, :, :] - s_hi[:, None, :] * s_lo[None, :, :])
        sin = (s_hi[:, None, :] * c_lo[None, :, :] + c_hi[:, None, :] * s_lo[None, :, :])
        return cos.reshape(rows, -1), sin.reshape(rows, -1)

    cos_r, sin_r = expand(chr_ref[...], shr_ref[...], clr_ref[...], slr_ref[...], tf)
    cos_c, sin_c = expand(chc_ref[...], shc_ref[...], clc_ref[...], slc_ref[...], n)
    fr = lax.broadcasted_iota(jnp.int32, (tf, n), 0) + j * tf
    tr = lax.broadcasted_iota(jnp.int32, (tf, n), 1)
    sin_r = jnp.where(fr == 0, jnp.where(tr % 2 == 0, 1.0, -1.0), sin_r)
    tc = lax.broadcasted_iota(jnp.int32, (n, tf), 0)
    fcol = lax.broadcasted_iota(jnp.int32, (n, tf), 1) + j * tf
    sin_c = jnp.where(fcol == 0, jnp.where(tc % 2 == 0, 1.0, -1.0), sin_c)
    fwd_ref[0, :tf, :] = cos_r.astype(BF16)
    fwd_ref[0, tf:, :] = sin_r.astype(BF16)
    inv_ref[0, :, :tf] = cos_c.astype(BF16)
    inv_ref[0, :, tf:] = sin_c.astype(BF16)


def _dft_tables(n):
    s = jnp.arange(n, dtype=jnp.int32)[None, :]

    def cos_sin(f):
        ang = ((f * s) % (2 * n)).astype(F32) * (math.pi / n)
        return jnp.cos(ang), jnp.sin(ang)

    ch, sh = cos_sin(jnp.arange(n // DFT_FINE, dtype=jnp.int32)[:, None] * DFT_FINE)
    cl, sl = cos_sin(jnp.arange(DFT_FINE, dtype=jnp.int32)[:, None])
    tf = min(FREQ_TILE, n)
    nf = n // tf
    hi_rows = pl.BlockSpec((tf // DFT_FINE, n), lambda j: (j, 0))
    lo_rows = pl.BlockSpec((DFT_FINE, n), lambda j: (0, 0))
    hi_cols = pl.BlockSpec((n // DFT_FINE, tf), lambda j: (0, j))
    lo_cols = pl.BlockSpec((DFT_FINE, tf), lambda j: (0, j))
    return pl.pallas_call(
        functools.partial(_dft_table_body, n=n, tf=tf),
        grid=(nf,),
        in_specs=[hi_rows, hi_rows, lo_rows, lo_rows, hi_cols, hi_cols, lo_cols, lo_cols],
        out_specs=[pl.BlockSpec((1, 2 * tf, n), lambda j: (j, 0, 0)),
                   pl.BlockSpec((1, n, 2 * tf), lambda j: (j, 0, 0))],
        out_shape=[jax.ShapeDtypeStruct((nf, 2 * tf, n), BF16),
                   jax.ShapeDtypeStruct((nf, n, 2 * tf), BF16)],
        compiler_params=_params("parallel"),
        name="dft_tables",
    )(ch, sh, cl, sl, ch, sh, cl, sl)


def _split_bf16(v):
    hi = v.astype(BF16)
    lo = (v - hi.astype(F32)).astype(BF16)
    return hi, lo


def _spectrum_body(fwd_ref, kp_ref, km_ref, kc_ref, ks_ref, *, tf):
    kp_hi, kp_lo = _split_bf16(kp_ref[...])
    km_hi, km_lo = _split_bf16(km_ref[...])
    fc = fwd_ref[0, :tf, :]
    fs = fwd_ref[0, tf:, :]
    kc_ref[...] = _dot(fc, kp_hi) + _dot(fc, kp_lo)
    ks_sin = _dot(fs, km_hi) + _dot(fs, km_lo)
    ks_nyq = _dot(fs, kp_hi) + _dot(fs, kp_lo)
    row = lax.broadcasted_iota(jnp.int32, ks_sin.shape, 0) + pl.program_id(0) * tf
    ks_ref[...] = jnp.where(row == 0, ks_nyq, ks_sin)


def _filter_spectrum(fwd, kplus, kminus):
    nf, tf2, n = fwd.shape
    tf = tf2 // 2
    filt = pl.BlockSpec((n, C_CH), lambda j: (0, 0))
    out = pl.BlockSpec((tf, C_CH), lambda j: (j, 0))
    return pl.pallas_call(
        functools.partial(_spectrum_body, tf=tf),
        grid=(nf,),
        in_specs=[pl.BlockSpec((1, tf2, n), lambda j: (j, 0, 0)), filt, filt],
        out_specs=[out, out],
        out_shape=[jax.ShapeDtypeStruct((n, C_CH), F32)] * 2,
        compiler_params=_params("parallel"),
        name="hyena_spectrum",
    )(fwd, kplus, kminus)


def _short_conv(z, cw_ref, cb_ref, cols, n):
    pos = lax.broadcasted_iota(jnp.int32, z.shape, 0)
    prev = jnp.where(pos == 0, 0.0, pltpu.roll(z, 1, 0))
    nxt = jnp.where(pos == n - 1, 0.0, pltpu.roll(z, n - 1, 0))
    return prev * cw_ref[0:1, cols] + z * cw_ref[1:2, cols] + nxt * cw_ref[2:3, cols] + cb_ref[:, cols]


def _hyena_gate_body(z_ref, cw_ref, cb_ref, skip_ref, u_ref, us_ref, *, n):
    zc = _short_conv(z_ref[0, :, C_CH:], cw_ref, cb_ref, slice(C_CH, 3 * C_CH), n)
    u = zc[:, C_CH:] * zc[:, :C_CH]
    u_ref[0] = u.astype(BF16)
    us_ref[0] = u * skip_ref[...]


def _hyena_gate(z, seg_start, n, cw, cb, skip):
    bsz = z.shape[0]
    seg = seg_start // n
    small = lambda r, c: pl.BlockSpec((r, c), lambda b: (0, 0))
    out = pl.BlockSpec((1, n, C_CH), lambda b: (b, 0, 0))
    return pl.pallas_call(
        functools.partial(_hyena_gate_body, n=n),
        grid=(bsz,),
        in_specs=[pl.BlockSpec((1, n, 3 * C_CH), lambda b: (b, seg, 0)),
                  small(3, 3 * C_CH), small(1, 3 * C_CH), small(1, C_CH)],
        out_specs=[out, out],
        out_shape=[jax.ShapeDtypeStruct((bsz, n, C_CH), BF16),
                   jax.ShapeDtypeStruct((bsz, n, C_CH), F32)],
        compiler_params=_params("parallel"),
        name="hyena_gate",
    )(z, cw, cb, skip)


HYENA_BATCH = 4


def _hyena_body(u_ref, us_ref, z_ref, cw_ref, cb_ref, fwd_ref, inv_ref, kc_ref, ks_ref, o_ref, y_ref, *, n, tf):
    j = pl.program_id(1)

    @pl.when(j == 0)
    def _():
        y_ref[...] = jnp.zeros_like(y_ref)

    kc = kc_ref[...]
    ks = ks_ref[...]
    row = lax.broadcasted_iota(jnp.int32, kc.shape, 0) + j * tf
    first = row == 0
    wgt = jnp.where(first, 1.0, 2.0) * (1.0 / (2 * n))
    for b in range(HYENA_BATCH):
        spec = _dot(fwd_ref[0], u_ref[b])
        uc, us = spec[:tf], spec[tf:]
        re = jnp.where(first, uc * kc, uc * kc - us * ks) * wgt
        im = jnp.where(first, us * ks, uc * ks + us * kc) * wgt
        y_ref[b] += _dot(inv_ref[0], jnp.concatenate([re, im], axis=0).astype(BF16))

    @pl.when(j == pl.num_programs(1) - 1)
    def _():
        for b in range(HYENA_BATCH):
            x0 = _short_conv(z_ref[b], cw_ref, cb_ref, slice(0, C_CH), n)
            o_ref[b] = ((y_ref[b] + us_ref[b]) * x0).astype(BF16)


def _hyena(u, us, z, seg_start, n, cw, cb, fwd, inv, kc, ks):
    bsz = u.shape[0]
    assert bsz % HYENA_BATCH == 0, bsz
    tf = min(FREQ_TILE, n)
    nf = n // tf
    seg = seg_start // n
    seq = pl.BlockSpec((HYENA_BATCH, n, C_CH), lambda g, j: (g, 0, 0))
    once = pl.Buffered(1)
    small = lambda r, c: pl.BlockSpec((r, c), lambda g, j: (0, 0))
    spec_blk = pl.BlockSpec((tf, C_CH), lambda g, j: (j, 0))
    return pl.pallas_call(
        functools.partial(_hyena_body, n=n, tf=tf),
        grid=(bsz // HYENA_BATCH, nf),
        in_specs=[seq,
                  pl.BlockSpec((HYENA_BATCH, n, C_CH), lambda g, j: (g, 0, 0), pipeline_mode=once),
                  pl.BlockSpec((HYENA_BATCH, n, C_CH), lambda g, j: (g, seg, 0), pipeline_mode=once),
                  small(3, 3 * C_CH), small(1, 3 * C_CH),
                  pl.BlockSpec((1, 2 * tf, n), lambda g, j: (j, 0, 0)),
                  pl.BlockSpec((1, n, 2 * tf), lambda g, j: (j, 0, 0)),
                  spec_blk, spec_blk],
        out_specs=seq,
        out_shape=jax.ShapeDtypeStruct((bsz, n, C_CH), BF16),
        scratch_shapes=[pltpu.VMEM((HYENA_BATCH, n, C_CH), F32)],
        compiler_params=_params("parallel", "arbitrary"),
        name="hyena_conv",
    )(u, us, z, cw, cb, fwd, inv, kc, ks)


def _dot3(a, b):
    a_hi, a_lo = _split_bf16(a)
    b_hi, b_lo = _split_bf16(b)
    return _dot(a_hi, b_hi) + _dot(a_hi, b_lo) + _dot(a_lo, b_hi)


POS_PAD = 128
FILTER_ROWS = 256


def _filter_body(pos_ref, decay_ref, w1_ref, b1_ref, w2_ref, b2_ref, w3_ref, freq_ref, kp_ref, km_ref):
    h = jnp.sin(freq_ref[0:1, :] * (_dot3(pos_ref[...], w1_ref[...]) + b1_ref[...]))
    h = jnp.sin(freq_ref[1:2, :] * (_dot3(h, w2_ref[...]) + b2_ref[...]))
    h = _dot3(h, w3_ref[...])
    decay = decay_ref[...]
    hf = h[:, :C_CH] * decay
    hb = h[:, C_CH:] * decay
    lag = lax.broadcasted_iota(jnp.int32, hb.shape, 0) + pl.program_id(0) * FILTER_ROWS
    hb = jnp.where(lag == 0, 0.0, hb)
    kp_ref[...] = hf + hb
    km_ref[...] = hf - hb


def _implicit_filters(n, w1, b1, w2, b2, w3, freq):
    t = jnp.linspace(0.0, 1.0, n, dtype=F32)[:, None]
    w = 2 * math.pi * jnp.arange(n, dtype=F32)[:, None] / n
    f = jnp.linspace(1e-4, C_POS_BANDS - 1, C_POS_BANDS, dtype=F32)[None]
    pos = jnp.concatenate([t, jnp.cos(f * w), -jnp.sin(f * w)], -1)
    n_pos = pos.shape[1]
    pos = jnp.pad(pos, ((0, 0), (0, POS_PAD - n_pos)))
    w1p = jnp.pad(w1, ((0, POS_PAD - n_pos), (0, 0)))
    lo = math.log(C_DECAY_TARGET) / C_SLOW_DECAY
    hi = math.log(C_DECAY_TARGET) / C_FAST_DECAY
    deltas = jnp.abs(jnp.linspace(lo, hi, C_CH, dtype=F32))
    decay = jnp.exp(-t * deltas[None])
    hid = w2.shape[0]
    rows = lambda c: pl.BlockSpec((FILTER_ROWS, c), lambda i: (i, 0))
    full = lambda r, c: pl.BlockSpec((r, c), lambda i: (0, 0))
    return pl.pallas_call(
        _filter_body,
        grid=(n // FILTER_ROWS,),
        in_specs=[rows(POS_PAD), rows(C_CH), full(POS_PAD, hid), full(1, hid), full(hid, hid), full(1, hid),
                  full(hid, 2 * C_CH), full(2, hid)],
        out_specs=[rows(C_CH), rows(C_CH)],
        out_shape=[jax.ShapeDtypeStruct((n, C_CH), F32)] * 2,
        compiler_params=_params("parallel"),
        name="hyena_filter",
    )(pos, decay, w1p, b1[None], w2, b2[None], w3, freq)


def _router_gates(logits_t, rb):
    tm = logits_t.shape[1]
    scores = _sigmoid(logits_t)
    sel = scores + rb
    sel3 = sel.reshape(N_GROUPS, GROUP_SIZE, tm)
    kk = lax.broadcasted_iota(jnp.int32, sel3.shape, 1)
    m1 = jnp.max(sel3, 1, keepdims=True)
    i1 = jnp.min(jnp.where(sel3 == m1, kk, GROUP_SIZE), 1, keepdims=True)
    m2 = jnp.max(jnp.where(kk == i1, -jnp.inf, sel3), 1, keepdims=True)
    gscore = (m1 + m2).reshape(N_GROUPS, tm)
    gid = lax.broadcasted_iota(jnp.int32, gscore.shape, 0)
    rank = jnp.zeros(gscore.shape, jnp.int32)
    for g in range(N_GROUPS):
        other = gscore[g:g + 1, :]
        ahead = (other > gscore) | ((other == gscore) & (g < gid))
        rank = rank + ahead.astype(jnp.int32)
    gsel = (rank < TOPK_GROUPS).reshape(N_GROUPS, 1, tm)
    cand = jnp.where(gsel, sel3, -jnp.inf).reshape(N_EXPERTS, tm)
    eid = lax.broadcasted_iota(jnp.int32, cand.shape, 0)
    chosen = jnp.zeros(cand.shape, jnp.bool_)
    for _ in range(TOP_K):
        m = jnp.max(cand, 0, keepdims=True)
        pick = eid == jnp.min(jnp.where(cand == m, eid, N_EXPERTS), 0, keepdims=True)
        chosen = chosen | pick
        cand = jnp.where(pick, -jnp.inf, cand)
    wsel = jnp.where(chosen, scores, 0.0)
    return wsel / jnp.sum(wsel, 0, keepdims=True) * ROUTED_SCALE


def _out_body(*refs, tm):
    n_x = tm // TOKEN_TILE + 1
    x_refs, oa_refs, ob_refs, oc_refs = (refs[j * n_x:(j + 1) * n_x] for j in range(4))
    (mod_ref, wa_ref, wb_ref, wc_ref, lng_ref, lnb_ref,
     rwt_ref, rb_ref, xmid_ref, h_ref, gate_ref) = refs[4 * n_x:]
    for k, rows, is_ctx in _sub_tiles(tm):
        attn = (_dot(_stream_tile(oa_refs, k, is_ctx), wa_ref[...])
                + _dot(_stream_tile(ob_refs, k, is_ctx), wb_ref[...])
                + _dot(_stream_tile(oc_refs, k, is_ctx), wc_ref[...]))
        y = DN_ALPHA * _stream_tile(x_refs, k, is_ctx) + _mod_row(mod_ref, is_ctx, 2) * attn
        xm = _layer_norm(y) * lng_ref[...] + lnb_ref[...]
        xmid_ref[0, rows, :] = xm
        h = _layer_norm(xm) * (1.0 + _mod_row(mod_ref, is_ctx, 4)) + _mod_row(mod_ref, is_ctx, 3)
        hi, lo = _split_bf16(h)
        h_ref[0, rows, :] = hi
        logits_t = _dot_nt(rwt_ref[...], jnp.concatenate([hi, lo, hi], axis=1))
        gates_t = _router_gates(logits_t, rb_ref[...])
        pad = jnp.zeros((128 - N_EXPERTS, gates_t.shape[1]), F32)
        gates = jnp.concatenate([gates_t, pad], axis=0).T
        for g in range(N_GROUPS):
            gate_ref[0, g, rows, :] = gates[:, g * GROUP_SIZE:(g + 1) * GROUP_SIZE]


def _out_proj(s, x_lat, x_ctx, ctx_block, oa, ob, oc, mod, wa, wb, wc, lng, lnb, rwt3, rb):
    bsz = x_lat.shape[0]
    d = D_MODEL
    tm = _step_tokens(s)
    tok = lambda w: pl.BlockSpec((1, tm, w), lambda b, i: (b, i, 0))
    full = lambda r, c: pl.BlockSpec((r, c), lambda b, i: (0, 0))
    return pl.pallas_call(
        functools.partial(_out_body, tm=tm),
        grid=(bsz, s // tm),
        in_specs=_stream_specs(tm, ctx_block) + _stream_specs(tm, 0, A_W) + _stream_specs(tm, 0, B_W)
        + _stream_specs(tm, 0, C_CH) + [
                  pl.BlockSpec((1, 2, 6, d), lambda b, i: (b, 0, 0, 0)),
                  full(A_W, d), full(B_W, d), full(C_CH, d), full(1, d), full(1, d),
                  full(N_EXPERTS, 3 * d), full(N_EXPERTS, 1)],
        out_specs=[tok(d), tok(d),
                   pl.BlockSpec((1, N_GROUPS, tm, GROUP_SIZE), lambda b, i: (b, 0, i, 0))],
        out_shape=[jax.ShapeDtypeStruct((bsz, s, d), F32),
                   jax.ShapeDtypeStruct((bsz, s, d), BF16),
                   jax.ShapeDtypeStruct((bsz, N_GROUPS, s, GROUP_SIZE), F32)],
        compiler_params=_params("parallel", "parallel"),
        name="proj_out_router",
    )(*_stream_args(x_lat, x_ctx, tm), *_stream_args(*oa, tm), *_stream_args(*ob, tm), *_stream_args(*oc, tm),
      mod, wa, wb, wc, lng, lnb, rwt3, rb)


MOE_CHAINS = 2


def _moe_body(h_ref, gate_ref, xmid_ref, mod_ref, w1_ref, w3_ref, w2_ref,
              sw1_ref, sw3_ref, sw2_ref, lng_ref, lnb_ref, o_ref, acc_ref, *, tm):
    g = pl.program_id(2)
    rc = tm // MOE_CHAINS

    @pl.when(g == 0)
    def _():
        h = h_ref[0]
        a = _dot(h, sw1_ref[...])
        act = (a * _sigmoid(a) * _dot(h, sw3_ref[...])).astype(BF16)
        acc_ref[...] = _dot(act, sw2_ref[...])

    w1 = jnp.concatenate([w1_ref[k] for k in range(GROUP_SIZE)], axis=1)
    w3 = jnp.concatenate([w3_ref[k] for k in range(GROUP_SIZE)], axis=1)
    w2 = w2_ref[...].reshape(GROUP_SIZE * EXPERT_HIDDEN, D_MODEL)
    for c in range(MOE_CHAINS):
        rows = slice(c * rc, (c + 1) * rc)
        h = h_ref[0, rows, :]
        a = _dot(h, w1)
        act = a * _sigmoid(a) * _dot(h, w3)
        gt = gate_ref[0, 0, rows, :]
        parts = [act[:, k * EXPERT_HIDDEN:(k + 1) * EXPERT_HIDDEN] * gt[:, k:k + 1]
                 for k in range(GROUP_SIZE)]
        acc_ref[rows, :] += _dot(jnp.concatenate(parts, axis=1).astype(BF16), w2)

    @pl.when(g == N_GROUPS - 1)
    def _():
        row = lax.broadcasted_iota(jnp.int32, (tm, 1), 0) + pl.program_id(1) * tm
        gate2 = jnp.where(row >= SEQ, mod_ref[0, 1, 5:6, :], mod_ref[0, 0, 5:6, :])
        y = DN_ALPHA * xmid_ref[0] + gate2 * acc_ref[...]
        o_ref[0] = _layer_norm(y) * lng_ref[...] + lnb_ref[...]


MOE_TILE_2BUF = 768
MOE_TILE_1BUF = 1024


def _moe(h, gates, xmid, mod, layer, w1_all, w3_all, w2_all, sw1, sw3, sw2, lng, lnb):
    bsz, s, d = h.shape
    tm = MOE_TILE_1BUF if s % MOE_TILE_1BUF == 0 else MOE_TILE_2BUF
    tok = pl.BlockSpec((1, tm, d), lambda b, i, g: (b, i, 0))
    tok_once = pl.BlockSpec((1, tm, d), lambda b, i, g: (b, i, 0),
                            pipeline_mode=pl.Buffered(1 if tm > MOE_TILE_2BUF else 2))
    full = lambda r, c: pl.BlockSpec((r, c), lambda b, i, g: (0, 0))
    up = pl.BlockSpec((None, GROUP_SIZE, d, EXPERT_HIDDEN), lambda b, i, g: (layer, g, 0, 0))
    return pl.pallas_call(
        functools.partial(_moe_body, tm=tm),
        grid=(bsz, s // tm, N_GROUPS),
        in_specs=[tok,
                  pl.BlockSpec((1, 1, tm, GROUP_SIZE), lambda b, i, g: (b, g, i, 0)),
                  tok_once,
                  pl.BlockSpec((1, 2, 6, d), lambda b, i, g: (b, 0, 0, 0)),
                  up, up,
                  pl.BlockSpec((None, GROUP_SIZE, EXPERT_HIDDEN, d), lambda b, i, g: (layer, g, 0, 0)),
                  full(d, SHARED_HIDDEN), full(d, SHARED_HIDDEN), full(SHARED_HIDDEN, d),
                  full(1, d), full(1, d)],
        out_specs=tok,
        out_shape=jax.ShapeDtypeStruct((bsz, s, d), F32),
        scratch_shapes=[pltpu.VMEM((tm, d), F32)],
        compiler_params=_params("parallel", "parallel", "arbitrary"),
        name="moe_ffn",
    )(h, gates, xmid, mod, w1_all, w3_all, w2_all, sw1, sw3, sw2, lng, lnb)


def _rope_tables():
    t = jnp.arange(SEQ)
    row = (t // GRID_W).astype(F32)
    col = (t % GRID_W).astype(F32)
    inv = ROPE_THETA ** (-jnp.arange(0, HALF, 2, dtype=F32) / HALF)
    ang = jnp.concatenate([row[:, None] * inv, col[:, None] * inv], -1)
    cos, sin = jnp.cos(ang), jnp.sin(ang)
    n_heads = ROPE_TABLE_W // HEAD_DIM
    cos_t = jnp.concatenate([jnp.tile(jnp.concatenate([cos, cos], -1), (1, n_heads)),
                             jnp.ones((CTX_LEN, ROPE_TABLE_W), F32)], 0)
    sin_t = jnp.concatenate([jnp.tile(jnp.concatenate([-sin, sin], -1), (1, n_heads)),
                             jnp.zeros((CTX_LEN, ROPE_TABLE_W), F32)], 0)
    return cos_t, sin_t


MOD_ROWS = 16
MOD_COL_TILE = 1536


def _modulation_body(c_ref, w_ref, b_ref, o_ref):
    cv = c_ref[...]
    o_ref[0] = _dot3(cv * _sigmoid(cv), w_ref[0]) + b_ref[0]


def _modulation(c, c_ctx, ada_w, ada_b):
    bsz = c.shape[0]
    assert bsz + 1 <= MOD_ROWS, bsz
    n_out = 6 * D_MODEL
    cvec = jnp.concatenate([c, c_ctx[None], jnp.zeros((MOD_ROWS - bsz - 1, D_MODEL), F32)], axis=0)
    m = pl.pallas_call(
        _modulation_body,
        grid=(DEPTH, n_out // MOD_COL_TILE),
        in_specs=[pl.BlockSpec((MOD_ROWS, D_MODEL), lambda l, j: (0, 0)),
                  pl.BlockSpec((1, D_MODEL, MOD_COL_TILE), lambda l, j: (l, 0, j)),
                  pl.BlockSpec((1, 1, MOD_COL_TILE), lambda l, j: (l, 0, j))],
        out_specs=pl.BlockSpec((1, MOD_ROWS, MOD_COL_TILE), lambda l, j: (l, 0, j)),
        out_shape=jax.ShapeDtypeStruct((DEPTH, MOD_ROWS, n_out), F32),
        compiler_params=_params("parallel", "parallel"),
        name="ada_modulation",
    )(cvec, ada_w, ada_b[:, None, :])
    m = m.reshape(DEPTH, MOD_ROWS, 6, D_MODEL)
    lat = m[:, :bsz]
    return jnp.stack([lat, jnp.broadcast_to(m[:, bsz:bsz + 1], lat.shape)], axis=2)


def kernel(x, c, ctx, c_ctx, ada_w, ada_b, w_in, q_gain, k_gain, nat_rpb, hy_conv_w, hy_conv_b, hy_f_w1, hy_f_b1, hy_f_w2, hy_f_b2, hy_f_w3, hy_freq, hy_skip, w_out, ln1_g, ln1_b, router_w, router_b, exp_w1, exp_w3, exp_w2, sh_w1, sh_w3, sh_w2, ln2_g, ln2_b):
    bsz = x.shape[0]
    d = D_MODEL
    x_lat, x_ctx, ctx_block = x, ctx, 0

    head_perm = np.concatenate([np.arange(0, HEAD_DIM, 2), np.arange(1, HEAD_DIM, 2)])
    qk_perm = (np.arange(QK_W // HEAD_DIM)[:, None] * HEAD_DIM + head_perm[None, :]).reshape(-1)
    perm_mat = jnp.asarray(np.eye(QK_W)[:, qk_perm], BF16)
    cos_t, sin_t = _rope_tables()
    bd = jnp.asarray(np.kron(np.eye(QK_W // HEAD_DIM), np.ones((HEAD_DIM, HEAD_DIM))), BF16)
    tables_lat = _dft_tables(SEQ)
    tables_ctx = _dft_tables(CTX_LEN)
    mod_all = _modulation(c, c_ctx, ada_w, ada_b)
    w1_bf, w3_bf, w2_bf = exp_w1.astype(BF16), exp_w3.astype(BF16), exp_w2.astype(BF16)

    for l in range(DEPTH):
        last = l == DEPTH - 1
        mod = mod_all[l]

        w_bf = w_in[l].astype(BF16)
        w_qk = jnp.dot(w_bf[:, :QK_W], perm_mat, preferred_element_type=F32).astype(BF16)
        w_bf = jnp.concatenate([w_qk, w_bf[:, QK_W:]], axis=1)
        gain = jnp.concatenate([jnp.tile(q_gain[l][head_perm] * ATT_SCALE, A_HEADS),
                                jnp.tile(k_gain[l][head_perm], A_KV_HEADS)])[None].astype(F32)
        qa, ka, va, qb, kb, vb, z = _project(x_lat, x_ctx, ctx_block, mod, w_bf, bd, gain, cos_t, sin_t)

        oa = [_gqa(qa, ka, va)]
        ob = [_na(qb, kb, vb, _na_bias_table(nat_rpb[l]))]
        if not last:
            oa.append(_ctx_attn(qa, ka, va, 1))
            ob.append(_ctx_attn(qb, kb, vb, NA_HEADS_PER_STEP))

        filt = (hy_f_w1[l], hy_f_b1[l], hy_f_w2[l], hy_f_b2[l], hy_f_w3[l], hy_freq[l])
        cw, cb, skip = hy_conv_w[l], hy_conv_b[l][None], hy_skip[l][None]
        segments = [(0, SEQ, tables_lat)] + ([] if last else [(SEQ, CTX_LEN, tables_ctx)])
        oc = []
        for seg_start, n, (fwd, inv) in segments:
            kc, ks = _filter_spectrum(fwd, *_implicit_filters(n, *filt))
            u, us = _hyena_gate(z, seg_start, n, cw, cb, skip)
            oc.append(_hyena(u, us, z, seg_start, n, cw, cb, fwd, inv, kc, ks))
        oa, ob, oc = ((o[0], o[-1]) for o in (oa, ob, oc))

        wo = w_out[l].astype(BF16)
        rw_hi, rw_lo = _split_bf16(router_w[l].T)
        rwt3 = jnp.concatenate([rw_hi, rw_hi, rw_lo], axis=1)
        xmid, h, gates = _out_proj(SEQ if last else S_ALL, x_lat, x_ctx, ctx_block, oa, ob, oc, mod,
                                   wo[:A_W], wo[A_W:A_W + B_W], wo[A_W + B_W:],
                                   ln1_g[l][None], ln1_b[l][None], rwt3, router_b[l][:, None])

        xs = _moe(h, gates, xmid, mod, l, w1_bf, w3_bf, w2_bf,
                  sh_w1[l].astype(BF16), sh_w3[l].astype(BF16), sh_w2[l].astype(BF16),
                  ln2_g[l][None], ln2_b[l][None])
        x_lat, x_ctx, ctx_block = xs, xs, SEQ // TOKEN_TILE
    return xs
```

```python
import functools
import math

import numpy as np
import jax
import jax.numpy as jnp
from jax import lax
from jax.experimental import pallas as pl
from jax.experimental.pallas import tpu as pltpu

F32 = jnp.float32
BF16 = jnp.bfloat16
HIGHEST = lax.Precision.HIGHEST

D_MODEL = 1024
SEQ = 2048
DEPTH = 2
GRID_W = 64
GRID_ROWS = SEQ // GRID_W
CTX_LEN = 256
S_ALL = SEQ + CTX_LEN
HEAD_DIM = 64
HALF = HEAD_DIM // 2
A_HEADS = 8
A_KV_HEADS = 2
A_REP = A_HEADS // A_KV_HEADS
ROPE_THETA = 10000.0
B_HEADS = 4
WIN_ROWS = 8
WIN_COLS = 16
C_CH = 256
C_POS_BANDS = 16
C_DECAY_TARGET = 1e-2
C_FAST_DECAY = 0.3
C_SLOW_DECAY = 1.5
A_W = A_HEADS * HEAD_DIM
A_KV_W = A_KV_HEADS * HEAD_DIM
B_W = B_HEADS * HEAD_DIM
A_K0 = A_W
A_V0 = A_K0 + A_KV_W
B_Q0 = A_V0 + A_KV_W
B_K0 = B_Q0 + B_W
B_V0 = B_K0 + B_W
C0 = B_V0 + B_W
PROJ_W = C0 + 3 * C_CH
QK_W = A_W + A_KV_W
N_EXPERTS = 64
TOP_K = 8
N_GROUPS = 8
GROUP_SIZE = N_EXPERTS // N_GROUPS
TOPK_GROUPS = 4
EXPERT_HIDDEN = 128
SHARED_HIDDEN = 256
ROUTED_SCALE = 2.5
DN_ALPHA = (2 * DEPTH) ** 0.25
LN_EPS = 1e-6
RMS_EPS = 1e-6
LOG2E = math.log2(math.e)
Q_SCALE = HEAD_DIM ** -0.5 * LOG2E
NEG_BIG = -1e30

VMEM_LIMIT_BYTES = 56 * 1024 * 1024
TOKEN_TILE = 256
ROPE_TABLE_W = 128
GQA_Q_TILE = 512
GQA_KEY_CHUNK = 768
FREQ_TILE = 256


def _params(*sem):
    return pltpu.CompilerParams(dimension_semantics=sem, vmem_limit_bytes=VMEM_LIMIT_BYTES)


def _layer_norm(v):
    mu = jnp.mean(v, -1, keepdims=True)
    vc = v - mu
    var = jnp.mean(vc * vc, -1, keepdims=True)
    return vc * lax.rsqrt(var + LN_EPS)


def _sigmoid(v):
    return 1.0 / (1.0 + jnp.exp(-v))


def _dot(a, b):
    return jnp.dot(a, b, preferred_element_type=F32)


def _dot_nt(a, b):
    return lax.dot_general(a, b, (((1,), (1,)), ((), ())), preferred_element_type=F32)


V_EXT_W = 2 * HEAD_DIM


def _with_ones(v):
    return jnp.concatenate([v.astype(BF16), jnp.ones(v.shape, BF16)], axis=1)


def _normalise(o_ext):
    return o_ext[:, :HEAD_DIM] / o_ext[:, HEAD_DIM:]


def _mod_row(mod_ref, is_ctx, idx):
    return jnp.where(is_ctx, mod_ref[0, 1, idx:idx + 1, :], mod_ref[0, 0, idx:idx + 1, :])


def _sub_tiles(tm):
    for k in range(tm // TOKEN_TILE):
        rows = slice(k * TOKEN_TILE, (k + 1) * TOKEN_TILE)
        is_ctx = (pl.program_id(1) * tm + k * TOKEN_TILE) >= SEQ
        yield k, rows, is_ctx


def _stream_specs(tm, ctx_block, width=D_MODEL):
    n_sub = tm // TOKEN_TILE
    last_lat = SEQ // TOKEN_TILE - 1
    lat = [pl.BlockSpec((1, TOKEN_TILE, width),
                        lambda b, i, k=k: (b, jnp.minimum(i * n_sub + k, last_lat), 0)) for k in range(n_sub)]
    return lat + [pl.BlockSpec((1, TOKEN_TILE, width), lambda b, i: (b, ctx_block, 0))]


def _stream_args(lat, ctx, tm):
    return [lat] * (tm // TOKEN_TILE) + [ctx]


def _stream_tile(x_refs, k, is_ctx):
    return jnp.where(is_ctx, x_refs[-1][0], x_refs[k][0])


def _proj_body(*refs, tm):
    n_x = tm // TOKEN_TILE + 1
    x_refs = refs[:n_x]
    (mod_ref, w_ref, bd_ref, gain_ref, cos_ref, sin_ref,
     qa_ref, ka_ref, va_ref, qb_ref, kb_ref, vb_ref, z_ref) = refs[n_x:]
    n_rep = QK_W // ROPE_TABLE_W
    for k, rows, is_ctx in _sub_tiles(tm):
        xn = _layer_norm(_stream_tile(x_refs, k, is_ctx))
        hx = (xn * (1.0 + _mod_row(mod_ref, is_ctx, 1)) + _mod_row(mod_ref, is_ctx, 0)).astype(BF16)
        px = _dot(hx, w_ref[...])
        qk = px[:, :QK_W]
        ss = _dot((qk * qk).astype(BF16), bd_ref[...])
        qkn = qk * lax.rsqrt(ss * (1.0 / HEAD_DIM) + RMS_EPS) * gain_ref[...]
        lane = lax.broadcasted_iota(jnp.int32, qkn.shape, 1)
        partner = jnp.where((lane % HEAD_DIM) < HALF,
                            pltpu.roll(qkn, QK_W - HALF, 1), pltpu.roll(qkn, HALF, 1))
        cos = jnp.concatenate([cos_ref[rows, :]] * n_rep, axis=1)
        sin = jnp.concatenate([sin_ref[rows, :]] * n_rep, axis=1)
        qkr = (qkn * cos + partner * sin).astype(BF16)
        for h in range(A_HEADS):
            qa_ref[0, h, rows, :] = qkr[:, h * HEAD_DIM:(h + 1) * HEAD_DIM]
        for g in range(A_KV_HEADS):
            ka_ref[0, g, rows, :] = qkr[:, A_K0 + g * HEAD_DIM:A_K0 + (g + 1) * HEAD_DIM]
            va_ref[0, g, rows, :] = _with_ones(px[:, A_V0 + g * HEAD_DIM:A_V0 + (g + 1) * HEAD_DIM])
        for h in range(B_HEADS):
            qb_ref[0, h, rows, :] = (px[:, B_Q0 + h * HEAD_DIM:B_Q0 + (h + 1) * HEAD_DIM] * Q_SCALE).astype(BF16)
            kb_ref[0, h, rows, :] = px[:, B_K0 + h * HEAD_DIM:B_K0 + (h + 1) * HEAD_DIM].astype(BF16)
            vb_ref[0, h, rows, :] = _with_ones(px[:, B_V0 + h * HEAD_DIM:B_V0 + (h + 1) * HEAD_DIM])
        z_ref[0, rows, :] = px[:, C0:]


def _step_tokens(s):
    return 3 * TOKEN_TILE if s % (3 * TOKEN_TILE) == 0 else 2 * TOKEN_TILE


def _project(x_lat, x_ctx, ctx_block, mod, w_bf, bd, gain, cos_t, sin_t):
    bsz, d, s = x_lat.shape[0], D_MODEL, S_ALL
    tm = _step_tokens(s)
    nt = s // tm
    heads = lambda n: pl.BlockSpec((1, n, tm, HEAD_DIM), lambda b, i: (b, 0, i, 0))
    hshape = lambda n: jax.ShapeDtypeStruct((bsz, n, s, HEAD_DIM), BF16)
    vals = lambda n: pl.BlockSpec((1, n, tm, V_EXT_W), lambda b, i: (b, 0, i, 0))
    vshape = lambda n: jax.ShapeDtypeStruct((bsz, n, s, V_EXT_W), BF16)
    return pl.pallas_call(
        functools.partial(_proj_body, tm=tm),
        grid=(bsz, nt),
        in_specs=_stream_specs(tm, ctx_block) + [
            pl.BlockSpec((1, 2, 6, d), lambda b, i: (b, 0, 0, 0)),
            pl.BlockSpec((d, PROJ_W), lambda b, i: (0, 0)),
            pl.BlockSpec((QK_W, QK_W), lambda b, i: (0, 0)),
            pl.BlockSpec((1, QK_W), lambda b, i: (0, 0)),
            pl.BlockSpec((tm, ROPE_TABLE_W), lambda b, i: (i, 0)),
            pl.BlockSpec((tm, ROPE_TABLE_W), lambda b, i: (i, 0)),
        ],
        out_specs=[heads(A_HEADS), heads(A_KV_HEADS), vals(A_KV_HEADS),
                   heads(B_HEADS), heads(B_HEADS), vals(B_HEADS),
                   pl.BlockSpec((1, tm, 3 * C_CH), lambda b, i: (b, i, 0))],
        out_shape=[hshape(A_HEADS), hshape(A_KV_HEADS), vshape(A_KV_HEADS),
                   hshape(B_HEADS), hshape(B_HEADS), vshape(B_HEADS),
                   jax.ShapeDtypeStruct((bsz, s, 3 * C_CH), F32)],
        compiler_params=_params("parallel", "parallel"),
        name="proj_in",
    )(*_stream_args(x_lat, x_ctx, tm), mod, w_bf, bd, gain, cos_t, sin_t)


def _gqa_body(q_ref, k_ref, v_ref, o_ref):
    n_keys = k_ref.shape[2]
    for h in range(A_REP):
        q = q_ref[0, h]
        m = acc = None
        for j in range(n_keys // GQA_KEY_CHUNK):
            keys = slice(j * GQA_KEY_CHUNK, (j + 1) * GQA_KEY_CHUNK)
            s = _dot_nt(q, k_ref[0, 0, keys, :])
            mj = jnp.max(s, -1, keepdims=True)
            if j == 0:
                m = mj
                acc = _dot(jnp.exp2(s - m).astype(BF16), v_ref[0, 0, keys, :])
            else:
                m_new = jnp.maximum(m, mj)
                alpha = jnp.exp2(m - m_new)
                acc = alpha * acc + _dot(jnp.exp2(s - m_new).astype(BF16), v_ref[0, 0, keys, :])
                m = m_new
        o_ref[0, :, h * HEAD_DIM:(h + 1) * HEAD_DIM] = _normalise(acc).astype(BF16)


def _gqa(qa, ka, va):
    bsz, _, s, _ = qa.shape
    tq = GQA_Q_TILE
    return pl.pallas_call(
        _gqa_body,
        grid=(bsz, A_KV_HEADS, SEQ // tq),
        in_specs=[
            pl.BlockSpec((1, A_REP, tq, HEAD_DIM), lambda b, g, i: (b, g, i, 0)),
            pl.BlockSpec((1, 1, s, HEAD_DIM), lambda b, g, i: (b, g, 0, 0)),
            pl.BlockSpec((1, 1, s, V_EXT_W), lambda b, g, i: (b, g, 0, 0)),
        ],
        out_specs=pl.BlockSpec((1, tq, A_REP * HEAD_DIM), lambda b, g, i: (b, i, g)),
        out_shape=jax.ShapeDtypeStruct((bsz, SEQ, A_W), BF16),
        compiler_params=_params("parallel", "parallel", "parallel"),
        name="gqa_attn",
    )(qa, ka, va)


NA_HEADS_PER_STEP = 4
NA_Q_ROWS = 8
NA_K_ROWS = 16
NA_Q_TOKENS = NA_Q_ROWS * GRID_W
NA_K_TOKENS = NA_K_ROWS * GRID_W
NA_Q_BLOCKS = GRID_ROWS // NA_Q_ROWS
NA_KEY_ROW0 = (0, 4, 12, 16)
NA_PATTERN_BLOCKS = (0, 1, 3)


def _na_body(q_ref, k_ref, v_ref, bias_ref, o_ref):
    j = pl.program_id(2)
    key_row0 = jnp.where(j == 0, NA_KEY_ROW0[0],
                         jnp.where(j == 1, NA_KEY_ROW0[1], jnp.where(j == 2, NA_KEY_ROW0[2], NA_KEY_ROW0[3])))
    k0 = pl.multiple_of(key_row0 * GRID_W, 4 * GRID_W)
    for hh in range(NA_HEADS_PER_STEP):
        q = q_ref[0, hh]
        kw = k_ref[0, hh, pl.ds(k0, NA_K_TOKENS), :]
        vw = v_ref[0, hh, pl.ds(k0, NA_K_TOKENS), :]
        kc = k_ref[0, hh, SEQ:, :]
        vc = v_ref[0, hh, SEQ:, :]
        sw = _dot_nt(q, kw) + bias_ref[hh, 0]
        sc = _dot_nt(q, kc)
        m = jnp.maximum(jnp.max(sw, -1, keepdims=True), jnp.max(sc, -1, keepdims=True))
        o_ext = _dot(jnp.exp2(sw - m).astype(BF16), vw) + _dot(jnp.exp2(sc - m).astype(BF16), vc)
        o_ref[0, :, hh * HEAD_DIM:(hh + 1) * HEAD_DIM] = _normalise(o_ext).astype(BF16)


def _na(qb, kb, vb, bias):
    bsz, _, s, _ = qb.shape
    hp = NA_HEADS_PER_STEP
    kv = lambda w: pl.BlockSpec((1, hp, s, w), lambda b, h, j: (b, h, 0, 0))
    return pl.pallas_call(
        _na_body,
        grid=(bsz, B_HEADS // hp, NA_Q_BLOCKS),
        in_specs=[pl.BlockSpec((1, hp, NA_Q_TOKENS, HEAD_DIM), lambda b, h, j: (b, h, j, 0)),
                  kv(HEAD_DIM), kv(V_EXT_W),
                  pl.BlockSpec((hp, 1, NA_Q_TOKENS, NA_K_TOKENS), lambda b, h, j: (h, (j + 1) // 2, 0, 0))],
        out_specs=pl.BlockSpec((1, NA_Q_TOKENS, hp * HEAD_DIM), lambda b, h, j: (b, j, h)),
        out_shape=jax.ShapeDtypeStruct((bsz, SEQ, B_W), BF16),
        compiler_params=_params("parallel", "parallel", "arbitrary"),
        name="nbr_attn",
    )(qb, kb, vb, bias)


def _na_bias_body(t_ref, o_ref):
    o_ref[...] = jnp.full(o_ref.shape, NEG_BIG, F32)
    for p, blk in enumerate(NA_PATTERN_BLOCKS):
        @pl.when(pl.program_id(1) == p)
        def _(blk=blk):
            for qi in range(NA_Q_ROWS):
                qrow = blk * NA_Q_ROWS + qi
                rs = min(max(qrow - WIN_ROWS // 2, 0), GRID_ROWS - WIN_ROWS)
                for krow in range(rs, rs + WIN_ROWS):
                    ki = krow - NA_KEY_ROW0[blk]
                    o_ref[0, 0, qi * GRID_W:(qi + 1) * GRID_W, ki * GRID_W:(ki + 1) * GRID_W] = (
                        t_ref[0, krow - qrow + WIN_ROWS - 1])


def _na_bias_table(rpb):
    col = np.arange(GRID_W)
    col_start = np.clip(col - WIN_COLS // 2, 0, GRID_W - WIN_COLS)
    in_cols = (col[None, :] >= col_start[:, None]) & (col[None, :] < col_start[:, None] + WIN_COLS)
    dc = np.clip(col[None, :] - col[:, None] + WIN_COLS - 1, 0, 2 * WIN_COLS - 2)
    onehot = jnp.asarray(dc[None] == np.arange(2 * WIN_COLS - 1)[:, None, None], F32)
    tiles = jnp.einsum('hdj,jqk->hdqk', rpb.astype(F32), onehot, precision=HIGHEST)
    tiles = jnp.where(in_cols[None, None], tiles * LOG2E, NEG_BIG)
    n_pat = len(NA_PATTERN_BLOCKS)
    return pl.pallas_call(
        _na_bias_body,
        grid=(B_HEADS, n_pat),
        in_specs=[pl.BlockSpec((1, 2 * WIN_ROWS - 1, GRID_W, GRID_W), lambda h, p: (h, 0, 0, 0))],
        out_specs=pl.BlockSpec((1, 1, NA_Q_TOKENS, NA_K_TOKENS), lambda h, p: (h, p, 0, 0)),
        out_shape=jax.ShapeDtypeStruct((B_HEADS, n_pat, NA_Q_TOKENS, NA_K_TOKENS), F32),
        compiler_params=_params("parallel", "arbitrary"),
        name="nbr_bias",
    )(tiles)


def _ctx_attn_body(q_ref, k_ref, v_ref, o_ref, *, rep):
    for g in range(k_ref.shape[1]):
        for r in range(rep):
            hh = g * rep + r
            s = _dot_nt(q_ref[0, hh], k_ref[0, g])
            m = jnp.max(s, -1, keepdims=True)
            o_ext = _dot(jnp.exp2(s - m).astype(BF16), v_ref[0, g])
            o_ref[0, :, hh * HEAD_DIM:(hh + 1) * HEAD_DIM] = _normalise(o_ext).astype(BF16)


def _ctx_attn(q, k, v, kv_per_step):
    bsz, n_q, _, _ = q.shape
    n_kv = k.shape[1]
    rep = n_q // n_kv
    seg = lambda n, w=HEAD_DIM: pl.BlockSpec((1, n, CTX_LEN, w), lambda b, h: (b, h, SEQ // CTX_LEN, 0))
    return pl.pallas_call(
        functools.partial(_ctx_attn_body, rep=rep),
        grid=(bsz, n_kv // kv_per_step),
        in_specs=[seg(kv_per_step * rep), seg(kv_per_step), seg(kv_per_step, V_EXT_W)],
        out_specs=pl.BlockSpec((1, CTX_LEN, kv_per_step * rep * HEAD_DIM), lambda b, h: (b, 0, h)),
        out_shape=jax.ShapeDtypeStruct((bsz, CTX_LEN, n_q * HEAD_DIM), BF16),
        compiler_params=_params("parallel", "parallel"),
        name="ctx_attn",
    )(q, k, v)


DFT_FINE = 16


def _dft_table_body(chr_ref, shr_ref, clr_ref, slr_ref, chc_ref, shc_ref, clc_ref, slc_ref,
                    fwd_ref, inv_ref, *, n, tf):
    j = pl.program_id(0)

    def expand(c_hi, s_hi, c_lo, s_lo, rows):
        cos = (c_hi[:, None, :] * c_lo[None, :, :] - s_hi[:, None, :] * s_lo[None, :, :])
        sin = (s_hi[:, None, :] * c_lo[None, :, :] + c_hi[:, None, :] * s_lo[None, :, :])
        return cos.reshape(rows, -1), sin.reshape(rows, -1)

    cos_r, sin_r = expand(chr_ref[...], shr_ref[...], clr_ref[...], slr_ref[...], tf)
    cos_c, sin_c = expand(chc_ref[...], shc_ref[...], clc_ref[...], slc_ref[...], n)
    fr = lax.broadcasted_iota(jnp.int32, (tf, n), 0) + j * tf
    tr = lax.broadcasted_iota(jnp.int32, (tf, n), 1)
    sin_r = jnp.where(fr == 0, jnp.where(tr % 2 == 0, 1.0, -1.0), sin_r)
    tc = lax.broadcasted_iota(jnp.int32, (n, tf), 0)
    fcol = lax.broadcasted_iota(jnp.int32, (n, tf), 1) + j * tf
    sin_c = jnp.where(fcol == 0, jnp.where(tc % 2 == 0, 1.0, -1.0), sin_c)
    fwd_ref[0, :tf, :] = cos_r.astype(BF16)
    fwd_ref[0, tf:, :] = sin_r.astype(BF16)
    inv_ref[0, :, :tf] = cos_c.astype(BF16)
    inv_ref[0, :, tf:] = sin_c.astype(BF16)


def _dft_tables(n):
    s = jnp.arange(n, dtype=jnp.int32)[None, :]

    def cos_sin(f):
        ang = ((f * s) % (2 * n)).astype(F32) * (math.pi / n)
        return jnp.cos(ang), jnp.sin(ang)

    ch, sh = cos_sin(jnp.arange(n // DFT_FINE, dtype=jnp.int32)[:, None] * DFT_FINE)
    cl, sl = cos_sin(jnp.arange(DFT_FINE, dtype=jnp.int32)[:, None])
    tf = min(FREQ_TILE, n)
    nf = n // tf
    hi_rows = pl.BlockSpec((tf // DFT_FINE, n), lambda j: (j, 0))
    lo_rows = pl.BlockSpec((DFT_FINE, n), lambda j: (0, 0))
    hi_cols = pl.BlockSpec((n // DFT_FINE, tf), lambda j: (0, j))
    lo_cols = pl.BlockSpec((DFT_FINE, tf), lambda j: (0, j))
    return pl.pallas_call(
        functools.partial(_dft_table_body, n=n, tf=tf),
        grid=(nf,),
        in_specs=[hi_rows, hi_rows, lo_rows, lo_rows, hi_cols, hi_cols, lo_cols, lo_cols],
        out_specs=[pl.BlockSpec((1, 2 * tf, n), lambda j: (j, 0, 0)),
                   pl.BlockSpec((1, n, 2 * tf), lambda j: (j, 0, 0))],
        out_shape=[jax.ShapeDtypeStruct((nf, 2 * tf, n), BF16),
                   jax.ShapeDtypeStruct((nf, n, 2 * tf), BF16)],
        compiler_params=_params("parallel"),
        name="dft_tables",
    )(ch, sh, cl, sl, ch, sh, cl, sl)


def _split_bf16(v):
    hi = v.astype(BF16)
    lo = (v - hi.astype(F32)).astype(BF16)
    return hi, lo


def _spectrum_body(fwd_ref, kp_ref, km_ref, kc_ref, ks_ref, *, tf):
    kp_hi, kp_lo = _split_bf16(kp_ref[...])
    km_hi, km_lo = _split_bf16(km_ref[...])
    fc = fwd_ref[0, :tf, :]
    fs = fwd_ref[0, tf:, :]
    kc_ref[...] = _dot(fc, kp_hi) + _dot(fc, kp_lo)
    ks_sin = _dot(fs, km_hi) + _dot(fs, km_lo)
    ks_nyq = _dot(fs, kp_hi) + _dot(fs, kp_lo)
    row = lax.broadcasted_iota(jnp.int32, ks_sin.shape, 0) + pl.program_id(0) * tf
    ks_ref[...] = jnp.where(row == 0, ks_nyq, ks_sin)


def _filter_spectrum(fwd, kplus, kminus):
    nf, tf2, n = fwd.shape
    tf = tf2 // 2
    filt = pl.BlockSpec((n, C_CH), lambda j: (0, 0))
    out = pl.BlockSpec((tf, C_CH), lambda j: (j, 0))
    return pl.pallas_call(
        functools.partial(_spectrum_body, tf=tf),
        grid=(nf,),
        in_specs=[pl.BlockSpec((1, tf2, n), lambda j: (j, 0, 0)), filt, filt],
        out_specs=[out, out],
        out_shape=[jax.ShapeDtypeStruct((n, C_CH), F32)] * 2,
        compiler_params=_params("parallel"),
        name="hyena_spectrum",
    )(fwd, kplus, kminus)


def _short_conv(z, cw_ref, cb_ref, cols, n):
    pos = lax.broadcasted_iota(jnp.int32, z.shape, 0)
    prev = jnp.where(pos == 0, 0.0, pltpu.roll(z, 1, 0))
    nxt = jnp.where(pos == n - 1, 0.0, pltpu.roll(z, n - 1, 0))
    return prev * cw_ref[0:1, cols] + z * cw_ref[1:2, cols] + nxt * cw_ref[2:3, cols] + cb_ref[:, cols]


def _hyena_gate_body(z_ref, cw_ref, cb_ref, skip_ref, u_ref, us_ref, *, n):
    zc = _short_conv(z_ref[0, :, C_CH:], cw_ref, cb_ref, slice(C_CH, 3 * C_CH), n)
    u = zc[:, C_CH:] * zc[:, :C_CH]
    u_ref[0] = u.astype(BF16)
    us_ref[0] = u * skip_ref[...]


def _hyena_gate(z, seg_start, n, cw, cb, skip):
    bsz = z.shape[0]
    seg = seg_start // n
    small = lambda r, c: pl.BlockSpec((r, c), lambda b: (0, 0))
    out = pl.BlockSpec((1, n, C_CH), lambda b: (b, 0, 0))
    return pl.pallas_call(
        functools.partial(_hyena_gate_body, n=n),
        grid=(bsz,),
        in_specs=[pl.BlockSpec((1, n, 3 * C_CH), lambda b: (b, seg, 0)),
                  small(3, 3 * C_CH), small(1, 3 * C_CH), small(1, C_CH)],
        out_specs=[out, out],
        out_shape=[jax.ShapeDtypeStruct((bsz, n, C_CH), BF16),
                   jax.ShapeDtypeStruct((bsz, n, C_CH), F32)],
        compiler_params=_params("parallel"),
        name="hyena_gate",
    )(z, cw, cb, skip)


HYENA_BATCH = 4


def _hyena_body(u_ref, us_ref, z_ref, cw_ref, cb_ref, fwd_ref, inv_ref, kc_ref, ks_ref, o_ref, y_ref, *, n, tf):
    j = pl.program_id(1)

    @pl.when(j == 0)
    def _():
        y_ref[...] = jnp.zeros_like(y_ref)

    kc = kc_ref[...]
    ks = ks_ref[...]
    row = lax.broadcasted_iota(jnp.int32, kc.shape, 0) + j * tf
    first = row == 0
    wgt = jnp.where(first, 1.0, 2.0) * (1.0 / (2 * n))
    for b in range(HYENA_BATCH):
        spec = _dot(fwd_ref[0], u_ref[b])
        uc, us = spec[:tf], spec[tf:]
        re = jnp.where(first, uc * kc, uc * kc - us * ks) * wgt
        im = jnp.where(first, us * ks, uc * ks + us * kc) * wgt
        y_ref[b] += _dot(inv_ref[0], jnp.concatenate([re, im], axis=0).astype(BF16))

    @pl.when(j == pl.num_programs(1) - 1)
    def _():
        for b in range(HYENA_BATCH):
            x0 = _short_conv(z_ref[b], cw_ref, cb_ref, slice(0, C_CH), n)
            o_ref[b] = ((y_ref[b] + us_ref[b]) * x0).astype(BF16)


def _hyena(u, us, z, seg_start, n, cw, cb, fwd, inv, kc, ks):
    bsz = u.shape[0]
    assert bsz % HYENA_BATCH == 0, bsz
    tf = min(FREQ_TILE, n)
    nf = n // tf
    seg = seg_start // n
    seq = pl.BlockSpec((HYENA_BATCH, n, C_CH), lambda g, j: (g, 0, 0))
    once = pl.Buffered(1)
    small = lambda r, c: pl.BlockSpec((r, c), lambda g, j: (0, 0))
    spec_blk = pl.BlockSpec((tf, C_CH), lambda g, j: (j, 0))
    return pl.pallas_call(
        functools.partial(_hyena_body, n=n, tf=tf),
        grid=(bsz // HYENA_BATCH, nf),
        in_specs=[seq,
                  pl.BlockSpec((HYENA_BATCH, n, C_CH), lambda g, j: (g, 0, 0), pipeline_mode=once),
                  pl.BlockSpec((HYENA_BATCH, n, C_CH), lambda g, j: (g, seg, 0), pipeline_mode=once),
                  small(3, 3 * C_CH), small(1, 3 * C_CH),
                  pl.BlockSpec((1, 2 * tf, n), lambda g, j: (j, 0, 0)),
                  pl.BlockSpec((1, n, 2 * tf), lambda g, j: (j, 0, 0)),
                  spec_blk, spec_blk],
        out_specs=seq,
        out_shape=jax.ShapeDtypeStruct((bsz, n, C_CH), BF16),
        scratch_shapes=[pltpu.VMEM((HYENA_BATCH, n, C_CH), F32)],
        compiler_params=_params("parallel", "arbitrary"),
        name="hyena_conv",
    )(u, us, z, cw, cb, fwd, inv, kc, ks)


def _dot3(a, b):
    a_hi, a_lo = _split_bf16(a)
    b_hi, b_lo = _split_bf16(b)
    return _dot(a_hi, b_hi) + _dot(a_hi, b_lo) + _dot(a_lo, b_hi)


POS_PAD = 128
FILTER_ROWS = 256


def _filter_body(pos_ref, decay_ref, w1_ref, b1_ref, w2_ref, b2_ref, w3_ref, freq_ref, kp_ref, km_ref):
    h = jnp.sin(freq_ref[0:1, :] * (_dot3(pos_ref[...], w1_ref[...]) + b1_ref[...]))
    h = jnp.sin(freq_ref[1:2, :] * (_dot3(h, w2_ref[...]) + b2_ref[...]))
    h = _dot3(h, w3_ref[...])
    decay = decay_ref[...]
    hf = h[:, :C_CH] * decay
    hb = h[:, C_CH:] * decay
    lag = lax.broadcasted_iota(jnp.int32, hb.shape, 0) + pl.program_id(0) * FILTER_ROWS
    hb = jnp.where(lag == 0, 0.0, hb)
    kp_ref[...] = hf + hb
    km_ref[...] = hf - hb


def _implicit_filters(n, w1, b1, w2, b2, w3, freq):
    t = jnp.linspace(0.0, 1.0, n, dtype=F32)[:, None]
    w = 2 * math.pi * jnp.arange(n, dtype=F32)[:, None] / n
    f = jnp.linspace(1e-4, C_POS_BANDS - 1, C_POS_BANDS, dtype=F32)[None]
    pos = jnp.concatenate([t, jnp.cos(f * w), -jnp.sin(f * w)], -1)
    n_pos = pos.shape[1]
    pos = jnp.pad(pos, ((0, 0), (0, POS_PAD - n_pos)))
    w1p = jnp.pad(w1, ((0, POS_PAD - n_pos), (0, 0)))
    lo = math.log(C_DECAY_TARGET) / C_SLOW_DECAY
    hi = math.log(C_DECAY_TARGET) / C_FAST_DECAY
    deltas = jnp.abs(jnp.linspace(lo, hi, C_CH, dtype=F32))
    decay = jnp.exp(-t * deltas[None])
    hid = w2.shape[0]
    rows = lambda c: pl.BlockSpec((FILTER_ROWS, c), lambda i: (i, 0))
    full = lambda r, c: pl.BlockSpec((r, c), lambda i: (0, 0))
    return pl.pallas_call(
        _filter_body,
        grid=(n // FILTER_ROWS,),
        in_specs=[rows(POS_PAD), rows(C_CH), full(POS_PAD, hid), full(1, hid), full(hid, hid), full(1, hid),
                  full(hid, 2 * C_CH), full(2, hid)],
        out_specs=[rows(C_CH), rows(C_CH)],
        out_shape=[jax.ShapeDtypeStruct((n, C_CH), F32)] * 2,
        compiler_params=_params("parallel"),
        name="hyena_filter",
    )(pos, decay, w1p, b1[None], w2, b2[None], w3, freq)


def _router_gates(logits_t, rb):
    tm = logits_t.shape[1]
    scores = _sigmoid(logits_t)
    sel = scores + rb
    sel3 = sel.reshape(N_GROUPS, GROUP_SIZE, tm)
    kk = lax.broadcasted_iota(jnp.int32, sel3.shape, 1)
    m1 = jnp.max(sel3, 1, keepdims=True)
    i1 = jnp.min(jnp.where(sel3 == m1, kk, GROUP_SIZE), 1, keepdims=True)
    m2 = jnp.max(jnp.where(kk == i1, -jnp.inf, sel3), 1, keepdims=True)
    gscore = (m1 + m2).reshape(N_GROUPS, tm)
    gid = lax.broadcasted_iota(jnp.int32, gscore.shape, 0)
    rank = jnp.zeros(gscore.shape, jnp.int32)
    for g in range(N_GROUPS):
        other = gscore[g:g + 1, :]
        ahead = (other > gscore) | ((other == gscore) & (g < gid))
        rank = rank + ahead.astype(jnp.int32)
    gsel = (rank < TOPK_GROUPS).reshape(N_GROUPS, 1, tm)
    cand = jnp.where(gsel, sel3, -jnp.inf).reshape(N_EXPERTS, tm)
    eid = lax.broadcasted_iota(jnp.int32, cand.shape, 0)
    chosen = jnp.zeros(cand.shape, jnp.bool_)
    for _ in range(TOP_K):
        m = jnp.max(cand, 0, keepdims=True)
        pick = eid == jnp.min(jnp.where(cand == m, eid, N_EXPERTS), 0, keepdims=True)
        chosen = chosen | pick
        cand = jnp.where(pick, -jnp.inf, cand)
    wsel = jnp.where(chosen, scores, 0.0)
    return wsel / jnp.sum(wsel, 0, keepdims=True) * ROUTED_SCALE


def _out_body(*refs, tm):
    n_x = tm // TOKEN_TILE + 1
    x_refs, oa_refs, ob_refs, oc_refs = (refs[j * n_x:(j + 1) * n_x] for j in range(4))
    (mod_ref, wa_ref, wb_ref, wc_ref, lng_ref, lnb_ref,
     rwt_ref, rb_ref, xmid_ref, h_ref, gate_ref) = refs[4 * n_x:]
    for k, rows, is_ctx in _sub_tiles(tm):
        attn = (_dot(_stream_tile(oa_refs, k, is_ctx), wa_ref[...])
                + _dot(_stream_tile(ob_refs, k, is_ctx), wb_ref[...])
                + _dot(_stream_tile(oc_refs, k, is_ctx), wc_ref[...]))
        y = DN_ALPHA * _stream_tile(x_refs, k, is_ctx) + _mod_row(mod_ref, is_ctx, 2) * attn
        xm = _layer_norm(y) * lng_ref[...] + lnb_ref[...]
        xmid_ref[0, rows, :] = xm
        h = _layer_norm(xm) * (1.0 + _mod_row(mod_ref, is_ctx, 4)) + _mod_row(mod_ref, is_ctx, 3)
        hi, lo = _split_bf16(h)
        h_ref[0, rows, :] = hi
        logits_t = _dot_nt(rwt_ref[...], jnp.concatenate([hi, lo, hi], axis=1))
        gates_t = _router_gates(logits_t, rb_ref[...])
        pad = jnp.zeros((128 - N_EXPERTS, gates_t.shape[1]), F32)
        gates = jnp.concatenate([gates_t, pad], axis=0).T
        for g in range(N_GROUPS):
            gate_ref[0, g, rows, :] = gates[:, g * GROUP_SIZE:(g + 1) * GROUP_SIZE]


def _out_proj(s, x_lat, x_ctx, ctx_block, oa, ob, oc, mod, wa, wb, wc, lng, lnb, rwt3, rb):
    bsz = x_lat.shape[0]
    d = D_MODEL
    tm = _step_tokens(s)
    tok = lambda w: pl.BlockSpec((1, tm, w), lambda b, i: (b, i, 0))
    full = lambda r, c: pl.BlockSpec((r, c), lambda b, i: (0, 0))
    return pl.pallas_call(
        functools.partial(_out_body, tm=tm),
        grid=(bsz, s // tm),
        in_specs=_stream_specs(tm, ctx_block) + _stream_specs(tm, 0, A_W) + _stream_specs(tm, 0, B_W)
        + _stream_specs(tm, 0, C_CH) + [
                  pl.BlockSpec((1, 2, 6, d), lambda b, i: (b, 0, 0, 0)),
                  full(A_W, d), full(B_W, d), full(C_CH, d), full(1, d), full(1, d),
                  full(N_EXPERTS, 3 * d), full(N_EXPERTS, 1)],
        out_specs=[tok(d), tok(d),
                   pl.BlockSpec((1, N_GROUPS, tm, GROUP_SIZE), lambda b, i: (b, 0, i, 0))],
        out_shape=[jax.ShapeDtypeStruct((bsz, s, d), F32),
                   jax.ShapeDtypeStruct((bsz, s, d), BF16),
                   jax.ShapeDtypeStruct((bsz, N_GROUPS, s, GROUP_SIZE), F32)],
        compiler_params=_params("parallel", "parallel"),
        name="proj_out_router",
    )(*_stream_args(x_lat, x_ctx, tm), *_stream_args(*oa, tm), *_stream_args(*ob, tm), *_stream_args(*oc, tm),
      mod, wa, wb, wc, lng, lnb, rwt3, rb)


MOE_CHAINS = 2


def _moe_body(h_ref, gate_ref, xmid_ref, mod_ref, w1_ref, w3_ref, w2_ref,
              sw1_ref, sw3_ref, sw2_ref, lng_ref, lnb_ref, o_ref, acc_ref, *, tm):
    g = pl.program_id(2)
    rc = tm // MOE_CHAINS

    @pl.when(g == 0)
    def _():
        h = h_ref[0]
        a = _dot(h, sw1_ref[...])
        act = (a * _sigmoid(a) * _dot(h, sw3_ref[...])).astype(BF16)
        acc_ref[...] = _dot(act, sw2_ref[...])

    w1 = jnp.concatenate([w1_ref[k] for k in range(GROUP_SIZE)], axis=1)
    w3 = jnp.concatenate([w3_ref[k] for k in range(GROUP_SIZE)], axis=1)
    w2 = w2_ref[...].reshape(GROUP_SIZE * EXPERT_HIDDEN, D_MODEL)
    for c in range(MOE_CHAINS):
        rows = slice(c * rc, (c + 1) * rc)
        h = h_ref[0, rows, :]
        a = _dot(h, w1)
        act = a * _sigmoid(a) * _dot(h, w3)
        gt = gate_ref[0, 0, rows, :]
        parts = [act[:, k * EXPERT_HIDDEN:(k + 1) * EXPERT_HIDDEN] * gt[:, k:k + 1]
                 for k in range(GROUP_SIZE)]
        acc_ref[rows, :] += _dot(jnp.concatenate(parts, axis=1).astype(BF16), w2)

    @pl.when(g == N_GROUPS - 1)
    def _():
        row = lax.broadcasted_iota(jnp.int32, (tm, 1), 0) + pl.program_id(1) * tm
        gate2 = jnp.where(row >= SEQ, mod_ref[0, 1, 5:6, :], mod_ref[0, 0, 5:6, :])
        y = DN_ALPHA * xmid_ref[0] + gate2 * acc_ref[...]
        o_ref[0] = _layer_norm(y) * lng_ref[...] + lnb_ref[...]


MOE_TILE_2BUF = 768
MOE_TILE_1BUF = 1024


def _moe(h, gates, xmid, mod, layer, w1_all, w3_all, w2_all, sw1, sw3, sw2, lng, lnb):
    bsz, s, d = h.shape
    tm = MOE_TILE_1BUF if s % MOE_TILE_1BUF == 0 else MOE_TILE_2BUF
    tok = pl.BlockSpec((1, tm, d), lambda b, i, g: (b, i, 0))
    tok_once = pl.BlockSpec((1, tm, d), lambda b, i, g: (b, i, 0),
                            pipeline_mode=pl.Buffered(1 if tm > MOE_TILE_2BUF else 2))
    full = lambda r, c: pl.BlockSpec((r, c), lambda b, i, g: (0, 0))
    up = pl.BlockSpec((None, GROUP_SIZE, d, EXPERT_HIDDEN), lambda b, i, g: (layer, g, 0, 0))
    return pl.pallas_call(
        functools.partial(_moe_body, tm=tm),
        grid=(bsz, s // tm, N_GROUPS),
        in_specs=[tok,
                  pl.BlockSpec((1, 1, tm, GROUP_SIZE), lambda b, i, g: (b, g, i, 0)),
                  tok_once,
                  pl.BlockSpec((1, 2, 6, d), lambda b, i, g: (b, 0, 0, 0)),
                  up, up,
                  pl.BlockSpec((None, GROUP_SIZE, EXPERT_HIDDEN, d), lambda b, i, g: (layer, g, 0, 0)),
                  full(d, SHARED_HIDDEN), full(d, SHARED_HIDDEN), full(SHARED_HIDDEN, d),
                  full(1, d), full(1, d)],
        out_specs=tok,
        out_shape=jax.ShapeDtypeStruct((bsz, s, d), F32),
        scratch_shapes=[pltpu.VMEM((tm, d), F32)],
        compiler_params=_params("parallel", "parallel", "arbitrary"),
        name="moe_ffn",
    )(h, gates, xmid, mod, w1_all, w3_all, w2_all, sw1, sw3, sw2, lng, lnb)


def _rope_tables():
    t = jnp.arange(SEQ)
    row = (t // GRID_W).astype(F32)
    col = (t % GRID_W).astype(F32)
    inv = ROPE_THETA ** (-jnp.arange(0, HALF, 2, dtype=F32) / HALF)
    ang = jnp.concatenate([row[:, None] * inv, col[:, None] * inv], -1)
    cos, sin = jnp.cos(ang), jnp.sin(ang)
    n_heads = ROPE_TABLE_W // HEAD_DIM
    cos_t = jnp.concatenate([jnp.tile(jnp.concatenate([cos, cos], -1), (1, n_heads)),
                             jnp.ones((CTX_LEN, ROPE_TABLE_W), F32)], 0)
    sin_t = jnp.concatenate([jnp.tile(jnp.concatenate([-sin, sin], -1), (1, n_heads)),
                             jnp.zeros((CTX_LEN, ROPE_TABLE_W), F32)], 0)
    return cos_t, sin_t


MOD_ROWS = 16
MOD_COL_TILE = 1536


def _modulation_body(c_ref, w_ref, b_ref, o_ref):
    cv = c_ref[...]
    o_ref[0] = _dot3(cv * _sigmoid(cv), w_ref[0]) + b_ref[0]


def _modulation(c, c_ctx, ada_w, ada_b):
    bsz = c.shape[0]
    assert bsz + 1 <= MOD_ROWS, bsz
    n_out = 6 * D_MODEL
    cvec = jnp.concatenate([c, c_ctx[None], jnp.zeros((MOD_ROWS - bsz - 1, D_MODEL), F32)], axis=0)
    m = pl.pallas_call(
        _modulation_body,
        grid=(DEPTH, n_out // MOD_COL_TILE),
        in_specs=[pl.BlockSpec((MOD_ROWS, D_MODEL), lambda l, j: (0, 0)),
                  pl.BlockSpec((1, D_MODEL, MOD_COL_TILE), lambda l, j: (l, 0, j)),
                  pl.BlockSpec((1, 1, MOD_COL_TILE), lambda l, j: (l, 0, j))],
        out_specs=pl.BlockSpec((1, MOD_ROWS, MOD_COL_TILE), lambda l, j: (l, 0, j)),
        out_shape=jax.ShapeDtypeStruct((DEPTH, MOD_ROWS, n_out), F32),
        compiler_params=_params("parallel", "parallel"),
        name="ada_modulation",
    )(cvec, ada_w, ada_b[:, None, :])
    m = m.reshape(DEPTH, MOD_ROWS, 6, D_MODEL)
    lat = m[:, :bsz]
    return jnp.stack([lat, jnp.broadcast_to(m[:, bsz:bsz + 1], lat.shape)], axis=2)


def kernel(x, c, ctx, c_ctx, ada_w, ada_b, w_in, q_gain, k_gain, nat_rpb, hy_conv_w, hy_conv_b, hy_f_w1, hy_f_b1, hy_f_w2, hy_f_b2, hy_f_w3, hy_freq, hy_skip, w_out, ln1_g, ln1_b, router_w, router_b, exp_w1, exp_w3, exp_w2, sh_w1, sh_w3, sh_w2, ln2_g, ln2_b):
    bsz = x.shape[0]
    d = D_MODEL
    x_lat, x_ctx, ctx_block = x, ctx, 0

    head_perm = np.concatenate([np.arange(0, HEAD_DIM, 2), np.arange(1, HEAD_DIM, 2)])
    qk_perm = (np.arange(QK_W // HEAD_DIM)[:, None] * HEAD_DIM + head_perm[None, :]).reshape(-1)
    perm_mat = jnp.asarray(np.eye(QK_W)[:, qk_perm], BF16)
    cos_t, sin_t = _rope_tables()
    bd = jnp.asarray(np.kron(np.eye(QK_W // HEAD_DIM), np.ones((HEAD_DIM, HEAD_DIM))), BF16)
    tables_lat = _dft_tables(SEQ)
    tables_ctx = _dft_tables(CTX_LEN)
    mod_all = _modulation(c, c_ctx, ada_w, ada_b)
    w1_bf, w3_bf, w2_bf = exp_w1.astype(BF16), exp_w3.astype(BF16), exp_w2.astype(BF16)

    for l in range(DEPTH):
        last = l == DEPTH - 1
        mod = mod_all[l]

        w_bf = w_in[l].astype(BF16)
        w_qk = jnp.dot(w_bf[:, :QK_W], perm_mat, preferred_element_type=F32).astype(BF16)
        w_bf = jnp.concatenate([w_qk, w_bf[:, QK_W:]], axis=1)
        gain = jnp.concatenate([jnp.tile(q_gain[l][head_perm] * Q_SCALE, A_HEADS),
                                jnp.tile(k_gain[l][head_perm], A_KV_HEADS)])[None].astype(F32)
        qa, ka, va, qb, kb, vb, z = _project(x_lat, x_ctx, ctx_block, mod, w_bf, bd, gain, cos_t, sin_t)

        oa = [_gqa(qa, ka, va)]
        ob = [_na(qb, kb, vb, _na_bias_table(nat_rpb[l]))]
        if not last:
            oa.append(_ctx_attn(qa, ka, va, 1))
            ob.append(_ctx_attn(qb, kb, vb, NA_HEADS_PER_STEP))

        filt = (hy_f_w1[l], hy_f_b1[l], hy_f_w2[l], hy_f_b2[l], hy_f_w3[l], hy_freq[l])
        cw, cb, skip = hy_conv_w[l], hy_conv_b[l][None], hy_skip[l][None]
        segments = [(0, SEQ, tables_lat)] + ([] if last else [(SEQ, CTX_LEN, tables_ctx)])
        oc = []
        for seg_start, n, (fwd, inv) in segments:
            kc, ks = _filter_spectrum(fwd, *_implicit_filters(n, *filt))
            u, us = _hyena_gate(z, seg_start, n, cw, cb, skip)
            oc.append(_hyena(u, us, z, seg_start, n, cw, cb, fwd, inv, kc, ks))
        oa, ob, oc = ((o[0], o[-1]) for o in (oa, ob, oc))

        wo = w_out[l].astype(BF16)
        rw_hi, rw_lo = _split_bf16(router_w[l].T)
        rwt3 = jnp.concatenate([rw_hi, rw_hi, rw_lo], axis=1)
        xmid, h, gates = _out_proj(SEQ if last else S_ALL, x_lat, x_ctx, ctx_block, oa, ob, oc, mod,
                                   wo[:A_W], wo[A_W:A_W + B_W], wo[A_W + B_W:],
                                   ln1_g[l][None], ln1_b[l][None], rwt3, router_b[l][:, None])

        xs = _moe(h, gates, xmid, mod, l, w1_bf, w3_bf, w2_bf,
                  sh_w1[l].astype(BF16), sh_w3[l].astype(BF16), sh_w2[l].astype(BF16),
                  ln2_g[l][None], ln2_b[l][None])
        x_lat, x_ctx, ctx_block = xs, xs, SEQ // TOKEN_TILE
    return xs
```

```python
import functools
import math

import numpy as np
import jax
import jax.numpy as jnp
from jax import lax
from jax.experimental import pallas as pl
from jax.experimental.pallas import tpu as pltpu

F32 = jnp.float32
BF16 = jnp.bfloat16
HIGHEST = lax.Precision.HIGHEST

D_MODEL = 1024
SEQ = 2048
DEPTH = 2
GRID_W = 64
GRID_ROWS = SEQ // GRID_W
CTX_LEN = 256
S_ALL = SEQ + CTX_LEN
HEAD_DIM = 64
HALF = HEAD_DIM // 2
A_HEADS = 8
A_KV_HEADS = 2
A_REP = A_HEADS // A_KV_HEADS
ROPE_THETA = 10000.0
B_HEADS = 4
WIN_ROWS = 8
WIN_COLS = 16
C_CH = 256
C_POS_BANDS = 16
C_DECAY_TARGET = 1e-2
C_FAST_DECAY = 0.3
C_SLOW_DECAY = 1.5
A_W = A_HEADS * HEAD_DIM
A_KV_W = A_KV_HEADS * HEAD_DIM
B_W = B_HEADS * HEAD_DIM
A_K0 = A_W
A_V0 = A_K0 + A_KV_W
B_Q0 = A_V0 + A_KV_W
B_K0 = B_Q0 + B_W
B_V0 = B_K0 + B_W
C0 = B_V0 + B_W
PROJ_W = C0 + 3 * C_CH
QK_W = A_W + A_KV_W
N_EXPERTS = 64
TOP_K = 8
N_GROUPS = 8
GROUP_SIZE = N_EXPERTS // N_GROUPS
TOPK_GROUPS = 4
EXPERT_HIDDEN = 128
SHARED_HIDDEN = 256
ROUTED_SCALE = 2.5
DN_ALPHA = (2 * DEPTH) ** 0.25
LN_EPS = 1e-6
RMS_EPS = 1e-6
LOG2E = math.log2(math.e)
Q_SCALE = HEAD_DIM ** -0.5 * LOG2E
NEG_BIG = -1e30

VMEM_LIMIT_BYTES = 56 * 1024 * 1024
TOKEN_TILE = 256
ROPE_TABLE_W = 128
GQA_Q_TILE = 512
GQA_KEY_CHUNK = 768
FREQ_TILE = 256


def _params(*sem):
    return pltpu.CompilerParams(dimension_semantics=sem, vmem_limit_bytes=VMEM_LIMIT_BYTES)


def _layer_norm(v):
    mu = jnp.mean(v, -1, keepdims=True)
    vc = v - mu
    var = jnp.mean(vc * vc, -1, keepdims=True)
    return vc * lax.rsqrt(var + LN_EPS)


def _sigmoid(v):
    return 1.0 / (1.0 + jnp.exp(-v))


def _dot(a, b):
    return jnp.dot(a, b, preferred_element_type=F32)


def _dot_nt(a, b):
    return lax.dot_general(a, b, (((1,), (1,)), ((), ())), preferred_element_type=F32)


V_EXT_W = 2 * HEAD_DIM


def _with_ones(v):
    return jnp.concatenate([v.astype(BF16), jnp.ones(v.shape, BF16)], axis=1)


def _normalise(o_ext):
    return o_ext[:, :HEAD_DIM] / o_ext[:, HEAD_DIM:]


def _mod_row(mod_ref, is_ctx, idx):
    return jnp.where(is_ctx, mod_ref[0, 1, idx:idx + 1, :], mod_ref[0, 0, idx:idx + 1, :])


def _sub_tiles(tm):
    for k in range(tm // TOKEN_TILE):
        rows = slice(k * TOKEN_TILE, (k + 1) * TOKEN_TILE)
        is_ctx = (pl.program_id(1) * tm + k * TOKEN_TILE) >= SEQ
        yield k, rows, is_ctx


def _stream_specs(tm, ctx_block, width=D_MODEL):
    n_sub = tm // TOKEN_TILE
    last_lat = SEQ // TOKEN_TILE - 1
    lat = [pl.BlockSpec((1, TOKEN_TILE, width),
                        lambda b, i, k=k: (b, jnp.minimum(i * n_sub + k, last_lat), 0)) for k in range(n_sub)]
    return lat + [pl.BlockSpec((1, TOKEN_TILE, width), lambda b, i: (b, ctx_block, 0))]


def _stream_args(lat, ctx, tm):
    return [lat] * (tm // TOKEN_TILE) + [ctx]


def _stream_tile(x_refs, k, is_ctx):
    return jnp.where(is_ctx, x_refs[-1][0], x_refs[k][0])


def _proj_body(*refs, tm):
    n_x = tm // TOKEN_TILE + 1
    x_refs = refs[:n_x]
    (mod_ref, w_ref, bd_ref, gain_ref, cos_ref, sin_ref,
     qa_ref, ka_ref, va_ref, qb_ref, kb_ref, vb_ref, z_ref) = refs[n_x:]
    n_rep = QK_W // ROPE_TABLE_W
    for k, rows, is_ctx in _sub_tiles(tm):
        xn = _layer_norm(_stream_tile(x_refs, k, is_ctx))
        hx = (xn * (1.0 + _mod_row(mod_ref, is_ctx, 1)) + _mod_row(mod_ref, is_ctx, 0)).astype(BF16)
        px = _dot(hx, w_ref[...])
        qk = px[:, :QK_W]
        ss = _dot((qk * qk).astype(BF16), bd_ref[...])
        qkn = qk * lax.rsqrt(ss * (1.0 / HEAD_DIM) + RMS_EPS) * gain_ref[...]
        lane = lax.broadcasted_iota(jnp.int32, qkn.shape, 1)
        partner = jnp.where((lane % HEAD_DIM) < HALF,
                            pltpu.roll(qkn, QK_W - HALF, 1), pltpu.roll(qkn, HALF, 1))
        cos = jnp.concatenate([cos_ref[rows, :]] * n_rep, axis=1)
        sin = jnp.concatenate([sin_ref[rows, :]] * n_rep, axis=1)
        qkr = (qkn * cos + partner * sin).astype(BF16)
        for h in range(A_HEADS):
            qa_ref[0, h, rows, :] = qkr[:, h * HEAD_DIM:(h + 1) * HEAD_DIM]
        for g in range(A_KV_HEADS):
            ka_ref[0, g, rows, :] = qkr[:, A_K0 + g * HEAD_DIM:A_K0 + (g + 1) * HEAD_DIM]
            va_ref[0, g, rows, :] = _with_ones(px[:, A_V0 + g * HEAD_DIM:A_V0 + (g + 1) * HEAD_DIM])
        for h in range(B_HEADS):
            qb_ref[0, h, rows, :] = (px[:, B_Q0 + h * HEAD_DIM:B_Q0 + (h + 1) * HEAD_DIM] * Q_SCALE).astype(BF16)
            kb_ref[0, h, rows, :] = px[:, B_K0 + h * HEAD_DIM:B_K0 + (h + 1) * HEAD_DIM].astype(BF16)
            vb_ref[0, h, rows, :] = _with_ones(px[:, B_V0 + h * HEAD_DIM:B_V0 + (h + 1) * HEAD_DIM])
        z_ref[0, rows, :] = px[:, C0:]


def _step_tokens(s):
    return 3 * TOKEN_TILE if s % (3 * TOKEN_TILE) == 0 else 2 * TOKEN_TILE


def _project(x_lat, x_ctx, ctx_block, mod, w_bf, bd, gain, cos_t, sin_t):
    bsz, d, s = x_lat.shape[0], D_MODEL, S_ALL
    tm = _step_tokens(s)
    nt = s // tm
    heads = lambda n: pl.BlockSpec((1, n, tm, HEAD_DIM), lambda b, i: (b, 0, i, 0))
    hshape = lambda n: jax.ShapeDtypeStruct((bsz, n, s, HEAD_DIM), BF16)
    vals = lambda n: pl.BlockSpec((1, n, tm, V_EXT_W), lambda b, i: (b, 0, i, 0))
    vshape = lambda n: jax.ShapeDtypeStruct((bsz, n, s, V_EXT_W), BF16)
    return pl.pallas_call(
        functools.partial(_proj_body, tm=tm),
        grid=(bsz, nt),
        in_specs=_stream_specs(tm, ctx_block) + [
            pl.BlockSpec((1, 2, 6, d), lambda b, i: (b, 0, 0, 0)),
            pl.BlockSpec((d, PROJ_W), lambda b, i: (0, 0)),
            pl.BlockSpec((QK_W, QK_W), lambda b, i: (0, 0)),
            pl.BlockSpec((1, QK_W), lambda b, i: (0, 0)),
            pl.BlockSpec((tm, ROPE_TABLE_W), lambda b, i: (i, 0)),
            pl.BlockSpec((tm, ROPE_TABLE_W), lambda b, i: (i, 0)),
        ],
        out_specs=[heads(A_HEADS), heads(A_KV_HEADS), vals(A_KV_HEADS),
                   heads(B_HEADS), heads(B_HEADS), vals(B_HEADS),
                   pl.BlockSpec((1, tm, 3 * C_CH), lambda b, i: (b, i, 0))],
        out_shape=[hshape(A_HEADS), hshape(A_KV_HEADS), vshape(A_KV_HEADS),
                   hshape(B_HEADS), hshape(B_HEADS), vshape(B_HEADS),
                   jax.ShapeDtypeStruct((bsz, s, 3 * C_CH), F32)],
        compiler_params=_params("parallel", "parallel"),
        name="proj_in",
    )(*_stream_args(x_lat, x_ctx, tm), mod, w_bf, bd, gain, cos_t, sin_t)


def _gqa_body(q_ref, k_ref, v_ref, o_ref):
    n_keys = k_ref.shape[2]
    for h in range(A_REP):
        q = q_ref[0, h]
        m = acc = None
        for j in range(n_keys // GQA_KEY_CHUNK):
            keys = slice(j * GQA_KEY_CHUNK, (j + 1) * GQA_KEY_CHUNK)
            s = _dot_nt(q, k_ref[0, 0, keys, :])
            mj = jnp.max(s, -1, keepdims=True)
            if j == 0:
                m = mj
                acc = _dot(jnp.exp2(s - m).astype(BF16), v_ref[0, 0, keys, :])
            else:
                m_new = jnp.maximum(m, mj)
                alpha = jnp.exp2(m - m_new)
                acc = alpha * acc + _dot(jnp.exp2(s - m_new).astype(BF16), v_ref[0, 0, keys, :])
                m = m_new
        o_ref[0, :, h * HEAD_DIM:(h + 1) * HEAD_DIM] = _normalise(acc).astype(BF16)


def _gqa(qa, ka, va):
    bsz, _, s, _ = qa.shape
    tq = GQA_Q_TILE
    return pl.pallas_call(
        _gqa_body,
        grid=(bsz, A_KV_HEADS, SEQ // tq),
        in_specs=[
            pl.BlockSpec((1, A_REP, tq, HEAD_DIM), lambda b, g, i: (b, g, i, 0)),
            pl.BlockSpec((1, 1, s, HEAD_DIM), lambda b, g, i: (b, g, 0, 0)),
            pl.BlockSpec((1, 1, s, V_EXT_W), lambda b, g, i: (b, g, 0, 0)),
        ],
        out_specs=pl.BlockSpec((1, tq, A_REP * HEAD_DIM), lambda b, g, i: (b, i, g)),
        out_shape=jax.ShapeDtypeStruct((bsz, SEQ, A_W), BF16),
        compiler_params=_params("parallel", "parallel", "parallel"),
        name="gqa_attn",
    )(qa, ka, va)


NA_HEADS_PER_STEP = 4
NA_Q_ROWS = 8
NA_K_ROWS = 16
NA_Q_TOKENS = NA_Q_ROWS * GRID_W
NA_K_TOKENS = NA_K_ROWS * GRID_W
NA_Q_BLOCKS = GRID_ROWS // NA_Q_ROWS
NA_KEY_ROW0 = (0, 4, 12, 16)
NA_PATTERN_BLOCKS = (0, 1, 3)


def _na_body(q_ref, k_ref, v_ref, bias_ref, o_ref):
    j = pl.program_id(2)
    key_row0 = jnp.where(j == 0, NA_KEY_ROW0[0],
                         jnp.where(j == 1, NA_KEY_ROW0[1], jnp.where(j == 2, NA_KEY_ROW0[2], NA_KEY_ROW0[3])))
    k0 = pl.multiple_of(key_row0 * GRID_W, 4 * GRID_W)
    for hh in range(NA_HEADS_PER_STEP):
        q = q_ref[0, hh]
        kw = k_ref[0, hh, pl.ds(k0, NA_K_TOKENS), :]
        vw = v_ref[0, hh, pl.ds(k0, NA_K_TOKENS), :]
        kc = k_ref[0, hh, SEQ:, :]
        vc = v_ref[0, hh, SEQ:, :]
        sw = _dot_nt(q, kw) + bias_ref[hh, 0]
        sc = _dot_nt(q, kc)
        m = jnp.maximum(jnp.max(sw, -1, keepdims=True), jnp.max(sc, -1, keepdims=True))
        o_ext = _dot(jnp.exp2(sw - m).astype(BF16), vw) + _dot(jnp.exp2(sc - m).astype(BF16), vc)
        o_ref[0, :, hh * HEAD_DIM:(hh + 1) * HEAD_DIM] = _normalise(o_ext).astype(BF16)


def _na(qb, kb, vb, bias):
    bsz, _, s, _ = qb.shape
    hp = NA_HEADS_PER_STEP
    kv = lambda w: pl.BlockSpec((1, hp, s, w), lambda b, h, j: (b, h, 0, 0))
    return pl.pallas_call(
        _na_body,
        grid=(bsz, B_HEADS // hp, NA_Q_BLOCKS),
        in_specs=[pl.BlockSpec((1, hp, NA_Q_TOKENS, HEAD_DIM), lambda b, h, j: (b, h, j, 0)),
                  kv(HEAD_DIM), kv(V_EXT_W),
                  pl.BlockSpec((hp, 1, NA_Q_TOKENS, NA_K_TOKENS), lambda b, h, j: (h, (j + 1) // 2, 0, 0))],
        out_specs=pl.BlockSpec((1, NA_Q_TOKENS, hp * HEAD_DIM), lambda b, h, j: (b, j, h)),
        out_shape=jax.ShapeDtypeStruct((bsz, SEQ, B_W), BF16),
        compiler_params=_params("parallel", "parallel", "arbitrary"),
        name="nbr_attn",
    )(qb, kb, vb, bias)


def _na_bias_body(t_ref, o_ref):
    o_ref[...] = jnp.full(o_ref.shape, NEG_BIG, F32)
    for p, blk in enumerate(NA_PATTERN_BLOCKS):
        @pl.when(pl.program_id(1) == p)
        def _(blk=blk):
            for qi in range(NA_Q_ROWS):
                qrow = blk * NA_Q_ROWS + qi
                rs = min(max(qrow - WIN_ROWS // 2, 0), GRID_ROWS - WIN_ROWS)
                for krow in range(rs, rs + WIN_ROWS):
                    ki = krow - NA_KEY_ROW0[blk]
                    o_ref[0, 0, qi * GRID_W:(qi + 1) * GRID_W, ki * GRID_W:(ki + 1) * GRID_W] = (
                        t_ref[0, krow - qrow + WIN_ROWS - 1])


def _na_bias_table(rpb):
    col = np.arange(GRID_W)
    col_start = np.clip(col - WIN_COLS // 2, 0, GRID_W - WIN_COLS)
    in_cols = (col[None, :] >= col_start[:, None]) & (col[None, :] < col_start[:, None] + WIN_COLS)
    dc = np.clip(col[None, :] - col[:, None] + WIN_COLS - 1, 0, 2 * WIN_COLS - 2)
    onehot = jnp.asarray(dc[None] == np.arange(2 * WIN_COLS - 1)[:, None, None], F32)
    tiles = jnp.einsum('hdj,jqk->hdqk', rpb.astype(F32), onehot, precision=HIGHEST)
    tiles = jnp.where(in_cols[None, None], tiles * LOG2E, NEG_BIG)
    n_pat = len(NA_PATTERN_BLOCKS)
    return pl.pallas_call(
        _na_bias_body,
        grid=(B_HEADS, n_pat),
        in_specs=[pl.BlockSpec((1, 2 * WIN_ROWS - 1, GRID_W, GRID_W), lambda h, p: (h, 0, 0, 0))],
        out_specs=pl.BlockSpec((1, 1, NA_Q_TOKENS, NA_K_TOKENS), lambda h, p: (h, p, 0, 0)),
        out_shape=jax.ShapeDtypeStruct((B_HEADS, n_pat, NA_Q_TOKENS, NA_K_TOKENS), F32),
        compiler_params=_params("parallel", "arbitrary"),
        name="nbr_bias",
    )(tiles)


def _ctx_attn_body(q_ref, k_ref, v_ref, o_ref, *, rep):
    for g in range(k_ref.shape[1]):
        for r in range(rep):
            hh = g * rep + r
            s = _dot_nt(q_ref[0, hh], k_ref[0, g])
            m = jnp.max(s, -1, keepdims=True)
            o_ext = _dot(jnp.exp2(s - m).astype(BF16), v_ref[0, g])
            o_ref[0, :, hh * HEAD_DIM:(hh + 1) * HEAD_DIM] = _normalise(o_ext).astype(BF16)


def _ctx_attn(q, k, v, kv_per_step):
    bsz, n_q, _, _ = q.shape
    n_kv = k.shape[1]
    rep = n_q // n_kv
    seg = lambda n, w=HEAD_DIM: pl.BlockSpec((1, n, CTX_LEN, w), lambda b, h: (b, h, SEQ // CTX_LEN, 0))
    return pl.pallas_call(
        functools.partial(_ctx_attn_body, rep=rep),
        grid=(bsz, n_kv // kv_per_step),
        in_specs=[seg(kv_per_step * rep), seg(kv_per_step), seg(kv_per_step, V_EXT_W)],
        out_specs=pl.BlockSpec((1, CTX_LEN, kv_per_step * rep * HEAD_DIM), lambda b, h: (b, 0, h)),
        out_shape=jax.ShapeDtypeStruct((bsz, CTX_LEN, n_q * HEAD_DIM), BF16),
        compiler_params=_params("parallel", "parallel"),
        name="ctx_attn",
    )(q, k, v)


DFT_FINE = 16


def _dft_table_body(chr_ref, shr_ref, clr_ref, slr_ref, chc_ref, shc_ref, clc_ref, slc_ref,
                    fwd_ref, inv_ref, *, n, tf):
    j = pl.program_id(0)

    def expand(c_hi, s_hi, c_lo, s_lo, rows):
        cos = (c_hi[:, None, :] * c_lo[None, :, :] - s_hi[:, None, :] * s_lo[None, :, :])
        sin = (s_hi[:, None, :] * c_lo[None, :, :] + c_hi[:, None, :] * s_lo[None, :, :])
        return cos.reshape(rows, -1), sin.reshape(rows, -1)

    cos_r, sin_r = expand(chr_ref[...], shr_ref[...], clr_ref[...], slr_ref[...], tf)
    cos_c, sin_c = expand(chc_ref[...], shc_ref[...], clc_ref[...], slc_ref[...], n)
    fr = lax.broadcasted_iota(jnp.int32, (tf, n), 0) + j * tf
    tr = lax.broadcasted_iota(jnp.int32, (tf, n), 1)
    sin_r = jnp.where(fr == 0, jnp.where(tr % 2 == 0, 1.0, -1.0), sin_r)
    tc = lax.broadcasted_iota(jnp.int32, (n, tf), 0)
    fcol = lax.broadcasted_iota(jnp.int32, (n, tf), 1) + j * tf
    sin_c = jnp.where(fcol == 0, jnp.where(tc % 2 == 0, 1.0, -1.0), sin_c)
    fwd_ref[0, :tf, :] = cos_r.astype(BF16)
    fwd_ref[0, tf:, :] = sin_r.astype(BF16)
    inv_ref[0, :, :tf] = cos_c.astype(BF16)
    inv_ref[0, :, tf:] = sin_c.astype(BF16)


def _dft_tables(n):
    s = jnp.arange(n, dtype=jnp.int32)[None, :]

    def cos_sin(f):
        ang = ((f * s) % (2 * n)).astype(F32) * (math.pi / n)
        return jnp.cos(ang), jnp.sin(ang)

    ch, sh = cos_sin(jnp.arange(n // DFT_FINE, dtype=jnp.int32)[:, None] * DFT_FINE)
    cl, sl = cos_sin(jnp.arange(DFT_FINE, dtype=jnp.int32)[:, None])
    tf = min(FREQ_TILE, n)
    nf = n // tf
    hi_rows = pl.BlockSpec((tf // DFT_FINE, n), lambda j: (j, 0))
    lo_rows = pl.BlockSpec((DFT_FINE, n), lambda j: (0, 0))
    hi_cols = pl.BlockSpec((n // DFT_FINE, tf), lambda j: (0, j))
    lo_cols = pl.BlockSpec((DFT_FINE, tf), lambda j: (0, j))
    return pl.pallas_call(
        functools.partial(_dft_table_body, n=n, tf=tf),
        grid=(nf,),
        in_specs=[hi_rows, hi_rows, lo_rows, lo_rows, hi_cols, hi_cols, lo_cols, lo_cols],
        out_specs=[pl.BlockSpec((1, 2 * tf, n), lambda j: (j, 0, 0)),
                   pl.BlockSpec((1, n, 2 * tf), lambda j: (j, 0, 0))],
        out_shape=[jax.ShapeDtypeStruct((nf, 2 * tf, n), BF16),
                   jax.ShapeDtypeStruct((nf, n, 2 * tf), BF16)],
        compiler_params=_params("parallel"),
        name="dft_tables",
    )(ch, sh, cl, sl, ch, sh, cl, sl)


def _split_bf16(v):
    hi = v.astype(BF16)
    lo = (v - hi.astype(F32)).astype(BF16)
    return hi, lo


def _spectrum_body(fwd_ref, kp_ref, km_ref, kc_ref, ks_ref, *, tf):
    kp_hi, kp_lo = _split_bf16(kp_ref[...])
    km_hi, km_lo = _split_bf16(km_ref[...])
    fc = fwd_ref[0, :tf, :]
    fs = fwd_ref[0, tf:, :]
    kc_ref[...] = _dot(fc, kp_hi) + _dot(fc, kp_lo)
    ks_sin = _dot(fs, km_hi) + _dot(fs, km_lo)
    ks_nyq = _dot(fs, kp_hi) + _dot(fs, kp_lo)
    row = lax.broadcasted_iota(jnp.int32, ks_sin.shape, 0) + pl.program_id(0) * tf
    ks_ref[...] = jnp.where(row == 0, ks_nyq, ks_sin)


def _filter_spectrum(fwd, kplus, kminus):
    nf, tf2, n = fwd.shape
    tf = tf2 // 2
    filt = pl.BlockSpec((n, C_CH), lambda j: (0, 0))
    out = pl.BlockSpec((tf, C_CH), lambda j: (j, 0))
    return pl.pallas_call(
        functools.partial(_spectrum_body, tf=tf),
        grid=(nf,),
        in_specs=[pl.BlockSpec((1, tf2, n), lambda j: (j, 0, 0)), filt, filt],
        out_specs=[out, out],
        out_shape=[jax.ShapeDtypeStruct((n, C_CH), F32)] * 2,
        compiler_params=_params("parallel"),
        name="hyena_spectrum",
    )(fwd, kplus, kminus)


def _short_conv(z, cw_ref, cb_ref, cols, n):
    pos = lax.broadcasted_iota(jnp.int32, z.shape, 0)
    prev = jnp.where(pos == 0, 0.0, pltpu.roll(z, 1, 0))
    nxt = jnp.where(pos == n - 1, 0.0, pltpu.roll(z, n - 1, 0))
    return prev * cw_ref[0:1, cols] + z * cw_ref[1:2, cols] + nxt * cw_ref[2:3, cols] + cb_ref[:, cols]


def _hyena_gate_body(z_ref, cw_ref, cb_ref, skip_ref, u_ref, us_ref, *, n):
    zc = _short_conv(z_ref[0, :, C_CH:], cw_ref, cb_ref, slice(C_CH, 3 * C_CH), n)
    u = zc[:, C_CH:] * zc[:, :C_CH]
    u_ref[0] = u.astype(BF16)
    us_ref[0] = u * skip_ref[...]


def _hyena_gate(z, seg_start, n, cw, cb, skip):
    bsz = z.shape[0]
    seg = seg_start // n
    small = lambda r, c: pl.BlockSpec((r, c), lambda b: (0, 0))
    out = pl.BlockSpec((1, n, C_CH), lambda b: (b, 0, 0))
    return pl.pallas_call(
        functools.partial(_hyena_gate_body, n=n),
        grid=(bsz,),
        in_specs=[pl.BlockSpec((1, n, 3 * C_CH), lambda b: (b, seg, 0)),
                  small(3, 3 * C_CH), small(1, 3 * C_CH), small(1, C_CH)],
        out_specs=[out, out],
        out_shape=[jax.ShapeDtypeStruct((bsz, n, C_CH), BF16),
                   jax.ShapeDtypeStruct((bsz, n, C_CH), F32)],
        compiler_params=_params("parallel"),
        name="hyena_gate",
    )(z, cw, cb, skip)


def _hyena_body(u_ref, us_ref, z_ref, cw_ref, cb_ref, fwd_ref, inv_ref, kc_ref, ks_ref, o_ref, *, n, tf):
    u = u_ref[0]
    y = None
    for j in range(n // tf):
        freq = slice(j * tf, (j + 1) * tf)
        kc = kc_ref[freq, :]
        ks = ks_ref[freq, :]
        spec = _dot(fwd_ref[j], u)
        uc, us = spec[:tf], spec[tf:]
        if j == 0:
            first = lax.broadcasted_iota(jnp.int32, kc.shape, 0) == 0
            wgt = jnp.where(first, 1.0, 2.0) * (1.0 / (2 * n))
            re = jnp.where(first, uc * kc, uc * kc - us * ks) * wgt
            im = jnp.where(first, us * ks, uc * ks + us * kc) * wgt
        else:
            re = (uc * kc - us * ks) * (1.0 / n)
            im = (uc * ks + us * kc) * (1.0 / n)
        part = _dot(inv_ref[j], jnp.concatenate([re, im], axis=0).astype(BF16))
        y = part if y is None else y + part
    x0 = _short_conv(z_ref[0], cw_ref, cb_ref, slice(0, C_CH), n)
    o_ref[0] = ((y + us_ref[0]) * x0).astype(BF16)


def _hyena(u, us, z, seg_start, n, cw, cb, fwd, inv, kc, ks):
    bsz = u.shape[0]
    tf = min(FREQ_TILE, n)
    nf = n // tf
    seg = seg_start // n
    seq = lambda **kw: pl.BlockSpec((1, n, C_CH), lambda b: (b, 0, 0), **kw)
    once = dict(pipeline_mode=pl.Buffered(1))
    small = lambda r, c: pl.BlockSpec((r, c), lambda b: (0, 0))
    return pl.pallas_call(
        functools.partial(_hyena_body, n=n, tf=tf),
        grid=(bsz,),
        in_specs=[seq(), seq(),
                  pl.BlockSpec((1, n, C_CH), lambda b: (b, seg, 0)),
                  small(3, 3 * C_CH), small(1, 3 * C_CH),
                  pl.BlockSpec((nf, 2 * tf, n), lambda b: (0, 0, 0), **once),
                  pl.BlockSpec((nf, n, 2 * tf), lambda b: (0, 0, 0), **once),
                  pl.BlockSpec((n, C_CH), lambda b: (0, 0), **once),
                  pl.BlockSpec((n, C_CH), lambda b: (0, 0), **once)],
        out_specs=seq(),
        out_shape=jax.ShapeDtypeStruct((bsz, n, C_CH), BF16),
        compiler_params=_params("parallel"),
        name="hyena_conv",
    )(u, us, z, cw, cb, fwd, inv, kc, ks)


def _dot3(a, b):
    a_hi, a_lo = _split_bf16(a)
    b_hi, b_lo = _split_bf16(b)
    return _dot(a_hi, b_hi) + _dot(a_hi, b_lo) + _dot(a_lo, b_hi)


POS_PAD = 128
FILTER_ROWS = 256


def _filter_body(pos_ref, decay_ref, w1_ref, b1_ref, w2_ref, b2_ref, w3_ref, freq_ref, kp_ref, km_ref):
    h = jnp.sin(freq_ref[0:1, :] * (_dot3(pos_ref[...], w1_ref[...]) + b1_ref[...]))
    h = jnp.sin(freq_ref[1:2, :] * (_dot3(h, w2_ref[...]) + b2_ref[...]))
    h = _dot3(h, w3_ref[...])
    decay = decay_ref[...]
    hf = h[:, :C_CH] * decay
    hb = h[:, C_CH:] * decay
    lag = lax.broadcasted_iota(jnp.int32, hb.shape, 0) + pl.program_id(0) * FILTER_ROWS
    hb = jnp.where(lag == 0, 0.0, hb)
    kp_ref[...] = hf + hb
    km_ref[...] = hf - hb


def _implicit_filters(n, w1, b1, w2, b2, w3, freq):
    t = jnp.linspace(0.0, 1.0, n, dtype=F32)[:, None]
    w = 2 * math.pi * jnp.arange(n, dtype=F32)[:, None] / n
    f = jnp.linspace(1e-4, C_POS_BANDS - 1, C_POS_BANDS, dtype=F32)[None]
    pos = jnp.concatenate([t, jnp.cos(f * w), -jnp.sin(f * w)], -1)
    n_pos = pos.shape[1]
    pos = jnp.pad(pos, ((0, 0), (0, POS_PAD - n_pos)))
    w1p = jnp.pad(w1, ((0, POS_PAD - n_pos), (0, 0)))
    lo = math.log(C_DECAY_TARGET) / C_SLOW_DECAY
    hi = math.log(C_DECAY_TARGET) / C_FAST_DECAY
    deltas = jnp.abs(jnp.linspace(lo, hi, C_CH, dtype=F32))
    decay = jnp.exp(-t * deltas[None])
    hid = w2.shape[0]
    rows = lambda c: pl.BlockSpec((FILTER_ROWS, c), lambda i: (i, 0))
    full = lambda r, c: pl.BlockSpec((r, c), lambda i: (0, 0))
    return pl.pallas_call(
        _filter_body,
        grid=(n // FILTER_ROWS,),
        in_specs=[rows(POS_PAD), rows(C_CH), full(POS_PAD, hid), full(1, hid), full(hid, hid), full(1, hid),
                  full(hid, 2 * C_CH), full(2, hid)],
        out_specs=[rows(C_CH), rows(C_CH)],
        out_shape=[jax.ShapeDtypeStruct((n, C_CH), F32)] * 2,
        compiler_params=_params("parallel"),
        name="hyena_filter",
    )(pos, decay, w1p, b1[None], w2, b2[None], w3, freq)


def _router_gates(logits_t, rb):
    tm = logits_t.shape[1]
    scores = _sigmoid(logits_t)
    sel = scores + rb
    sel3 = sel.reshape(N_GROUPS, GROUP_SIZE, tm)
    kk = lax.broadcasted_iota(jnp.int32, sel3.shape, 1)
    m1 = jnp.max(sel3, 1, keepdims=True)
    i1 = jnp.min(jnp.where(sel3 == m1, kk, GROUP_SIZE), 1, keepdims=True)
    m2 = jnp.max(jnp.where(kk == i1, -jnp.inf, sel3), 1, keepdims=True)
    gscore = (m1 + m2).reshape(N_GROUPS, tm)
    gid = lax.broadcasted_iota(jnp.int32, gscore.shape, 0)
    rank = jnp.zeros(gscore.shape, jnp.int32)
    for g in range(N_GROUPS):
        other = gscore[g:g + 1, :]
        ahead = (other > gscore) | ((other == gscore) & (g < gid))
        rank = rank + ahead.astype(jnp.int32)
    gsel = (rank < TOPK_GROUPS).reshape(N_GROUPS, 1, tm)
    cand = jnp.where(gsel, sel3, -jnp.inf).reshape(N_EXPERTS, tm)
    eid = lax.broadcasted_iota(jnp.int32, cand.shape, 0)
    chosen = jnp.zeros(cand.shape, jnp.bool_)
    for _ in range(TOP_K):
        m = jnp.max(cand, 0, keepdims=True)
        pick = eid == jnp.min(jnp.where(cand == m, eid, N_EXPERTS), 0, keepdims=True)
        chosen = chosen | pick
        cand = jnp.where(pick, -jnp.inf, cand)
    wsel = jnp.where(chosen, scores, 0.0)
    return wsel / jnp.sum(wsel, 0, keepdims=True) * ROUTED_SCALE


def _out_body(*refs, tm):
    n_x = tm // TOKEN_TILE + 1
    x_refs, oa_refs, ob_refs, oc_refs = (refs[j * n_x:(j + 1) * n_x] for j in range(4))
    (mod_ref, wa_ref, wb_ref, wc_ref, lng_ref, lnb_ref,
     rwt_ref, rb_ref, xmid_ref, h_ref, gate_ref) = refs[4 * n_x:]
    for k, rows, is_ctx in _sub_tiles(tm):
        attn = (_dot(_stream_tile(oa_refs, k, is_ctx), wa_ref[...])
                + _dot(_stream_tile(ob_refs, k, is_ctx), wb_ref[...])
                + _dot(_stream_tile(oc_refs, k, is_ctx), wc_ref[...]))
        y = DN_ALPHA * _stream_tile(x_refs, k, is_ctx) + _mod_row(mod_ref, is_ctx, 2) * attn
        xm = _layer_norm(y) * lng_ref[...] + lnb_ref[...]
        xmid_ref[0, rows, :] = xm
        h = _layer_norm(xm) * (1.0 + _mod_row(mod_ref, is_ctx, 4)) + _mod_row(mod_ref, is_ctx, 3)
        hi, lo = _split_bf16(h)
        h_ref[0, rows, :] = hi
        logits_t = _dot_nt(rwt_ref[...], jnp.concatenate([hi, lo, hi], axis=1))
        gates_t = _router_gates(logits_t, rb_ref[...])
        pad = jnp.zeros((128 - N_EXPERTS, gates_t.shape[1]), F32)
        gates = jnp.concatenate([gates_t, pad], axis=0).T
        for g in range(N_GROUPS):
            gate_ref[0, g, rows, :] = gates[:, g * GROUP_SIZE:(g + 1) * GROUP_SIZE]


def _out_proj(s, x_lat, x_ctx, ctx_block, oa, ob, oc, mod, wa, wb, wc, lng, lnb, rwt3, rb):
    bsz = x_lat.shape[0]
    d = D_MODEL
    tm = _step_tokens(s)
    tok = lambda w: pl.BlockSpec((1, tm, w), lambda b, i: (b, i, 0))
    full = lambda r, c: pl.BlockSpec((r, c), lambda b, i: (0, 0))
    return pl.pallas_call(
        functools.partial(_out_body, tm=tm),
        grid=(bsz, s // tm),
        in_specs=_stream_specs(tm, ctx_block) + _stream_specs(tm, 0, A_W) + _stream_specs(tm, 0, B_W)
        + _stream_specs(tm, 0, C_CH) + [
                  pl.BlockSpec((1, 2, 6, d), lambda b, i: (b, 0, 0, 0)),
                  full(A_W, d), full(B_W, d), full(C_CH, d), full(1, d), full(1, d),
                  full(N_EXPERTS, 3 * d), full(N_EXPERTS, 1)],
        out_specs=[tok(d), tok(d),
                   pl.BlockSpec((1, N_GROUPS, tm, GROUP_SIZE), lambda b, i: (b, 0, i, 0))],
        out_shape=[jax.ShapeDtypeStruct((bsz, s, d), F32),
                   jax.ShapeDtypeStruct((bsz, s, d), BF16),
                   jax.ShapeDtypeStruct((bsz, N_GROUPS, s, GROUP_SIZE), F32)],
        compiler_params=_params("parallel", "parallel"),
        name="proj_out_router",
    )(*_stream_args(x_lat, x_ctx, tm), *_stream_args(*oa, tm), *_stream_args(*ob, tm), *_stream_args(*oc, tm),
      mod, wa, wb, wc, lng, lnb, rwt3, rb)


MOE_CHAINS = 2


def _moe_body(h_ref, gate_ref, xmid_ref, mod_ref, w1_ref, w3_ref, w2_ref,
              sw1_ref, sw3_ref, sw2_ref, lng_ref, lnb_ref, o_ref, acc_ref, *, tm):
    g = pl.program_id(2)
    rc = tm // MOE_CHAINS

    @pl.when(g == 0)
    def _():
        h = h_ref[0]
        a = _dot(h, sw1_ref[...])
        act = (a * _sigmoid(a) * _dot(h, sw3_ref[...])).astype(BF16)
        acc_ref[...] = _dot(act, sw2_ref[...])

    w1 = jnp.concatenate([w1_ref[k] for k in range(GROUP_SIZE)], axis=1)
    w3 = jnp.concatenate([w3_ref[k] for k in range(GROUP_SIZE)], axis=1)
    w2 = w2_ref[...].reshape(GROUP_SIZE * EXPERT_HIDDEN, D_MODEL)
    for c in range(MOE_CHAINS):
        rows = slice(c * rc, (c + 1) * rc)
        h = h_ref[0, rows, :]
        a = _dot(h, w1)
        act = a * _sigmoid(a) * _dot(h, w3)
        gt = gate_ref[0, 0, rows, :]
        parts = [act[:, k * EXPERT_HIDDEN:(k + 1) * EXPERT_HIDDEN] * gt[:, k:k + 1]
                 for k in range(GROUP_SIZE)]
        acc_ref[rows, :] += _dot(jnp.concatenate(parts, axis=1).astype(BF16), w2)

    @pl.when(g == N_GROUPS - 1)
    def _():
        row = lax.broadcasted_iota(jnp.int32, (tm, 1), 0) + pl.program_id(1) * tm
        gate2 = jnp.where(row >= SEQ, mod_ref[0, 1, 5:6, :], mod_ref[0, 0, 5:6, :])
        y = DN_ALPHA * xmid_ref[0] + gate2 * acc_ref[...]
        o_ref[0] = _layer_norm(y) * lng_ref[...] + lnb_ref[...]


MOE_TILE_2BUF = 768
MOE_TILE_1BUF = 1024


def _moe(h, gates, xmid, mod, layer, w1_all, w3_all, w2_all, sw1, sw3, sw2, lng, lnb):
    bsz, s, d = h.shape
    tm = MOE_TILE_1BUF if s % MOE_TILE_1BUF == 0 else MOE_TILE_2BUF
    tok = pl.BlockSpec((1, tm, d), lambda b, i, g: (b, i, 0))
    tok_once = pl.BlockSpec((1, tm, d), lambda b, i, g: (b, i, 0),
                            pipeline_mode=pl.Buffered(1 if tm > MOE_TILE_2BUF else 2))
    full = lambda r, c: pl.BlockSpec((r, c), lambda b, i, g: (0, 0))
    up = pl.BlockSpec((None, GROUP_SIZE, d, EXPERT_HIDDEN), lambda b, i, g: (layer, g, 0, 0))
    return pl.pallas_call(
        functools.partial(_moe_body, tm=tm),
        grid=(bsz, s // tm, N_GROUPS),
        in_specs=[tok,
                  pl.BlockSpec((1, 1, tm, GROUP_SIZE), lambda b, i, g: (b, g, i, 0)),
                  tok_once,
                  pl.BlockSpec((1, 2, 6, d), lambda b, i, g: (b, 0, 0, 0)),
                  up, up,
                  pl.BlockSpec((None, GROUP_SIZE, EXPERT_HIDDEN, d), lambda b, i, g: (layer, g, 0, 0)),
                  full(d, SHARED_HIDDEN), full(d, SHARED_HIDDEN), full(SHARED_HIDDEN, d),
                  full(1, d), full(1, d)],
        out_specs=tok,
        out_shape=jax.ShapeDtypeStruct((bsz, s, d), F32),
        scratch_shapes=[pltpu.VMEM((tm, d), F32)],
        compiler_params=_params("parallel", "parallel", "arbitrary"),
        name="moe_ffn",
    )(h, gates, xmid, mod, w1_all, w3_all, w2_all, sw1, sw3, sw2, lng, lnb)


def _rope_tables():
    t = jnp.arange(SEQ)
    row = (t // GRID_W).astype(F32)
    col = (t % GRID_W).astype(F32)
    inv = ROPE_THETA ** (-jnp.arange(0, HALF, 2, dtype=F32) / HALF)
    ang = jnp.concatenate([row[:, None] * inv, col[:, None] * inv], -1)
    cos, sin = jnp.cos(ang), jnp.sin(ang)
    n_heads = ROPE_TABLE_W // HEAD_DIM
    cos_t = jnp.concatenate([jnp.tile(jnp.concatenate([cos, cos], -1), (1, n_heads)),
                             jnp.ones((CTX_LEN, ROPE_TABLE_W), F32)], 0)
    sin_t = jnp.concatenate([jnp.tile(jnp.concatenate([-sin, sin], -1), (1, n_heads)),
                             jnp.zeros((CTX_LEN, ROPE_TABLE_W), F32)], 0)
    return cos_t, sin_t


MOD_ROWS = 16
MOD_COL_TILE = 1536


def _modulation_body(c_ref, w_ref, b_ref, o_ref):
    cv = c_ref[...]
    o_ref[0] = _dot3(cv * _sigmoid(cv), w_ref[0]) + b_ref[0]


def _modulation(c, c_ctx, ada_w, ada_b):
    bsz = c.shape[0]
    assert bsz + 1 <= MOD_ROWS, bsz
    n_out = 6 * D_MODEL
    cvec = jnp.concatenate([c, c_ctx[None], jnp.zeros((MOD_ROWS - bsz - 1, D_MODEL), F32)], axis=0)
    m = pl.pallas_call(
        _modulation_body,
        grid=(DEPTH, n_out // MOD_COL_TILE),
        in_specs=[pl.BlockSpec((MOD_ROWS, D_MODEL), lambda l, j: (0, 0)),
                  pl.BlockSpec((1, D_MODEL, MOD_COL_TILE), lambda l, j: (l, 0, j)),
                  pl.BlockSpec((1, 1, MOD_COL_TILE), lambda l, j: (l, 0, j))],
        out_specs=pl.BlockSpec((1, MOD_ROWS, MOD_COL_TILE), lambda l, j: (l, 0, j)),
        out_shape=jax.ShapeDtypeStruct((DEPTH, MOD_ROWS, n_out), F32),
        compiler_params=_params("parallel", "parallel"),
        name="ada_modulation",
    )(cvec, ada_w, ada_b[:, None, :])
    m = m.reshape(DEPTH, MOD_ROWS, 6, D_MODEL)
    lat = m[:, :bsz]
    return jnp.stack([lat, jnp.broadcast_to(m[:, bsz:bsz + 1], lat.shape)], axis=2)


def kernel(x, c, ctx, c_ctx, ada_w, ada_b, w_in, q_gain, k_gain, nat_rpb, hy_conv_w, hy_conv_b, hy_f_w1, hy_f_b1, hy_f_w2, hy_f_b2, hy_f_w3, hy_freq, hy_skip, w_out, ln1_g, ln1_b, router_w, router_b, exp_w1, exp_w3, exp_w2, sh_w1, sh_w3, sh_w2, ln2_g, ln2_b):
    bsz = x.shape[0]
    d = D_MODEL
    x_lat, x_ctx, ctx_block = x, ctx, 0

    head_perm = np.concatenate([np.arange(0, HEAD_DIM, 2), np.arange(1, HEAD_DIM, 2)])
    qk_perm = (np.arange(QK_W // HEAD_DIM)[:, None] * HEAD_DIM + head_perm[None, :]).reshape(-1)
    perm_mat = jnp.asarray(np.eye(QK_W)[:, qk_perm], BF16)
    cos_t, sin_t = _rope_tables()
    bd = jnp.asarray(np.kron(np.eye(QK_W // HEAD_DIM), np.ones((HEAD_DIM, HEAD_DIM))), BF16)
    tables_lat = _dft_tables(SEQ)
    tables_ctx = _dft_tables(CTX_LEN)
    mod_all = _modulation(c, c_ctx, ada_w, ada_b)
    w1_bf, w3_bf, w2_bf = exp_w1.astype(BF16), exp_w3.astype(BF16), exp_w2.astype(BF16)

    for l in range(DEPTH):
        last = l == DEPTH - 1
        mod = mod_all[l]

        w_bf = w_in[l].astype(BF16)
        w_qk = jnp.dot(w_bf[:, :QK_W], perm_mat, preferred_element_type=F32).astype(BF16)
        w_bf = jnp.concatenate([w_qk, w_bf[:, QK_W:]], axis=1)
        gain = jnp.concatenate([jnp.tile(q_gain[l][head_perm] * Q_SCALE, A_HEADS),
                                jnp.tile(k_gain[l][head_perm], A_KV_HEADS)])[None].astype(F32)
        qa, ka, va, qb, kb, vb, z = _project(x_lat, x_ctx, ctx_block, mod, w_bf, bd, gain, cos_t, sin_t)

        oa = [_gqa(qa, ka, va)]
        ob = [_na(qb, kb, vb, _na_bias_table(nat_rpb[l]))]
        if not last:
            oa.append(_ctx_attn(qa, ka, va, 1))
            ob.append(_ctx_attn(qb, kb, vb, NA_HEADS_PER_STEP))

        filt = (hy_f_w1[l], hy_f_b1[l], hy_f_w2[l], hy_f_b2[l], hy_f_w3[l], hy_freq[l])
        cw, cb, skip = hy_conv_w[l], hy_conv_b[l][None], hy_skip[l][None]
        segments = [(0, SEQ, tables_lat)] + ([] if last else [(SEQ, CTX_LEN, tables_ctx)])
        oc = []
        for seg_start, n, (fwd, inv) in segments:
            kc, ks = _filter_spectrum(fwd, *_implicit_filters(n, *filt))
            u, us = _hyena_gate(z, seg_start, n, cw, cb, skip)
            oc.append(_hyena(u, us, z, seg_start, n, cw, cb, fwd, inv, kc, ks))
        oa, ob, oc = ((o[0], o[-1]) for o in (oa, ob, oc))

        wo = w_out[l].astype(BF16)
        rw_hi, rw_lo = _split_bf16(router_w[l].T)
        rwt3 = jnp.concatenate([rw_hi, rw_hi, rw_lo], axis=1)
        xmid, h, gates = _out_proj(SEQ if last else S_ALL, x_lat, x_ctx, ctx_block, oa, ob, oc, mod,
                                   wo[:A_W], wo[A_W:A_W + B_W], wo[A_W + B_W:],
                                   ln1_g[l][None], ln1_b[l][None], rwt3, router_b[l][:, None])

        xs = _moe(h, gates, xmid, mod, l, w1_bf, w3_bf, w2_bf,
                  sh_w1[l].astype(BF16), sh_w3[l].astype(BF16), sh_w2[l].astype(BF16),
                  ln2_g[l][None], ln2_b[l][None])
        x_lat, x_ctx, ctx_block = xs, xs, SEQ // TOKEN_TILE
    return xs
```

```python
import functools
import math

import numpy as np
import jax
import jax.numpy as jnp
from jax import lax
from jax.experimental import pallas as pl
from jax.experimental.pallas import tpu as pltpu

F32 = jnp.float32
BF16 = jnp.bfloat16
HIGHEST = lax.Precision.HIGHEST

D_MODEL = 1024
SEQ = 2048
DEPTH = 2
GRID_W = 64
GRID_ROWS = SEQ // GRID_W
CTX_LEN = 256
S_ALL = SEQ + CTX_LEN
HEAD_DIM = 64
HALF = HEAD_DIM // 2
A_HEADS = 8
A_KV_HEADS = 2
A_REP = A_HEADS // A_KV_HEADS
ROPE_THETA = 10000.0
B_HEADS = 4
WIN_ROWS = 8
WIN_COLS = 16
C_CH = 256
C_POS_BANDS = 16
C_DECAY_TARGET = 1e-2
C_FAST_DECAY = 0.3
C_SLOW_DECAY = 1.5
A_W = A_HEADS * HEAD_DIM
A_KV_W = A_KV_HEADS * HEAD_DIM
B_W = B_HEADS * HEAD_DIM
A_K0 = A_W
A_V0 = A_K0 + A_KV_W
B_Q0 = A_V0 + A_KV_W
B_K0 = B_Q0 + B_W
B_V0 = B_K0 + B_W
C0 = B_V0 + B_W
PROJ_W = C0 + 3 * C_CH
QK_W = A_W + A_KV_W
N_EXPERTS = 64
TOP_K = 8
N_GROUPS = 8
GROUP_SIZE = N_EXPERTS // N_GROUPS
TOPK_GROUPS = 4
EXPERT_HIDDEN = 128
SHARED_HIDDEN = 256
ROUTED_SCALE = 2.5
DN_ALPHA = (2 * DEPTH) ** 0.25
LN_EPS = 1e-6
RMS_EPS = 1e-6
LOG2E = math.log2(math.e)
Q_SCALE = HEAD_DIM ** -0.5 * LOG2E
NEG_BIG = -1e30

VMEM_LIMIT_BYTES = 56 * 1024 * 1024
TOKEN_TILE = 256
RMS_BLOCK = 256
ROPE_TABLE_W = 128
GQA_Q_TILE = 1024
GQA_KEY_CHUNK = 768
FREQ_TILE = 256


def _params(*sem):
    return pltpu.CompilerParams(dimension_semantics=sem, vmem_limit_bytes=VMEM_LIMIT_BYTES)


def _layer_norm(v):
    mu = jnp.mean(v, -1, keepdims=True)
    vc = v - mu
    var = jnp.mean(vc * vc, -1, keepdims=True)
    return vc * lax.rsqrt(var + LN_EPS)


def _sigmoid(v):
    return 1.0 / (1.0 + jnp.exp(-v))


def _dot(a, b):
    return jnp.dot(a, b, preferred_element_type=F32)


def _dot_nt(a, b):
    return lax.dot_general(a, b, (((1,), (1,)), ((), ())), preferred_element_type=F32)


V_EXT_W = 2 * HEAD_DIM


def _with_ones(v):
    return jnp.concatenate([v.astype(BF16), jnp.ones(v.shape, BF16)], axis=1)


def _normalise(o_ext):
    return o_ext[:, :HEAD_DIM] / o_ext[:, HEAD_DIM:]


def _mod_row(mod_ref, is_ctx, idx):
    return jnp.where(is_ctx, mod_ref[0, 1, idx:idx + 1, :], mod_ref[0, 0, idx:idx + 1, :])


def _sub_tiles(tm):
    for k in range(tm // TOKEN_TILE):
        rows = slice(k * TOKEN_TILE, (k + 1) * TOKEN_TILE)
        is_ctx = (pl.program_id(1) * tm + k * TOKEN_TILE) >= SEQ
        yield k, rows, is_ctx


def _stream_specs(tm, ctx_block, width=D_MODEL):
    n_sub = tm // TOKEN_TILE
    last_lat = SEQ // TOKEN_TILE - 1
    lat = [pl.BlockSpec((1, TOKEN_TILE, width),
                        lambda b, i, k=k: (b, jnp.minimum(i * n_sub + k, last_lat), 0)) for k in range(n_sub)]
    return lat + [pl.BlockSpec((1, TOKEN_TILE, width), lambda b, i: (b, ctx_block, 0))]


def _stream_args(lat, ctx, tm):
    return [lat] * (tm // TOKEN_TILE) + [ctx]


def _stream_tile(x_refs, k, is_ctx):
    return jnp.where(is_ctx, x_refs[-1][0], x_refs[k][0])


def _proj_body(*refs, tm):
    n_x = tm // TOKEN_TILE + 1
    x_refs = refs[:n_x]
    (mod_ref, w_ref, bd_ref, gain_ref, cos_ref, sin_ref,
     qa_ref, ka_ref, va_ref, qb_ref, kb_ref, vb_ref, z_ref) = refs[n_x:]
    n_rep = QK_W // ROPE_TABLE_W
    for k, rows, is_ctx in _sub_tiles(tm):
        xn = _layer_norm(_stream_tile(x_refs, k, is_ctx))
        hx = (xn * (1.0 + _mod_row(mod_ref, is_ctx, 1)) + _mod_row(mod_ref, is_ctx, 0)).astype(BF16)
        px = _dot(hx, w_ref[...])
        qk = px[:, :QK_W]
        sq = (qk * qk).astype(BF16)
        ss = jnp.concatenate(
            [_dot(sq[:, c:min(c + RMS_BLOCK, QK_W)], bd_ref[:min(RMS_BLOCK, QK_W - c), :min(RMS_BLOCK, QK_W - c)])
             for c in range(0, QK_W, RMS_BLOCK)], axis=1)
        qkn = qk * lax.rsqrt(ss * (1.0 / HEAD_DIM) + RMS_EPS) * gain_ref[...]
        lane = lax.broadcasted_iota(jnp.int32, qkn.shape, 1)
        partner = jnp.where((lane % HEAD_DIM) < HALF,
                            pltpu.roll(qkn, QK_W - HALF, 1), pltpu.roll(qkn, HALF, 1))
        cos = jnp.concatenate([cos_ref[rows, :]] * n_rep, axis=1)
        sin = jnp.concatenate([sin_ref[rows, :]] * n_rep, axis=1)
        qkr = (qkn * cos + partner * sin).astype(BF16)
        for h in range(A_HEADS):
            qa_ref[0, h, rows, :] = qkr[:, h * HEAD_DIM:(h + 1) * HEAD_DIM]
        for g in range(A_KV_HEADS):
            ka_ref[0, g, rows, :] = qkr[:, A_K0 + g * HEAD_DIM:A_K0 + (g + 1) * HEAD_DIM]
            va_ref[0, g, rows, :] = _with_ones(px[:, A_V0 + g * HEAD_DIM:A_V0 + (g + 1) * HEAD_DIM])
        for h in range(B_HEADS):
            qb_ref[0, h, rows, :] = (px[:, B_Q0 + h * HEAD_DIM:B_Q0 + (h + 1) * HEAD_DIM] * Q_SCALE).astype(BF16)
            kb_ref[0, h, rows, :] = px[:, B_K0 + h * HEAD_DIM:B_K0 + (h + 1) * HEAD_DIM].astype(BF16)
            vb_ref[0, h, rows, :] = _with_ones(px[:, B_V0 + h * HEAD_DIM:B_V0 + (h + 1) * HEAD_DIM])
        z_ref[0, rows, :] = px[:, C0:]


def _step_tokens(s):
    return 3 * TOKEN_TILE if s % (3 * TOKEN_TILE) == 0 else 2 * TOKEN_TILE


def _project(x_lat, x_ctx, ctx_block, mod, w_bf, bd, gain, cos_t, sin_t):
    bsz, d, s = x_lat.shape[0], D_MODEL, S_ALL
    tm = _step_tokens(s)
    nt = s // tm
    heads = lambda n: pl.BlockSpec((1, n, tm, HEAD_DIM), lambda b, i: (b, 0, i, 0))
    hshape = lambda n: jax.ShapeDtypeStruct((bsz, n, s, HEAD_DIM), BF16)
    vals = lambda n: pl.BlockSpec((1, n, tm, V_EXT_W), lambda b, i: (b, 0, i, 0))
    vshape = lambda n: jax.ShapeDtypeStruct((bsz, n, s, V_EXT_W), BF16)
    return pl.pallas_call(
        functools.partial(_proj_body, tm=tm),
        grid=(bsz, nt),
        in_specs=_stream_specs(tm, ctx_block) + [
            pl.BlockSpec((1, 2, 6, d), lambda b, i: (b, 0, 0, 0)),
            pl.BlockSpec((d, PROJ_W), lambda b, i: (0, 0)),
            pl.BlockSpec((RMS_BLOCK, RMS_BLOCK), lambda b, i: (0, 0)),
            pl.BlockSpec((1, QK_W), lambda b, i: (0, 0)),
            pl.BlockSpec((tm, ROPE_TABLE_W), lambda b, i: (i, 0)),
            pl.BlockSpec((tm, ROPE_TABLE_W), lambda b, i: (i, 0)),
        ],
        out_specs=[heads(A_HEADS), heads(A_KV_HEADS), vals(A_KV_HEADS),
                   heads(B_HEADS), heads(B_HEADS), vals(B_HEADS),
                   pl.BlockSpec((1, tm, 3 * C_CH), lambda b, i: (b, i, 0))],
        out_shape=[hshape(A_HEADS), hshape(A_KV_HEADS), vshape(A_KV_HEADS),
                   hshape(B_HEADS), hshape(B_HEADS), vshape(B_HEADS),
                   jax.ShapeDtypeStruct((bsz, s, 3 * C_CH), F32)],
        compiler_params=_params("parallel", "parallel"),
        name="proj_in",
    )(*_stream_args(x_lat, x_ctx, tm), mod, w_bf, bd, gain, cos_t, sin_t)


def _gqa_body(q_ref, k_ref, v_ref, o_ref):
    n_keys = k_ref.shape[2]
    for h in range(A_REP):
        q = q_ref[0, h]
        m = acc = None
        for j in range(n_keys // GQA_KEY_CHUNK):
            keys = slice(j * GQA_KEY_CHUNK, (j + 1) * GQA_KEY_CHUNK)
            s = _dot_nt(q, k_ref[0, 0, keys, :])
            mj = jnp.max(s, -1, keepdims=True)
            if j == 0:
                m = mj
                acc = _dot(jnp.exp2(s - m).astype(BF16), v_ref[0, 0, keys, :])
            else:
                m_new = jnp.maximum(m, mj)
                alpha = jnp.exp2(m - m_new)
                acc = alpha * acc + _dot(jnp.exp2(s - m_new).astype(BF16), v_ref[0, 0, keys, :])
                m = m_new
        o_ref[0, :, h * HEAD_DIM:(h + 1) * HEAD_DIM] = _normalise(acc).astype(BF16)


def _gqa(qa, ka, va):
    bsz, _, s, _ = qa.shape
    tq = GQA_Q_TILE
    return pl.pallas_call(
        _gqa_body,
        grid=(bsz, A_KV_HEADS, SEQ // tq),
        in_specs=[
            pl.BlockSpec((1, A_REP, tq, HEAD_DIM), lambda b, g, i: (b, g, i, 0)),
            pl.BlockSpec((1, 1, s, HEAD_DIM), lambda b, g, i: (b, g, 0, 0)),
            pl.BlockSpec((1, 1, s, V_EXT_W), lambda b, g, i: (b, g, 0, 0)),
        ],
        out_specs=pl.BlockSpec((1, tq, A_REP * HEAD_DIM), lambda b, g, i: (b, i, g)),
        out_shape=jax.ShapeDtypeStruct((bsz, SEQ, A_W), BF16),
        compiler_params=_params("parallel", "parallel", "parallel"),
        name="gqa_attn",
    )(qa, ka, va)


NA_HEADS_PER_STEP = 4
NA_Q_ROWS = 8
NA_K_ROWS = 16
NA_Q_TOKENS = NA_Q_ROWS * GRID_W
NA_K_TOKENS = NA_K_ROWS * GRID_W
NA_Q_BLOCKS = GRID_ROWS // NA_Q_ROWS
NA_KEY_ROW0 = (0, 4, 12, 16)
NA_PATTERN_BLOCKS = (0, 1, 3)


def _na_body(q_ref, k_ref, v_ref, bias_ref, o_ref):
    j = pl.program_id(2)
    key_row0 = jnp.where(j == 0, NA_KEY_ROW0[0],
                         jnp.where(j == 1, NA_KEY_ROW0[1], jnp.where(j == 2, NA_KEY_ROW0[2], NA_KEY_ROW0[3])))
    k0 = pl.multiple_of(key_row0 * GRID_W, 4 * GRID_W)
    for hh in range(NA_HEADS_PER_STEP):
        q = q_ref[0, hh]
        kw = k_ref[0, hh, pl.ds(k0, NA_K_TOKENS), :]
        vw = v_ref[0, hh, pl.ds(k0, NA_K_TOKENS), :]
        kc = k_ref[0, hh, SEQ:, :]
        vc = v_ref[0, hh, SEQ:, :]
        sw = _dot_nt(q, kw) + bias_ref[hh, 0]
        sc = _dot_nt(q, kc)
        m = jnp.maximum(jnp.max(sw, -1, keepdims=True), jnp.max(sc, -1, keepdims=True))
        o_ext = _dot(jnp.exp2(sw - m).astype(BF16), vw) + _dot(jnp.exp2(sc - m).astype(BF16), vc)
        o_ref[0, :, hh * HEAD_DIM:(hh + 1) * HEAD_DIM] = _normalise(o_ext).astype(BF16)


def _na(qb, kb, vb, bias):
    bsz, _, s, _ = qb.shape
    hp = NA_HEADS_PER_STEP
    kv = lambda w: pl.BlockSpec((1, hp, s, w), lambda b, h, j: (b, h, 0, 0))
    return pl.pallas_call(
        _na_body,
        grid=(bsz, B_HEADS // hp, NA_Q_BLOCKS),
        in_specs=[pl.BlockSpec((1, hp, NA_Q_TOKENS, HEAD_DIM), lambda b, h, j: (b, h, j, 0)),
                  kv(HEAD_DIM), kv(V_EXT_W),
                  pl.BlockSpec((hp, 1, NA_Q_TOKENS, NA_K_TOKENS), lambda b, h, j: (h, (j + 1) // 2, 0, 0))],
        out_specs=pl.BlockSpec((1, NA_Q_TOKENS, hp * HEAD_DIM), lambda b, h, j: (b, j, h)),
        out_shape=jax.ShapeDtypeStruct((bsz, SEQ, B_W), BF16),
        compiler_params=_params("parallel", "parallel", "arbitrary"),
        name="nbr_attn",
    )(qb, kb, vb, bias)


def _na_bias_body(t_ref, o_ref):
    o_ref[...] = jnp.full(o_ref.shape, NEG_BIG, F32)
    for p, blk in enumerate(NA_PATTERN_BLOCKS):
        @pl.when(pl.program_id(1) == p)
        def _(blk=blk):
            for qi in range(NA_Q_ROWS):
                qrow = blk * NA_Q_ROWS + qi
                rs = min(max(qrow - WIN_ROWS // 2, 0), GRID_ROWS - WIN_ROWS)
                for krow in range(rs, rs + WIN_ROWS):
                    ki = krow - NA_KEY_ROW0[blk]
                    o_ref[0, 0, qi * GRID_W:(qi + 1) * GRID_W, ki * GRID_W:(ki + 1) * GRID_W] = (
                        t_ref[0, krow - qrow + WIN_ROWS - 1])


def _na_bias_table(rpb):
    col = np.arange(GRID_W)
    col_start = np.clip(col - WIN_COLS // 2, 0, GRID_W - WIN_COLS)
    in_cols = (col[None, :] >= col_start[:, None]) & (col[None, :] < col_start[:, None] + WIN_COLS)
    dc = np.clip(col[None, :] - col[:, None] + WIN_COLS - 1, 0, 2 * WIN_COLS - 2)
    onehot = jnp.asarray(dc[None] == np.arange(2 * WIN_COLS - 1)[:, None, None], F32)
    tiles = jnp.einsum('hdj,jqk->hdqk', rpb.astype(F32), onehot, precision=HIGHEST)
    tiles = jnp.where(in_cols[None, None], tiles * LOG2E, NEG_BIG)
    n_pat = len(NA_PATTERN_BLOCKS)
    return pl.pallas_call(
        _na_bias_body,
        grid=(B_HEADS, n_pat),
        in_specs=[pl.BlockSpec((1, 2 * WIN_ROWS - 1, GRID_W, GRID_W), lambda h, p: (h, 0, 0, 0))],
        out_specs=pl.BlockSpec((1, 1, NA_Q_TOKENS, NA_K_TOKENS), lambda h, p: (h, p, 0, 0)),
        out_shape=jax.ShapeDtypeStruct((B_HEADS, n_pat, NA_Q_TOKENS, NA_K_TOKENS), F32),
        compiler_params=_params("parallel", "arbitrary"),
        name="nbr_bias",
    )(tiles)


def _ctx_attn_body(q_ref, k_ref, v_ref, o_ref, *, rep):
    for g in range(k_ref.shape[1]):
        for r in range(rep):
            hh = g * rep + r
            s = _dot_nt(q_ref[0, hh], k_ref[0, g])
            m = jnp.max(s, -1, keepdims=True)
            o_ext = _dot(jnp.exp2(s - m).astype(BF16), v_ref[0, g])
            o_ref[0, :, hh * HEAD_DIM:(hh + 1) * HEAD_DIM] = _normalise(o_ext).astype(BF16)


def _ctx_attn(q, k, v, kv_per_step):
    bsz, n_q, _, _ = q.shape
    n_kv = k.shape[1]
    rep = n_q // n_kv
    seg = lambda n, w=HEAD_DIM: pl.BlockSpec((1, n, CTX_LEN, w), lambda b, h: (b, h, SEQ // CTX_LEN, 0))
    return pl.pallas_call(
        functools.partial(_ctx_attn_body, rep=rep),
        grid=(bsz, n_kv // kv_per_step),
        in_specs=[seg(kv_per_step * rep), seg(kv_per_step), seg(kv_per_step, V_EXT_W)],
        out_specs=pl.BlockSpec((1, CTX_LEN, kv_per_step * rep * HEAD_DIM), lambda b, h: (b, 0, h)),
        out_shape=jax.ShapeDtypeStruct((bsz, CTX_LEN, n_q * HEAD_DIM), BF16),
        compiler_params=_params("parallel", "parallel"),
        name="ctx_attn",
    )(q, k, v)


DFT_FINE = 16


def _dft_table_body(chr_ref, shr_ref, clr_ref, slr_ref, chc_ref, shc_ref, clc_ref, slc_ref,
                    fwd_ref, inv_ref, *, n, tf):
    j = pl.program_id(0)

    def expand(c_hi, s_hi, c_lo, s_lo, rows):
        cos = (c_hi[:, None, :] * c_lo[None, :, :] - s_hi[:, None, :] * s_lo[None, :, :])
        sin = (s_hi[:, None, :] * c_lo[None, :, :] + c_hi[:, None, :] * s_lo[None, :, :])
        return cos.reshape(rows, -1), sin.reshape(rows, -1)

    cos_r, sin_r = expand(chr_ref[...], shr_ref[...], clr_ref[...], slr_ref[...], tf)
    cos_c, sin_c = expand(chc_ref[...], shc_ref[...], clc_ref[...], slc_ref[...], n)
    fr = lax.broadcasted_iota(jnp.int32, (tf, n), 0) + j * tf
    tr = lax.broadcasted_iota(jnp.int32, (tf, n), 1)
    sin_r = jnp.where(fr == 0, jnp.where(tr % 2 == 0, 1.0, -1.0), sin_r)
    tc = lax.broadcasted_iota(jnp.int32, (n, tf), 0)
    fcol = lax.broadcasted_iota(jnp.int32, (n, tf), 1) + j * tf
    sin_c = jnp.where(fcol == 0, jnp.where(tc % 2 == 0, 1.0, -1.0), sin_c)
    fwd_ref[0, :tf, :] = cos_r.astype(BF16)
    fwd_ref[0, tf:, :] = sin_r.astype(BF16)
    inv_ref[0, :, :tf] = cos_c.astype(BF16)
    inv_ref[0, :, tf:] = sin_c.astype(BF16)


def _dft_tables(n):
    s = jnp.arange(n, dtype=jnp.int32)[None, :]

    def cos_sin(f):
        ang = ((f * s) % (2 * n)).astype(F32) * (math.pi / n)
        return jnp.cos(ang), jnp.sin(ang)

    ch, sh = cos_sin(jnp.arange(n // DFT_FINE, dtype=jnp.int32)[:, None] * DFT_FINE)
    cl, sl = cos_sin(jnp.arange(DFT_FINE, dtype=jnp.int32)[:, None])
    tf = min(FREQ_TILE, n)
    nf = n // tf
    hi_rows = pl.BlockSpec((tf // DFT_FINE, n), lambda j: (j, 0))
    lo_rows = pl.BlockSpec((DFT_FINE, n), lambda j: (0, 0))
    hi_cols = pl.BlockSpec((n // DFT_FINE, tf), lambda j: (0, j))
    lo_cols = pl.BlockSpec((DFT_FINE, tf), lambda j: (0, j))
    return pl.pallas_call(
        functools.partial(_dft_table_body, n=n, tf=tf),
        grid=(nf,),
        in_specs=[hi_rows, hi_rows, lo_rows, lo_rows, hi_cols, hi_cols, lo_cols, lo_cols],
        out_specs=[pl.BlockSpec((1, 2 * tf, n), lambda j: (j, 0, 0)),
                   pl.BlockSpec((1, n, 2 * tf), lambda j: (j, 0, 0))],
        out_shape=[jax.ShapeDtypeStruct((nf, 2 * tf, n), BF16),
                   jax.ShapeDtypeStruct((nf, n, 2 * tf), BF16)],
        compiler_params=_params("parallel"),
        name="dft_tables",
    )(ch, sh, cl, sl, ch, sh, cl, sl)


def _split_bf16(v):
    hi = v.astype(BF16)
    lo = (v - hi.astype(F32)).astype(BF16)
    return hi, lo


def _spectrum_body(fwd_ref, kp_ref, km_ref, kc_ref, ks_ref, *, tf):
    kp_hi, kp_lo = _split_bf16(kp_ref[...])
    km_hi, km_lo = _split_bf16(km_ref[...])
    fc = fwd_ref[0, :tf, :]
    fs = fwd_ref[0, tf:, :]
    kc_ref[...] = _dot(fc, kp_hi) + _dot(fc, kp_lo)
    ks_sin = _dot(fs, km_hi) + _dot(fs, km_lo)
    ks_nyq = _dot(fs, kp_hi) + _dot(fs, kp_lo)
    row = lax.broadcasted_iota(jnp.int32, ks_sin.shape, 0) + pl.program_id(0) * tf
    ks_ref[...] = jnp.where(row == 0, ks_nyq, ks_sin)


def _filter_spectrum(fwd, kplus, kminus):
    nf, tf2, n = fwd.shape
    tf = tf2 // 2
    filt = pl.BlockSpec((n, C_CH), lambda j: (0, 0))
    out = pl.BlockSpec((tf, C_CH), lambda j: (j, 0))
    return pl.pallas_call(
        functools.partial(_spectrum_body, tf=tf),
        grid=(nf,),
        in_specs=[pl.BlockSpec((1, tf2, n), lambda j: (j, 0, 0)), filt, filt],
        out_specs=[out, out],
        out_shape=[jax.ShapeDtypeStruct((n, C_CH), F32)] * 2,
        compiler_params=_params("parallel"),
        name="hyena_spectrum",
    )(fwd, kplus, kminus)


def _short_conv(z, cw_ref, cb_ref, cols, n):
    pos = lax.broadcasted_iota(jnp.int32, z.shape, 0)
    prev = jnp.where(pos == 0, 0.0, pltpu.roll(z, 1, 0))
    nxt = jnp.where(pos == n - 1, 0.0, pltpu.roll(z, n - 1, 0))
    return prev * cw_ref[0:1, cols] + z * cw_ref[1:2, cols] + nxt * cw_ref[2:3, cols] + cb_ref[:, cols]


def _hyena_gate_body(z_ref, cw_ref, cb_ref, skip_ref, u_ref, us_ref, *, n):
    zc = _short_conv(z_ref[0, :, C_CH:], cw_ref, cb_ref, slice(C_CH, 3 * C_CH), n)
    u = zc[:, C_CH:] * zc[:, :C_CH]
    u_ref[0] = u.astype(BF16)
    us_ref[0] = u * skip_ref[...]


def _hyena_gate(z, seg_start, n, cw, cb, skip):
    bsz = z.shape[0]
    seg = seg_start // n
    small = lambda r, c: pl.BlockSpec((r, c), lambda b: (0, 0))
    out = pl.BlockSpec((1, n, C_CH), lambda b: (b, 0, 0))
    return pl.pallas_call(
        functools.partial(_hyena_gate_body, n=n),
        grid=(bsz,),
        in_specs=[pl.BlockSpec((1, n, 3 * C_CH), lambda b: (b, seg, 0)),
                  small(3, 3 * C_CH), small(1, 3 * C_CH), small(1, C_CH)],
        out_specs=[out, out],
        out_shape=[jax.ShapeDtypeStruct((bsz, n, C_CH), BF16),
                   jax.ShapeDtypeStruct((bsz, n, C_CH), F32)],
        compiler_params=_params("parallel"),
        name="hyena_gate",
    )(z, cw, cb, skip)


def _hyena_body(u_ref, us_ref, z_ref, cw_ref, cb_ref, fwd_ref, inv_ref, kc_ref, ks_ref, o_ref, *, n, tf):
    u = u_ref[0]
    y = None
    for j in range(n // tf):
        freq = slice(j * tf, (j + 1) * tf)
        kc = kc_ref[freq, :]
        ks = ks_ref[freq, :]
        spec = _dot(fwd_ref[j], u)
        uc, us = spec[:tf], spec[tf:]
        if j == 0:
            first = lax.broadcasted_iota(jnp.int32, kc.shape, 0) == 0
            wgt = jnp.where(first, 1.0, 2.0) * (1.0 / (2 * n))
            re = jnp.where(first, uc * kc, uc * kc - us * ks) * wgt
            im = jnp.where(first, us * ks, uc * ks + us * kc) * wgt
        else:
            re = (uc * kc - us * ks) * (1.0 / n)
            im = (uc * ks + us * kc) * (1.0 / n)
        part = _dot(inv_ref[j], jnp.concatenate([re, im], axis=0).astype(BF16))
        y = part if y is None else y + part
    x0 = _short_conv(z_ref[0], cw_ref, cb_ref, slice(0, C_CH), n)
    o_ref[0] = ((y + us_ref[0]) * x0).astype(BF16)


def _hyena(u, us, z, seg_start, n, cw, cb, fwd, inv, kc, ks):
    bsz = u.shape[0]
    tf = min(FREQ_TILE, n)
    nf = n // tf
    seg = seg_start // n
    seq = lambda **kw: pl.BlockSpec((1, n, C_CH), lambda b: (b, 0, 0), **kw)
    once = dict(pipeline_mode=pl.Buffered(1))
    small = lambda r, c: pl.BlockSpec((r, c), lambda b: (0, 0))
    return pl.pallas_call(
        functools.partial(_hyena_body, n=n, tf=tf),
        grid=(bsz,),
        in_specs=[seq(), seq(),
                  pl.BlockSpec((1, n, C_CH), lambda b: (b, seg, 0)),
                  small(3, 3 * C_CH), small(1, 3 * C_CH),
                  pl.BlockSpec((nf, 2 * tf, n), lambda b: (0, 0, 0), **once),
                  pl.BlockSpec((nf, n, 2 * tf), lambda b: (0, 0, 0), **once),
                  pl.BlockSpec((n, C_CH), lambda b: (0, 0), **once),
                  pl.BlockSpec((n, C_CH), lambda b: (0, 0), **once)],
        out_specs=seq(),
        out_shape=jax.ShapeDtypeStruct((bsz, n, C_CH), BF16),
        compiler_params=_params("parallel"),
        name="hyena_conv",
    )(u, us, z, cw, cb, fwd, inv, kc, ks)


def _dot3(a, b):
    a_hi, a_lo = _split_bf16(a)
    b_hi, b_lo = _split_bf16(b)
    return _dot(a_hi, b_hi) + _dot(a_hi, b_lo) + _dot(a_lo, b_hi)


POS_PAD = 128
FILTER_ROWS = 256


def _filter_body(pos_ref, decay_ref, w1_ref, b1_ref, w2_ref, b2_ref, w3_ref, freq_ref, kp_ref, km_ref):
    h = jnp.sin(freq_ref[0:1, :] * (_dot3(pos_ref[...], w1_ref[...]) + b1_ref[...]))
    h = jnp.sin(freq_ref[1:2, :] * (_dot3(h, w2_ref[...]) + b2_ref[...]))
    h = _dot3(h, w3_ref[...])
    decay = decay_ref[...]
    hf = h[:, :C_CH] * decay
    hb = h[:, C_CH:] * decay
    lag = lax.broadcasted_iota(jnp.int32, hb.shape, 0) + pl.program_id(0) * FILTER_ROWS
    hb = jnp.where(lag == 0, 0.0, hb)
    kp_ref[...] = hf + hb
    km_ref[...] = hf - hb


def _implicit_filters(n, w1, b1, w2, b2, w3, freq):
    t = jnp.linspace(0.0, 1.0, n, dtype=F32)[:, None]
    w = 2 * math.pi * jnp.arange(n, dtype=F32)[:, None] / n
    f = jnp.linspace(1e-4, C_POS_BANDS - 1, C_POS_BANDS, dtype=F32)[None]
    pos = jnp.concatenate([t, jnp.cos(f * w), -jnp.sin(f * w)], -1)
    n_pos = pos.shape[1]
    pos = jnp.pad(pos, ((0, 0), (0, POS_PAD - n_pos)))
    w1p = jnp.pad(w1, ((0, POS_PAD - n_pos), (0, 0)))
    lo = math.log(C_DECAY_TARGET) / C_SLOW_DECAY
    hi = math.log(C_DECAY_TARGET) / C_FAST_DECAY
    deltas = jnp.abs(jnp.linspace(lo, hi, C_CH, dtype=F32))
    decay = jnp.exp(-t * deltas[None])
    hid = w2.shape[0]
    rows = lambda c: pl.BlockSpec((FILTER_ROWS, c), lambda i: (i, 0))
    full = lambda r, c: pl.BlockSpec((r, c), lambda i: (0, 0))
    return pl.pallas_call(
        _filter_body,
        grid=(n // FILTER_ROWS,),
        in_specs=[rows(POS_PAD), rows(C_CH), full(POS_PAD, hid), full(1, hid), full(hid, hid), full(1, hid),
                  full(hid, 2 * C_CH), full(2, hid)],
        out_specs=[rows(C_CH), rows(C_CH)],
        out_shape=[jax.ShapeDtypeStruct((n, C_CH), F32)] * 2,
        compiler_params=_params("parallel"),
        name="hyena_filter",
    )(pos, decay, w1p, b1[None], w2, b2[None], w3, freq)


def _router_gates(logits_t, rb):
    tm = logits_t.shape[1]
    scores = _sigmoid(logits_t)
    sel = scores + rb
    sel3 = sel.reshape(N_GROUPS, GROUP_SIZE, tm)
    kk = lax.broadcasted_iota(jnp.int32, sel3.shape, 1)
    m1 = jnp.max(sel3, 1, keepdims=True)
    i1 = jnp.min(jnp.where(sel3 == m1, kk, GROUP_SIZE), 1, keepdims=True)
    m2 = jnp.max(jnp.where(kk == i1, -jnp.inf, sel3), 1, keepdims=True)
    gscore = (m1 + m2).reshape(N_GROUPS, tm)
    gid = lax.broadcasted_iota(jnp.int32, gscore.shape, 0)
    rank = jnp.zeros(gscore.shape, jnp.int32)
    for g in range(N_GROUPS):
        other = gscore[g:g + 1, :]
        ahead = (other > gscore) | ((other == gscore) & (g < gid))
        rank = rank + ahead.astype(jnp.int32)
    gsel = (rank < TOPK_GROUPS).reshape(N_GROUPS, 1, tm)
    cand = jnp.where(gsel, sel3, -jnp.inf).reshape(N_EXPERTS, tm)
    eid = lax.broadcasted_iota(jnp.int32, cand.shape, 0)
    chosen = jnp.zeros(cand.shape, jnp.bool_)
    for _ in range(TOP_K):
        m = jnp.max(cand, 0, keepdims=True)
        pick = eid == jnp.min(jnp.where(cand == m, eid, N_EXPERTS), 0, keepdims=True)
        chosen = chosen | pick
        cand = jnp.where(pick, -jnp.inf, cand)
    wsel = jnp.where(chosen, scores, 0.0)
    return wsel / jnp.sum(wsel, 0, keepdims=True) * ROUTED_SCALE


def _out_body(*refs, tm):
    n_x = tm // TOKEN_TILE + 1
    x_refs, oa_refs, ob_refs, oc_refs = (refs[j * n_x:(j + 1) * n_x] for j in range(4))
    (mod_ref, wa_ref, wb_ref, wc_ref, lng_ref, lnb_ref,
     rwt_ref, rb_ref, xmid_ref, h_ref, gate_ref) = refs[4 * n_x:]
    for k, rows, is_ctx in _sub_tiles(tm):
        attn = (_dot(_stream_tile(oa_refs, k, is_ctx), wa_ref[...])
                + _dot(_stream_tile(ob_refs, k, is_ctx), wb_ref[...])
                + _dot(_stream_tile(oc_refs, k, is_ctx), wc_ref[...]))
        y = DN_ALPHA * _stream_tile(x_refs, k, is_ctx) + _mod_row(mod_ref, is_ctx, 2) * attn
        xm = _layer_norm(y) * lng_ref[...] + lnb_ref[...]
        xmid_ref[0, rows, :] = xm
        h = _layer_norm(xm) * (1.0 + _mod_row(mod_ref, is_ctx, 4)) + _mod_row(mod_ref, is_ctx, 3)
        hi, lo = _split_bf16(h)
        h_ref[0, rows, :] = hi
        logits_t = _dot_nt(rwt_ref[...], jnp.concatenate([hi, lo, hi], axis=1))
        gates_t = _router_gates(logits_t, rb_ref[...])
        pad = jnp.zeros((128 - N_EXPERTS, gates_t.shape[1]), F32)
        gates = jnp.concatenate([gates_t, pad], axis=0).T
        for g in range(N_GROUPS):
            gate_ref[0, g, rows, :] = gates[:, g * GROUP_SIZE:(g + 1) * GROUP_SIZE]


def _out_proj(s, x_lat, x_ctx, ctx_block, oa, ob, oc, mod, wa, wb, wc, lng, lnb, rwt3, rb):
    bsz = x_lat.shape[0]
    d = D_MODEL
    tm = _step_tokens(s)
    tok = lambda w: pl.BlockSpec((1, tm, w), lambda b, i: (b, i, 0))
    full = lambda r, c: pl.BlockSpec((r, c), lambda b, i: (0, 0))
    return pl.pallas_call(
        functools.partial(_out_body, tm=tm),
        grid=(bsz, s // tm),
        in_specs=_stream_specs(tm, ctx_block) + _stream_specs(tm, 0, A_W) + _stream_specs(tm, 0, B_W)
        + _stream_specs(tm, 0, C_CH) + [
                  pl.BlockSpec((1, 2, 6, d), lambda b, i: (b, 0, 0, 0)),
                  full(A_W, d), full(B_W, d), full(C_CH, d), full(1, d), full(1, d),
                  full(N_EXPERTS, 3 * d), full(N_EXPERTS, 1)],
        out_specs=[tok(d), tok(d),
                   pl.BlockSpec((1, N_GROUPS, tm, GROUP_SIZE), lambda b, i: (b, 0, i, 0))],
        out_shape=[jax.ShapeDtypeStruct((bsz, s, d), F32),
                   jax.ShapeDtypeStruct((bsz, s, d), BF16),
                   jax.ShapeDtypeStruct((bsz, N_GROUPS, s, GROUP_SIZE), F32)],
        compiler_params=_params("parallel", "parallel"),
        name="proj_out_router",
    )(*_stream_args(x_lat, x_ctx, tm), *_stream_args(*oa, tm), *_stream_args(*ob, tm), *_stream_args(*oc, tm),
      mod, wa, wb, wc, lng, lnb, rwt3, rb)


MOE_CHAINS = 2


def _moe_body(h_ref, gate_ref, xmid_ref, mod_ref, w1_ref, w3_ref, w2_ref,
              sw1_ref, sw3_ref, sw2_ref, lng_ref, lnb_ref, o_ref, acc_ref, *, tm):
    g = pl.program_id(2)
    rc = tm // MOE_CHAINS

    @pl.when(g == 0)
    def _():
        h = h_ref[0]
        a = _dot(h, sw1_ref[...])
        act = (a * _sigmoid(a) * _dot(h, sw3_ref[...])).astype(BF16)
        acc_ref[...] = _dot(act, sw2_ref[...])

    w1 = jnp.concatenate([w1_ref[k] for k in range(GROUP_SIZE)], axis=1)
    w3 = jnp.concatenate([w3_ref[k] for k in range(GROUP_SIZE)], axis=1)
    w2 = w2_ref[...].reshape(GROUP_SIZE * EXPERT_HIDDEN, D_MODEL)
    for c in range(MOE_CHAINS):
        rows = slice(c * rc, (c + 1) * rc)
        h = h_ref[0, rows, :]
        a = _dot(h, w1)
        act = a * _sigmoid(a) * _dot(h, w3)
        gt = gate_ref[0, 0, rows, :]
        parts = [act[:, k * EXPERT_HIDDEN:(k + 1) * EXPERT_HIDDEN] * gt[:, k:k + 1]
                 for k in range(GROUP_SIZE)]
        acc_ref[rows, :] += _dot(jnp.concatenate(parts, axis=1).astype(BF16), w2)

    @pl.when(g == N_GROUPS - 1)
    def _():
        row = lax.broadcasted_iota(jnp.int32, (tm, 1), 0) + pl.program_id(1) * tm
        gate2 = jnp.where(row >= SEQ, mod_ref[0, 1, 5:6, :], mod_ref[0, 0, 5:6, :])
        y = DN_ALPHA * xmid_ref[0] + gate2 * acc_ref[...]
        o_ref[0] = _layer_norm(y) * lng_ref[...] + lnb_ref[...]


MOE_TILE_2BUF = 768
MOE_TILE_1BUF = 1024


def _moe(h, gates, xmid, mod, layer, w1_all, w3_all, w2_all, sw1, sw3, sw2, lng, lnb):
    bsz, s, d = h.shape
    tm = MOE_TILE_1BUF if s % MOE_TILE_1BUF == 0 else MOE_TILE_2BUF
    tok = pl.BlockSpec((1, tm, d), lambda b, i, g: (b, i, 0))
    tok_once = pl.BlockSpec((1, tm, d), lambda b, i, g: (b, i, 0),
                            pipeline_mode=pl.Buffered(1 if tm > MOE_TILE_2BUF else 2))
    full = lambda r, c: pl.BlockSpec((r, c), lambda b, i, g: (0, 0))
    up = pl.BlockSpec((None, GROUP_SIZE, d, EXPERT_HIDDEN), lambda b, i, g: (layer, g, 0, 0))
    return pl.pallas_call(
        functools.partial(_moe_body, tm=tm),
        grid=(bsz, s // tm, N_GROUPS),
        in_specs=[tok,
                  pl.BlockSpec((1, 1, tm, GROUP_SIZE), lambda b, i, g: (b, g, i, 0)),
                  tok_once,
                  pl.BlockSpec((1, 2, 6, d), lambda b, i, g: (b, 0, 0, 0)),
                  up, up,
                  pl.BlockSpec((None, GROUP_SIZE, EXPERT_HIDDEN, d), lambda b, i, g: (layer, g, 0, 0)),
                  full(d, SHARED_HIDDEN), full(d, SHARED_HIDDEN), full(SHARED_HIDDEN, d),
                  full(1, d), full(1, d)],
        out_specs=tok,
        out_shape=jax.ShapeDtypeStruct((bsz, s, d), F32),
        scratch_shapes=[pltpu.VMEM((tm, d), F32)],
        compiler_params=_params("parallel", "parallel", "arbitrary"),
        name="moe_ffn",
    )(h, gates, xmid, mod, w1_all, w3_all, w2_all, sw1, sw3, sw2, lng, lnb)


def _rope_tables():
    t = jnp.arange(SEQ)
    row = (t // GRID_W).astype(F32)
    col = (t % GRID_W).astype(F32)
    inv = ROPE_THETA ** (-jnp.arange(0, HALF, 2, dtype=F32) / HALF)
    ang = jnp.concatenate([row[:, None] * inv, col[:, None] * inv], -1)
    cos, sin = jnp.cos(ang), jnp.sin(ang)
    n_heads = ROPE_TABLE_W // HEAD_DIM
    cos_t = jnp.concatenate([jnp.tile(jnp.concatenate([cos, cos], -1), (1, n_heads)),
                             jnp.ones((CTX_LEN, ROPE_TABLE_W), F32)], 0)
    sin_t = jnp.concatenate([jnp.tile(jnp.concatenate([-sin, sin], -1), (1, n_heads)),
                             jnp.zeros((CTX_LEN, ROPE_TABLE_W), F32)], 0)
    return cos_t, sin_t


MOD_ROWS = 16
MOD_COL_TILE = 1536


def _modulation_body(c_ref, w_ref, b_ref, o_ref):
    cv = c_ref[...]
    o_ref[0] = _dot3(cv * _sigmoid(cv), w_ref[0]) + b_ref[0]


def _modulation(c, c_ctx, ada_w, ada_b):
    bsz = c.shape[0]
    assert bsz + 1 <= MOD_ROWS, bsz
    n_out = 6 * D_MODEL
    cvec = jnp.concatenate([c, c_ctx[None], jnp.zeros((MOD_ROWS - bsz - 1, D_MODEL), F32)], axis=0)
    m = pl.pallas_call(
        _modulation_body,
        grid=(DEPTH, n_out // MOD_COL_TILE),
        in_specs=[pl.BlockSpec((MOD_ROWS, D_MODEL), lambda l, j: (0, 0)),
                  pl.BlockSpec((1, D_MODEL, MOD_COL_TILE), lambda l, j: (l, 0, j)),
                  pl.BlockSpec((1, 1, MOD_COL_TILE), lambda l, j: (l, 0, j))],
        out_specs=pl.BlockSpec((1, MOD_ROWS, MOD_COL_TILE), lambda l, j: (l, 0, j)),
        out_shape=jax.ShapeDtypeStruct((DEPTH, MOD_ROWS, n_out), F32),
        compiler_params=_params("parallel", "parallel"),
        name="ada_modulation",
    )(cvec, ada_w, ada_b[:, None, :])
    m = m.reshape(DEPTH, MOD_ROWS, 6, D_MODEL)
    lat = m[:, :bsz]
    return jnp.stack([lat, jnp.broadcast_to(m[:, bsz:bsz + 1], lat.shape)], axis=2)


def kernel(x, c, ctx, c_ctx, ada_w, ada_b, w_in, q_gain, k_gain, nat_rpb, hy_conv_w, hy_conv_b, hy_f_w1, hy_f_b1, hy_f_w2, hy_f_b2, hy_f_w3, hy_freq, hy_skip, w_out, ln1_g, ln1_b, router_w, router_b, exp_w1, exp_w3, exp_w2, sh_w1, sh_w3, sh_w2, ln2_g, ln2_b):
    bsz = x.shape[0]
    d = D_MODEL
    x_lat, x_ctx, ctx_block = x, ctx, 0

    head_perm = np.concatenate([np.arange(0, HEAD_DIM, 2), np.arange(1, HEAD_DIM, 2)])
    qk_perm = (np.arange(QK_W // HEAD_DIM)[:, None] * HEAD_DIM + head_perm[None, :]).reshape(-1)
    perm_mat = jnp.asarray(np.eye(QK_W)[:, qk_perm], BF16)
    cos_t, sin_t = _rope_tables()
    bd = jnp.asarray(np.kron(np.eye(RMS_BLOCK // HEAD_DIM), np.ones((HEAD_DIM, HEAD_DIM))), BF16)
    tables_lat = _dft_tables(SEQ)
    tables_ctx = _dft_tables(CTX_LEN)
    mod_all = _modulation(c, c_ctx, ada_w, ada_b)
    w1_bf, w3_bf, w2_bf = exp_w1.astype(BF16), exp_w3.astype(BF16), exp_w2.astype(BF16)

    for l in range(DEPTH):
        last = l == DEPTH - 1
        mod = mod_all[l]

        w_bf = w_in[l].astype(BF16)
        w_qk = jnp.dot(w_bf[:, :QK_W], perm_mat, preferred_element_type=F32).astype(BF16)
        w_bf = jnp.concatenate([w_qk, w_bf[:, QK_W:]], axis=1)
        gain = jnp.concatenate([jnp.tile(q_gain[l][head_perm] * Q_SCALE, A_HEADS),
                                jnp.tile(k_gain[l][head_perm], A_KV_HEADS)])[None].astype(F32)
        qa, ka, va, qb, kb, vb, z = _project(x_lat, x_ctx, ctx_block, mod, w_bf, bd, gain, cos_t, sin_t)

        oa = [_gqa(qa, ka, va)]
        ob = [_na(qb, kb, vb, _na_bias_table(nat_rpb[l]))]
        if not last:
            oa.append(_ctx_attn(qa, ka, va, 1))
            ob.append(_ctx_attn(qb, kb, vb, NA_HEADS_PER_STEP))

        filt = (hy_f_w1[l], hy_f_b1[l], hy_f_w2[l], hy_f_b2[l], hy_f_w3[l], hy_freq[l])
        cw, cb, skip = hy_conv_w[l], hy_conv_b[l][None], hy_skip[l][None]
        segments = [(0, SEQ, tables_lat)] + ([] if last else [(SEQ, CTX_LEN, tables_ctx)])
        oc = []
        for seg_start, n, (fwd, inv) in segments:
            kc, ks = _filter_spectrum(fwd, *_implicit_filters(n, *filt))
            u, us = _hyena_gate(z, seg_start, n, cw, cb, skip)
            oc.append(_hyena(u, us, z, seg_start, n, cw, cb, fwd, inv, kc, ks))
        oa, ob, oc = ((o[0], o[-1]) for o in (oa, ob, oc))

        wo = w_out[l].astype(BF16)
        rw_hi, rw_lo = _split_bf16(router_w[l].T)
        rwt3 = jnp.concatenate([rw_hi, rw_hi, rw_lo], axis=1)
        xmid, h, gates = _out_proj(SEQ if last else S_ALL, x_lat, x_ctx, ctx_block, oa, ob, oc, mod,
                                   wo[:A_W], wo[A_W:A_W + B_W], wo[A_W + B_W:],
                                   ln1_g[l][None], ln1_b[l][None], rwt3, router_b[l][:, None])

        xs = _moe(h, gates, xmid, mod, l, w1_bf, w3_bf, w2_bf,
                  sh_w1[l].astype(BF16), sh_w3[l].astype(BF16), sh_w2[l].astype(BF16),
                  ln2_g[l][None], ln2_b[l][None])
        x_lat, x_ctx, ctx_block = xs, xs, SEQ // TOKEN_TILE
    return xs
```

```python
import functools
import math

import numpy as np
import jax
import jax.numpy as jnp
from jax import lax
from jax.experimental import pallas as pl
from jax.experimental.pallas import tpu as pltpu

F32 = jnp.float32
BF16 = jnp.bfloat16
HIGHEST = lax.Precision.HIGHEST

D_MODEL = 1024
SEQ = 2048
DEPTH = 2
GRID_W = 64
GRID_ROWS = SEQ // GRID_W
CTX_LEN = 256
S_ALL = SEQ + CTX_LEN
HEAD_DIM = 64
HALF = HEAD_DIM // 2
A_HEADS = 8
A_KV_HEADS = 2
A_REP = A_HEADS // A_KV_HEADS
ROPE_THETA = 10000.0
B_HEADS = 4
WIN_ROWS = 8
WIN_COLS = 16
C_CH = 256
C_POS_BANDS = 16
C_DECAY_TARGET = 1e-2
C_FAST_DECAY = 0.3
C_SLOW_DECAY = 1.5
A_W = A_HEADS * HEAD_DIM
A_KV_W = A_KV_HEADS * HEAD_DIM
B_W = B_HEADS * HEAD_DIM
A_K0 = A_W
A_V0 = A_K0 + A_KV_W
B_Q0 = A_V0 + A_KV_W
B_K0 = B_Q0 + B_W
B_V0 = B_K0 + B_W
C0 = B_V0 + B_W
PROJ_W = C0 + 3 * C_CH
QK_W = A_W + A_KV_W
N_EXPERTS = 64
TOP_K = 8
N_GROUPS = 8
GROUP_SIZE = N_EXPERTS // N_GROUPS
TOPK_GROUPS = 4
EXPERT_HIDDEN = 128
SHARED_HIDDEN = 256
ROUTED_SCALE = 2.5
DN_ALPHA = (2 * DEPTH) ** 0.25
LN_EPS = 1e-6
RMS_EPS = 1e-6
LOG2E = math.log2(math.e)
Q_SCALE = HEAD_DIM ** -0.5 * LOG2E
NEG_BIG = -1e30

VMEM_LIMIT_BYTES = 56 * 1024 * 1024
TOKEN_TILE = 256
RMS_BLOCK = 256
ROPE_TABLE_W = 128
GQA_Q_TILE = 2048
GQA_KEY_CHUNK = 768
FREQ_TILE = 256


def _params(*sem):
    return pltpu.CompilerParams(dimension_semantics=sem, vmem_limit_bytes=VMEM_LIMIT_BYTES)


def _layer_norm(v):
    mu = jnp.mean(v, -1, keepdims=True)
    vc = v - mu
    var = jnp.mean(vc * vc, -1, keepdims=True)
    return vc * lax.rsqrt(var + LN_EPS)


def _sigmoid(v):
    return 1.0 / (1.0 + jnp.exp(-v))


def _dot(a, b):
    return jnp.dot(a, b, preferred_element_type=F32)


def _dot_nt(a, b):
    return lax.dot_general(a, b, (((1,), (1,)), ((), ())), preferred_element_type=F32)


V_EXT_W = 2 * HEAD_DIM


def _with_ones(v):
    return jnp.concatenate([v.astype(BF16), jnp.ones(v.shape, BF16)], axis=1)


def _normalise(o_ext):
    return o_ext[:, :HEAD_DIM] / o_ext[:, HEAD_DIM:]


def _mod_row(mod_ref, is_ctx, idx):
    return jnp.where(is_ctx, mod_ref[0, 1, idx:idx + 1, :], mod_ref[0, 0, idx:idx + 1, :])


def _sub_tiles(tm):
    for k in range(tm // TOKEN_TILE):
        rows = slice(k * TOKEN_TILE, (k + 1) * TOKEN_TILE)
        is_ctx = (pl.program_id(1) * tm + k * TOKEN_TILE) >= SEQ
        yield k, rows, is_ctx


def _stream_specs(tm, ctx_block, width=D_MODEL):
    n_sub = tm // TOKEN_TILE
    last_lat = SEQ // TOKEN_TILE - 1
    lat = [pl.BlockSpec((1, TOKEN_TILE, width),
                        lambda b, i, k=k: (b, jnp.minimum(i * n_sub + k, last_lat), 0)) for k in range(n_sub)]
    return lat + [pl.BlockSpec((1, TOKEN_TILE, width), lambda b, i: (b, ctx_block, 0))]


def _stream_args(lat, ctx, tm):
    return [lat] * (tm // TOKEN_TILE) + [ctx]


def _stream_tile(x_refs, k, is_ctx):
    return jnp.where(is_ctx, x_refs[-1][0], x_refs[k][0])


def _proj_body(*refs, tm):
    n_x = tm // TOKEN_TILE + 1
    x_refs = refs[:n_x]
    (mod_ref, w_ref, bd_ref, gain_ref, cos_ref, sin_ref,
     qa_ref, ka_ref, va_ref, qb_ref, kb_ref, vb_ref, z_ref) = refs[n_x:]
    n_rep = QK_W // ROPE_TABLE_W
    for k, rows, is_ctx in _sub_tiles(tm):
        xn = _layer_norm(_stream_tile(x_refs, k, is_ctx))
        hx = (xn * (1.0 + _mod_row(mod_ref, is_ctx, 1)) + _mod_row(mod_ref, is_ctx, 0)).astype(BF16)
        px = _dot(hx, w_ref[...])
        qk = px[:, :QK_W]
        sq = (qk * qk).astype(BF16)
        ss = jnp.concatenate(
            [_dot(sq[:, c:min(c + RMS_BLOCK, QK_W)], bd_ref[:min(RMS_BLOCK, QK_W - c), :min(RMS_BLOCK, QK_W - c)])
             for c in range(0, QK_W, RMS_BLOCK)], axis=1)
        qkn = qk * lax.rsqrt(ss * (1.0 / HEAD_DIM) + RMS_EPS) * gain_ref[...]
        lane = lax.broadcasted_iota(jnp.int32, qkn.shape, 1)
        partner = jnp.where((lane % HEAD_DIM) < HALF,
                            pltpu.roll(qkn, QK_W - HALF, 1), pltpu.roll(qkn, HALF, 1))
        cos = jnp.concatenate([cos_ref[rows, :]] * n_rep, axis=1)
        sin = jnp.concatenate([sin_ref[rows, :]] * n_rep, axis=1)
        qkr = (qkn * cos + partner * sin).astype(BF16)
        for h in range(A_HEADS):
            qa_ref[0, h, rows, :] = qkr[:, h * HEAD_DIM:(h + 1) * HEAD_DIM]
        for g in range(A_KV_HEADS):
            ka_ref[0, g, rows, :] = qkr[:, A_K0 + g * HEAD_DIM:A_K0 + (g + 1) * HEAD_DIM]
            va_ref[0, g, rows, :] = _with_ones(px[:, A_V0 + g * HEAD_DIM:A_V0 + (g + 1) * HEAD_DIM])
        for h in range(B_HEADS):
            qb_ref[0, h, rows, :] = (px[:, B_Q0 + h * HEAD_DIM:B_Q0 + (h + 1) * HEAD_DIM] * Q_SCALE).astype(BF16)
            kb_ref[0, h, rows, :] = px[:, B_K0 + h * HEAD_DIM:B_K0 + (h + 1) * HEAD_DIM].astype(BF16)
            vb_ref[0, h, rows, :] = _with_ones(px[:, B_V0 + h * HEAD_DIM:B_V0 + (h + 1) * HEAD_DIM])
        z_ref[0, rows, :] = px[:, C0:]


def _step_tokens(s):
    return 3 * TOKEN_TILE if s % (3 * TOKEN_TILE) == 0 else 2 * TOKEN_TILE


def _project(x_lat, x_ctx, ctx_block, mod, w_bf, bd, gain, cos_t, sin_t):
    bsz, d, s = x_lat.shape[0], D_MODEL, S_ALL
    tm = _step_tokens(s)
    nt = s // tm
    heads = lambda n: pl.BlockSpec((1, n, tm, HEAD_DIM), lambda b, i: (b, 0, i, 0))
    hshape = lambda n: jax.ShapeDtypeStruct((bsz, n, s, HEAD_DIM), BF16)
    vals = lambda n: pl.BlockSpec((1, n, tm, V_EXT_W), lambda b, i: (b, 0, i, 0))
    vshape = lambda n: jax.ShapeDtypeStruct((bsz, n, s, V_EXT_W), BF16)
    return pl.pallas_call(
        functools.partial(_proj_body, tm=tm),
        grid=(bsz, nt),
        in_specs=_stream_specs(tm, ctx_block) + [
            pl.BlockSpec((1, 2, 6, d), lambda b, i: (b, 0, 0, 0)),
            pl.BlockSpec((d, PROJ_W), lambda b, i: (0, 0)),
            pl.BlockSpec((RMS_BLOCK, RMS_BLOCK), lambda b, i: (0, 0)),
            pl.BlockSpec((1, QK_W), lambda b, i: (0, 0)),
            pl.BlockSpec((tm, ROPE_TABLE_W), lambda b, i: (i, 0)),
            pl.BlockSpec((tm, ROPE_TABLE_W), lambda b, i: (i, 0)),
        ],
        out_specs=[heads(A_HEADS), heads(A_KV_HEADS), vals(A_KV_HEADS),
                   heads(B_HEADS), heads(B_HEADS), vals(B_HEADS),
                   pl.BlockSpec((1, tm, 3 * C_CH), lambda b, i: (b, i, 0))],
        out_shape=[hshape(A_HEADS), hshape(A_KV_HEADS), vshape(A_KV_HEADS),
                   hshape(B_HEADS), hshape(B_HEADS), vshape(B_HEADS),
                   jax.ShapeDtypeStruct((bsz, s, 3 * C_CH), F32)],
        compiler_params=_params("parallel", "parallel"),
        name="proj_in",
    )(*_stream_args(x_lat, x_ctx, tm), mod, w_bf, bd, gain, cos_t, sin_t)


def _gqa_body(q_ref, k_ref, v_ref, o_ref):
    n_keys = k_ref.shape[2]
    for h in range(A_REP):
        q = q_ref[0, h]
        m = acc = None
        for j in range(n_keys // GQA_KEY_CHUNK):
            keys = slice(j * GQA_KEY_CHUNK, (j + 1) * GQA_KEY_CHUNK)
            s = _dot_nt(q, k_ref[0, 0, keys, :])
            mj = jnp.max(s, -1, keepdims=True)
            if j == 0:
                m = mj
                acc = _dot(jnp.exp2(s - m).astype(BF16), v_ref[0, 0, keys, :])
            else:
                m_new = jnp.maximum(m, mj)
                alpha = jnp.exp2(m - m_new)
                acc = alpha * acc + _dot(jnp.exp2(s - m_new).astype(BF16), v_ref[0, 0, keys, :])
                m = m_new
        o_ref[0, :, h * HEAD_DIM:(h + 1) * HEAD_DIM] = _normalise(acc).astype(BF16)


def _gqa(qa, ka, va):
    bsz, _, s, _ = qa.shape
    tq = GQA_Q_TILE
    return pl.pallas_call(
        _gqa_body,
        grid=(bsz, A_KV_HEADS, SEQ // tq),
        in_specs=[
            pl.BlockSpec((1, A_REP, tq, HEAD_DIM), lambda b, g, i: (b, g, i, 0)),
            pl.BlockSpec((1, 1, s, HEAD_DIM), lambda b, g, i: (b, g, 0, 0)),
            pl.BlockSpec((1, 1, s, V_EXT_W), lambda b, g, i: (b, g, 0, 0)),
        ],
        out_specs=pl.BlockSpec((1, tq, A_REP * HEAD_DIM), lambda b, g, i: (b, i, g)),
        out_shape=jax.ShapeDtypeStruct((bsz, SEQ, A_W), BF16),
        compiler_params=_params("parallel", "parallel", "parallel"),
        name="gqa_attn",
    )(qa, ka, va)


NA_HEADS_PER_STEP = 4
NA_Q_ROWS = 8
NA_K_ROWS = 16
NA_Q_TOKENS = NA_Q_ROWS * GRID_W
NA_K_TOKENS = NA_K_ROWS * GRID_W
NA_Q_BLOCKS = GRID_ROWS // NA_Q_ROWS
NA_KEY_ROW0 = (0, 4, 12, 16)
NA_PATTERN_BLOCKS = (0, 1, 3)


def _na_body(q_ref, k_ref, v_ref, bias_ref, o_ref):
    j = pl.program_id(2)
    key_row0 = jnp.where(j == 0, NA_KEY_ROW0[0],
                         jnp.where(j == 1, NA_KEY_ROW0[1], jnp.where(j == 2, NA_KEY_ROW0[2], NA_KEY_ROW0[3])))
    k0 = pl.multiple_of(key_row0 * GRID_W, 4 * GRID_W)
    for hh in range(NA_HEADS_PER_STEP):
        q = q_ref[0, hh]
        kw = k_ref[0, hh, pl.ds(k0, NA_K_TOKENS), :]
        vw = v_ref[0, hh, pl.ds(k0, NA_K_TOKENS), :]
        kc = k_ref[0, hh, SEQ:, :]
        vc = v_ref[0, hh, SEQ:, :]
        sw = _dot_nt(q, kw) + bias_ref[hh, 0]
        sc = _dot_nt(q, kc)
        m = jnp.maximum(jnp.max(sw, -1, keepdims=True), jnp.max(sc, -1, keepdims=True))
        o_ext = _dot(jnp.exp2(sw - m).astype(BF16), vw) + _dot(jnp.exp2(sc - m).astype(BF16), vc)
        o_ref[0, :, hh * HEAD_DIM:(hh + 1) * HEAD_DIM] = _normalise(o_ext).astype(BF16)


def _na(qb, kb, vb, bias):
    bsz, _, s, _ = qb.shape
    hp = NA_HEADS_PER_STEP
    kv = lambda w: pl.BlockSpec((1, hp, s, w), lambda b, h, j: (b, h, 0, 0))
    return pl.pallas_call(
        _na_body,
        grid=(bsz, B_HEADS // hp, NA_Q_BLOCKS),
        in_specs=[pl.BlockSpec((1, hp, NA_Q_TOKENS, HEAD_DIM), lambda b, h, j: (b, h, j, 0)),
                  kv(HEAD_DIM), kv(V_EXT_W),
                  pl.BlockSpec((hp, 1, NA_Q_TOKENS, NA_K_TOKENS), lambda b, h, j: (h, (j + 1) // 2, 0, 0))],
        out_specs=pl.BlockSpec((1, NA_Q_TOKENS, hp * HEAD_DIM), lambda b, h, j: (b, j, h)),
        out_shape=jax.ShapeDtypeStruct((bsz, SEQ, B_W), BF16),
        compiler_params=_params("parallel", "parallel", "arbitrary"),
        name="nbr_attn",
    )(qb, kb, vb, bias)


def _na_bias_body(t_ref, o_ref):
    o_ref[...] = jnp.full(o_ref.shape, NEG_BIG, F32)
    for p, blk in enumerate(NA_PATTERN_BLOCKS):
        @pl.when(pl.program_id(1) == p)
        def _(blk=blk):
            for qi in range(NA_Q_ROWS):
                qrow = blk * NA_Q_ROWS + qi
                rs = min(max(qrow - WIN_ROWS // 2, 0), GRID_ROWS - WIN_ROWS)
                for krow in range(rs, rs + WIN_ROWS):
                    ki = krow - NA_KEY_ROW0[blk]
                    o_ref[0, 0, qi * GRID_W:(qi + 1) * GRID_W, ki * GRID_W:(ki + 1) * GRID_W] = (
                        t_ref[0, krow - qrow + WIN_ROWS - 1])


def _na_bias_table(rpb):
    col = np.arange(GRID_W)
    col_start = np.clip(col - WIN_COLS // 2, 0, GRID_W - WIN_COLS)
    in_cols = (col[None, :] >= col_start[:, None]) & (col[None, :] < col_start[:, None] + WIN_COLS)
    dc = np.clip(col[None, :] - col[:, None] + WIN_COLS - 1, 0, 2 * WIN_COLS - 2)
    onehot = jnp.asarray(dc[None] == np.arange(2 * WIN_COLS - 1)[:, None, None], F32)
    tiles = jnp.einsum('hdj,jqk->hdqk', rpb.astype(F32), onehot, precision=HIGHEST)
    tiles = jnp.where(in_cols[None, None], tiles * LOG2E, NEG_BIG)
    n_pat = len(NA_PATTERN_BLOCKS)
    return pl.pallas_call(
        _na_bias_body,
        grid=(B_HEADS, n_pat),
        in_specs=[pl.BlockSpec((1, 2 * WIN_ROWS - 1, GRID_W, GRID_W), lambda h, p: (h, 0, 0, 0))],
        out_specs=pl.BlockSpec((1, 1, NA_Q_TOKENS, NA_K_TOKENS), lambda h, p: (h, p, 0, 0)),
        out_shape=jax.ShapeDtypeStruct((B_HEADS, n_pat, NA_Q_TOKENS, NA_K_TOKENS), F32),
        compiler_params=_params("parallel", "arbitrary"),
        name="nbr_bias",
    )(tiles)


def _ctx_attn_body(q_ref, k_ref, v_ref, o_ref, *, rep):
    for g in range(k_ref.shape[1]):
        for r in range(rep):
            hh = g * rep + r
            s = _dot_nt(q_ref[0, hh], k_ref[0, g])
            m = jnp.max(s, -1, keepdims=True)
            o_ext = _dot(jnp.exp2(s - m).astype(BF16), v_ref[0, g])
            o_ref[0, :, hh * HEAD_DIM:(hh + 1) * HEAD_DIM] = _normalise(o_ext).astype(BF16)


def _ctx_attn(q, k, v, kv_per_step):
    bsz, n_q, _, _ = q.shape
    n_kv = k.shape[1]
    rep = n_q // n_kv
    seg = lambda n, w=HEAD_DIM: pl.BlockSpec((1, n, CTX_LEN, w), lambda b, h: (b, h, SEQ // CTX_LEN, 0))
    return pl.pallas_call(
        functools.partial(_ctx_attn_body, rep=rep),
        grid=(bsz, n_kv // kv_per_step),
        in_specs=[seg(kv_per_step * rep), seg(kv_per_step), seg(kv_per_step, V_EXT_W)],
        out_specs=pl.BlockSpec((1, CTX_LEN, kv_per_step * rep * HEAD_DIM), lambda b, h: (b, 0, h)),
        out_shape=jax.ShapeDtypeStruct((bsz, CTX_LEN, n_q * HEAD_DIM), BF16),
        compiler_params=_params("parallel", "parallel"),
        name="ctx_attn",
    )(q, k, v)


DFT_FINE = 16


def _dft_table_body(chr_ref, shr_ref, clr_ref, slr_ref, chc_ref, shc_ref, clc_ref, slc_ref,
                    fwd_ref, inv_ref, *, n, tf):
    j = pl.program_id(0)

    def expand(c_hi, s_hi, c_lo, s_lo, rows):
        cos = (c_hi[:, None, :] * c_lo[None, :, :] - s_hi[:, None, :] * s_lo[None, :, :])
        sin = (s_hi[:, None, :] * c_lo[None, :, :] + c_hi[:, None, :] * s_lo[None, :, :])
        return cos.reshape(rows, -1), sin.reshape(rows, -1)

    cos_r, sin_r = expand(chr_ref[...], shr_ref[...], clr_ref[...], slr_ref[...], tf)
    cos_c, sin_c = expand(chc_ref[...], shc_ref[...], clc_ref[...], slc_ref[...], n)
    fr = lax.broadcasted_iota(jnp.int32, (tf, n), 0) + j * tf
    tr = lax.broadcasted_iota(jnp.int32, (tf, n), 1)
    sin_r = jnp.where(fr == 0, jnp.where(tr % 2 == 0, 1.0, -1.0), sin_r)
    tc = lax.broadcasted_iota(jnp.int32, (n, tf), 0)
    fcol = lax.broadcasted_iota(jnp.int32, (n, tf), 1) + j * tf
    sin_c = jnp.where(fcol == 0, jnp.where(tc % 2 == 0, 1.0, -1.0), sin_c)
    fwd_ref[0, :tf, :] = cos_r.astype(BF16)
    fwd_ref[0, tf:, :] = sin_r.astype(BF16)
    inv_ref[0, :, :tf] = cos_c.astype(BF16)
    inv_ref[0, :, tf:] = sin_c.astype(BF16)


def _dft_tables(n):
    s = jnp.arange(n, dtype=jnp.int32)[None, :]

    def cos_sin(f):
        ang = ((f * s) % (2 * n)).astype(F32) * (math.pi / n)
        return jnp.cos(ang), jnp.sin(ang)

    ch, sh = cos_sin(jnp.arange(n // DFT_FINE, dtype=jnp.int32)[:, None] * DFT_FINE)
    cl, sl = cos_sin(jnp.arange(DFT_FINE, dtype=jnp.int32)[:, None])
    tf = min(FREQ_TILE, n)
    nf = n // tf
    hi_rows = pl.BlockSpec((tf // DFT_FINE, n), lambda j: (j, 0))
    lo_rows = pl.BlockSpec((DFT_FINE, n), lambda j: (0, 0))
    hi_cols = pl.BlockSpec((n // DFT_FINE, tf), lambda j: (0, j))
    lo_cols = pl.BlockSpec((DFT_FINE, tf), lambda j: (0, j))
    return pl.pallas_call(
        functools.partial(_dft_table_body, n=n, tf=tf),
        grid=(nf,),
        in_specs=[hi_rows, hi_rows, lo_rows, lo_rows, hi_cols, hi_cols, lo_cols, lo_cols],
        out_specs=[pl.BlockSpec((1, 2 * tf, n), lambda j: (j, 0, 0)),
                   pl.BlockSpec((1, n, 2 * tf), lambda j: (j, 0, 0))],
        out_shape=[jax.ShapeDtypeStruct((nf, 2 * tf, n), BF16),
                   jax.ShapeDtypeStruct((nf, n, 2 * tf), BF16)],
        compiler_params=_params("parallel"),
        name="dft_tables",
    )(ch, sh, cl, sl, ch, sh, cl, sl)


def _split_bf16(v):
    hi = v.astype(BF16)
    lo = (v - hi.astype(F32)).astype(BF16)
    return hi, lo


def _spectrum_body(fwd_ref, kp_ref, km_ref, kc_ref, ks_ref, *, tf):
    kp_hi, kp_lo = _split_bf16(kp_ref[...])
    km_hi, km_lo = _split_bf16(km_ref[...])
    fc = fwd_ref[0, :tf, :]
    fs = fwd_ref[0, tf:, :]
    kc_ref[...] = _dot(fc, kp_hi) + _dot(fc, kp_lo)
    ks_sin = _dot(fs, km_hi) + _dot(fs, km_lo)
    ks_nyq = _dot(fs, kp_hi) + _dot(fs, kp_lo)
    row = lax.broadcasted_iota(jnp.int32, ks_sin.shape, 0) + pl.program_id(0) * tf
    ks_ref[...] = jnp.where(row == 0, ks_nyq, ks_sin)


def _filter_spectrum(fwd, kplus, kminus):
    nf, tf2, n = fwd.shape
    tf = tf2 // 2
    filt = pl.BlockSpec((n, C_CH), lambda j: (0, 0))
    out = pl.BlockSpec((tf, C_CH), lambda j: (j, 0))
    return pl.pallas_call(
        functools.partial(_spectrum_body, tf=tf),
        grid=(nf,),
        in_specs=[pl.BlockSpec((1, tf2, n), lambda j: (j, 0, 0)), filt, filt],
        out_specs=[out, out],
        out_shape=[jax.ShapeDtypeStruct((n, C_CH), F32)] * 2,
        compiler_params=_params("parallel"),
        name="hyena_spectrum",
    )(fwd, kplus, kminus)


def _short_conv(z, cw_ref, cb_ref, cols, n):
    pos = lax.broadcasted_iota(jnp.int32, z.shape, 0)
    prev = jnp.where(pos == 0, 0.0, pltpu.roll(z, 1, 0))
    nxt = jnp.where(pos == n - 1, 0.0, pltpu.roll(z, n - 1, 0))
    return prev * cw_ref[0:1, cols] + z * cw_ref[1:2, cols] + nxt * cw_ref[2:3, cols] + cb_ref[:, cols]


def _hyena_gate_body(z_ref, cw_ref, cb_ref, skip_ref, u_ref, us_ref, *, n):
    zc = _short_conv(z_ref[0, :, C_CH:], cw_ref, cb_ref, slice(C_CH, 3 * C_CH), n)
    u = zc[:, C_CH:] * zc[:, :C_CH]
    u_ref[0] = u.astype(BF16)
    us_ref[0] = u * skip_ref[...]


def _hyena_gate(z, seg_start, n, cw, cb, skip):
    bsz = z.shape[0]
    seg = seg_start // n
    small = lambda r, c: pl.BlockSpec((r, c), lambda b: (0, 0))
    out = pl.BlockSpec((1, n, C_CH), lambda b: (b, 0, 0))
    return pl.pallas_call(
        functools.partial(_hyena_gate_body, n=n),
        grid=(bsz,),
        in_specs=[pl.BlockSpec((1, n, 3 * C_CH), lambda b: (b, seg, 0)),
                  small(3, 3 * C_CH), small(1, 3 * C_CH), small(1, C_CH)],
        out_specs=[out, out],
        out_shape=[jax.ShapeDtypeStruct((bsz, n, C_CH), BF16),
                   jax.ShapeDtypeStruct((bsz, n, C_CH), F32)],
        compiler_params=_params("parallel"),
        name="hyena_gate",
    )(z, cw, cb, skip)


def _hyena_body(u_ref, us_ref, z_ref, cw_ref, cb_ref, fwd_ref, inv_ref, kc_ref, ks_ref, o_ref, *, n, tf):
    u = u_ref[0]
    y = None
    for j in range(n // tf):
        freq = slice(j * tf, (j + 1) * tf)
        kc = kc_ref[freq, :]
        ks = ks_ref[freq, :]
        spec = _dot(fwd_ref[j], u)
        uc, us = spec[:tf], spec[tf:]
        if j == 0:
            first = lax.broadcasted_iota(jnp.int32, kc.shape, 0) == 0
            wgt = jnp.where(first, 1.0, 2.0) * (1.0 / (2 * n))
            re = jnp.where(first, uc * kc, uc * kc - us * ks) * wgt
            im = jnp.where(first, us * ks, uc * ks + us * kc) * wgt
        else:
            re = (uc * kc - us * ks) * (1.0 / n)
            im = (uc * ks + us * kc) * (1.0 / n)
        part = _dot(inv_ref[j], jnp.concatenate([re, im], axis=0).astype(BF16))
        y = part if y is None else y + part
    x0 = _short_conv(z_ref[0], cw_ref, cb_ref, slice(0, C_CH), n)
    o_ref[0] = ((y + us_ref[0]) * x0).astype(BF16)


def _hyena(u, us, z, seg_start, n, cw, cb, fwd, inv, kc, ks):
    bsz = u.shape[0]
    tf = min(FREQ_TILE, n)
    nf = n // tf
    seg = seg_start // n
    seq = lambda **kw: pl.BlockSpec((1, n, C_CH), lambda b: (b, 0, 0), **kw)
    once = dict(pipeline_mode=pl.Buffered(1))
    small = lambda r, c: pl.BlockSpec((r, c), lambda b: (0, 0))
    return pl.pallas_call(
        functools.partial(_hyena_body, n=n, tf=tf),
        grid=(bsz,),
        in_specs=[seq(), seq(),
                  pl.BlockSpec((1, n, C_CH), lambda b: (b, seg, 0)),
                  small(3, 3 * C_CH), small(1, 3 * C_CH),
                  pl.BlockSpec((nf, 2 * tf, n), lambda b: (0, 0, 0), **once),
                  pl.BlockSpec((nf, n, 2 * tf), lambda b: (0, 0, 0), **once),
                  pl.BlockSpec((n, C_CH), lambda b: (0, 0), **once),
                  pl.BlockSpec((n, C_CH), lambda b: (0, 0), **once)],
        out_specs=seq(),
        out_shape=jax.ShapeDtypeStruct((bsz, n, C_CH), BF16),
        compiler_params=_params("parallel"),
        name="hyena_conv",
    )(u, us, z, cw, cb, fwd, inv, kc, ks)


def _dot3(a, b):
    a_hi, a_lo = _split_bf16(a)
    b_hi, b_lo = _split_bf16(b)
    return _dot(a_hi, b_hi) + _dot(a_hi, b_lo) + _dot(a_lo, b_hi)


POS_PAD = 128
FILTER_ROWS = 256


def _filter_body(pos_ref, decay_ref, w1_ref, b1_ref, w2_ref, b2_ref, w3_ref, freq_ref, kp_ref, km_ref):
    h = jnp.sin(freq_ref[0:1, :] * (_dot3(pos_ref[...], w1_ref[...]) + b1_ref[...]))
    h = jnp.sin(freq_ref[1:2, :] * (_dot3(h, w2_ref[...]) + b2_ref[...]))
    h = _dot3(h, w3_ref[...])
    decay = decay_ref[...]
    hf = h[:, :C_CH] * decay
    hb = h[:, C_CH:] * decay
    lag = lax.broadcasted_iota(jnp.int32, hb.shape, 0) + pl.program_id(0) * FILTER_ROWS
    hb = jnp.where(lag == 0, 0.0, hb)
    kp_ref[...] = hf + hb
    km_ref[...] = hf - hb


def _implicit_filters(n, w1, b1, w2, b2, w3, freq):
    t = jnp.linspace(0.0, 1.0, n, dtype=F32)[:, None]
    w = 2 * math.pi * jnp.arange(n, dtype=F32)[:, None] / n
    f = jnp.linspace(1e-4, C_POS_BANDS - 1, C_POS_BANDS, dtype=F32)[None]
    pos = jnp.concatenate([t, jnp.cos(f * w), -jnp.sin(f * w)], -1)
    n_pos = pos.shape[1]
    pos = jnp.pad(pos, ((0, 0), (0, POS_PAD - n_pos)))
    w1p = jnp.pad(w1, ((0, POS_PAD - n_pos), (0, 0)))
    lo = math.log(C_DECAY_TARGET) / C_SLOW_DECAY
    hi = math.log(C_DECAY_TARGET) / C_FAST_DECAY
    deltas = jnp.abs(jnp.linspace(lo, hi, C_CH, dtype=F32))
    decay = jnp.exp(-t * deltas[None])
    hid = w2.shape[0]
    rows = lambda c: pl.BlockSpec((FILTER_ROWS, c), lambda i: (i, 0))
    full = lambda r, c: pl.BlockSpec((r, c), lambda i: (0, 0))
    return pl.pallas_call(
        _filter_body,
        grid=(n // FILTER_ROWS,),
        in_specs=[rows(POS_PAD), rows(C_CH), full(POS_PAD, hid), full(1, hid), full(hid, hid), full(1, hid),
                  full(hid, 2 * C_CH), full(2, hid)],
        out_specs=[rows(C_CH), rows(C_CH)],
        out_shape=[jax.ShapeDtypeStruct((n, C_CH), F32)] * 2,
        compiler_params=_params("parallel"),
        name="hyena_filter",
    )(pos, decay, w1p, b1[None], w2, b2[None], w3, freq)


def _router_gates(logits_t, rb):
    tm = logits_t.shape[1]
    scores = _sigmoid(logits_t)
    sel = scores + rb
    sel3 = sel.reshape(N_GROUPS, GROUP_SIZE, tm)
    kk = lax.broadcasted_iota(jnp.int32, sel3.shape, 1)
    m1 = jnp.max(sel3, 1, keepdims=True)
    i1 = jnp.min(jnp.where(sel3 == m1, kk, GROUP_SIZE), 1, keepdims=True)
    m2 = jnp.max(jnp.where(kk == i1, -jnp.inf, sel3), 1, keepdims=True)
    gscore = (m1 + m2).reshape(N_GROUPS, tm)
    gid = lax.broadcasted_iota(jnp.int32, gscore.shape, 0)
    rank = jnp.zeros(gscore.shape, jnp.int32)
    for g in range(N_GROUPS):
        other = gscore[g:g + 1, :]
        ahead = (other > gscore) | ((other == gscore) & (g < gid))
        rank = rank + ahead.astype(jnp.int32)
    gsel = (rank < TOPK_GROUPS).reshape(N_GROUPS, 1, tm)
    cand = jnp.where(gsel, sel3, -jnp.inf).reshape(N_EXPERTS, tm)
    eid = lax.broadcasted_iota(jnp.int32, cand.shape, 0)
    chosen = jnp.zeros(cand.shape, jnp.bool_)
    for _ in range(TOP_K):
        m = jnp.max(cand, 0, keepdims=True)
        pick = eid == jnp.min(jnp.where(cand == m, eid, N_EXPERTS), 0, keepdims=True)
        chosen = chosen | pick
        cand = jnp.where(pick, -jnp.inf, cand)
    wsel = jnp.where(chosen, scores, 0.0)
    return wsel / jnp.sum(wsel, 0, keepdims=True) * ROUTED_SCALE


def _out_body(*refs, tm):
    n_x = tm // TOKEN_TILE + 1
    x_refs, oa_refs, ob_refs, oc_refs = (refs[j * n_x:(j + 1) * n_x] for j in range(4))
    (mod_ref, wa_ref, wb_ref, wc_ref, lng_ref, lnb_ref,
     rwt_ref, rb_ref, xmid_ref, h_ref, gate_ref) = refs[4 * n_x:]
    for k, rows, is_ctx in _sub_tiles(tm):
        attn = (_dot(_stream_tile(oa_refs, k, is_ctx), wa_ref[...])
                + _dot(_stream_tile(ob_refs, k, is_ctx), wb_ref[...])
                + _dot(_stream_tile(oc_refs, k, is_ctx), wc_ref[...]))
        y = DN_ALPHA * _stream_tile(x_refs, k, is_ctx) + _mod_row(mod_ref, is_ctx, 2) * attn
        xm = _layer_norm(y) * lng_ref[...] + lnb_ref[...]
        xmid_ref[0, rows, :] = xm
        h = _layer_norm(xm) * (1.0 + _mod_row(mod_ref, is_ctx, 4)) + _mod_row(mod_ref, is_ctx, 3)
        hi, lo = _split_bf16(h)
        h_ref[0, rows, :] = hi
        logits_t = _dot_nt(rwt_ref[...], jnp.concatenate([hi, lo, hi], axis=1))
        gates_t = _router_gates(logits_t, rb_ref[...])
        pad = jnp.zeros((128 - N_EXPERTS, gates_t.shape[1]), F32)
        gates = jnp.concatenate([gates_t, pad], axis=0).T
        for g in range(N_GROUPS):
            gate_ref[0, g, rows, :] = gates[:, g * GROUP_SIZE:(g + 1) * GROUP_SIZE]


def _out_proj(s, x_lat, x_ctx, ctx_block, oa, ob, oc, mod, wa, wb, wc, lng, lnb, rwt3, rb):
    bsz = x_lat.shape[0]
    d = D_MODEL
    tm = _step_tokens(s)
    tok = lambda w: pl.BlockSpec((1, tm, w), lambda b, i: (b, i, 0))
    full = lambda r, c: pl.BlockSpec((r, c), lambda b, i: (0, 0))
    return pl.pallas_call(
        functools.partial(_out_body, tm=tm),
        grid=(bsz, s // tm),
        in_specs=_stream_specs(tm, ctx_block) + _stream_specs(tm, 0, A_W) + _stream_specs(tm, 0, B_W)
        + _stream_specs(tm, 0, C_CH) + [
                  pl.BlockSpec((1, 2, 6, d), lambda b, i: (b, 0, 0, 0)),
                  full(A_W, d), full(B_W, d), full(C_CH, d), full(1, d), full(1, d),
                  full(N_EXPERTS, 3 * d), full(N_EXPERTS, 1)],
        out_specs=[tok(d), tok(d),
                   pl.BlockSpec((1, N_GROUPS, tm, GROUP_SIZE), lambda b, i: (b, 0, i, 0))],
        out_shape=[jax.ShapeDtypeStruct((bsz, s, d), F32),
                   jax.ShapeDtypeStruct((bsz, s, d), BF16),
                   jax.ShapeDtypeStruct((bsz, N_GROUPS, s, GROUP_SIZE), F32)],
        compiler_params=_params("parallel", "parallel"),
        name="proj_out_router",
    )(*_stream_args(x_lat, x_ctx, tm), *_stream_args(*oa, tm), *_stream_args(*ob, tm), *_stream_args(*oc, tm),
      mod, wa, wb, wc, lng, lnb, rwt3, rb)


MOE_CHAINS = 2


def _moe_body(h_ref, gate_ref, xmid_ref, mod_ref, w1_ref, w3_ref, w2_ref,
              sw1_ref, sw3_ref, sw2_ref, lng_ref, lnb_ref, o_ref, acc_ref, *, tm):
    g = pl.program_id(2)
    rc = tm // MOE_CHAINS

    @pl.when(g == 0)
    def _():
        h = h_ref[0]
        a = _dot(h, sw1_ref[...])
        act = (a * _sigmoid(a) * _dot(h, sw3_ref[...])).astype(BF16)
        acc_ref[...] = _dot(act, sw2_ref[...])

    w1 = jnp.concatenate([w1_ref[k] for k in range(GROUP_SIZE)], axis=1)
    w3 = jnp.concatenate([w3_ref[k] for k in range(GROUP_SIZE)], axis=1)
    w2 = w2_ref[...].reshape(GROUP_SIZE * EXPERT_HIDDEN, D_MODEL)
    for c in range(MOE_CHAINS):
        rows = slice(c * rc, (c + 1) * rc)
        h = h_ref[0, rows, :]
        a = _dot(h, w1)
        act = a * _sigmoid(a) * _dot(h, w3)
        gt = gate_ref[0, 0, rows, :]
        parts = [act[:, k * EXPERT_HIDDEN:(k + 1) * EXPERT_HIDDEN] * gt[:, k:k + 1]
                 for k in range(GROUP_SIZE)]
        acc_ref[rows, :] += _dot(jnp.concatenate(parts, axis=1).astype(BF16), w2)

    @pl.when(g == N_GROUPS - 1)
    def _():
        row = lax.broadcasted_iota(jnp.int32, (tm, 1), 0) + pl.program_id(1) * tm
        gate2 = jnp.where(row >= SEQ, mod_ref[0, 1, 5:6, :], mod_ref[0, 0, 5:6, :])
        y = DN_ALPHA * xmid_ref[0] + gate2 * acc_ref[...]
        o_ref[0] = _layer_norm(y) * lng_ref[...] + lnb_ref[...]


MOE_TILES_PER_SAMPLE = 2


def _moe(h, gates, xmid, mod, layer, w1_all, w3_all, w2_all, sw1, sw3, sw2, lng, lnb):
    bsz, s, d = h.shape
    tm = s // MOE_TILES_PER_SAMPLE
    tok = pl.BlockSpec((1, tm, d), lambda b, i, g: (b, i, 0))
    tok_once = tok
    full = lambda r, c: pl.BlockSpec((r, c), lambda b, i, g: (0, 0))
    up = pl.BlockSpec((None, GROUP_SIZE, d, EXPERT_HIDDEN), lambda b, i, g: (layer, g, 0, 0))
    return pl.pallas_call(
        functools.partial(_moe_body, tm=tm),
        grid=(bsz, s // tm, N_GROUPS),
        in_specs=[tok,
                  pl.BlockSpec((1, 1, tm, GROUP_SIZE), lambda b, i, g: (b, g, i, 0)),
                  tok_once,
                  pl.BlockSpec((1, 2, 6, d), lambda b, i, g: (b, 0, 0, 0)),
                  up, up,
                  pl.BlockSpec((None, GROUP_SIZE, EXPERT_HIDDEN, d), lambda b, i, g: (layer, g, 0, 0)),
                  full(d, SHARED_HIDDEN), full(d, SHARED_HIDDEN), full(SHARED_HIDDEN, d),
                  full(1, d), full(1, d)],
        out_specs=tok,
        out_shape=jax.ShapeDtypeStruct((bsz, s, d), F32),
        scratch_shapes=[pltpu.VMEM((tm, d), F32)],
        compiler_params=_params("parallel", "parallel", "arbitrary"),
        name="moe_ffn",
    )(h, gates, xmid, mod, w1_all, w3_all, w2_all, sw1, sw3, sw2, lng, lnb)


def _rope_tables():
    t = jnp.arange(SEQ)
    row = (t // GRID_W).astype(F32)
    col = (t % GRID_W).astype(F32)
    inv = ROPE_THETA ** (-jnp.arange(0, HALF, 2, dtype=F32) / HALF)
    ang = jnp.concatenate([row[:, None] * inv, col[:, None] * inv], -1)
    cos, sin = jnp.cos(ang), jnp.sin(ang)
    n_heads = ROPE_TABLE_W // HEAD_DIM
    cos_t = jnp.concatenate([jnp.tile(jnp.concatenate([cos, cos], -1), (1, n_heads)),
                             jnp.ones((CTX_LEN, ROPE_TABLE_W), F32)], 0)
    sin_t = jnp.concatenate([jnp.tile(jnp.concatenate([-sin, sin], -1), (1, n_heads)),
                             jnp.zeros((CTX_LEN, ROPE_TABLE_W), F32)], 0)
    return cos_t, sin_t


MOD_ROWS = 16
MOD_COL_TILE = 1536


def _modulation_body(c_ref, w_ref, b_ref, o_ref):
    cv = c_ref[...]
    o_ref[0] = _dot3(cv * _sigmoid(cv), w_ref[0]) + b_ref[0]


def _modulation(c, c_ctx, ada_w, ada_b):
    bsz = c.shape[0]
    assert bsz + 1 <= MOD_ROWS, bsz
    n_out = 6 * D_MODEL
    cvec = jnp.concatenate([c, c_ctx[None], jnp.zeros((MOD_ROWS - bsz - 1, D_MODEL), F32)], axis=0)
    m = pl.pallas_call(
        _modulation_body,
        grid=(DEPTH, n_out // MOD_COL_TILE),
        in_specs=[pl.BlockSpec((MOD_ROWS, D_MODEL), lambda l, j: (0, 0)),
                  pl.BlockSpec((1, D_MODEL, MOD_COL_TILE), lambda l, j: (l, 0, j)),
                  pl.BlockSpec((1, 1, MOD_COL_TILE), lambda l, j: (l, 0, j))],
        out_specs=pl.BlockSpec((1, MOD_ROWS, MOD_COL_TILE), lambda l, j: (l, 0, j)),
        out_shape=jax.ShapeDtypeStruct((DEPTH, MOD_ROWS, n_out), F32),
        compiler_params=_params("parallel", "parallel"),
        name="ada_modulation",
    )(cvec, ada_w, ada_b[:, None, :])
    m = m.reshape(DEPTH, MOD_ROWS, 6, D_MODEL)
    lat = m[:, :bsz]
    return jnp.stack([lat, jnp.broadcast_to(m[:, bsz:bsz + 1], lat.shape)], axis=2)


def kernel(x, c, ctx, c_ctx, ada_w, ada_b, w_in, q_gain, k_gain, nat_rpb, hy_conv_w, hy_conv_b, hy_f_w1, hy_f_b1, hy_f_w2, hy_f_b2, hy_f_w3, hy_freq, hy_skip, w_out, ln1_g, ln1_b, router_w, router_b, exp_w1, exp_w3, exp_w2, sh_w1, sh_w3, sh_w2, ln2_g, ln2_b):
    bsz = x.shape[0]
    d = D_MODEL
    x_lat, x_ctx, ctx_block = x, ctx, 0

    head_perm = np.concatenate([np.arange(0, HEAD_DIM, 2), np.arange(1, HEAD_DIM, 2)])
    qk_perm = (np.arange(QK_W // HEAD_DIM)[:, None] * HEAD_DIM + head_perm[None, :]).reshape(-1)
    perm_mat = jnp.asarray(np.eye(QK_W)[:, qk_perm], BF16)
    cos_t, sin_t = _rope_tables()
    bd = jnp.asarray(np.kron(np.eye(RMS_BLOCK // HEAD_DIM), np.ones((HEAD_DIM, HEAD_DIM))), BF16)
    tables_lat = _dft_tables(SEQ)
    tables_ctx = _dft_tables(CTX_LEN)
    mod_all = _modulation(c, c_ctx, ada_w, ada_b)
    w1_bf, w3_bf, w2_bf = exp_w1.astype(BF16), exp_w3.astype(BF16), exp_w2.astype(BF16)

    for l in range(DEPTH):
        last = l == DEPTH - 1
        mod = mod_all[l]

        w_bf = w_in[l].astype(BF16)
        w_qk = jnp.dot(w_bf[:, :QK_W], perm_mat, preferred_element_type=F32).astype(BF16)
        w_bf = jnp.concatenate([w_qk, w_bf[:, QK_W:]], axis=1)
        gain = jnp.concatenate([jnp.tile(q_gain[l][head_perm] * Q_SCALE, A_HEADS),
                                jnp.tile(k_gain[l][head_perm], A_KV_HEADS)])[None].astype(F32)
        qa, ka, va, qb, kb, vb, z = _project(x_lat, x_ctx, ctx_block, mod, w_bf, bd, gain, cos_t, sin_t)

        oa = [_gqa(qa, ka, va)]
        ob = [_na(qb, kb, vb, _na_bias_table(nat_rpb[l]))]
        if not last:
            oa.append(_ctx_attn(qa, ka, va, 1))
            ob.append(_ctx_attn(qb, kb, vb, NA_HEADS_PER_STEP))

        filt = (hy_f_w1[l], hy_f_b1[l], hy_f_w2[l], hy_f_b2[l], hy_f_w3[l], hy_freq[l])
        cw, cb, skip = hy_conv_w[l], hy_conv_b[l][None], hy_skip[l][None]
        segments = [(0, SEQ, tables_lat)] + ([] if last else [(SEQ, CTX_LEN, tables_ctx)])
        oc = []
        for seg_start, n, (fwd, inv) in segments:
            kc, ks = _filter_spectrum(fwd, *_implicit_filters(n, *filt))
            u, us = _hyena_gate(z, seg_start, n, cw, cb, skip)
            oc.append(_hyena(u, us, z, seg_start, n, cw, cb, fwd, inv, kc, ks))
        oa, ob, oc = ((o[0], o[-1]) for o in (oa, ob, oc))

        wo = w_out[l].astype(BF16)
        rw_hi, rw_lo = _split_bf16(router_w[l].T)
        rwt3 = jnp.concatenate([rw_hi, rw_hi, rw_lo], axis=1)
        xmid, h, gates = _out_proj(SEQ if last else S_ALL, x_lat, x_ctx, ctx_block, oa, ob, oc, mod,
                                   wo[:A_W], wo[A_W:A_W + B_W], wo[A_W + B_W:],
                                   ln1_g[l][None], ln1_b[l][None], rwt3, router_b[l][:, None])

        xs = _moe(h, gates, xmid, mod, l, w1_bf, w3_bf, w2_bf,
                  sh_w1[l].astype(BF16), sh_w3[l].astype(BF16), sh_w2[l].astype(BF16),
                  ln2_g[l][None], ln2_b[l][None])
        x_lat, x_ctx, ctx_block = xs, xs, SEQ // TOKEN_TILE
    return xs
```

```python
import functools
import math

import numpy as np
import jax
import jax.numpy as jnp
from jax import lax
from jax.experimental import pallas as pl
from jax.experimental.pallas import tpu as pltpu

F32 = jnp.float32
BF16 = jnp.bfloat16
HIGHEST = lax.Precision.HIGHEST

D_MODEL = 1024
SEQ = 2048
DEPTH = 2
GRID_W = 64
GRID_ROWS = SEQ // GRID_W
CTX_LEN = 256
S_ALL = SEQ + CTX_LEN
HEAD_DIM = 64
HALF = HEAD_DIM // 2
A_HEADS = 8
A_KV_HEADS = 2
A_REP = A_HEADS // A_KV_HEADS
ROPE_THETA = 10000.0
B_HEADS = 4
WIN_ROWS = 8
WIN_COLS = 16
C_CH = 256
C_POS_BANDS = 16
C_DECAY_TARGET = 1e-2
C_FAST_DECAY = 0.3
C_SLOW_DECAY = 1.5
A_W = A_HEADS * HEAD_DIM
A_KV_W = A_KV_HEADS * HEAD_DIM
B_W = B_HEADS * HEAD_DIM
A_K0 = A_W
A_V0 = A_K0 + A_KV_W
B_Q0 = A_V0 + A_KV_W
B_K0 = B_Q0 + B_W
B_V0 = B_K0 + B_W
C0 = B_V0 + B_W
PROJ_W = C0 + 3 * C_CH
QK_W = A_W + A_KV_W
N_EXPERTS = 64
TOP_K = 8
N_GROUPS = 8
GROUP_SIZE = N_EXPERTS // N_GROUPS
TOPK_GROUPS = 4
EXPERT_HIDDEN = 128
SHARED_HIDDEN = 256
ROUTED_SCALE = 2.5
DN_ALPHA = (2 * DEPTH) ** 0.25
LN_EPS = 1e-6
RMS_EPS = 1e-6
LOG2E = math.log2(math.e)
Q_SCALE = HEAD_DIM ** -0.5 * LOG2E
NEG_BIG = -1e30

VMEM_LIMIT_BYTES = 56 * 1024 * 1024
TOKEN_TILE = 256
RMS_BLOCK = 256
ROPE_TABLE_W = 128
GQA_Q_TILE = 2048
GQA_KEY_CHUNK = 768
FREQ_TILE = 256


def _params(*sem):
    return pltpu.CompilerParams(dimension_semantics=sem, vmem_limit_bytes=VMEM_LIMIT_BYTES)


def _layer_norm(v):
    mu = jnp.mean(v, -1, keepdims=True)
    vc = v - mu
    var = jnp.mean(vc * vc, -1, keepdims=True)
    return vc * lax.rsqrt(var + LN_EPS)


def _sigmoid(v):
    return 1.0 / (1.0 + jnp.exp(-v))


def _dot(a, b):
    return jnp.dot(a, b, preferred_element_type=F32)


def _dot_nt(a, b):
    return lax.dot_general(a, b, (((1,), (1,)), ((), ())), preferred_element_type=F32)


V_EXT_W = 2 * HEAD_DIM


def _with_ones(v):
    return jnp.concatenate([v.astype(BF16), jnp.ones(v.shape, BF16)], axis=1)


def _normalise(o_ext):
    return o_ext[:, :HEAD_DIM] / o_ext[:, HEAD_DIM:]


def _mod_row(mod_ref, is_ctx, idx):
    return jnp.where(is_ctx, mod_ref[0, 1, idx:idx + 1, :], mod_ref[0, 0, idx:idx + 1, :])


def _sub_tiles(tm):
    for k in range(tm // TOKEN_TILE):
        rows = slice(k * TOKEN_TILE, (k + 1) * TOKEN_TILE)
        is_ctx = (pl.program_id(1) * tm + k * TOKEN_TILE) >= SEQ
        yield k, rows, is_ctx


def _stream_specs(tm, ctx_block, width=D_MODEL):
    n_sub = tm // TOKEN_TILE
    last_lat = SEQ // TOKEN_TILE - 1
    lat = [pl.BlockSpec((1, TOKEN_TILE, width),
                        lambda b, i, k=k: (b, jnp.minimum(i * n_sub + k, last_lat), 0)) for k in range(n_sub)]
    return lat + [pl.BlockSpec((1, TOKEN_TILE, width), lambda b, i: (b, ctx_block, 0))]


def _stream_args(lat, ctx, tm):
    return [lat] * (tm // TOKEN_TILE) + [ctx]


def _stream_tile(x_refs, k, is_ctx):
    return jnp.where(is_ctx, x_refs[-1][0], x_refs[k][0])


def _proj_body(*refs, tm):
    n_x = tm // TOKEN_TILE + 1
    x_refs = refs[:n_x]
    (mod_ref, w_ref, bd_ref, gain_ref, cos_ref, sin_ref,
     qa_ref, ka_ref, va_ref, qb_ref, kb_ref, vb_ref, z_ref) = refs[n_x:]
    n_rep = QK_W // ROPE_TABLE_W
    for k, rows, is_ctx in _sub_tiles(tm):
        xn = _layer_norm(_stream_tile(x_refs, k, is_ctx))
        hx = (xn * (1.0 + _mod_row(mod_ref, is_ctx, 1)) + _mod_row(mod_ref, is_ctx, 0)).astype(BF16)
        px = _dot(hx, w_ref[...])
        qk = px[:, :QK_W]
        sq = (qk * qk).astype(BF16)
        ss = jnp.concatenate(
            [_dot(sq[:, c:min(c + RMS_BLOCK, QK_W)], bd_ref[:min(RMS_BLOCK, QK_W - c), :min(RMS_BLOCK, QK_W - c)])
             for c in range(0, QK_W, RMS_BLOCK)], axis=1)
        qkn = qk * lax.rsqrt(ss * (1.0 / HEAD_DIM) + RMS_EPS) * gain_ref[...]
        lane = lax.broadcasted_iota(jnp.int32, qkn.shape, 1)
        partner = jnp.where((lane % HEAD_DIM) < HALF,
                            pltpu.roll(qkn, QK_W - HALF, 1), pltpu.roll(qkn, HALF, 1))
        cos = jnp.concatenate([cos_ref[rows, :]] * n_rep, axis=1)
        sin = jnp.concatenate([sin_ref[rows, :]] * n_rep, axis=1)
        qkr = (qkn * cos + partner * sin).astype(BF16)
        for h in range(A_HEADS):
            qa_ref[0, h, rows, :] = qkr[:, h * HEAD_DIM:(h + 1) * HEAD_DIM]
        for g in range(A_KV_HEADS):
            ka_ref[0, g, rows, :] = qkr[:, A_K0 + g * HEAD_DIM:A_K0 + (g + 1) * HEAD_DIM]
            va_ref[0, g, rows, :] = _with_ones(px[:, A_V0 + g * HEAD_DIM:A_V0 + (g + 1) * HEAD_DIM])
        for h in range(B_HEADS):
            qb_ref[0, h, rows, :] = (px[:, B_Q0 + h * HEAD_DIM:B_Q0 + (h + 1) * HEAD_DIM] * Q_SCALE).astype(BF16)
            kb_ref[0, h, rows, :] = px[:, B_K0 + h * HEAD_DIM:B_K0 + (h + 1) * HEAD_DIM].astype(BF16)
            vb_ref[0, h, rows, :] = _with_ones(px[:, B_V0 + h * HEAD_DIM:B_V0 + (h + 1) * HEAD_DIM])
        z_ref[0, rows, :] = px[:, C0:]


def _step_tokens(s):
    return 4 * TOKEN_TILE if s % (4 * TOKEN_TILE) == 0 else 3 * TOKEN_TILE


def _project(x_lat, x_ctx, ctx_block, mod, w_bf, bd, gain, cos_t, sin_t):
    bsz, d, s = x_lat.shape[0], D_MODEL, S_ALL
    tm = _step_tokens(s)
    nt = s // tm
    heads = lambda n: pl.BlockSpec((1, n, tm, HEAD_DIM), lambda b, i: (b, 0, i, 0))
    hshape = lambda n: jax.ShapeDtypeStruct((bsz, n, s, HEAD_DIM), BF16)
    vals = lambda n: pl.BlockSpec((1, n, tm, V_EXT_W), lambda b, i: (b, 0, i, 0))
    vshape = lambda n: jax.ShapeDtypeStruct((bsz, n, s, V_EXT_W), BF16)
    return pl.pallas_call(
        functools.partial(_proj_body, tm=tm),
        grid=(bsz, nt),
        in_specs=_stream_specs(tm, ctx_block) + [
            pl.BlockSpec((1, 2, 6, d), lambda b, i: (b, 0, 0, 0)),
            pl.BlockSpec((d, PROJ_W), lambda b, i: (0, 0)),
            pl.BlockSpec((RMS_BLOCK, RMS_BLOCK), lambda b, i: (0, 0)),
            pl.BlockSpec((1, QK_W), lambda b, i: (0, 0)),
            pl.BlockSpec((tm, ROPE_TABLE_W), lambda b, i: (i, 0)),
            pl.BlockSpec((tm, ROPE_TABLE_W), lambda b, i: (i, 0)),
        ],
        out_specs=[heads(A_HEADS), heads(A_KV_HEADS), vals(A_KV_HEADS),
                   heads(B_HEADS), heads(B_HEADS), vals(B_HEADS),
                   pl.BlockSpec((1, tm, 3 * C_CH), lambda b, i: (b, i, 0))],
        out_shape=[hshape(A_HEADS), hshape(A_KV_HEADS), vshape(A_KV_HEADS),
                   hshape(B_HEADS), hshape(B_HEADS), vshape(B_HEADS),
                   jax.ShapeDtypeStruct((bsz, s, 3 * C_CH), F32)],
        compiler_params=_params("parallel", "parallel"),
        name="proj_in",
    )(*_stream_args(x_lat, x_ctx, tm), mod, w_bf, bd, gain, cos_t, sin_t)


def _gqa_body(q_ref, k_ref, v_ref, o_ref):
    n_keys = k_ref.shape[2]
    for h in range(A_REP):
        q = q_ref[0, h]
        m = acc = None
        for j in range(n_keys // GQA_KEY_CHUNK):
            keys = slice(j * GQA_KEY_CHUNK, (j + 1) * GQA_KEY_CHUNK)
            s = _dot_nt(q, k_ref[0, 0, keys, :])
            mj = jnp.max(s, -1, keepdims=True)
            if j == 0:
                m = mj
                acc = _dot(jnp.exp2(s - m).astype(BF16), v_ref[0, 0, keys, :])
            else:
                m_new = jnp.maximum(m, mj)
                alpha = jnp.exp2(m - m_new)
                acc = alpha * acc + _dot(jnp.exp2(s - m_new).astype(BF16), v_ref[0, 0, keys, :])
                m = m_new
        o_ref[0, :, h * HEAD_DIM:(h + 1) * HEAD_DIM] = _normalise(acc).astype(BF16)


def _gqa(qa, ka, va):
    bsz, _, s, _ = qa.shape
    tq = GQA_Q_TILE
    return pl.pallas_call(
        _gqa_body,
        grid=(bsz, A_KV_HEADS, SEQ // tq),
        in_specs=[
            pl.BlockSpec((1, A_REP, tq, HEAD_DIM), lambda b, g, i: (b, g, i, 0)),
            pl.BlockSpec((1, 1, s, HEAD_DIM), lambda b, g, i: (b, g, 0, 0)),
            pl.BlockSpec((1, 1, s, V_EXT_W), lambda b, g, i: (b, g, 0, 0)),
        ],
        out_specs=pl.BlockSpec((1, tq, A_REP * HEAD_DIM), lambda b, g, i: (b, i, g)),
        out_shape=jax.ShapeDtypeStruct((bsz, SEQ, A_W), BF16),
        compiler_params=_params("parallel", "parallel", "parallel"),
        name="gqa_attn",
    )(qa, ka, va)


NA_HEADS_PER_STEP = 4
NA_Q_ROWS = 8
NA_K_ROWS = 16
NA_Q_TOKENS = NA_Q_ROWS * GRID_W
NA_K_TOKENS = NA_K_ROWS * GRID_W
NA_Q_BLOCKS = GRID_ROWS // NA_Q_ROWS
NA_KEY_ROW0 = (0, 4, 12, 16)
NA_PATTERN_BLOCKS = (0, 1, 3)


def _na_body(q_ref, k_ref, v_ref, bias_ref, o_ref):
    j = pl.program_id(2)
    key_row0 = jnp.where(j == 0, NA_KEY_ROW0[0],
                         jnp.where(j == 1, NA_KEY_ROW0[1], jnp.where(j == 2, NA_KEY_ROW0[2], NA_KEY_ROW0[3])))
    k0 = pl.multiple_of(key_row0 * GRID_W, 4 * GRID_W)
    for hh in range(NA_HEADS_PER_STEP):
        q = q_ref[0, hh]
        kw = k_ref[0, hh, pl.ds(k0, NA_K_TOKENS), :]
        vw = v_ref[0, hh, pl.ds(k0, NA_K_TOKENS), :]
        kc = k_ref[0, hh, SEQ:, :]
        vc = v_ref[0, hh, SEQ:, :]
        sw = _dot_nt(q, kw) + bias_ref[hh, 0]
        sc = _dot_nt(q, kc)
        m = jnp.maximum(jnp.max(sw, -1, keepdims=True), jnp.max(sc, -1, keepdims=True))
        o_ext = _dot(jnp.exp2(sw - m).astype(BF16), vw) + _dot(jnp.exp2(sc - m).astype(BF16), vc)
        o_ref[0, :, hh * HEAD_DIM:(hh + 1) * HEAD_DIM] = _normalise(o_ext).astype(BF16)


def _na(qb, kb, vb, bias):
    bsz, _, s, _ = qb.shape
    hp = NA_HEADS_PER_STEP
    kv = lambda w: pl.BlockSpec((1, hp, s, w), lambda b, h, j: (b, h, 0, 0))
    return pl.pallas_call(
        _na_body,
        grid=(bsz, B_HEADS // hp, NA_Q_BLOCKS),
        in_specs=[pl.BlockSpec((1, hp, NA_Q_TOKENS, HEAD_DIM), lambda b, h, j: (b, h, j, 0)),
                  kv(HEAD_DIM), kv(V_EXT_W),
                  pl.BlockSpec((hp, 1, NA_Q_TOKENS, NA_K_TOKENS), lambda b, h, j: (h, (j + 1) // 2, 0, 0))],
        out_specs=pl.BlockSpec((1, NA_Q_TOKENS, hp * HEAD_DIM), lambda b, h, j: (b, j, h)),
        out_shape=jax.ShapeDtypeStruct((bsz, SEQ, B_W), BF16),
        compiler_params=_params("parallel", "parallel", "arbitrary"),
        name="nbr_attn",
    )(qb, kb, vb, bias)


def _na_bias_body(t_ref, o_ref):
    o_ref[...] = jnp.full(o_ref.shape, NEG_BIG, F32)
    for p, blk in enumerate(NA_PATTERN_BLOCKS):
        @pl.when(pl.program_id(1) == p)
        def _(blk=blk):
            for qi in range(NA_Q_ROWS):
                qrow = blk * NA_Q_ROWS + qi
                rs = min(max(qrow - WIN_ROWS // 2, 0), GRID_ROWS - WIN_ROWS)
                for krow in range(rs, rs + WIN_ROWS):
                    ki = krow - NA_KEY_ROW0[blk]
                    o_ref[0, 0, qi * GRID_W:(qi + 1) * GRID_W, ki * GRID_W:(ki + 1) * GRID_W] = (
                        t_ref[0, krow - qrow + WIN_ROWS - 1])


def _na_bias_table(rpb):
    col = np.arange(GRID_W)
    col_start = np.clip(col - WIN_COLS // 2, 0, GRID_W - WIN_COLS)
    in_cols = (col[None, :] >= col_start[:, None]) & (col[None, :] < col_start[:, None] + WIN_COLS)
    dc = np.clip(col[None, :] - col[:, None] + WIN_COLS - 1, 0, 2 * WIN_COLS - 2)
    onehot = jnp.asarray(dc[None] == np.arange(2 * WIN_COLS - 1)[:, None, None], F32)
    tiles = jnp.einsum('hdj,jqk->hdqk', rpb.astype(F32), onehot, precision=HIGHEST)
    tiles = jnp.where(in_cols[None, None], tiles * LOG2E, NEG_BIG)
    n_pat = len(NA_PATTERN_BLOCKS)
    return pl.pallas_call(
        _na_bias_body,
        grid=(B_HEADS, n_pat),
        in_specs=[pl.BlockSpec((1, 2 * WIN_ROWS - 1, GRID_W, GRID_W), lambda h, p: (h, 0, 0, 0))],
        out_specs=pl.BlockSpec((1, 1, NA_Q_TOKENS, NA_K_TOKENS), lambda h, p: (h, p, 0, 0)),
        out_shape=jax.ShapeDtypeStruct((B_HEADS, n_pat, NA_Q_TOKENS, NA_K_TOKENS), F32),
        compiler_params=_params("parallel", "arbitrary"),
        name="nbr_bias",
    )(tiles)


def _ctx_attn_body(q_ref, k_ref, v_ref, o_ref, *, rep):
    for g in range(k_ref.shape[1]):
        for r in range(rep):
            hh = g * rep + r
            s = _dot_nt(q_ref[0, hh], k_ref[0, g])
            m = jnp.max(s, -1, keepdims=True)
            o_ext = _dot(jnp.exp2(s - m).astype(BF16), v_ref[0, g])
            o_ref[0, :, hh * HEAD_DIM:(hh + 1) * HEAD_DIM] = _normalise(o_ext).astype(BF16)


def _ctx_attn(q, k, v, kv_per_step):
    bsz, n_q, _, _ = q.shape
    n_kv = k.shape[1]
    rep = n_q // n_kv
    seg = lambda n, w=HEAD_DIM: pl.BlockSpec((1, n, CTX_LEN, w), lambda b, h: (b, h, SEQ // CTX_LEN, 0))
    return pl.pallas_call(
        functools.partial(_ctx_attn_body, rep=rep),
        grid=(bsz, n_kv // kv_per_step),
        in_specs=[seg(kv_per_step * rep), seg(kv_per_step), seg(kv_per_step, V_EXT_W)],
        out_specs=pl.BlockSpec((1, CTX_LEN, kv_per_step * rep * HEAD_DIM), lambda b, h: (b, 0, h)),
        out_shape=jax.ShapeDtypeStruct((bsz, CTX_LEN, n_q * HEAD_DIM), BF16),
        compiler_params=_params("parallel", "parallel"),
        name="ctx_attn",
    )(q, k, v)


DFT_FINE = 16


def _dft_table_body(chr_ref, shr_ref, clr_ref, slr_ref, chc_ref, shc_ref, clc_ref, slc_ref,
                    fwd_ref, inv_ref, *, n, tf):
    j = pl.program_id(0)

    def expand(c_hi, s_hi, c_lo, s_lo, rows):
        cos = (c_hi[:, None, :] * c_lo[None, :, :] - s_hi[:, None, :] * s_lo[None, :, :])
        sin = (s_hi[:, None, :] * c_lo[None, :, :] + c_hi[:, None, :] * s_lo[None, :, :])
        return cos.reshape(rows, -1), sin.reshape(rows, -1)

    cos_r, sin_r = expand(chr_ref[...], shr_ref[...], clr_ref[...], slr_ref[...], tf)
    cos_c, sin_c = expand(chc_ref[...], shc_ref[...], clc_ref[...], slc_ref[...], n)
    fr = lax.broadcasted_iota(jnp.int32, (tf, n), 0) + j * tf
    tr = lax.broadcasted_iota(jnp.int32, (tf, n), 1)
    sin_r = jnp.where(fr == 0, jnp.where(tr % 2 == 0, 1.0, -1.0), sin_r)
    tc = lax.broadcasted_iota(jnp.int32, (n, tf), 0)
    fcol = lax.broadcasted_iota(jnp.int32, (n, tf), 1) + j * tf
    sin_c = jnp.where(fcol == 0, jnp.where(tc % 2 == 0, 1.0, -1.0), sin_c)
    fwd_ref[0, :tf, :] = cos_r.astype(BF16)
    fwd_ref[0, tf:, :] = sin_r.astype(BF16)
    inv_ref[0, :, :tf] = cos_c.astype(BF16)
    inv_ref[0, :, tf:] = sin_c.astype(BF16)


def _dft_tables(n):
    s = jnp.arange(n, dtype=jnp.int32)[None, :]

    def cos_sin(f):
        ang = ((f * s) % (2 * n)).astype(F32) * (math.pi / n)
        return jnp.cos(ang), jnp.sin(ang)

    ch, sh = cos_sin(jnp.arange(n // DFT_FINE, dtype=jnp.int32)[:, None] * DFT_FINE)
    cl, sl = cos_sin(jnp.arange(DFT_FINE, dtype=jnp.int32)[:, None])
    tf = min(FREQ_TILE, n)
    nf = n // tf
    hi_rows = pl.BlockSpec((tf // DFT_FINE, n), lambda j: (j, 0))
    lo_rows = pl.BlockSpec((DFT_FINE, n), lambda j: (0, 0))
    hi_cols = pl.BlockSpec((n // DFT_FINE, tf), lambda j: (0, j))
    lo_cols = pl.BlockSpec((DFT_FINE, tf), lambda j: (0, j))
    return pl.pallas_call(
        functools.partial(_dft_table_body, n=n, tf=tf),
        grid=(nf,),
        in_specs=[hi_rows, hi_rows, lo_rows, lo_rows, hi_cols, hi_cols, lo_cols, lo_cols],
        out_specs=[pl.BlockSpec((1, 2 * tf, n), lambda j: (j, 0, 0)),
                   pl.BlockSpec((1, n, 2 * tf), lambda j: (j, 0, 0))],
        out_shape=[jax.ShapeDtypeStruct((nf, 2 * tf, n), BF16),
                   jax.ShapeDtypeStruct((nf, n, 2 * tf), BF16)],
        compiler_params=_params("parallel"),
        name="dft_tables",
    )(ch, sh, cl, sl, ch, sh, cl, sl)


def _split_bf16(v):
    hi = v.astype(BF16)
    lo = (v - hi.astype(F32)).astype(BF16)
    return hi, lo


def _spectrum_body(fwd_ref, kp_ref, km_ref, kc_ref, ks_ref, *, tf):
    kp_hi, kp_lo = _split_bf16(kp_ref[...])
    km_hi, km_lo = _split_bf16(km_ref[...])
    fc = fwd_ref[0, :tf, :]
    fs = fwd_ref[0, tf:, :]
    kc_ref[...] = _dot(fc, kp_hi) + _dot(fc, kp_lo)
    ks_sin = _dot(fs, km_hi) + _dot(fs, km_lo)
    ks_nyq = _dot(fs, kp_hi) + _dot(fs, kp_lo)
    row = lax.broadcasted_iota(jnp.int32, ks_sin.shape, 0) + pl.program_id(0) * tf
    ks_ref[...] = jnp.where(row == 0, ks_nyq, ks_sin)


def _filter_spectrum(fwd, kplus, kminus):
    nf, tf2, n = fwd.shape
    tf = tf2 // 2
    filt = pl.BlockSpec((n, C_CH), lambda j: (0, 0))
    out = pl.BlockSpec((tf, C_CH), lambda j: (j, 0))
    return pl.pallas_call(
        functools.partial(_spectrum_body, tf=tf),
        grid=(nf,),
        in_specs=[pl.BlockSpec((1, tf2, n), lambda j: (j, 0, 0)), filt, filt],
        out_specs=[out, out],
        out_shape=[jax.ShapeDtypeStruct((n, C_CH), F32)] * 2,
        compiler_params=_params("parallel"),
        name="hyena_spectrum",
    )(fwd, kplus, kminus)


def _short_conv(z, cw_ref, cb_ref, cols, n):
    pos = lax.broadcasted_iota(jnp.int32, z.shape, 0)
    prev = jnp.where(pos == 0, 0.0, pltpu.roll(z, 1, 0))
    nxt = jnp.where(pos == n - 1, 0.0, pltpu.roll(z, n - 1, 0))
    return prev * cw_ref[0:1, cols] + z * cw_ref[1:2, cols] + nxt * cw_ref[2:3, cols] + cb_ref[:, cols]


def _hyena_gate_body(z_ref, cw_ref, cb_ref, skip_ref, u_ref, us_ref, *, n):
    zc = _short_conv(z_ref[0, :, C_CH:], cw_ref, cb_ref, slice(C_CH, 3 * C_CH), n)
    u = zc[:, C_CH:] * zc[:, :C_CH]
    u_ref[0] = u.astype(BF16)
    us_ref[0] = u * skip_ref[...]


def _hyena_gate(z, seg_start, n, cw, cb, skip):
    bsz = z.shape[0]
    seg = seg_start // n
    small = lambda r, c: pl.BlockSpec((r, c), lambda b: (0, 0))
    out = pl.BlockSpec((1, n, C_CH), lambda b: (b, 0, 0))
    return pl.pallas_call(
        functools.partial(_hyena_gate_body, n=n),
        grid=(bsz,),
        in_specs=[pl.BlockSpec((1, n, 3 * C_CH), lambda b: (b, seg, 0)),
                  small(3, 3 * C_CH), small(1, 3 * C_CH), small(1, C_CH)],
        out_specs=[out, out],
        out_shape=[jax.ShapeDtypeStruct((bsz, n, C_CH), BF16),
                   jax.ShapeDtypeStruct((bsz, n, C_CH), F32)],
        compiler_params=_params("parallel"),
        name="hyena_gate",
    )(z, cw, cb, skip)


def _hyena_body(u_ref, us_ref, z_ref, cw_ref, cb_ref, fwd_ref, inv_ref, kc_ref, ks_ref, o_ref, *, n, tf):
    u = u_ref[0]
    y = None
    for j in range(n // tf):
        freq = slice(j * tf, (j + 1) * tf)
        kc = kc_ref[freq, :]
        ks = ks_ref[freq, :]
        spec = _dot(fwd_ref[j], u)
        uc, us = spec[:tf], spec[tf:]
        if j == 0:
            first = lax.broadcasted_iota(jnp.int32, kc.shape, 0) == 0
            wgt = jnp.where(first, 1.0, 2.0) * (1.0 / (2 * n))
            re = jnp.where(first, uc * kc, uc * kc - us * ks) * wgt
            im = jnp.where(first, us * ks, uc * ks + us * kc) * wgt
        else:
            re = (uc * kc - us * ks) * (1.0 / n)
            im = (uc * ks + us * kc) * (1.0 / n)
        part = _dot(inv_ref[j], jnp.concatenate([re, im], axis=0).astype(BF16))
        y = part if y is None else y + part
    x0 = _short_conv(z_ref[0], cw_ref, cb_ref, slice(0, C_CH), n)
    o_ref[0] = ((y + us_ref[0]) * x0).astype(BF16)


def _hyena(u, us, z, seg_start, n, cw, cb, fwd, inv, kc, ks):
    bsz = u.shape[0]
    tf = min(FREQ_TILE, n)
    nf = n // tf
    seg = seg_start // n
    seq = lambda **kw: pl.BlockSpec((1, n, C_CH), lambda b: (b, 0, 0), **kw)
    once = dict(pipeline_mode=pl.Buffered(1))
    small = lambda r, c: pl.BlockSpec((r, c), lambda b: (0, 0))
    return pl.pallas_call(
        functools.partial(_hyena_body, n=n, tf=tf),
        grid=(bsz,),
        in_specs=[seq(), seq(),
                  pl.BlockSpec((1, n, C_CH), lambda b: (b, seg, 0)),
                  small(3, 3 * C_CH), small(1, 3 * C_CH),
                  pl.BlockSpec((nf, 2 * tf, n), lambda b: (0, 0, 0), **once),
                  pl.BlockSpec((nf, n, 2 * tf), lambda b: (0, 0, 0), **once),
                  pl.BlockSpec((n, C_CH), lambda b: (0, 0), **once),
                  pl.BlockSpec((n, C_CH), lambda b: (0, 0), **once)],
        out_specs=seq(),
        out_shape=jax.ShapeDtypeStruct((bsz, n, C_CH), BF16),
        compiler_params=_params("parallel"),
        name="hyena_conv",
    )(u, us, z, cw, cb, fwd, inv, kc, ks)


def _dot3(a, b):
    a_hi, a_lo = _split_bf16(a)
    b_hi, b_lo = _split_bf16(b)
    return _dot(a_hi, b_hi) + _dot(a_hi, b_lo) + _dot(a_lo, b_hi)


POS_PAD = 128
FILTER_ROWS = 256


def _filter_body(pos_ref, decay_ref, w1_ref, b1_ref, w2_ref, b2_ref, w3_ref, freq_ref, kp_ref, km_ref):
    h = jnp.sin(freq_ref[0:1, :] * (_dot3(pos_ref[...], w1_ref[...]) + b1_ref[...]))
    h = jnp.sin(freq_ref[1:2, :] * (_dot3(h, w2_ref[...]) + b2_ref[...]))
    h = _dot3(h, w3_ref[...])
    decay = decay_ref[...]
    hf = h[:, :C_CH] * decay
    hb = h[:, C_CH:] * decay
    lag = lax.broadcasted_iota(jnp.int32, hb.shape, 0) + pl.program_id(0) * FILTER_ROWS
    hb = jnp.where(lag == 0, 0.0, hb)
    kp_ref[...] = hf + hb
    km_ref[...] = hf - hb


def _implicit_filters(n, w1, b1, w2, b2, w3, freq):
    t = jnp.linspace(0.0, 1.0, n, dtype=F32)[:, None]
    w = 2 * math.pi * jnp.arange(n, dtype=F32)[:, None] / n
    f = jnp.linspace(1e-4, C_POS_BANDS - 1, C_POS_BANDS, dtype=F32)[None]
    pos = jnp.concatenate([t, jnp.cos(f * w), -jnp.sin(f * w)], -1)
    n_pos = pos.shape[1]
    pos = jnp.pad(pos, ((0, 0), (0, POS_PAD - n_pos)))
    w1p = jnp.pad(w1, ((0, POS_PAD - n_pos), (0, 0)))
    lo = math.log(C_DECAY_TARGET) / C_SLOW_DECAY
    hi = math.log(C_DECAY_TARGET) / C_FAST_DECAY
    deltas = jnp.abs(jnp.linspace(lo, hi, C_CH, dtype=F32))
    decay = jnp.exp(-t * deltas[None])
    hid = w2.shape[0]
    rows = lambda c: pl.BlockSpec((FILTER_ROWS, c), lambda i: (i, 0))
    full = lambda r, c: pl.BlockSpec((r, c), lambda i: (0, 0))
    return pl.pallas_call(
        _filter_body,
        grid=(n // FILTER_ROWS,),
        in_specs=[rows(POS_PAD), rows(C_CH), full(POS_PAD, hid), full(1, hid), full(hid, hid), full(1, hid),
                  full(hid, 2 * C_CH), full(2, hid)],
        out_specs=[rows(C_CH), rows(C_CH)],
        out_shape=[jax.ShapeDtypeStruct((n, C_CH), F32)] * 2,
        compiler_params=_params("parallel"),
        name="hyena_filter",
    )(pos, decay, w1p, b1[None], w2, b2[None], w3, freq)


def _router_gates(logits_t, rb):
    tm = logits_t.shape[1]
    scores = _sigmoid(logits_t)
    sel = scores + rb
    sel3 = sel.reshape(N_GROUPS, GROUP_SIZE, tm)
    kk = lax.broadcasted_iota(jnp.int32, sel3.shape, 1)
    m1 = jnp.max(sel3, 1, keepdims=True)
    i1 = jnp.min(jnp.where(sel3 == m1, kk, GROUP_SIZE), 1, keepdims=True)
    m2 = jnp.max(jnp.where(kk == i1, -jnp.inf, sel3), 1, keepdims=True)
    gscore = (m1 + m2).reshape(N_GROUPS, tm)
    gid = lax.broadcasted_iota(jnp.int32, gscore.shape, 0)
    rank = jnp.zeros(gscore.shape, jnp.int32)
    for g in range(N_GROUPS):
        other = gscore[g:g + 1, :]
        ahead = (other > gscore) | ((other == gscore) & (g < gid))
        rank = rank + ahead.astype(jnp.int32)
    gsel = (rank < TOPK_GROUPS).reshape(N_GROUPS, 1, tm)
    cand = jnp.where(gsel, sel3, -jnp.inf).reshape(N_EXPERTS, tm)
    eid = lax.broadcasted_iota(jnp.int32, cand.shape, 0)
    chosen = jnp.zeros(cand.shape, jnp.bool_)
    for _ in range(TOP_K):
        m = jnp.max(cand, 0, keepdims=True)
        pick = eid == jnp.min(jnp.where(cand == m, eid, N_EXPERTS), 0, keepdims=True)
        chosen = chosen | pick
        cand = jnp.where(pick, -jnp.inf, cand)
    wsel = jnp.where(chosen, scores, 0.0)
    return wsel / jnp.sum(wsel, 0, keepdims=True) * ROUTED_SCALE


def _out_body(*refs, tm):
    n_x = tm // TOKEN_TILE + 1
    x_refs, oa_refs, ob_refs, oc_refs = (refs[j * n_x:(j + 1) * n_x] for j in range(4))
    (mod_ref, wa_ref, wb_ref, wc_ref, lng_ref, lnb_ref,
     rwt_ref, rb_ref, xmid_ref, h_ref, gate_ref) = refs[4 * n_x:]
    for k, rows, is_ctx in _sub_tiles(tm):
        attn = (_dot(_stream_tile(oa_refs, k, is_ctx), wa_ref[...])
                + _dot(_stream_tile(ob_refs, k, is_ctx), wb_ref[...])
                + _dot(_stream_tile(oc_refs, k, is_ctx), wc_ref[...]))
        y = DN_ALPHA * _stream_tile(x_refs, k, is_ctx) + _mod_row(mod_ref, is_ctx, 2) * attn
        xm = _layer_norm(y) * lng_ref[...] + lnb_ref[...]
        xmid_ref[0, rows, :] = xm
        h = _layer_norm(xm) * (1.0 + _mod_row(mod_ref, is_ctx, 4)) + _mod_row(mod_ref, is_ctx, 3)
        hi, lo = _split_bf16(h)
        h_ref[0, rows, :] = hi
        logits_t = _dot_nt(rwt_ref[...], jnp.concatenate([hi, lo, hi], axis=1))
        gates_t = _router_gates(logits_t, rb_ref[...])
        pad = jnp.zeros((128 - N_EXPERTS, gates_t.shape[1]), F32)
        gates = jnp.concatenate([gates_t, pad], axis=0).T
        for g in range(N_GROUPS):
            gate_ref[0, g, rows, :] = gates[:, g * GROUP_SIZE:(g + 1) * GROUP_SIZE]


def _out_proj(s, x_lat, x_ctx, ctx_block, oa, ob, oc, mod, wa, wb, wc, lng, lnb, rwt3, rb):
    bsz = x_lat.shape[0]
    d = D_MODEL
    tm = _step_tokens(s)
    tok = lambda w: pl.BlockSpec((1, tm, w), lambda b, i: (b, i, 0))
    full = lambda r, c: pl.BlockSpec((r, c), lambda b, i: (0, 0))
    return pl.pallas_call(
        functools.partial(_out_body, tm=tm),
        grid=(bsz, s // tm),
        in_specs=_stream_specs(tm, ctx_block) + _stream_specs(tm, 0, A_W) + _stream_specs(tm, 0, B_W)
        + _stream_specs(tm, 0, C_CH) + [
                  pl.BlockSpec((1, 2, 6, d), lambda b, i: (b, 0, 0, 0)),
                  full(A_W, d), full(B_W, d), full(C_CH, d), full(1, d), full(1, d),
                  full(N_EXPERTS, 3 * d), full(N_EXPERTS, 1)],
        out_specs=[tok(d), tok(d),
                   pl.BlockSpec((1, N_GROUPS, tm, GROUP_SIZE), lambda b, i: (b, 0, i, 0))],
        out_shape=[jax.ShapeDtypeStruct((bsz, s, d), F32),
                   jax.ShapeDtypeStruct((bsz, s, d), BF16),
                   jax.ShapeDtypeStruct((bsz, N_GROUPS, s, GROUP_SIZE), F32)],
        compiler_params=_params("parallel", "parallel"),
        name="proj_out_router",
    )(*_stream_args(x_lat, x_ctx, tm), *_stream_args(*oa, tm), *_stream_args(*ob, tm), *_stream_args(*oc, tm),
      mod, wa, wb, wc, lng, lnb, rwt3, rb)


MOE_CHAINS = 2


def _moe_body(h_ref, gate_ref, xmid_ref, mod_ref, w1_ref, w3_ref, w2_ref,
              sw1_ref, sw3_ref, sw2_ref, lng_ref, lnb_ref, o_ref, acc_ref, *, tm):
    g = pl.program_id(2)
    rc = tm // MOE_CHAINS

    @pl.when(g == 0)
    def _():
        h = h_ref[0]
        a = _dot(h, sw1_ref[...])
        act = (a * _sigmoid(a) * _dot(h, sw3_ref[...])).astype(BF16)
        acc_ref[...] = _dot(act, sw2_ref[...])

    w1 = jnp.concatenate([w1_ref[k] for k in range(GROUP_SIZE)], axis=1)
    w3 = jnp.concatenate([w3_ref[k] for k in range(GROUP_SIZE)], axis=1)
    w2 = w2_ref[...].reshape(GROUP_SIZE * EXPERT_HIDDEN, D_MODEL)
    for c in range(MOE_CHAINS):
        rows = slice(c * rc, (c + 1) * rc)
        h = h_ref[0, rows, :]
        a = _dot(h, w1)
        act = a * _sigmoid(a) * _dot(h, w3)
        gt = gate_ref[0, 0, rows, :]
        parts = [act[:, k * EXPERT_HIDDEN:(k + 1) * EXPERT_HIDDEN] * gt[:, k:k + 1]
                 for k in range(GROUP_SIZE)]
        acc_ref[rows, :] += _dot(jnp.concatenate(parts, axis=1).astype(BF16), w2)

    @pl.when(g == N_GROUPS - 1)
    def _():
        row = lax.broadcasted_iota(jnp.int32, (tm, 1), 0) + pl.program_id(1) * tm
        gate2 = jnp.where(row >= SEQ, mod_ref[0, 1, 5:6, :], mod_ref[0, 0, 5:6, :])
        y = DN_ALPHA * xmid_ref[0] + gate2 * acc_ref[...]
        o_ref[0] = _layer_norm(y) * lng_ref[...] + lnb_ref[...]


MOE_TILES_PER_SAMPLE = 2


def _moe(h, gates, xmid, mod, layer, w1_all, w3_all, w2_all, sw1, sw3, sw2, lng, lnb):
    bsz, s, d = h.shape
    tm = s // MOE_TILES_PER_SAMPLE
    tok = pl.BlockSpec((1, tm, d), lambda b, i, g: (b, i, 0))
    tok_once = tok
    full = lambda r, c: pl.BlockSpec((r, c), lambda b, i, g: (0, 0))
    up = pl.BlockSpec((None, GROUP_SIZE, d, EXPERT_HIDDEN), lambda b, i, g: (layer, g, 0, 0))
    return pl.pallas_call(
        functools.partial(_moe_body, tm=tm),
        grid=(bsz, s // tm, N_GROUPS),
        in_specs=[tok,
                  pl.BlockSpec((1, 1, tm, GROUP_SIZE), lambda b, i, g: (b, g, i, 0)),
                  tok_once,
                  pl.BlockSpec((1, 2, 6, d), lambda b, i, g: (b, 0, 0, 0)),
                  up, up,
                  pl.BlockSpec((None, GROUP_SIZE, EXPERT_HIDDEN, d), lambda b, i, g: (layer, g, 0, 0)),
                  full(d, SHARED_HIDDEN), full(d, SHARED_HIDDEN), full(SHARED_HIDDEN, d),
                  full(1, d), full(1, d)],
        out_specs=tok,
        out_shape=jax.ShapeDtypeStruct((bsz, s, d), F32),
        scratch_shapes=[pltpu.VMEM((tm, d), F32)],
        compiler_params=_params("parallel", "parallel", "arbitrary"),
        name="moe_ffn",
    )(h, gates, xmid, mod, w1_all, w3_all, w2_all, sw1, sw3, sw2, lng, lnb)


def _rope_tables():
    t = jnp.arange(SEQ)
    row = (t // GRID_W).astype(F32)
    col = (t % GRID_W).astype(F32)
    inv = ROPE_THETA ** (-jnp.arange(0, HALF, 2, dtype=F32) / HALF)
    ang = jnp.concatenate([row[:, None] * inv, col[:, None] * inv], -1)
    cos, sin = jnp.cos(ang), jnp.sin(ang)
    n_heads = ROPE_TABLE_W // HEAD_DIM
    cos_t = jnp.concatenate([jnp.tile(jnp.concatenate([cos, cos], -1), (1, n_heads)),
                             jnp.ones((CTX_LEN, ROPE_TABLE_W), F32)], 0)
    sin_t = jnp.concatenate([jnp.tile(jnp.concatenate([-sin, sin], -1), (1, n_heads)),
                             jnp.zeros((CTX_LEN, ROPE_TABLE_W), F32)], 0)
    return cos_t, sin_t


MOD_ROWS = 16
MOD_COL_TILE = 1536


def _modulation_body(c_ref, w_ref, b_ref, o_ref):
    cv = c_ref[...]
    o_ref[0] = _dot3(cv * _sigmoid(cv), w_ref[0]) + b_ref[0]


def _modulation(c, c_ctx, ada_w, ada_b):
    bsz = c.shape[0]
    assert bsz + 1 <= MOD_ROWS, bsz
    n_out = 6 * D_MODEL
    cvec = jnp.concatenate([c, c_ctx[None], jnp.zeros((MOD_ROWS - bsz - 1, D_MODEL), F32)], axis=0)
    m = pl.pallas_call(
        _modulation_body,
        grid=(DEPTH, n_out // MOD_COL_TILE),
        in_specs=[pl.BlockSpec((MOD_ROWS, D_MODEL), lambda l, j: (0, 0)),
                  pl.BlockSpec((1, D_MODEL, MOD_COL_TILE), lambda l, j: (l, 0, j)),
                  pl.BlockSpec((1, 1, MOD_COL_TILE), lambda l, j: (l, 0, j))],
        out_specs=pl.BlockSpec((1, MOD_ROWS, MOD_COL_TILE), lambda l, j: (l, 0, j)),
        out_shape=jax.ShapeDtypeStruct((DEPTH, MOD_ROWS, n_out), F32),
        compiler_params=_params("parallel", "parallel"),
        name="ada_modulation",
    )(cvec, ada_w, ada_b[:, None, :])
    m = m.reshape(DEPTH, MOD_ROWS, 6, D_MODEL)
    lat = m[:, :bsz]
    return jnp.stack([lat, jnp.broadcast_to(m[:, bsz:bsz + 1], lat.shape)], axis=2)


def kernel(x, c, ctx, c_ctx, ada_w, ada_b, w_in, q_gain, k_gain, nat_rpb, hy_conv_w, hy_conv_b, hy_f_w1, hy_f_b1, hy_f_w2, hy_f_b2, hy_f_w3, hy_freq, hy_skip, w_out, ln1_g, ln1_b, router_w, router_b, exp_w1, exp_w3, exp_w2, sh_w1, sh_w3, sh_w2, ln2_g, ln2_b):
    bsz = x.shape[0]
    d = D_MODEL
    x_lat, x_ctx, ctx_block = x, ctx, 0

    head_perm = np.concatenate([np.arange(0, HEAD_DIM, 2), np.arange(1, HEAD_DIM, 2)])
    qk_perm = (np.arange(QK_W // HEAD_DIM)[:, None] * HEAD_DIM + head_perm[None, :]).reshape(-1)
    perm_mat = jnp.asarray(np.eye(QK_W)[:, qk_perm], BF16)
    cos_t, sin_t = _rope_tables()
    bd = jnp.asarray(np.kron(np.eye(RMS_BLOCK // HEAD_DIM), np.ones((HEAD_DIM, HEAD_DIM))), BF16)
    tables_lat = _dft_tables(SEQ)
    tables_ctx = _dft_tables(CTX_LEN)
    mod_all = _modulation(c, c_ctx, ada_w, ada_b)
    w1_bf, w3_bf, w2_bf = exp_w1.astype(BF16), exp_w3.astype(BF16), exp_w2.astype(BF16)

    for l in range(DEPTH):
        last = l == DEPTH - 1
        mod = mod_all[l]

        w_bf = w_in[l].astype(BF16)
        w_qk = jnp.dot(w_bf[:, :QK_W], perm_mat, preferred_element_type=F32).astype(BF16)
        w_bf = jnp.concatenate([w_qk, w_bf[:, QK_W:]], axis=1)
        gain = jnp.concatenate([jnp.tile(q_gain[l][head_perm] * Q_SCALE, A_HEADS),
                                jnp.tile(k_gain[l][head_perm], A_KV_HEADS)])[None].astype(F32)
        qa, ka, va, qb, kb, vb, z = _project(x_lat, x_ctx, ctx_block, mod, w_bf, bd, gain, cos_t, sin_t)

        oa = [_gqa(qa, ka, va)]
        ob = [_na(qb, kb, vb, _na_bias_table(nat_rpb[l]))]
        if not last:
            oa.append(_ctx_attn(qa, ka, va, 1))
            ob.append(_ctx_attn(qb, kb, vb, NA_HEADS_PER_STEP))

        filt = (hy_f_w1[l], hy_f_b1[l], hy_f_w2[l], hy_f_b2[l], hy_f_w3[l], hy_freq[l])
        cw, cb, skip = hy_conv_w[l], hy_conv_b[l][None], hy_skip[l][None]
        segments = [(0, SEQ, tables_lat)] + ([] if last else [(SEQ, CTX_LEN, tables_ctx)])
        oc = []
        for seg_start, n, (fwd, inv) in segments:
            kc, ks = _filter_spectrum(fwd, *_implicit_filters(n, *filt))
            u, us = _hyena_gate(z, seg_start, n, cw, cb, skip)
            oc.append(_hyena(u, us, z, seg_start, n, cw, cb, fwd, inv, kc, ks))
        oa, ob, oc = ((o[0], o[-1]) for o in (oa, ob, oc))

        wo = w_out[l].astype(BF16)
        rw_hi, rw_lo = _split_bf16(router_w[l].T)
        rwt3 = jnp.concatenate([rw_hi, rw_hi, rw_lo], axis=1)
        xmid, h, gates = _out_proj(SEQ if last else S_ALL, x_lat, x_ctx, ctx_block, oa, ob, oc, mod,
                                   wo[:A_W], wo[A_W:A_W + B_W], wo[A_W + B_W:],
                                   ln1_g[l][None], ln1_b[l][None], rwt3, router_b[l][:, None])

        xs = _moe(h, gates, xmid, mod, l, w1_bf, w3_bf, w2_bf,
                  sh_w1[l].astype(BF16), sh_w3[l].astype(BF16), sh_w2[l].astype(BF16),
                  ln2_g[l][None], ln2_b[l][None])
        x_lat, x_ctx, ctx_block = xs, xs, SEQ // TOKEN_TILE
    return xs
```
